```python
import math
import jax, jax.numpy as jnp
from jax import lax
import numpy as np

D_MODEL = 1024
BATCH = 16
SEQ = 2048
DEPTH = 2

N_HEADS = 16
HEAD_DIM = D_MODEL // N_HEADS
Q_CHUNK = 64
NEG_INF = -1e30
LN_EPS = 1e-5
MOBA_BLOCK = 256
MOBA_TOPK = 3
NSA_KV_HEADS = 4
NSA_GROUP = N_HEADS // NSA_KV_HEADS
CMP_LEN = 32
CMP_STRIDE = 16
CMP_HIDDEN = 256
SLC_BLOCK = 64
SLC_TOPN = 16
SLC_LOCAL = 2
WINDOW = 512
REL_BUCKETS = 32
REL_MAX_DIST = 128
N_EXPERTS = 32
N_GROUPS = 8
EXPERTS_PER_GROUP = N_EXPERTS // N_GROUPS
TOP_K = 2
D_EXPERT = 512
MOE_BLOCK = 128
N_MOBA_LAYERS = (DEPTH + 1) // 2
N_NSA_LAYERS = DEPTH // 2
DEEPNORM_ALPHA = (2 * DEPTH) ** 0.25
DEEPNORM_BETA = (8 * DEPTH) ** -0.25

kernel_name = 'moba_nsa_deepnorm_grouped_moe'


def layer_norm(x, g, b):
    xf = x.astype(jnp.float32)
    mu = jnp.mean(xf, axis=-1, keepdims=True)
    var = jnp.mean(jnp.square(xf - mu), axis=-1, keepdims=True)
    y = (xf - mu) * lax.rsqrt(var + LN_EPS)
    return (y * g.astype(jnp.float32) + b.astype(jnp.float32)).astype(x.dtype)


def t5_bucket(rel):
    n = jnp.maximum(rel, 0)
    max_exact = REL_BUCKETS // 2
    nf = jnp.maximum(n, 1).astype(jnp.float32)
    large = max_exact + (jnp.log(nf / max_exact) / math.log(REL_MAX_DIST / max_exact)
                         * (REL_BUCKETS - max_exact)).astype(jnp.int32)
    large = jnp.minimum(large, REL_BUCKETS - 1)
    return jnp.where(n < max_exact, n, large)


def masked_softmax(s, valid):
    return jax.nn.softmax(jnp.where(valid, s, NEG_INF), axis=-1)


def gather_blocks(blocks, idx):
    return jax.vmap(lambda bl, ix: bl[ix])(blocks, idx)


def moba_attention(x, w_in, w_out, rel_bias):
    B, S, _ = x.shape
    H, d, BLK = N_HEADS, HEAD_DIM, MOBA_BLOCK
    scale = d ** -0.5
    proj = x @ w_in
    q, k, v = jnp.split(proj, 3, axis=-1)
    to_heads = lambda t: t.reshape(B, S, H, d).transpose(0, 2, 1, 3)
    q, k, v = to_heads(q), to_heads(k), to_heads(v)
    S_pad = -(-S // BLK) * BLK
    padw = ((0, 0), (0, 0), (0, S_pad - S), (0, 0))
    q, k, v = jnp.pad(q, padw), jnp.pad(k, padw), jnp.pad(v, padw)
    nb = S_pad // BLK
    topk = min(MOBA_TOPK, nb)
    tbl = rel_bias.T

    def one_seq(args):
        qs, ks, vs = args
        kb = ks.reshape(H, nb, BLK, d)
        vb = vs.reshape(H, nb, BLK, d)
        kmean = jnp.mean(kb.astype(jnp.float32), axis=2).astype(ks.dtype)

        def one_chunk(c):
            start = c * Q_CHUNK
            qc = lax.dynamic_slice_in_dim(qs, start, Q_CHUNK, axis=1)
            qpos = start + jnp.arange(Q_CHUNK)
            qblk = start // BLK
            gate = jnp.einsum('hqd,hnd->hqn', qc, kmean).astype(jnp.float32)
            gate = jnp.where(jnp.arange(nb) < qblk, gate, -jnp.inf)
            _, idx = lax.top_k(gate, topk)
            sel_valid = idx < qblk
            kg = gather_blocks(kb, idx)
            vg = gather_blocks(vb, idx)
            kpos_sel = idx[..., None] * BLK + jnp.arange(BLK)
            s_sel = jnp.einsum('hqd,hqnkd->hqnk', qc, kg).reshape(H, Q_CHUNK, topk * BLK)
            own_start = qblk * BLK
            ko = lax.dynamic_slice_in_dim(ks, own_start, BLK, axis=1)
            vo = lax.dynamic_slice_in_dim(vs, own_start, BLK, axis=1)
            kpos_own = own_start + jnp.arange(BLK)
            s_own = jnp.einsum('hqd,hkd->hqk', qc, ko)
            scores = jnp.concatenate([s_sel, s_own], axis=-1).astype(jnp.float32) * scale
            kpos = jnp.concatenate([kpos_sel.reshape(H, Q_CHUNK, topk * BLK),
                                    jnp.broadcast_to(kpos_own, (H, Q_CHUNK, BLK))], axis=-1)
            valid_sel = jnp.broadcast_to(sel_valid[..., None], (H, Q_CHUNK, topk, BLK)).reshape(H, Q_CHUNK, topk * BLK)
            valid_own = jnp.broadcast_to(kpos_own[None, :] <= qpos[:, None], (H, Q_CHUNK, BLK))
            valid = jnp.concatenate([valid_sel, valid_own], axis=-1)
            rel = qpos[None, :, None] - kpos
            bias = jax.vmap(lambda t, bk: t[bk])(tbl, t5_bucket(rel)).astype(jnp.float32)
            p = masked_softmax(scores + bias, valid).astype(vs.dtype)
            o = (jnp.einsum('hqnk,hqnkd->hqd', p[..., :topk * BLK].reshape(H, Q_CHUNK, topk, BLK), vg)
                 + jnp.einsum('hqk,hkd->hqd', p[..., topk * BLK:], vo))
            return o

        outs = lax.map(one_chunk, jnp.arange(S_pad // Q_CHUNK))
        return outs.transpose(1, 0, 2, 3).reshape(H, S_pad, d)

    o = lax.map(one_seq, (q, k, v))
    o = o[:, :, :S].transpose(0, 2, 1, 3).reshape(B, S, H * d)
    return o @ w_out


def nsa_attention(x, w_in, w_out, pos_k, pos_v, ck_w1, ck_w2, cv_w1, cv_w2, rel_bias):
    B, S, _ = x.shape
    H, Hkv, G, d = N_HEADS, NSA_KV_HEADS, NSA_GROUP, HEAD_DIM
    scale = d ** -0.5
    proj = x @ w_in
    sizes = [H * d] + [Hkv * d] * 6 + [3 * H]
    q, kc, vc, ks, vs, kw, vw, g = jnp.split(proj, np.cumsum(sizes)[:-1].tolist(), axis=-1)
    q = q.reshape(B, S, Hkv, G, d).transpose(0, 2, 3, 1, 4)
    kv_heads = lambda t: t.reshape(B, S, Hkv, d)
    kc, vc, ks, vs, kw, vw = [kv_heads(t) for t in (kc, vc, ks, vs, kw, vw)]
    g = jax.nn.sigmoid(g).reshape(B, S, Hkv, G, 3).transpose(0, 2, 3, 1, 4)

    n_cmp = (S - CMP_LEN) // CMP_STRIDE + 1
    cidx = np.arange(n_cmp)[:, None] * CMP_STRIDE + np.arange(CMP_LEN)[None, :]

    def compress(t, pos, w1, w2):
        blk = t[:, cidx] + pos[None, None, :, None, :]
        blk = blk.transpose(0, 1, 3, 2, 4).reshape(B, n_cmp, Hkv, CMP_LEN * d)
        return (jax.nn.gelu(blk @ w1) @ w2).transpose(0, 2, 1, 3)

    kcmp = compress(kc, pos_k, ck_w1, ck_w2)
    vcmp = compress(vc, pos_v, cv_w1, cv_w2)
    to_hm = lambda t: t.transpose(0, 2, 1, 3)
    ks, vs = to_hm(ks), to_hm(vs)
    padw = ((0, 0), (0, 0), (WINDOW, 0), (0, 0))
    kw_pad, vw_pad = jnp.pad(to_hm(kw), padw), jnp.pad(to_hm(vw), padw)

    n_slc = S // SLC_BLOCK
    topn = min(SLC_TOPN, n_slc)
    ci = np.arange(n_cmp)[:, None] * CMP_STRIDE
    sj = np.arange(n_slc)[None, :] * SLC_BLOCK
    cmp_to_slc = jnp.asarray(((ci < sj + SLC_BLOCK) & (ci + CMP_LEN > sj)).astype(np.float32))
    cmp_end = jnp.arange(n_cmp) * CMP_STRIDE + CMP_LEN - 1
    tbl = rel_bias.T.reshape(Hkv, G, REL_BUCKETS)

    def one_seq(args):
        qs, kcm, vcm, kss, vss, kwp, vwp, gs = args
        ksb = kss.reshape(Hkv, n_slc, SLC_BLOCK, d)
        vsb = vss.reshape(Hkv, n_slc, SLC_BLOCK, d)

        def one_chunk(c):
            start = c * Q_CHUNK
            qc = lax.dynamic_slice_in_dim(qs, start, Q_CHUNK, axis=2)
            qpos = start + jnp.arange(Q_CHUNK)
            s_c = jnp.einsum('hgqd,hnd->hgqn', qc, kcm).astype(jnp.float32) * scale
            v_c = cmp_end[None, :] <= qpos[:, None]
            p_c = masked_softmax(s_c, v_c) * v_c
            o_c = jnp.einsum('hgqn,hnd->hgqd', p_c.astype(vcm.dtype), vcm)
            imp = jnp.einsum('hgqn,nj->hqj', p_c, cmp_to_slc)
            qblk = qpos // SLC_BLOCK
            j = jnp.arange(n_slc)[None, :]
            qb = qblk[:, None]
            forced = (j == 0) | ((j <= qb) & (j > qb - SLC_LOCAL))
            future = j > qb
            imp = jnp.where(forced[None], jnp.inf, jnp.where(future[None], -jnp.inf, imp))
            _, idx = lax.top_k(imp, topn)
            sel_valid = idx <= qblk[None, :, None]
            kg = gather_blocks(ksb, idx)
            vg = gather_blocks(vsb, idx)
            kpos_s = idx[..., None] * SLC_BLOCK + jnp.arange(SLC_BLOCK)
            rel_s = qpos[None, :, None, None] - kpos_s
            valid_s = sel_valid[..., None] & (rel_s >= 0)
            s_s = jnp.einsum('hgqd,hqnkd->hgqnk', qc, kg).astype(jnp.float32) * scale
            bias_s = jax.vmap(lambda t, bk: t[:, bk])(tbl, t5_bucket(rel_s)).astype(jnp.float32)
            s_s = jnp.where(valid_s[:, None], s_s + bias_s, NEG_INF).reshape(Hkv, G, Q_CHUNK, topn * SLC_BLOCK)
            p_s = jax.nn.softmax(s_s, axis=-1).reshape(Hkv, G, Q_CHUNK, topn, SLC_BLOCK)
            o_s = jnp.einsum('hgqnk,hqnkd->hgqd', p_s.astype(vg.dtype), vg)
            kwin = lax.dynamic_slice_in_dim(kwp, start, WINDOW + Q_CHUNK, axis=1)
            vwin = lax.dynamic_slice_in_dim(vwp, start, WINDOW + Q_CHUNK, axis=1)
            kpos_w = start - WINDOW + jnp.arange(WINDOW + Q_CHUNK)
            rel_w = qpos[:, None] - kpos_w[None, :]
            valid_w = (rel_w >= 0) & (rel_w < WINDOW) & (kpos_w[None, :] >= 0)
            s_w = jnp.einsum('hgqd,hkd->hgqk', qc, kwin).astype(jnp.float32) * scale
            bias_w = tbl[:, :, t5_bucket(rel_w)].astype(jnp.float32)
            p_w = masked_softmax(s_w + bias_w, valid_w)
            o_w = jnp.einsum('hgqk,hkd->hgqd', p_w.astype(vwin.dtype), vwin)
            gc = lax.dynamic_slice_in_dim(gs, start, Q_CHUNK, axis=2)
            return gc[..., 0:1] * o_c + gc[..., 1:2] * o_s + gc[..., 2:3] * o_w

        outs = lax.map(one_chunk, jnp.arange(S // Q_CHUNK))
        return outs.transpose(1, 2, 0, 3, 4).reshape(Hkv, G, S, d)

    o = lax.map(one_seq, (q, kcmp, vcmp, ks, vs, kw_pad, vw_pad, g))
    o = o.transpose(0, 3, 1, 2, 4).reshape(B, S, H * d)
    return o @ w_out


def grouped_moe(x, router_w, router_b, w_gate, w_up, w_down):
    B, S, D = x.shape
    N = B * S
    h = x.reshape(N, D)
    probs = jax.nn.softmax((h @ router_w).astype(jnp.float32) + router_b.astype(jnp.float32), axis=-1)
    grp = probs.reshape(N, N_GROUPS, EXPERTS_PER_GROUP)
    grp_score = lax.top_k(grp, TOP_K)[0].sum(-1)
    best = jnp.argmax(grp_score, axis=-1)
    in_grp = jnp.take_along_axis(grp, best[:, None, None], axis=1)[:, 0]
    wts, local = lax.top_k(in_grp, TOP_K)
    wts = wts / jnp.sum(wts, axis=-1, keepdims=True)
    expert = best[:, None] * EXPERTS_PER_GROUP + local
    A = N * TOP_K
    e_flat = expert.reshape(A)
    tok = jnp.repeat(jnp.arange(N), TOP_K)
    w_flat = wts.reshape(A)
    order = jnp.argsort(e_flat)
    e_s, tok_s, w_s = e_flat[order], tok[order], w_flat[order]
    counts = jnp.bincount(e_flat, length=N_EXPERTS)
    starts = jnp.cumsum(counts) - counts
    padded = (counts + MOE_BLOCK - 1) // MOE_BLOCK * MOE_BLOCK
    pends = jnp.cumsum(padded)
    pstarts = pends - padded
    dest = pstarts[e_s] + (jnp.arange(A) - starts[e_s])
    R = A + N_EXPERTS * MOE_BLOCK
    n_blk = R // MOE_BLOCK
    row_tok = jnp.zeros((R,), jnp.int32).at[dest].set(tok_s)
    row_w = jnp.zeros((R,), jnp.float32).at[dest].set(w_s)
    blk_e = jnp.minimum(jnp.searchsorted(pends, jnp.arange(n_blk) * MOE_BLOCK, side='right'), N_EXPERTS - 1)
    xb = h[row_tok].reshape(n_blk, MOE_BLOCK, D)

    def expert_block(args):
        xt, e = args
        hid = jax.nn.silu(xt @ w_gate[e]) * (xt @ w_up[e])
        return hid @ w_down[e]

    yb = lax.map(expert_block, (xb, blk_e)).reshape(R, D)
    out = jnp.zeros((N, D), x.dtype).at[row_tok].add(yb * row_w[:, None].astype(yb.dtype))
    return out.reshape(B, S, D)


def setup_inputs(seed: int = 0) -> dict:
    key = jax.random.key(seed)
    ks = jax.random.split(key, 20)
    nrm = lambda k, shape, s: jax.random.normal(k, shape, jnp.float32) * s
    HD = N_HEADS * HEAD_DIM
    d_in_nsa = HD + 6 * NSA_KV_HEADS * HEAD_DIM + 3 * N_HEADS
    return dict(
        x=nrm(ks[0], (BATCH, SEQ, D_MODEL), 1.0),
        rel_bias=nrm(ks[1], (REL_BUCKETS, N_HEADS), 0.5),
        router_w=nrm(ks[2], (D_MODEL, N_EXPERTS), D_MODEL ** -0.5),
        router_b=nrm(ks[3], (N_EXPERTS,), 0.01),
        ln_g=1.0 + nrm(ks[4], (DEPTH, 2, D_MODEL), 0.05),
        ln_b=nrm(ks[5], (DEPTH, 2, D_MODEL), 0.05),
        moba_w_in=nrm(ks[6], (N_MOBA_LAYERS, D_MODEL, 3 * HD), D_MODEL ** -0.5),
        moba_w_out=nrm(ks[7], (N_MOBA_LAYERS, HD, D_MODEL), DEEPNORM_BETA * HD ** -0.5),
        nsa_w_in=nrm(ks[8], (N_NSA_LAYERS, D_MODEL, d_in_nsa), D_MODEL ** -0.5),
        nsa_w_out=nrm(ks[9], (N_NSA_LAYERS, HD, D_MODEL), DEEPNORM_BETA * HD ** -0.5),
        nsa_pos_k=nrm(ks[10], (N_NSA_LAYERS, CMP_LEN, HEAD_DIM), 0.1),
        nsa_pos_v=nrm(ks[11], (N_NSA_LAYERS, CMP_LEN, HEAD_DIM), 0.1),
        nsa_ck_w1=nrm(ks[12], (N_NSA_LAYERS, CMP_LEN * HEAD_DIM, CMP_HIDDEN), (CMP_LEN * HEAD_DIM) ** -0.5),
        nsa_ck_w2=nrm(ks[13], (N_NSA_LAYERS, CMP_HIDDEN, HEAD_DIM), CMP_HIDDEN ** -0.5),
        nsa_cv_w1=nrm(ks[14], (N_NSA_LAYERS, CMP_LEN * HEAD_DIM, CMP_HIDDEN), (CMP_LEN * HEAD_DIM) ** -0.5),
        nsa_cv_w2=nrm(ks[15], (N_NSA_LAYERS, CMP_HIDDEN, HEAD_DIM), CMP_HIDDEN ** -0.5),
        moe_w_gate=nrm(ks[16], (DEPTH, N_EXPERTS, D_MODEL, D_EXPERT), D_MODEL ** -0.5),
        moe_w_up=nrm(ks[17], (DEPTH, N_EXPERTS, D_MODEL, D_EXPERT), D_MODEL ** -0.5),
        moe_w_down=nrm(ks[18], (DEPTH, N_EXPERTS, D_EXPERT, D_MODEL), DEEPNORM_BETA * D_EXPERT ** -0.5),
    )


def reference(x, rel_bias, router_w, router_b, ln_g, ln_b, moba_w_in, moba_w_out,
              nsa_w_in, nsa_w_out, nsa_pos_k, nsa_pos_v, nsa_ck_w1, nsa_ck_w2,
              nsa_cv_w1, nsa_cv_w2, moe_w_gate, moe_w_up, moe_w_down):
    for i in range(DEPTH):
        j = i // 2
        if i % 2 == 0:
            mix = moba_attention(x, moba_w_in[j], moba_w_out[j], rel_bias)
        else:
            mix = nsa_attention(x, nsa_w_in[j], nsa_w_out[j], nsa_pos_k[j], nsa_pos_v[j],
                                nsa_ck_w1[j], nsa_ck_w2[j], nsa_cv_w1[j], nsa_cv_w2[j], rel_bias)
        x = layer_norm(DEEPNORM_ALPHA * x + mix, ln_g[i, 0], ln_b[i, 0])
        ffn = grouped_moe(x, router_w, router_b, moe_w_gate[i], moe_w_up[i], moe_w_down[i])
        x = layer_norm(DEEPNORM_ALPHA * x + ffn, ln_g[i, 1], ln_b[i, 1])
    return x
```

```python
import math
from functools import partial

import numpy as np
import jax
import jax.numpy as jnp
from jax import lax
from jax.experimental import pallas as pl
from jax.experimental.pallas import tpu as pltpu

F32, BF16, I32 = jnp.float32, jnp.bfloat16, jnp.int32

D_MODEL = 1024
N_HEADS = 16
HEAD_DIM = 64
DEPTH = 2
NEG_INF = -1e30
LN_EPS = 1e-5
MOBA_BLOCK = 256
MOBA_TOPK = 3
NSA_KV_HEADS = 4
NSA_GROUP = N_HEADS // NSA_KV_HEADS
CMP_LEN = 32
CMP_STRIDE = 16
CMP_HIDDEN = 256
SLC_BLOCK = 64
SLC_TOPN = 16
SLC_LOCAL = 2
WINDOW = 512
REL_BUCKETS = 32
REL_MAX_DIST = 128
N_EXPERTS = 32
N_GROUPS = 8
EXPERTS_PER_GROUP = N_EXPERTS // N_GROUPS
D_EXPERT = 512
DEEPNORM_ALPHA = (2 * DEPTH) ** 0.25
SCALE = HEAD_DIM ** -0.5

LANES = 128
TILE = 256
MM_TM = 512
MM_TN = 512
MOE_TB = 256
VMEM_LIMIT = 48 * 1024 * 1024

_NT = (((1,), (1,)), ((), ()))


def _cparams(*sem):
    return pltpu.CompilerParams(dimension_semantics=sem, vmem_limit_bytes=VMEM_LIMIT)


def _mm_body(a_ref, b_ref, o_ref):
    o_ref[...] = jnp.dot(a_ref[...].astype(BF16), b_ref[...],
                         preferred_element_type=F32).astype(o_ref.dtype)


def _matmul(a, b, out_dtype):
    M, K = a.shape
    N = b.shape[1]
    tn = min(MM_TN, N)
    return pl.pallas_call(
        _mm_body,
        grid=(M // MM_TM, N // tn),
        in_specs=[pl.BlockSpec((MM_TM, K), lambda i, j: (i, 0)),
                  pl.BlockSpec((K, tn), lambda i, j: (0, j))],
        out_specs=pl.BlockSpec((MM_TM, tn), lambda i, j: (i, j)),
        out_shape=jax.ShapeDtypeStruct((M, N), out_dtype),
        compiler_params=_cparams("parallel", "arbitrary"),
        name="in_proj",
    )(a, b)


def _layer_norm_rows(z, g, b):
    mu = jnp.mean(z, axis=-1, keepdims=True)
    zc = z - mu
    var = jnp.mean(zc * zc, axis=-1, keepdims=True)
    return zc * lax.rsqrt(var + LN_EPS) * g + b


def _proj_ln_body(a_ref, w_ref, x_ref, g_ref, b_ref, o_ref, ob_ref):
    y = jnp.dot(a_ref[...], w_ref[...], preferred_element_type=F32)
    out = _layer_norm_rows(DEEPNORM_ALPHA * x_ref[...] + y, g_ref[...], b_ref[...])
    o_ref[...] = out
    ob_ref[...] = out.astype(BF16)


def _proj_ln(a, w, x, g, b):
    M, K = a.shape
    D = w.shape[1]
    row = pl.BlockSpec((MM_TM, D), lambda i: (i, 0))
    vec = pl.BlockSpec((1, D), lambda i: (0, 0))
    return pl.pallas_call(
        _proj_ln_body,
        grid=(M // MM_TM,),
        in_specs=[pl.BlockSpec((MM_TM, K), lambda i: (i, 0)),
                  pl.BlockSpec((K, D), lambda i: (0, 0)), row, vec, vec],
        out_specs=[row, row],
        out_shape=[jax.ShapeDtypeStruct((M, D), F32), jax.ShapeDtypeStruct((M, D), BF16)],
        compiler_params=_cparams("parallel"),
        name="out_proj_ln",
    )(a, w, x, g.reshape(1, D), b.reshape(1, D))


def _combine_ln_body(x_ref, y0_ref, y1_ref, g_ref, b_ref, o_ref, ob_ref):
    ffn = y0_ref[...] + y1_ref[...]
    out = _layer_norm_rows(DEEPNORM_ALPHA * x_ref[...] + ffn, g_ref[...], b_ref[...])
    o_ref[...] = out
    ob_ref[...] = out.astype(BF16)


def _combine_ln(x, y0, y1, g, b):
    M, D = x.shape
    row = pl.BlockSpec((MM_TM, D), lambda i: (i, 0))
    vec = pl.BlockSpec((1, D), lambda i: (0, 0))
    return pl.pallas_call(
        _combine_ln_body,
        grid=(M // MM_TM,),
        in_specs=[row, row, row, vec, vec],
        out_specs=[row, row],
        out_shape=[jax.ShapeDtypeStruct((M, D), F32), jax.ShapeDtypeStruct((M, D), BF16)],
        compiler_params=_cparams("parallel"),
        name="moe_combine_ln",
    )(x, y0, y1, g.reshape(1, D), b.reshape(1, D))


def _t5_bucket_np(rel):
    n = np.maximum(rel, 0)
    max_exact = REL_BUCKETS // 2
    nf = np.maximum(n, 1).astype(np.float32)
    large = max_exact + (np.log(nf / np.float32(max_exact))
                         / np.float32(math.log(REL_MAX_DIST / max_exact))
                         * np.float32(REL_BUCKETS - max_exact)).astype(np.int32)
    large = np.minimum(large, REL_BUCKETS - 1)
    return np.where(n < max_exact, n, large).astype(np.int32)


def _bias_body(tbl_ref, bk_ref, o_ref):
    h = pl.program_id(0)
    for dl in range(2):
        bk = bk_ref[dl]
        acc = jnp.zeros((TILE, TILE), F32)
        for b in range(REL_BUCKETS):
            acc = jnp.where(bk == b, tbl_ref[h * REL_BUCKETS + b], acc)
        o_ref[dl, 0] = acc


def _bias_tiles(rel_bias):
    r = np.arange(TILE)[:, None]
    c = np.arange(TILE)[None, :]
    assert int(_t5_bucket_np(np.array(TILE + 1))) == REL_BUCKETS - 1
    bk = np.stack([_t5_bucket_np(r - c), _t5_bucket_np(TILE + r - c)])
    return pl.pallas_call(
        _bias_body,
        grid=(N_HEADS,),
        in_specs=[pl.BlockSpec(memory_space=pltpu.SMEM),
                  pl.BlockSpec((2, TILE, TILE), lambda h: (0, 0, 0))],
        out_specs=pl.BlockSpec((2, 1, TILE, TILE), lambda h: (0, h, 0, 0)),
        out_shape=jax.ShapeDtypeStruct((2, N_HEADS, TILE, TILE), F32),
        name="t5_bias_tiles",
    )(rel_bias.T.reshape(-1), jnp.asarray(bk))


def _init_state(m_ref, l_ref, acc_ref):
    m_ref[...] = jnp.full(m_ref.shape, NEG_INF, F32)
    l_ref[...] = jnp.zeros(l_ref.shape, F32)
    acc_ref[...] = jnp.zeros(acc_ref.shape, F32)


def _attend(q, k, v, add, m_ref, l_ref, acc_ref):
    s = lax.dot_general(q, k, _NT, preferred_element_type=F32) + add
    m_prev = m_ref[...]
    m_new = jnp.maximum(m_prev, jnp.max(s, axis=-1, keepdims=True))
    a = jnp.exp(m_prev - m_new)
    p = jnp.exp(s - m_new)
    l_ref[...] = a * l_ref[...] + jnp.sum(p, axis=-1, keepdims=True)
    acc_ref[...] = a * acc_ref[...] + jnp.dot(p.astype(BF16), v, preferred_element_type=F32)
    m_ref[...] = m_new


def _rank_before(vals, rows):
    idx = lax.broadcasted_iota(I32, vals.shape, 0)
    rank = jnp.zeros(vals.shape, I32)
    for m in range(rows):
        row = vals[m:m + 1, :]
        beats = (row > vals) | ((row == vals) & (idx > m))
        rank = rank + jnp.where(beats, 1, 0)
    return rank


def _moba_body(q_ref, k_ref, v_ref, bias_ref, o_ref, kmean_ref, neg_ref, m_ref, l_ref, acc_ref):
    i = pl.program_id(2)
    nb = k_ref.shape[1] // TILE

    @pl.when(i == 0)
    def _():
        for n in range(nb):
            kb = k_ref[0, n * TILE:(n + 1) * TILE, :].astype(F32)
            kmean_ref[n:n + 1, :] = jnp.sum(kb, axis=0, keepdims=True) * (1.0 / TILE)

    lane = lax.broadcasted_iota(I32, (TILE, LANES), 1)
    q = q_ref[0] * SCALE
    zero = jnp.zeros_like(q)
    q2 = jnp.concatenate([jnp.where(lane < HEAD_DIM, q, zero),
                          jnp.where(lane >= HEAD_DIM, q, zero)], axis=0)

    km = kmean_ref[...]
    k_hi = km.astype(BF16)
    k_lo = (km - k_hi.astype(F32)).astype(BF16)
    gate = (lax.dot_general(k_hi, q2, _NT, preferred_element_type=F32)
            + lax.dot_general(k_lo, q2, _NT, preferred_element_type=F32))
    blk = lax.broadcasted_iota(I32, gate.shape, 0)
    gate = jnp.where(blk < i, gate, -jnp.inf)
    rank = _rank_before(gate, nb)
    sel_t = jnp.where((rank < MOBA_TOPK) & (blk < i), 1.0, 0.0).astype(F32)
    sel = sel_t.T
    for n in range(nb):
        neg_ref[n] = (sel[:, n:n + 1] - 1.0) * (-NEG_INF)

    _init_state(m_ref, l_ref, acc_ref)
    row = lax.broadcasted_iota(I32, (2 * TILE, TILE), 0) & (TILE - 1)
    col = lax.broadcasted_iota(I32, (2 * TILE, TILE), 1)
    own = pl.multiple_of(i * TILE, TILE)
    _attend(q2, k_ref[0, pl.ds(own, TILE), :], v_ref[0, pl.ds(own, TILE), :],
            bias_ref[0, 0] + jnp.where(col <= row, 0.0, NEG_INF), m_ref, l_ref, acc_ref)

    @pl.when(i >= 1)
    def _():
        off = pl.multiple_of((i - 1) * TILE, TILE)
        _attend(q2, k_ref[0, pl.ds(off, TILE), :], v_ref[0, pl.ds(off, TILE), :],
                bias_ref[1, 0] + neg_ref[i - 1], m_ref, l_ref, acc_ref)

    far_bias = bias_ref[1, 0, :, 0:1]

    def far(n, carry):
        off = pl.multiple_of(n * TILE, TILE)
        _attend(q2, k_ref[0, pl.ds(off, TILE), :], v_ref[0, pl.ds(off, TILE), :],
                far_bias + neg_ref[n], m_ref, l_ref, acc_ref)
        return carry

    lax.fori_loop(0, jnp.maximum(i - 1, 0), far, 0)

    o = acc_ref[...] / l_ref[...]
    o_ref[0] = jnp.where(lane < HEAD_DIM, o[:TILE], o[TILE:]).astype(o_ref.dtype)


def _moba_attention(proj, bias):
    B, S, _ = proj.shape
    n_pairs = N_HEADS // 2
    nq = S // TILE
    bias2 = bias.reshape(2, n_pairs, 2 * TILE, TILE)
    return pl.pallas_call(
        _moba_body,
        grid=(B, n_pairs, nq),
        in_specs=[pl.BlockSpec((1, TILE, LANES), lambda b, p, i: (b, i, p)),
                  pl.BlockSpec((1, S, LANES), lambda b, p, i: (b, 0, n_pairs + p)),
                  pl.BlockSpec((1, S, LANES), lambda b, p, i: (b, 0, 2 * n_pairs + p)),
                  pl.BlockSpec((2, 1, 2 * TILE, TILE), lambda b, p, i: (0, p, 0, 0))],
        out_specs=pl.BlockSpec((1, TILE, LANES), lambda b, p, i: (b, i, p)),
        out_shape=jax.ShapeDtypeStruct((B, S, N_HEADS * HEAD_DIM), BF16),
        scratch_shapes=[pltpu.VMEM((S // TILE, LANES), F32),
                        pltpu.VMEM((S // TILE, 2 * TILE, 1), F32),
                        pltpu.VMEM((2 * TILE, 1), F32),
                        pltpu.VMEM((2 * TILE, 1), F32),
                        pltpu.VMEM((2 * TILE, LANES), F32)],
        compiler_params=_cparams("parallel", "parallel", "arbitrary"),
        name="moba_attention",
    )(proj, proj, proj, bias2)


def _gelu_tanh(x):
    return 0.5 * x * (1.0 + jnp.tanh(math.sqrt(2.0 / math.pi) * (x + 0.044715 * (x * x * x))))


def _compress_body(t_ref, pos_ref, w1_ref, w2_ref, o_ref):
    groups = t_ref.shape[2]
    half = t_ref.shape[3]
    t = t_ref[0].reshape(NSA_KV_HEADS * groups, half).astype(F32)
    first = jnp.dot((t + pos_ref[0:1, :]).astype(BF16), w1_ref[0:half, :], preferred_element_type=F32)
    second = jnp.dot((t + pos_ref[1:2, :]).astype(BF16), w1_ref[half:2 * half, :],
                     preferred_element_type=F32)
    rows = first.shape[0]
    pre = first + pltpu.roll(second, rows - 1, 0)
    out = jnp.dot(_gelu_tanh(pre).astype(BF16), w2_ref[...], preferred_element_type=F32)
    for h in range(NSA_KV_HEADS):
        o_ref[0, h] = out[h * groups:(h + 1) * groups].astype(o_ref.dtype)


def _compress(t, pos, w1, w2):
    B, Hkv, groups, half = t.shape
    return pl.pallas_call(
        _compress_body,
        grid=(B,),
        in_specs=[pl.BlockSpec((1, Hkv, groups, half), lambda b: (b, 0, 0, 0)),
                  pl.BlockSpec((2, half), lambda b: (0, 0)),
                  pl.BlockSpec((2 * half, CMP_HIDDEN), lambda b: (0, 0)),
                  pl.BlockSpec((CMP_HIDDEN, HEAD_DIM), lambda b: (0, 0))],
        out_specs=pl.BlockSpec((1, Hkv, groups, HEAD_DIM), lambda b: (b, 0, 0, 0)),
        out_shape=jax.ShapeDtypeStruct((B, Hkv, groups, HEAD_DIM), BF16),
        compiler_params=_cparams("parallel"),
        name="nsa_compress",
    )(t, pos.reshape(2, half), w1.astype(BF16), w2.astype(BF16))


def _swap_halves(x):
    return jnp.concatenate([x[:, HEAD_DIM:], x[:, :HEAD_DIM]], axis=1)


def _tile4(x):
    return jnp.concatenate([x] * NSA_GROUP, axis=0)


def _nsa_body(q_ref, kc_ref, vc_ref, ks_ref, vs_ref, kw_ref, vw_ref, g_ref, bias_ref, c2s_ref,
              exp_ref, o_ref, m_ref, l_ref, acc_ref):
    i = pl.program_id(2)
    n_cmp = kc_ref.shape[1]
    n_slc = c2s_ref.shape[0]
    lane = lax.broadcasted_iota(I32, (TILE, LANES), 1)
    lo_half = lane < HEAD_DIM
    rows4 = NSA_GROUP * TILE
    qpos4 = i * TILE + (lax.broadcasted_iota(I32, (rows4, n_cmp), 0) & (TILE - 1))
    cmp_valid = CMP_STRIDE * lax.broadcasted_iota(I32, (rows4, n_cmp), 1) + (CMP_LEN - 1) <= qpos4
    row = lax.broadcasted_iota(I32, (TILE, TILE), 0)
    col = lax.broadcasted_iota(I32, (TILE, TILE), 1)
    gates = jax.nn.sigmoid(g_ref[0])
    own = pl.multiple_of(i * TILE, TILE)
    near = pl.multiple_of(jnp.maximum(i - 1, 0) * TILE, TILE)
    far2 = pl.multiple_of(jnp.maximum(i - 2, 0) * TILE, TILE)
    qall = q_ref[0] * SCALE

    for a in range(2):
        keep = lo_half if a == 0 else jnp.logical_not(lo_half)
        heads = []
        for g in range(NSA_GROUP):
            cb = a * 2 + g // 2
            x = qall[:, cb * LANES:(cb + 1) * LANES]
            if g % 2 != a:
                x = _swap_halves(x)
            heads.append(jnp.where(keep, x, jnp.zeros_like(x)))
        q4 = jnp.concatenate(heads, axis=0)

        s_c = lax.dot_general(q4, kc_ref[0], _NT, preferred_element_type=F32)
        s_c = jnp.where(cmp_valid, s_c, NEG_INF)
        m_c = jnp.max(s_c, axis=-1, keepdims=True)
        e_c = jnp.where(cmp_valid, jnp.exp(s_c - m_c), 0.0)
        l_c = jnp.sum(e_c, axis=-1, keepdims=True)
        p_c = e_c / jnp.where(l_c > 0.0, l_c, 1.0)
        o_c = jnp.dot(p_c.astype(BF16), vc_ref[0], preferred_element_type=F32)

        p_sum = p_c[0:TILE]
        for g in range(1, NSA_GROUP):
            p_sum = p_sum + p_c[g * TILE:(g + 1) * TILE]
        p_hi = p_sum.astype(BF16)
        p_lo = (p_sum - p_hi.astype(F32)).astype(BF16)
        imp = (lax.dot_general(c2s_ref[...], p_hi, _NT, preferred_element_type=F32)
               + lax.dot_general(c2s_ref[...], p_lo, _NT, preferred_element_type=F32))
        j = lax.broadcasted_iota(I32, imp.shape, 0)
        qb = (i * TILE + lax.broadcasted_iota(I32, imp.shape, 1)) >> int(math.log2(SLC_BLOCK))
        forced = (j == 0) | ((j <= qb) & (j > qb - SLC_LOCAL))
        imp = jnp.where(forced, jnp.inf, jnp.where(j > qb, -jnp.inf, imp))
        rank = _rank_before(imp, n_slc)
        sel_t = jnp.where((rank < SLC_TOPN) & (j <= qb), 1.0, 0.0).astype(F32)
        sel = sel_t.T.astype(BF16)

        def sel_neg(t):
            hit = jnp.dot(sel, exp_ref[t], preferred_element_type=F32)
            return (hit - 1.0) * (-NEG_INF)

        far_bias = bias_ref[1, a, :, 0:1]

        _init_state(m_ref, l_ref, acc_ref)
        diag_neg = jnp.where(col <= row, 0.0, NEG_INF)
        _attend(q4, ks_ref[0, pl.ds(own, TILE), :], vs_ref[0, pl.ds(own, TILE), :],
                bias_ref[0, a] + _tile4(sel_neg(i) + diag_neg), m_ref, l_ref, acc_ref)

        @pl.when(i >= 1)
        def _():
            _attend(q4, ks_ref[0, pl.ds(near, TILE), :], vs_ref[0, pl.ds(near, TILE), :],
                    bias_ref[1, a] + _tile4(sel_neg(i - 1)), m_ref, l_ref, acc_ref)

        def far(t, carry):
            off = pl.multiple_of(t * TILE, TILE)
            _attend(q4, ks_ref[0, pl.ds(off, TILE), :], vs_ref[0, pl.ds(off, TILE), :],
                    far_bias + _tile4(sel_neg(t)), m_ref, l_ref, acc_ref)
            return carry

        lax.fori_loop(0, jnp.maximum(i - 1, 0), far, 0)
        o_s = acc_ref[...] / l_ref[...]

        _init_state(m_ref, l_ref, acc_ref)
        _attend(q4, kw_ref[0, pl.ds(own, TILE), :], vw_ref[0, pl.ds(own, TILE), :],
                bias_ref[0, a] + _tile4(diag_neg), m_ref, l_ref, acc_ref)

        @pl.when(i >= 1)
        def _():
            _attend(q4, kw_ref[0, pl.ds(near, TILE), :], vw_ref[0, pl.ds(near, TILE), :],
                    bias_ref[1, a], m_ref, l_ref, acc_ref)

        @pl.when(i >= 2)
        def _():
            _attend(q4, kw_ref[0, pl.ds(far2, TILE), :], vw_ref[0, pl.ds(far2, TILE), :],
                    far_bias + _tile4(jnp.where(col > row, 0.0, NEG_INF)), m_ref, l_ref, acc_ref)

        o_w = acc_ref[...] / l_ref[...]

        outs = []
        for g in range(NSA_GROUP):
            c0 = 3 * (NSA_GROUP * a + g)
            rs = slice(g * TILE, (g + 1) * TILE)
            og = (gates[:, c0:c0 + 1] * o_c[rs] + gates[:, c0 + 1:c0 + 2] * o_s[rs]
                  + gates[:, c0 + 2:c0 + 3] * o_w[rs])
            if g % 2 != a:
                og = pltpu.roll(og, HEAD_DIM, 1)
            outs.append(og)
        for c in range(2):
            blk = jnp.where(lo_half, outs[2 * c], outs[2 * c + 1])
            cb = a * 2 + c
            o_ref[0, :, cb * LANES:(cb + 1) * LANES] = blk.astype(o_ref.dtype)


def _nsa_tables(S):
    n_cmp_pad = S // CMP_STRIDE
    n_slc = S // SLC_BLOCK
    ci = np.arange(n_cmp_pad)[:, None] * CMP_STRIDE
    sj = np.arange(n_slc)[None, :] * SLC_BLOCK
    c2s = ((ci < sj + SLC_BLOCK) & (ci + CMP_LEN > sj)).astype(np.float32)
    c2s[(S - CMP_LEN) // CMP_STRIDE + 1:] = 0.0
    expand = (np.arange(S)[None, :] // SLC_BLOCK == np.arange(n_slc)[:, None]).astype(np.float32)
    expand = expand.reshape(n_slc, S // TILE, TILE).transpose(1, 0, 2)
    return jnp.asarray(c2s.T, BF16), jnp.asarray(expand, BF16)


def _nsa_attention(proj, gate_logits, kcmp, vcmp, bias):
    B, S, _ = proj.shape
    nq = S // TILE
    n_cmp = kcmp.shape[1]
    n_slc = S // SLC_BLOCK
    c2s, expand = _nsa_tables(S)
    bias4 = bias.reshape(2, NSA_KV_HEADS, NSA_GROUP * TILE, TILE)
    qw = 2 * NSA_GROUP * HEAD_DIM
    q_blocks = N_HEADS * HEAD_DIM // LANES
    kv_blocks = NSA_KV_HEADS * HEAD_DIM // LANES

    def kv_spec(which):
        base = q_blocks + which * kv_blocks
        return pl.BlockSpec((1, S, LANES), lambda b, p, i: (b, 0, base + p))

    cmp_spec = pl.BlockSpec((1, n_cmp, LANES), lambda b, p, i: (b, 0, p))
    return pl.pallas_call(
        _nsa_body,
        grid=(B, 2, nq),
        in_specs=[pl.BlockSpec((1, TILE, qw), lambda b, p, i: (b, i, p)),
                  cmp_spec, cmp_spec,
                  kv_spec(2), kv_spec(3), kv_spec(4), kv_spec(5),
                  pl.BlockSpec((1, TILE, LANES), lambda b, p, i: (b, i, p)),
                  pl.BlockSpec((2, 2, NSA_GROUP * TILE, TILE), lambda b, p, i: (0, p, 0, 0)),
                  pl.BlockSpec((n_slc, n_cmp), lambda b, p, i: (0, 0)),
                  pl.BlockSpec((nq, n_slc, TILE), lambda b, p, i: (0, 0, 0))],
        out_specs=pl.BlockSpec((1, TILE, qw), lambda b, p, i: (b, i, p)),
        out_shape=jax.ShapeDtypeStruct((B, S, N_HEADS * HEAD_DIM), BF16),
        scratch_shapes=[pltpu.VMEM((NSA_GROUP * TILE, 1), F32),
                        pltpu.VMEM((NSA_GROUP * TILE, 1), F32),
                        pltpu.VMEM((NSA_GROUP * TILE, LANES), F32)],
        compiler_params=_cparams("parallel", "parallel", "arbitrary"),
        name="nsa_attention",
    )(proj, kcmp, vcmp, proj, proj, proj, proj, gate_logits, bias4, c2s, expand)


def _split_bf16(x):
    hi = x.astype(BF16)
    return hi, (x - hi.astype(F32)).astype(BF16)


def _router_body(x_ref, w_ref, b_ref, idx_ref, wt_ref):
    x_hi, x_lo = _split_bf16(x_ref[...])
    w_hi, w_lo = _split_bf16(w_ref[...])
    logits = (lax.dot_general(w_hi, x_hi, _NT, preferred_element_type=F32)
              + lax.dot_general(w_hi, x_lo, _NT, preferred_element_type=F32)
              + lax.dot_general(w_lo, x_hi, _NT, preferred_element_type=F32)) + b_ref[:, 0:1]
    m = jnp.max(logits, axis=0, keepdims=True)
    e = jnp.exp(logits - m)
    probs = e / jnp.sum(e, axis=0, keepdims=True)
    pk = [probs[k * N_GROUPS:(k + 1) * N_GROUPS] for k in range(EXPERTS_PER_GROUP)]
    hi1, lo1 = jnp.maximum(pk[0], pk[1]), jnp.minimum(pk[0], pk[1])
    hi2, lo2 = jnp.maximum(pk[2], pk[3]), jnp.minimum(pk[2], pk[3])
    score = jnp.maximum(hi1, hi2) + jnp.maximum(jnp.minimum(hi1, hi2), jnp.maximum(lo1, lo2))
    grp = lax.broadcasted_iota(I32, score.shape, 0)
    best = jnp.min(jnp.where(score == jnp.max(score, axis=0, keepdims=True), grp, N_GROUPS),
                   axis=0, keepdims=True)
    v = [jnp.sum(jnp.where(grp == best, p, 0.0), axis=0, keepdims=True) for p in pk]
    v1 = jnp.maximum(jnp.maximum(v[0], v[1]), jnp.maximum(v[2], v[3]))
    i1 = jnp.where(v[0] == v1, 0, jnp.where(v[1] == v1, 1, jnp.where(v[2] == v1, 2, 3)))
    rest = [jnp.where(i1 == k, -1.0, v[k]) for k in range(EXPERTS_PER_GROUP)]
    v2 = jnp.maximum(jnp.maximum(rest[0], rest[1]), jnp.maximum(rest[2], rest[3]))
    i2 = jnp.where(rest[0] == v2, 0, jnp.where(rest[1] == v2, 1, jnp.where(rest[2] == v2, 2, 3)))
    tot = v1 + v2
    idx_ref[...] = jnp.concatenate([best * EXPERTS_PER_GROUP + i1, best * EXPERTS_PER_GROUP + i2], axis=0)
    wt_ref[...] = jnp.concatenate([v1 / tot, v2 / tot], axis=0)


def _router(x, router_w, router_b):
    N, D = x.shape
    perm = np.arange(N_EXPERTS).reshape(N_GROUPS, EXPERTS_PER_GROUP).T.reshape(-1)
    w = router_w.T[perm]
    b = jnp.broadcast_to(router_b[perm][:, None], (N_EXPERTS, LANES))
    tm = 1024
    return pl.pallas_call(
        _router_body,
        grid=(N // tm,),
        in_specs=[pl.BlockSpec((tm, D), lambda i: (i, 0)),
                  pl.BlockSpec((N_EXPERTS, D), lambda i: (0, 0)),
                  pl.BlockSpec((N_EXPERTS, LANES), lambda i: (0, 0))],
        out_specs=[pl.BlockSpec((2, tm), lambda i: (0, i)), pl.BlockSpec((2, tm), lambda i: (0, i))],
        out_shape=[jax.ShapeDtypeStruct((2, N), I32), jax.ShapeDtypeStruct((2, N), F32)],
        compiler_params=_cparams("parallel"),
        name="moe_router",
    )(x, w, b)


def _expert_body(blk_e_ref, n_used_ref, x_ref, rw_ref, wg_ref, wu_ref, wd_ref, o_ref):
    i = pl.program_id(0)

    @pl.when(i < n_used_ref[0])
    def _():
        x = x_ref[...]
        gate = jnp.dot(x, wg_ref[0], preferred_element_type=F32)
        up = jnp.dot(x, wu_ref[0], preferred_element_type=F32)
        hid = (gate * jax.nn.sigmoid(gate) * up).astype(BF16)
        o_ref[...] = jnp.dot(hid, wd_ref[0], preferred_element_type=F32) * rw_ref[:, 0:1]

    @pl.when(i >= n_used_ref[0])
    def _():
        o_ref[...] = jnp.zeros(o_ref.shape, o_ref.dtype)


def _experts(xs, row_w, blk_e, n_used, wg, wu, wd):
    R, D = xs.shape
    n_blk = R // MOE_TB

    def live(i, be, nu):
        return jnp.minimum(i, nu[0] - 1)

    grid_spec = pltpu.PrefetchScalarGridSpec(
        num_scalar_prefetch=2,
        grid=(n_blk,),
        in_specs=[pl.BlockSpec((MOE_TB, D), lambda i, be, nu: (live(i, be, nu), 0)),
                  pl.BlockSpec((MOE_TB, LANES), lambda i, be, nu: (live(i, be, nu), 0)),
                  pl.BlockSpec((1, D, D_EXPERT), lambda i, be, nu: (be[i], 0, 0)),
                  pl.BlockSpec((1, D, D_EXPERT), lambda i, be, nu: (be[i], 0, 0)),
                  pl.BlockSpec((1, D_EXPERT, D), lambda i, be, nu: (be[i], 0, 0))],
        out_specs=pl.BlockSpec((MOE_TB, D), lambda i, be, nu: (i, 0)),
    )
    return pl.pallas_call(
        _expert_body,
        grid_spec=grid_spec,
        out_shape=jax.ShapeDtypeStruct((R, D), F32),
        compiler_params=_cparams("arbitrary"),
        name="moe_experts",
    )(blk_e, n_used, xs, row_w, wg, wu, wd)


def _moe_ln(h, hb, router_w, router_b, wg, wu, wd, g, b):
    N, D = h.shape
    A = 2 * N
    idx, wts = _router(h, router_w, router_b)
    e_flat = idx.T.reshape(A)
    w_flat = wts.T.reshape(A)
    tok = jnp.arange(A, dtype=I32) // 2
    order = jnp.argsort(e_flat)
    e_s = e_flat[order]
    counts = jnp.zeros((N_EXPERTS,), I32).at[e_flat].add(1)
    starts = jnp.cumsum(counts) - counts
    padded = (counts + MOE_TB - 1) // MOE_TB * MOE_TB
    pends = jnp.cumsum(padded)
    pstarts = pends - padded
    dest_s = pstarts[e_s] + (jnp.arange(A, dtype=I32) - starts[e_s])
    R = A + N_EXPERTS * MOE_TB
    n_blk = R // MOE_TB
    dest = jnp.zeros((A,), I32).at[order].set(dest_s)
    row_tok = jnp.zeros((R,), I32).at[dest].set(tok)
    row_w = jnp.zeros((R,), F32).at[dest].set(w_flat)
    blk_e = jnp.minimum(jnp.searchsorted(pends, jnp.arange(n_blk, dtype=I32) * MOE_TB, side='right'),
                        N_EXPERTS - 1).astype(I32)
    n_used = (pends[-1:] // MOE_TB).astype(I32)
    xs = hb[row_tok]
    yb = _experts(xs, jnp.broadcast_to(row_w[:, None], (R, LANES)), blk_e, n_used,
                  wg.astype(BF16), wu.astype(BF16), wd.astype(BF16))
    d2 = dest.reshape(N, 2)
    return _combine_ln(h, yb[d2[:, 0]], yb[d2[:, 1]], g, b)


def _moba_layer(h, w_in, w_out, bias, g, b, B, S):
    proj = _matmul(h, w_in.astype(BF16), BF16)
    att = _moba_attention(proj.reshape(B, S, -1), bias)
    return _proj_ln(att.reshape(B * S, -1), w_out.astype(BF16), h, g, b)


def _nsa_layer(h, hb, w_in, w_out, pos_k, pos_v, ck_w1, ck_w2, cv_w1, cv_w2, bias, g, b, B, S):
    HD = N_HEADS * HEAD_DIM
    kvw = NSA_KV_HEADS * HEAD_DIM
    main = HD + 6 * kvw
    proj = _matmul(hb, w_in[:, :main].astype(BF16), BF16).reshape(B, S, main)
    per_pair = 3 * N_HEADS // 2
    wg = w_in[:, main:].reshape(D_MODEL, 2, per_pair)
    wg = jnp.pad(wg, ((0, 0), (0, 0), (0, LANES - per_pair))).reshape(D_MODEL, 2 * LANES)
    gate_logits = _matmul(hb, wg.astype(BF16), F32).reshape(B, S, 2 * LANES)

    def grouped(t):
        t = t.reshape(B, S, NSA_KV_HEADS, HEAD_DIM).transpose(0, 2, 1, 3)
        return t.reshape(B, NSA_KV_HEADS, S // CMP_STRIDE, CMP_STRIDE * HEAD_DIM)

    def paired(t):
        return t.transpose(0, 2, 1, 3).reshape(B, t.shape[2], kvw)

    kcmp = paired(_compress(grouped(proj[..., HD:HD + kvw]), pos_k, ck_w1, ck_w2))
    vcmp = paired(_compress(grouped(proj[..., HD + kvw:HD + 2 * kvw]), pos_v, cv_w1, cv_w2))
    att = _nsa_attention(proj, gate_logits, kcmp, vcmp, bias)
    return _proj_ln(att.reshape(B * S, HD), w_out.astype(BF16), h, g, b)


def kernel(x, rel_bias, router_w, router_b, ln_g, ln_b, moba_w_in, moba_w_out, nsa_w_in, nsa_w_out,
           nsa_pos_k, nsa_pos_v, nsa_ck_w1, nsa_ck_w2, nsa_cv_w1, nsa_cv_w2,
           moe_w_gate, moe_w_up, moe_w_down):
    B, S, D = x.shape
    bias = _bias_tiles(rel_bias)
    h = x.reshape(B * S, D)
    h, hb = _moba_layer(h, moba_w_in[0], moba_w_out[0], bias, ln_g[0, 0], ln_b[0, 0], B, S)
    h, hb = _moe_ln(h, hb, router_w, router_b, moe_w_gate[0], moe_w_up[0], moe_w_down[0],
                    ln_g[0, 1], ln_b[0, 1])
    h, hb = _nsa_layer(h, hb, nsa_w_in[0], nsa_w_out[0], nsa_pos_k[0], nsa_pos_v[0],
                       nsa_ck_w1[0], nsa_ck_w2[0], nsa_cv_w1[0], nsa_cv_w2[0],
                       bias, ln_g[1, 0], ln_b[1, 0], B, S)
    h, hb = _moe_ln(h, hb, router_w, router_b, moe_w_gate[1], moe_w_up[1], moe_w_down[1],
                    ln_g[1, 1], ln_b[1, 1])
    return h.reshape(B, S, D)
```

```python
import math

import numpy as np
import jax
import jax.numpy as jnp
from jax import lax
from jax.experimental import pallas as pl
from jax.experimental.pallas import tpu as pltpu

F32, BF16, I32 = jnp.float32, jnp.bfloat16, jnp.int32

D_MODEL = 1024
N_HEADS = 16
HEAD_DIM = 64
DEPTH = 2
NEG_INF = -1e30
LN_EPS = 1e-5
MOBA_BLOCK = 256
MOBA_TOPK = 3
NSA_KV_HEADS = 4
NSA_GROUP = N_HEADS // NSA_KV_HEADS
CMP_LEN = 32
CMP_STRIDE = 16
CMP_HIDDEN = 256
SLC_BLOCK = 64
SLC_TOPN = 16
SLC_LOCAL = 2
WINDOW = 512
REL_BUCKETS = 32
REL_MAX_DIST = 128
N_EXPERTS = 32
N_GROUPS = 8
EXPERTS_PER_GROUP = N_EXPERTS // N_GROUPS
D_EXPERT = 512
DEEPNORM_ALPHA = (2 * DEPTH) ** 0.25
SCALE = HEAD_DIM ** -0.5

LANES = 128
TILE = 256
MM_TM = 512
MM_TN = 512
MOE_TB = 256
VMEM_LIMIT = 48 * 1024 * 1024

_NT = (((1,), (1,)), ((), ()))


def _cparams(*sem):
    return pltpu.CompilerParams(dimension_semantics=sem, vmem_limit_bytes=VMEM_LIMIT)


def _mm_body(a_ref, b_ref, o_ref):
    o_ref[...] = jnp.dot(a_ref[...].astype(BF16), b_ref[...],
                         preferred_element_type=F32).astype(o_ref.dtype)


def _matmul(a, b, out_dtype):
    M, K = a.shape
    N = b.shape[1]
    tn = min(MM_TN, N)
    return pl.pallas_call(
        _mm_body,
        grid=(M // MM_TM, N // tn),
        in_specs=[pl.BlockSpec((MM_TM, K), lambda i, j: (i, 0)),
                  pl.BlockSpec((K, tn), lambda i, j: (0, j))],
        out_specs=pl.BlockSpec((MM_TM, tn), lambda i, j: (i, j)),
        out_shape=jax.ShapeDtypeStruct((M, N), out_dtype),
        compiler_params=_cparams("parallel", "arbitrary"),
        name="in_proj",
    )(a, b)


def _mm_t_body(w_ref, a_ref, o_ref):
    r = lax.dot_general(w_ref[...], a_ref[...].astype(BF16), _NT, preferred_element_type=F32)
    for t in range(o_ref.shape[1]):
        o_ref[0, t] = r[:, t * TILE:(t + 1) * TILE].astype(o_ref.dtype)


def _matmul_t(w_t, a, B, S, out_dtype):
    Nout, K = w_t.shape
    tn = min(MM_TN, Nout)
    per_seq = S // MM_TM
    sub = MM_TM // TILE
    return pl.pallas_call(
        _mm_t_body,
        grid=(B * per_seq, Nout // tn),
        in_specs=[pl.BlockSpec((tn, K), lambda i, j: (j, 0)),
                  pl.BlockSpec((MM_TM, K), lambda i, j: (i, 0))],
        out_specs=pl.BlockSpec((1, sub, tn, TILE), lambda i, j: (i // per_seq, i % per_seq, j, 0)),
        out_shape=jax.ShapeDtypeStruct((B, S // TILE, Nout, TILE), out_dtype),
        compiler_params=_cparams("parallel", "arbitrary"),
        name="in_proj_t",
    )(w_t, a)


def _layer_norm_rows(z, g, b):
    mu = jnp.mean(z, axis=-1, keepdims=True)
    zc = z - mu
    var = jnp.mean(zc * zc, axis=-1, keepdims=True)
    return zc * lax.rsqrt(var + LN_EPS) * g + b


def _proj_ln_body(a_ref, w_ref, x_ref, g_ref, b_ref, o_ref, ob_ref):
    y = jnp.dot(a_ref[...], w_ref[...], preferred_element_type=F32)
    out = _layer_norm_rows(DEEPNORM_ALPHA * x_ref[...] + y, g_ref[...], b_ref[...])
    o_ref[...] = out
    ob_ref[...] = out.astype(BF16)


def _proj_ln(a, w, x, g, b):
    M, K = a.shape
    D = w.shape[1]
    row = pl.BlockSpec((MM_TM, D), lambda i: (i, 0))
    vec = pl.BlockSpec((1, D), lambda i: (0, 0))
    return pl.pallas_call(
        _proj_ln_body,
        grid=(M // MM_TM,),
        in_specs=[pl.BlockSpec((MM_TM, K), lambda i: (i, 0)),
                  pl.BlockSpec((K, D), lambda i: (0, 0)), row, vec, vec],
        out_specs=[row, row],
        out_shape=[jax.ShapeDtypeStruct((M, D), F32), jax.ShapeDtypeStruct((M, D), BF16)],
        compiler_params=_cparams("parallel"),
        name="out_proj_ln",
    )(a, w, x, g.reshape(1, D), b.reshape(1, D))


def _combine_ln_body(x_ref, y0_ref, y1_ref, g_ref, b_ref, o_ref, ob_ref):
    ffn = y0_ref[...] + y1_ref[...]
    out = _layer_norm_rows(DEEPNORM_ALPHA * x_ref[...] + ffn, g_ref[...], b_ref[...])
    o_ref[...] = out
    ob_ref[...] = out.astype(BF16)


def _combine_ln(x, y0, y1, g, b):
    M, D = x.shape
    row = pl.BlockSpec((MM_TM, D), lambda i: (i, 0))
    vec = pl.BlockSpec((1, D), lambda i: (0, 0))
    return pl.pallas_call(
        _combine_ln_body,
        grid=(M // MM_TM,),
        in_specs=[row, row, row, vec, vec],
        out_specs=[row, row],
        out_shape=[jax.ShapeDtypeStruct((M, D), F32), jax.ShapeDtypeStruct((M, D), BF16)],
        compiler_params=_cparams("parallel"),
        name="moe_combine_ln",
    )(x, y0, y1, g.reshape(1, D), b.reshape(1, D))


def _t5_bucket_np(rel):
    n = np.maximum(rel, 0)
    max_exact = REL_BUCKETS // 2
    nf = np.maximum(n, 1).astype(np.float32)
    large = max_exact + (np.log(nf / np.float32(max_exact))
                         / np.float32(math.log(REL_MAX_DIST / max_exact))
                         * np.float32(REL_BUCKETS - max_exact)).astype(np.int32)
    large = np.minimum(large, REL_BUCKETS - 1)
    return np.where(n < max_exact, n, large).astype(np.int32)


def _bias_body(tbl_ref, bk_ref, o_ref):
    h = pl.program_id(0)
    for dl in range(2):
        bk = bk_ref[dl]
        acc = jnp.zeros((TILE, TILE), F32)
        for b in range(REL_BUCKETS):
            acc = jnp.where(bk == b, tbl_ref[h * REL_BUCKETS + b], acc)
        o_ref[dl, 0] = acc


def _bias_tiles(rel_bias):
    key = np.arange(TILE)[:, None]
    qry = np.arange(TILE)[None, :]
    assert int(_t5_bucket_np(np.array(TILE + 1))) == REL_BUCKETS - 1
    bk = np.stack([_t5_bucket_np(qry - key), _t5_bucket_np(TILE + qry - key)])
    return pl.pallas_call(
        _bias_body,
        grid=(N_HEADS,),
        in_specs=[pl.BlockSpec(memory_space=pltpu.SMEM),
                  pl.BlockSpec((2, TILE, TILE), lambda h: (0, 0, 0))],
        out_specs=pl.BlockSpec((2, 1, TILE, TILE), lambda h: (0, h, 0, 0)),
        out_shape=jax.ShapeDtypeStruct((2, N_HEADS, TILE, TILE), F32),
        name="t5_bias_tiles",
    )(rel_bias.T.reshape(-1), jnp.asarray(bk))


def _heads_on_lanes(bias, per_block):
    two, H, T, _ = bias.shape
    b = bias.reshape(two, H // per_block, per_block, T, T).transpose(0, 1, 3, 2, 4)
    return b.reshape(two, H // per_block, T, per_block * T)


def _init_state(m_ref, l_ref, acc_ref):
    m_ref[...] = jnp.full(m_ref.shape, NEG_INF, F32)
    l_ref[...] = jnp.zeros(l_ref.shape, F32)
    acc_ref[...] = jnp.zeros(acc_ref.shape, F32)


def _attend(q, k, vt, add, m_ref, l_ref, acc_ref):
    s = lax.dot_general(k, q, _NT, preferred_element_type=F32) + add
    m_prev = m_ref[...]
    m_new = jnp.maximum(m_prev, jnp.max(s, axis=0, keepdims=True))
    a = jnp.exp(m_prev - m_new)
    p = jnp.exp(s - m_new)
    l_ref[...] = a * l_ref[...] + jnp.sum(p, axis=0, keepdims=True)
    acc_ref[...] = a * acc_ref[...] + jnp.dot(vt, p.astype(BF16), preferred_element_type=F32)
    m_ref[...] = m_new


def _rank_before(vals, rows):
    idx = lax.broadcasted_iota(I32, vals.shape, 0)
    rank = jnp.zeros(vals.shape, I32)
    for m in range(rows):
        row = vals[m:m + 1, :]
        beats = (row > vals) | ((row == vals) & (idx > m))
        rank = rank + jnp.where(beats, 1, 0)
    return rank


def _moba_body(q_ref, k_ref, vt_ref, bias_ref, o_ref, kmean_ref, neg_ref, m_ref, l_ref, acc_ref):
    i = pl.program_id(2)
    nb = k_ref.shape[1] // TILE

    @pl.when(i == 0)
    def _():
        for n in range(nb):
            kb = k_ref[0, n * TILE:(n + 1) * TILE, :].astype(F32)
            kmean_ref[n:n + 1, :] = jnp.sum(kb, axis=0, keepdims=True) * (1.0 / TILE)

    lane = lax.broadcasted_iota(I32, (TILE, LANES), 1)
    q = q_ref[0] * SCALE
    zero = jnp.zeros_like(q)
    q2 = jnp.concatenate([jnp.where(lane < HEAD_DIM, q, zero),
                          jnp.where(lane >= HEAD_DIM, q, zero)], axis=0)

    km = kmean_ref[...]
    k_hi = km.astype(BF16)
    k_lo = (km - k_hi.astype(F32)).astype(BF16)
    gate = (lax.dot_general(k_hi, q2, _NT, preferred_element_type=F32)
            + lax.dot_general(k_lo, q2, _NT, preferred_element_type=F32))
    blk = lax.broadcasted_iota(I32, gate.shape, 0)
    gate = jnp.where(blk < i, gate, -jnp.inf)
    rank = _rank_before(gate, nb)
    neg_ref[...] = jnp.where((rank < MOBA_TOPK) & (blk < i), 0.0, NEG_INF)

    _init_state(m_ref, l_ref, acc_ref)
    key = lax.broadcasted_iota(I32, (TILE, 2 * TILE), 0)
    qry = lax.broadcasted_iota(I32, (TILE, 2 * TILE), 1) & (TILE - 1)
    own = pl.multiple_of(i * TILE, TILE)
    _attend(q2, k_ref[0, pl.ds(own, TILE), :], vt_ref[0, i],
            bias_ref[0, 0] + jnp.where(key <= qry, 0.0, NEG_INF), m_ref, l_ref, acc_ref)

    @pl.when(i >= 1)
    def _():
        off = pl.multiple_of((i - 1) * TILE, TILE)
        _attend(q2, k_ref[0, pl.ds(off, TILE), :], vt_ref[0, i - 1],
                bias_ref[1, 0] + neg_ref[pl.ds(i - 1, 1), :], m_ref, l_ref, acc_ref)

    far_bias = bias_ref[1, 0, 0:1, :]

    def far(n, carry):
        off = pl.multiple_of(n * TILE, TILE)
        _attend(q2, k_ref[0, pl.ds(off, TILE), :], vt_ref[0, n],
                far_bias + neg_ref[pl.ds(n, 1), :], m_ref, l_ref, acc_ref)
        return carry

    lax.fori_loop(0, jnp.maximum(i - 1, 0), far, 0)

    o = acc_ref[...] / l_ref[...]
    o = jnp.concatenate([o[:HEAD_DIM, :TILE], o[HEAD_DIM:, TILE:]], axis=0)
    o_ref[0] = o.T.astype(o_ref.dtype)


def _moba_attention(qk, vt, bias):
    B, S, _ = qk.shape
    n_pairs = N_HEADS // 2
    nq = S // TILE
    return pl.pallas_call(
        _moba_body,
        grid=(B, n_pairs, nq),
        in_specs=[pl.BlockSpec((1, TILE, LANES), lambda b, p, i: (b, i, p)),
                  pl.BlockSpec((1, S, LANES), lambda b, p, i: (b, 0, n_pairs + p)),
                  pl.BlockSpec((1, nq, LANES, TILE), lambda b, p, i: (b, 0, p, 0)),
                  pl.BlockSpec((2, 1, TILE, 2 * TILE), lambda b, p, i: (0, p, 0, 0))],
        out_specs=pl.BlockSpec((1, TILE, LANES), lambda b, p, i: (b, i, p)),
        out_shape=jax.ShapeDtypeStruct((B, S, N_HEADS * HEAD_DIM), BF16),
        scratch_shapes=[pltpu.VMEM((nq, LANES), F32),
                        pltpu.VMEM((nq, 2 * TILE), F32),
                        pltpu.VMEM((1, 2 * TILE), F32),
                        pltpu.VMEM((1, 2 * TILE), F32),
                        pltpu.VMEM((LANES, 2 * TILE), F32)],
        compiler_params=_cparams("parallel", "parallel", "arbitrary"),
        name="moba_attention",
    )(qk, qk, vt, _heads_on_lanes(bias, 2))


def _gelu_tanh(x):
    return 0.5 * x * (1.0 + jnp.tanh(math.sqrt(2.0 / math.pi) * (x + 0.044715 * (x * x * x))))


def _compress_body(t_ref, pos_ref, w1_ref, w2_ref, o_ref):
    groups = t_ref.shape[2]
    half = t_ref.shape[3]
    t = t_ref[0].reshape(NSA_KV_HEADS * groups, half).astype(F32)
    first = jnp.dot((t + pos_ref[0:1, :]).astype(BF16), w1_ref[0:half, :], preferred_element_type=F32)
    second = jnp.dot((t + pos_ref[1:2, :]).astype(BF16), w1_ref[half:2 * half, :],
                     preferred_element_type=F32)
    rows = first.shape[0]
    pre = first + pltpu.roll(second, rows - 1, 0)
    out = jnp.dot(_gelu_tanh(pre).astype(BF16), w2_ref[...], preferred_element_type=F32)
    for h in range(NSA_KV_HEADS):
        o_ref[0, h] = out[h * groups:(h + 1) * groups].astype(o_ref.dtype)


def _compress(t, pos, w1, w2):
    B, Hkv, groups, half = t.shape
    return pl.pallas_call(
        _compress_body,
        grid=(B,),
        in_specs=[pl.BlockSpec((1, Hkv, groups, half), lambda b: (b, 0, 0, 0)),
                  pl.BlockSpec((2, half), lambda b: (0, 0)),
                  pl.BlockSpec((2 * half, CMP_HIDDEN), lambda b: (0, 0)),
                  pl.BlockSpec((CMP_HIDDEN, HEAD_DIM), lambda b: (0, 0))],
        out_specs=pl.BlockSpec((1, Hkv, groups, HEAD_DIM), lambda b: (b, 0, 0, 0)),
        out_shape=jax.ShapeDtypeStruct((B, Hkv, groups, HEAD_DIM), BF16),
        compiler_params=_cparams("parallel"),
        name="nsa_compress",
    )(t, pos.reshape(2, half), w1.astype(BF16), w2.astype(BF16))


def _swap_halves(x):
    return jnp.concatenate([x[:, HEAD_DIM:], x[:, :HEAD_DIM]], axis=1)


def _group_lanes(x):
    return jnp.concatenate([x] * NSA_GROUP, axis=1)


def _nsa_body(q_ref, kc_ref, vct_ref, ks_ref, vst_ref, kw_ref, vwt_ref, gt_ref, bias_ref, c2s_ref,
              o_ref, selneg_ref, m_ref, l_ref, acc_ref):
    i = pl.program_id(2)
    n_cmp = kc_ref.shape[1]
    n_slc = c2s_ref.shape[0]
    per_tile = TILE // SLC_BLOCK
    cols = NSA_GROUP * TILE
    lane = lax.broadcasted_iota(I32, (TILE, LANES), 1)
    lo_half = lane < HEAD_DIM
    qpos = i * TILE + (lax.broadcasted_iota(I32, (n_cmp, cols), 1) & (TILE - 1))
    cmp_valid = CMP_STRIDE * lax.broadcasted_iota(I32, (n_cmp, cols), 0) + (CMP_LEN - 1) <= qpos
    key = lax.broadcasted_iota(I32, (TILE, TILE), 0)
    qry = lax.broadcasted_iota(I32, (TILE, TILE), 1)
    diag_neg = jnp.where(key <= qry, 0.0, NEG_INF)
    gates = jax.nn.sigmoid(gt_ref[0, 0])
    own = pl.multiple_of(i * TILE, TILE)
    near_t = jnp.maximum(i - 1, 0)
    far2_t = jnp.maximum(i - 2, 0)
    near = pl.multiple_of(near_t * TILE, TILE)
    far2 = pl.multiple_of(far2_t * TILE, TILE)
    qall = q_ref[0] * SCALE
    pieces = []

    for a in range(2):
        keep = lo_half if a == 0 else jnp.logical_not(lo_half)
        heads = []
        for g in range(NSA_GROUP):
            cb = a * 2 + g // 2
            x = qall[:, cb * LANES:(cb + 1) * LANES]
            if g % 2 != a:
                x = _swap_halves(x)
            heads.append(jnp.where(keep, x, jnp.zeros_like(x)))
        q4 = jnp.concatenate(heads, axis=0)

        s_c = lax.dot_general(kc_ref[0], q4, _NT, preferred_element_type=F32)
        s_c = jnp.where(cmp_valid, s_c, NEG_INF)
        m_c = jnp.max(s_c, axis=0, keepdims=True)
        e_c = jnp.where(cmp_valid, jnp.exp(s_c - m_c), 0.0)
        l_c = jnp.sum(e_c, axis=0, keepdims=True)
        p_c = e_c / jnp.where(l_c > 0.0, l_c, 1.0)
        o_c = jnp.dot(vct_ref[0], p_c.astype(BF16), preferred_element_type=F32)

        p_sum = p_c[:, 0:TILE]
        for g in range(1, NSA_GROUP):
            p_sum = p_sum + p_c[:, g * TILE:(g + 1) * TILE]
        p_hi = p_sum.astype(BF16)
        p_lo = (p_sum - p_hi.astype(F32)).astype(BF16)
        imp = (jnp.dot(c2s_ref[...], p_hi, preferred_element_type=F32)
               + jnp.dot(c2s_ref[...], p_lo, preferred_element_type=F32))
        j = lax.broadcasted_iota(I32, imp.shape, 0)
        qb = (i * TILE + lax.broadcasted_iota(I32, imp.shape, 1)) >> int(math.log2(SLC_BLOCK))
        forced = (j == 0) | ((j <= qb) & (j > qb - SLC_LOCAL))
        imp = jnp.where(forced, jnp.inf, jnp.where(j > qb, -jnp.inf, imp))
        rank = _rank_before(imp, n_slc)
        selneg_ref[...] = jnp.where((rank < SLC_TOPN) & (j <= qb), 0.0, NEG_INF)

        def sel_neg(t):
            rows = [jnp.broadcast_to(selneg_ref[pl.ds(per_tile * t + c, 1), :], (SLC_BLOCK, TILE))
                    for c in range(per_tile)]
            return jnp.concatenate(rows, axis=0)

        far_bias = bias_ref[1, a, 0:1, :]

        _init_state(m_ref, l_ref, acc_ref)
        _attend(q4, ks_ref[0, pl.ds(own, TILE), :], vst_ref[0, i],
                bias_ref[0, a] + _group_lanes(sel_neg(i) + diag_neg), m_ref, l_ref, acc_ref)

        @pl.when(i >= 1)
        def _():
            _attend(q4, ks_ref[0, pl.ds(near, TILE), :], vst_ref[0, near_t],
                    bias_ref[1, a] + _group_lanes(sel_neg(near_t)), m_ref, l_ref, acc_ref)

        def far(t, carry):
            off = pl.multiple_of(t * TILE, TILE)
            _attend(q4, ks_ref[0, pl.ds(off, TILE), :], vst_ref[0, t],
                    far_bias + _group_lanes(sel_neg(t)), m_ref, l_ref, acc_ref)
            return carry

        lax.fori_loop(0, jnp.maximum(i - 1, 0), far, 0)
        o_s = acc_ref[...] / l_ref[...]

        _init_state(m_ref, l_ref, acc_ref)
        _attend(q4, kw_ref[0, pl.ds(own, TILE), :], vwt_ref[0, i],
                bias_ref[0, a] + _group_lanes(diag_neg), m_ref, l_ref, acc_ref)

        @pl.when(i >= 1)
        def _():
            _attend(q4, kw_ref[0, pl.ds(near, TILE), :], vwt_ref[0, near_t],
                    bias_ref[1, a], m_ref, l_ref, acc_ref)

        @pl.when(i >= 2)
        def _():
            _attend(q4, kw_ref[0, pl.ds(far2, TILE), :], vwt_ref[0, far2_t],
                    far_bias + _group_lanes(jnp.where(key > qry, 0.0, NEG_INF)), m_ref, l_ref, acc_ref)

        o_w = acc_ref[...] / l_ref[...]

        rs = slice(a * HEAD_DIM, (a + 1) * HEAD_DIM)
        for g in range(NSA_GROUP):
            c0 = 3 * (NSA_GROUP * a + g)
            ls = slice(g * TILE, (g + 1) * TILE)
            pieces.append(gates[c0:c0 + 1, :] * o_c[rs, ls] + gates[c0 + 1:c0 + 2, :] * o_s[rs, ls]
                          + gates[c0 + 2:c0 + 3, :] * o_w[rs, ls])

    o_ref[0] = jnp.concatenate(pieces, axis=0).T.astype(o_ref.dtype)


def _cmp_to_slc(S):
    n_cmp_pad = S // CMP_STRIDE
    n_slc = S // SLC_BLOCK
    ci = np.arange(n_cmp_pad)[:, None] * CMP_STRIDE
    sj = np.arange(n_slc)[None, :] * SLC_BLOCK
    c2s = ((ci < sj + SLC_BLOCK) & (ci + CMP_LEN > sj)).astype(np.float32)
    c2s[(S - CMP_LEN) // CMP_STRIDE + 1:] = 0.0
    return jnp.asarray(c2s.T, BF16)


def _nsa_attention(proj, vt, gate_t, kcmp, vcmp_t, bias):
    B, S, _ = proj.shape
    nq = S // TILE
    n_cmp = kcmp.shape[1]
    n_slc = S // SLC_BLOCK
    qw = 2 * NSA_GROUP * HEAD_DIM
    q_blocks = N_HEADS * HEAD_DIM // LANES
    kv_blocks = NSA_KV_HEADS * HEAD_DIM // LANES

    def k_spec(which):
        base = q_blocks + which * kv_blocks
        return pl.BlockSpec((1, S, LANES), lambda b, p, i: (b, 0, base + p))

    def vt_spec(which):
        base = which * kv_blocks
        return pl.BlockSpec((1, nq, LANES, TILE), lambda b, p, i: (b, 0, base + p, 0))

    return pl.pallas_call(
        _nsa_body,
        grid=(B, 2, nq),
        in_specs=[pl.BlockSpec((1, TILE, qw), lambda b, p, i: (b, i, p)),
                  pl.BlockSpec((1, n_cmp, LANES), lambda b, p, i: (b, 0, p)),
                  pl.BlockSpec((1, LANES, n_cmp), lambda b, p, i: (b, p, 0)),
                  k_spec(2), vt_spec(0), k_spec(3), vt_spec(1),
                  pl.BlockSpec((1, 1, LANES, TILE), lambda b, p, i: (b, i, p, 0)),
                  pl.BlockSpec((2, 2, TILE, NSA_GROUP * TILE), lambda b, p, i: (0, p, 0, 0)),
                  pl.BlockSpec((n_slc, n_cmp), lambda b, p, i: (0, 0))],
        out_specs=pl.BlockSpec((1, TILE, qw), lambda b, p, i: (b, i, p)),
        out_shape=jax.ShapeDtypeStruct((B, S, N_HEADS * HEAD_DIM), BF16),
        scratch_shapes=[pltpu.VMEM((n_slc, TILE), F32),
                        pltpu.VMEM((1, NSA_GROUP * TILE), F32),
                        pltpu.VMEM((1, NSA_GROUP * TILE), F32),
                        pltpu.VMEM((LANES, NSA_GROUP * TILE), F32)],
        compiler_params=_cparams("parallel", "parallel", "arbitrary"),
        name="nsa_attention",
    )(proj, kcmp, vcmp_t, proj, vt, proj, vt, gate_t, _heads_on_lanes(bias, NSA_GROUP), _cmp_to_slc(S))


def _split_bf16(x):
    hi = x.astype(BF16)
    return hi, (x - hi.astype(F32)).astype(BF16)


def _router_body(x_ref, w_ref, b_ref, idx_ref, wt_ref):
    x_hi, x_lo = _split_bf16(x_ref[...])
    w_hi, w_lo = _split_bf16(w_ref[...])
    logits = (lax.dot_general(w_hi, x_hi, _NT, preferred_element_type=F32)
              + lax.dot_general(w_hi, x_lo, _NT, preferred_element_type=F32)
              + lax.dot_general(w_lo, x_hi, _NT, preferred_element_type=F32)) + b_ref[:, 0:1]
    m = jnp.max(logits, axis=0, keepdims=True)
    e = jnp.exp(logits - m)
    probs = e / jnp.sum(e, axis=0, keepdims=True)
    pk = [probs[k * N_GROUPS:(k + 1) * N_GROUPS] for k in range(EXPERTS_PER_GROUP)]
    hi1, lo1 = jnp.maximum(pk[0], pk[1]), jnp.minimum(pk[0], pk[1])
    hi2, lo2 = jnp.maximum(pk[2], pk[3]), jnp.minimum(pk[2], pk[3])
    score = jnp.maximum(hi1, hi2) + jnp.maximum(jnp.minimum(hi1, hi2), jnp.maximum(lo1, lo2))
    grp = lax.broadcasted_iota(I32, score.shape, 0)
    best = jnp.min(jnp.where(score == jnp.max(score, axis=0, keepdims=True), grp, N_GROUPS),
                   axis=0, keepdims=True)
    v = [jnp.sum(jnp.where(grp == best, p, 0.0), axis=0, keepdims=True) for p in pk]
    v1 = jnp.maximum(jnp.maximum(v[0], v[1]), jnp.maximum(v[2], v[3]))
    i1 = jnp.where(v[0] == v1, 0, jnp.where(v[1] == v1, 1, jnp.where(v[2] == v1, 2, 3)))
    rest = [jnp.where(i1 == k, -1.0, v[k]) for k in range(EXPERTS_PER_GROUP)]
    v2 = jnp.maximum(jnp.maximum(rest[0], rest[1]), jnp.maximum(rest[2], rest[3]))
    i2 = jnp.where(rest[0] == v2, 0, jnp.where(rest[1] == v2, 1, jnp.where(rest[2] == v2, 2, 3)))
    tot = v1 + v2
    idx_ref[...] = jnp.concatenate([best * EXPERTS_PER_GROUP + i1, best * EXPERTS_PER_GROUP + i2], axis=0)
    wt_ref[...] = jnp.concatenate([v1 / tot, v2 / tot], axis=0)


def _router(x, router_w, router_b):
    N, D = x.shape
    perm = np.arange(N_EXPERTS).reshape(N_GROUPS, EXPERTS_PER_GROUP).T.reshape(-1)
    w = router_w.T[perm]
    b = jnp.broadcast_to(router_b[perm][:, None], (N_EXPERTS, LANES))
    tm = 1024
    return pl.pallas_call(
        _router_body,
        grid=(N // tm,),
        in_specs=[pl.BlockSpec((tm, D), lambda i: (i, 0)),
                  pl.BlockSpec((N_EXPERTS, D), lambda i: (0, 0)),
                  pl.BlockSpec((N_EXPERTS, LANES), lambda i: (0, 0))],
        out_specs=[pl.BlockSpec((2, tm), lambda i: (0, i)), pl.BlockSpec((2, tm), lambda i: (0, i))],
        out_shape=[jax.ShapeDtypeStruct((2, N), I32), jax.ShapeDtypeStruct((2, N), F32)],
        compiler_params=_cparams("parallel"),
        name="moe_router",
    )(x, w, b)


def _expert_body(blk_e_ref, n_used_ref, x_ref, rw_ref, wg_ref, wu_ref, wd_ref, o_ref):
    i = pl.program_id(0)

    @pl.when(i < n_used_ref[0])
    def _():
        x = x_ref[...]
        gate = jnp.dot(x, wg_ref[0], preferred_element_type=F32)
        up = jnp.dot(x, wu_ref[0], preferred_element_type=F32)
        hid = (gate * jax.nn.sigmoid(gate) * up).astype(BF16)
        o_ref[...] = jnp.dot(hid, wd_ref[0], preferred_element_type=F32) * rw_ref[:, 0:1]

    @pl.when(i >= n_used_ref[0])
    def _():
        o_ref[...] = jnp.zeros(o_ref.shape, o_ref.dtype)


def _experts(xs, row_w, blk_e, n_used, wg, wu, wd):
    R, D = xs.shape
    n_blk = R // MOE_TB

    def live(i, be, nu):
        return jnp.minimum(i, nu[0] - 1)

    grid_spec = pltpu.PrefetchScalarGridSpec(
        num_scalar_prefetch=2,
        grid=(n_blk,),
        in_specs=[pl.BlockSpec((MOE_TB, D), lambda i, be, nu: (live(i, be, nu), 0)),
                  pl.BlockSpec((MOE_TB, LANES), lambda i, be, nu: (live(i, be, nu), 0)),
                  pl.BlockSpec((1, D, D_EXPERT), lambda i, be, nu: (be[i], 0, 0)),
                  pl.BlockSpec((1, D, D_EXPERT), lambda i, be, nu: (be[i], 0, 0)),
                  pl.BlockSpec((1, D_EXPERT, D), lambda i, be, nu: (be[i], 0, 0))],
        out_specs=pl.BlockSpec((MOE_TB, D), lambda i, be, nu: (i, 0)),
    )
    return pl.pallas_call(
        _expert_body,
        grid_spec=grid_spec,
        out_shape=jax.ShapeDtypeStruct((R, D), F32),
        compiler_params=_cparams("arbitrary"),
        name="moe_experts",
    )(blk_e, n_used, xs, row_w, wg, wu, wd)


def _moe_ln(h, hb, router_w, router_b, wg, wu, wd, g, b):
    N, D = h.shape
    A = 2 * N
    idx, wts = _router(h, router_w, router_b)
    e_flat = idx.T.reshape(A)
    w_flat = wts.T.reshape(A)
    tok = jnp.arange(A, dtype=I32) // 2
    order = jnp.argsort(e_flat)
    e_s = e_flat[order]
    counts = jnp.zeros((N_EXPERTS,), I32).at[e_flat].add(1)
    starts = jnp.cumsum(counts) - counts
    padded = (counts + MOE_TB - 1) // MOE_TB * MOE_TB
    pends = jnp.cumsum(padded)
    pstarts = pends - padded
    dest_s = pstarts[e_s] + (jnp.arange(A, dtype=I32) - starts[e_s])
    R = A + N_EXPERTS * MOE_TB
    n_blk = R // MOE_TB
    dest = jnp.zeros((A,), I32).at[order].set(dest_s)
    row_tok = jnp.zeros((R,), I32).at[dest].set(tok)
    row_w = jnp.zeros((R,), F32).at[dest].set(w_flat)
    blk_e = jnp.minimum(jnp.searchsorted(pends, jnp.arange(n_blk, dtype=I32) * MOE_TB, side='right'),
                        N_EXPERTS - 1).astype(I32)
    n_used = (pends[-1:] // MOE_TB).astype(I32)
    xs = hb[row_tok]
    yb = _experts(xs, jnp.broadcast_to(row_w[:, None], (R, LANES)), blk_e, n_used,
                  wg.astype(BF16), wu.astype(BF16), wd.astype(BF16))
    d2 = dest.reshape(N, 2)
    return _combine_ln(h, yb[d2[:, 0]], yb[d2[:, 1]], g, b)


def _moba_layer(h, w_in, w_out, bias, g, b, B, S):
    HD = N_HEADS * HEAD_DIM
    qk = _matmul(h, w_in[:, :2 * HD].astype(BF16), BF16).reshape(B, S, 2 * HD)
    vt = _matmul_t(w_in[:, 2 * HD:].T.astype(BF16), h, B, S, BF16)
    att = _moba_attention(qk, vt, bias)
    return _proj_ln(att.reshape(B * S, HD), w_out.astype(BF16), h, g, b)


def _nsa_layer(h, hb, w_in, w_out, pos_k, pos_v, ck_w1, ck_w2, cv_w1, cv_w2, bias, g, b, B, S):
    HD = N_HEADS * HEAD_DIM
    kvw = NSA_KV_HEADS * HEAD_DIM
    col = lambda k: slice(HD + k * kvw, HD + (k + 1) * kvw)
    w_rows = jnp.concatenate([w_in[:, :HD + 2 * kvw], w_in[:, col(2)], w_in[:, col(4)]], axis=1)
    proj = _matmul(hb, w_rows.astype(BF16), BF16).reshape(B, S, HD + 4 * kvw)
    w_vt = jnp.concatenate([w_in[:, col(3)], w_in[:, col(5)]], axis=1).T
    vt = _matmul_t(w_vt.astype(BF16), hb, B, S, BF16)
    per_pair = 3 * N_HEADS // 2
    wg = w_in[:, HD + 6 * kvw:].reshape(D_MODEL, 2, per_pair)
    wg = jnp.pad(wg, ((0, 0), (0, 0), (0, LANES - per_pair))).reshape(D_MODEL, 2 * LANES).T
    gate_t = _matmul_t(wg.astype(BF16), hb, B, S, F32)

    def grouped(t):
        t = t.reshape(B, S, NSA_KV_HEADS, HEAD_DIM).transpose(0, 2, 1, 3)
        return t.reshape(B, NSA_KV_HEADS, S // CMP_STRIDE, CMP_STRIDE * HEAD_DIM)

    kcmp = _compress(grouped(proj[..., HD:HD + kvw]), pos_k, ck_w1, ck_w2)
    vcmp = _compress(grouped(proj[..., HD + kvw:HD + 2 * kvw]), pos_v, cv_w1, cv_w2)
    n_cmp = kcmp.shape[2]
    kcmp = kcmp.transpose(0, 2, 1, 3).reshape(B, n_cmp, kvw)
    vcmp_t = vcmp.transpose(0, 1, 3, 2).reshape(B, kvw, n_cmp)
    att = _nsa_attention(proj, vt, gate_t, kcmp, vcmp_t, bias)
    return _proj_ln(att.reshape(B * S, HD), w_out.astype(BF16), h, g, b)


def kernel(x, rel_bias, router_w, router_b, ln_g, ln_b, moba_w_in, moba_w_out, nsa_w_in, nsa_w_out,
           nsa_pos_k, nsa_pos_v, nsa_ck_w1, nsa_ck_w2, nsa_cv_w1, nsa_cv_w2,
           moe_w_gate, moe_w_up, moe_w_down):
    B, S, D = x.shape
    bias = _bias_tiles(rel_bias)
    h = x.reshape(B * S, D)
    h, hb = _moba_layer(h, moba_w_in[0], moba_w_out[0], bias, ln_g[0, 0], ln_b[0, 0], B, S)
    h, hb = _moe_ln(h, hb, router_w, router_b, moe_w_gate[0], moe_w_up[0], moe_w_down[0],
                    ln_g[0, 1], ln_b[0, 1])
    h, hb = _nsa_layer(h, hb, nsa_w_in[0], nsa_w_out[0], nsa_pos_k[0], nsa_pos_v[0],
                       nsa_ck_w1[0], nsa_ck_w2[0], nsa_cv_w1[0], nsa_cv_w2[0],
                       bias, ln_g[1, 0], ln_b[1, 0], B, S)
    h, hb = _moe_ln(h, hb, router_w, router_b, moe_w_gate[1], moe_w_up[1], moe_w_down[1],
                    ln_g[1, 1], ln_b[1, 1])
    return h.reshape(B, S, D)
```

```python
import math

import numpy as np
import jax
import jax.numpy as jnp
from jax import lax
from jax.experimental import pallas as pl
from jax.experimental.pallas import tpu as pltpu

F32, BF16, I32 = jnp.float32, jnp.bfloat16, jnp.int32

D_MODEL = 1024
N_HEADS = 16
HEAD_DIM = 64
DEPTH = 2
NEG_INF = -1e30
LN_EPS = 1e-5
MOBA_BLOCK = 256
MOBA_TOPK = 3
NSA_KV_HEADS = 4
NSA_GROUP = N_HEADS // NSA_KV_HEADS
CMP_LEN = 32
CMP_STRIDE = 16
CMP_HIDDEN = 256
SLC_BLOCK = 64
SLC_TOPN = 16
SLC_LOCAL = 2
WINDOW = 512
REL_BUCKETS = 32
REL_MAX_DIST = 128
N_EXPERTS = 32
N_GROUPS = 8
EXPERTS_PER_GROUP = N_EXPERTS // N_GROUPS
D_EXPERT = 512
DEEPNORM_ALPHA = (2 * DEPTH) ** 0.25
SCALE = HEAD_DIM ** -0.5

LANES = 128
TILE = 256
MM_TM = 1024
MM_TN = 1024
LN_TM = 512
MOE_TB = 256
VMEM_LIMIT = 48 * 1024 * 1024

_NT = (((1,), (1,)), ((), ()))


def _cparams(*sem):
    return pltpu.CompilerParams(dimension_semantics=sem, vmem_limit_bytes=VMEM_LIMIT)


def _mm_body(a_ref, b_ref, o_ref):
    o_ref[...] = jnp.dot(a_ref[...].astype(BF16), b_ref[...],
                         preferred_element_type=F32).astype(o_ref.dtype)


def _matmul(a, b, out_dtype):
    M, K = a.shape
    N = b.shape[1]
    tn = min(MM_TN, N)
    return pl.pallas_call(
        _mm_body,
        grid=(M // MM_TM, N // tn),
        in_specs=[pl.BlockSpec((MM_TM, K), lambda i, j: (i, 0)),
                  pl.BlockSpec((K, tn), lambda i, j: (0, j))],
        out_specs=pl.BlockSpec((MM_TM, tn), lambda i, j: (i, j)),
        out_shape=jax.ShapeDtypeStruct((M, N), out_dtype),
        compiler_params=_cparams("parallel", "arbitrary"),
        name="in_proj",
    )(a, b)


def _mm_t_body(w_ref, a_ref, o_ref):
    r = lax.dot_general(w_ref[...], a_ref[...].astype(BF16), _NT, preferred_element_type=F32)
    for t in range(o_ref.shape[1]):
        o_ref[0, t] = r[:, t * TILE:(t + 1) * TILE].astype(o_ref.dtype)


def _matmul_t(w_t, a, B, S, out_dtype):
    Nout, K = w_t.shape
    tn = min(MM_TN, Nout)
    per_seq = S // MM_TM
    sub = MM_TM // TILE
    return pl.pallas_call(
        _mm_t_body,
        grid=(B * per_seq, Nout // tn),
        in_specs=[pl.BlockSpec((tn, K), lambda i, j: (j, 0)),
                  pl.BlockSpec((MM_TM, K), lambda i, j: (i, 0))],
        out_specs=pl.BlockSpec((1, sub, tn, TILE), lambda i, j: (i // per_seq, i % per_seq, j, 0)),
        out_shape=jax.ShapeDtypeStruct((B, S // TILE, Nout, TILE), out_dtype),
        compiler_params=_cparams("parallel", "arbitrary"),
        name="in_proj_t",
    )(w_t, a)


def _layer_norm_rows(z, g, b):
    mu = jnp.mean(z, axis=-1, keepdims=True)
    zc = z - mu
    var = jnp.mean(zc * zc, axis=-1, keepdims=True)
    return zc * lax.rsqrt(var + LN_EPS) * g + b


def _proj_ln_body(a_ref, w_ref, x_ref, g_ref, b_ref, o_ref, ob_ref):
    y = jnp.dot(a_ref[...], w_ref[...], preferred_element_type=F32)
    out = _layer_norm_rows(DEEPNORM_ALPHA * x_ref[...] + y, g_ref[...], b_ref[...])
    o_ref[...] = out
    ob_ref[...] = out.astype(BF16)


def _proj_ln(a, w, x, g, b):
    M, K = a.shape
    D = w.shape[1]
    row = pl.BlockSpec((LN_TM, D), lambda i: (i, 0))
    vec = pl.BlockSpec((1, D), lambda i: (0, 0))
    return pl.pallas_call(
        _proj_ln_body,
        grid=(M // LN_TM,),
        in_specs=[pl.BlockSpec((LN_TM, K), lambda i: (i, 0)),
                  pl.BlockSpec((K, D), lambda i: (0, 0)), row, vec, vec],
        out_specs=[row, row],
        out_shape=[jax.ShapeDtypeStruct((M, D), F32), jax.ShapeDtypeStruct((M, D), BF16)],
        compiler_params=_cparams("parallel"),
        name="out_proj_ln",
    )(a, w, x, g.reshape(1, D), b.reshape(1, D))


def _t5_bucket_np(rel):
    n = np.maximum(rel, 0)
    max_exact = REL_BUCKETS // 2
    nf = np.maximum(n, 1).astype(np.float32)
    large = max_exact + (np.log(nf / np.float32(max_exact))
                         / np.float32(math.log(REL_MAX_DIST / max_exact))
                         * np.float32(REL_BUCKETS - max_exact)).astype(np.int32)
    large = np.minimum(large, REL_BUCKETS - 1)
    return np.where(n < max_exact, n, large).astype(np.int32)


def _bias_body(tbl_ref, bk_ref, o_ref):
    h = pl.program_id(0)
    for dl in range(2):
        bk = bk_ref[dl]
        acc = jnp.zeros((TILE, TILE), F32)
        for b in range(REL_BUCKETS):
            acc = jnp.where(bk == b, tbl_ref[h * REL_BUCKETS + b], acc)
        o_ref[dl, 0] = acc


def _bias_tiles(rel_bias):
    key = np.arange(TILE)[:, None]
    qry = np.arange(TILE)[None, :]
    assert int(_t5_bucket_np(np.array(TILE + 1))) == REL_BUCKETS - 1
    bk = np.stack([_t5_bucket_np(qry - key), _t5_bucket_np(TILE + qry - key)])
    return pl.pallas_call(
        _bias_body,
        grid=(N_HEADS,),
        in_specs=[pl.BlockSpec(memory_space=pltpu.SMEM),
                  pl.BlockSpec((2, TILE, TILE), lambda h: (0, 0, 0))],
        out_specs=pl.BlockSpec((2, 1, TILE, TILE), lambda h: (0, h, 0, 0)),
        out_shape=jax.ShapeDtypeStruct((2, N_HEADS, TILE, TILE), F32),
        name="t5_bias_tiles",
    )(rel_bias.T.reshape(-1), jnp.asarray(bk))


def _heads_on_lanes(bias, per_block):
    two, H, T, _ = bias.shape
    b = bias.reshape(two, H // per_block, per_block, T, T).transpose(0, 1, 3, 2, 4)
    return b.reshape(two, H // per_block, T, per_block * T)


def _init_state(m_ref, l_ref, acc_ref):
    m_ref[...] = jnp.full(m_ref.shape, NEG_INF, F32)
    l_ref[...] = jnp.zeros(l_ref.shape, F32)
    acc_ref[...] = jnp.zeros(acc_ref.shape, F32)


def _attend(q, k, vt, add, m_ref, l_ref, acc_ref):
    s = lax.dot_general(k, q, _NT, preferred_element_type=F32) + add
    m_prev = m_ref[...]
    m_new = jnp.maximum(m_prev, jnp.max(s, axis=0, keepdims=True))
    a = jnp.exp(m_prev - m_new)
    p = jnp.exp(s - m_new)
    l_ref[...] = a * l_ref[...] + jnp.sum(p, axis=0, keepdims=True)
    acc_ref[...] = a * acc_ref[...] + jnp.dot(vt, p.astype(BF16), preferred_element_type=F32)
    m_ref[...] = m_new


def _rank_before(vals, rows):
    idx = lax.broadcasted_iota(I32, vals.shape, 0)
    rank = jnp.zeros(vals.shape, I32)
    for m in range(rows):
        row = vals[m:m + 1, :]
        beats = (row > vals) | ((row == vals) & (idx > m))
        rank = rank + jnp.where(beats, 1, 0)
    return rank


MOBA_STREAMS = 2


def _moba_body(q_ref, k_ref, vt_ref, bias_ref, o_ref, kmean_ref, neg_ref, m_ref, l_ref, acc_ref):
    i = pl.program_id(2)
    nb = k_ref.shape[1] // TILE
    streams = range(MOBA_STREAMS)
    lanes_of = lambda s: slice(s * LANES, (s + 1) * LANES)

    @pl.when(i == 0)
    def _():
        for s in streams:
            for n in range(nb):
                kb = k_ref[0, n * TILE:(n + 1) * TILE, lanes_of(s)].astype(F32)
                kmean_ref[s, n:n + 1, :] = jnp.sum(kb, axis=0, keepdims=True) * (1.0 / TILE)

    lane = lax.broadcasted_iota(I32, (TILE, LANES), 1)
    q2s = []
    for s in streams:
        q = q_ref[0, :, lanes_of(s)] * SCALE
        zero = jnp.zeros_like(q)
        q2 = jnp.concatenate([jnp.where(lane < HEAD_DIM, q, zero),
                              jnp.where(lane >= HEAD_DIM, q, zero)], axis=0)
        km = kmean_ref[s]
        k_hi = km.astype(BF16)
        k_lo = (km - k_hi.astype(F32)).astype(BF16)
        gate = (lax.dot_general(k_hi, q2, _NT, preferred_element_type=F32)
                + lax.dot_general(k_lo, q2, _NT, preferred_element_type=F32))
        blk = lax.broadcasted_iota(I32, gate.shape, 0)
        gate = jnp.where(blk < i, gate, -jnp.inf)
        rank = _rank_before(gate, nb)
        neg_ref[s] = jnp.where((rank < MOBA_TOPK) & (blk < i), 0.0, NEG_INF)
        _init_state(m_ref.at[s], l_ref.at[s], acc_ref.at[s])
        q2s.append(q2)

    def attend_all(t, add_of):
        off = pl.multiple_of(t * TILE, TILE)
        for s in streams:
            _attend(q2s[s], k_ref[0, pl.ds(off, TILE), lanes_of(s)], vt_ref[0, t, lanes_of(s), :],
                    add_of(s), m_ref.at[s], l_ref.at[s], acc_ref.at[s])

    key = lax.broadcasted_iota(I32, (TILE, 2 * TILE), 0)
    qry = lax.broadcasted_iota(I32, (TILE, 2 * TILE), 1) & (TILE - 1)
    causal_neg = jnp.where(key <= qry, 0.0, NEG_INF)
    attend_all(i, lambda s: bias_ref[0, s] + causal_neg)

    @pl.when(i >= 1)
    def _():
        attend_all(i - 1, lambda s: bias_ref[1, s] + neg_ref[s, pl.ds(i - 1, 1), :])

    def far(n, carry):
        attend_all(n, lambda s: bias_ref[1, s, 0:1, :] + neg_ref[s, pl.ds(n, 1), :])
        return carry

    lax.fori_loop(0, jnp.maximum(i - 1, 0), far, 0)

    for s in streams:
        o = acc_ref[s] / l_ref[s]
        o = jnp.concatenate([o[:HEAD_DIM, :TILE], o[HEAD_DIM:, TILE:]], axis=0)
        o_ref[0, :, lanes_of(s)] = o.T.astype(o_ref.dtype)


def _moba_attention(qk, vt, bias):
    B, S, _ = qk.shape
    n_steps = N_HEADS // 2 // MOBA_STREAMS
    nq = S // TILE
    w = MOBA_STREAMS * LANES
    return pl.pallas_call(
        _moba_body,
        grid=(B, n_steps, nq),
        in_specs=[pl.BlockSpec((1, TILE, w), lambda b, p, i: (b, i, p)),
                  pl.BlockSpec((1, S, w), lambda b, p, i: (b, 0, n_steps + p)),
                  pl.BlockSpec((1, nq, w, TILE), lambda b, p, i: (b, 0, p, 0)),
                  pl.BlockSpec((2, MOBA_STREAMS, TILE, 2 * TILE), lambda b, p, i: (0, p, 0, 0))],
        out_specs=pl.BlockSpec((1, TILE, w), lambda b, p, i: (b, i, p)),
        out_shape=jax.ShapeDtypeStruct((B, S, N_HEADS * HEAD_DIM), BF16),
        scratch_shapes=[pltpu.VMEM((MOBA_STREAMS, nq, LANES), F32),
                        pltpu.VMEM((MOBA_STREAMS, nq, 2 * TILE), F32),
                        pltpu.VMEM((MOBA_STREAMS, 1, 2 * TILE), F32),
                        pltpu.VMEM((MOBA_STREAMS, 1, 2 * TILE), F32),
                        pltpu.VMEM((MOBA_STREAMS, LANES, 2 * TILE), F32)],
        compiler_params=_cparams("parallel", "parallel", "arbitrary"),
        name="moba_attention",
    )(qk, qk, vt, _heads_on_lanes(bias, 2))


def _gelu_tanh(x):
    return 0.5 * x * (1.0 + jnp.tanh(math.sqrt(2.0 / math.pi) * (x + 0.044715 * (x * x * x))))


def _compress_body(t_ref, pos_ref, w1_ref, w2_ref, o_ref):
    groups = t_ref.shape[2]
    half = t_ref.shape[3]
    t = t_ref[0].reshape(NSA_KV_HEADS * groups, half).astype(F32)
    first = jnp.dot((t + pos_ref[0:1, :]).astype(BF16), w1_ref[0:half, :], preferred_element_type=F32)
    second = jnp.dot((t + pos_ref[1:2, :]).astype(BF16), w1_ref[half:2 * half, :],
                     preferred_element_type=F32)
    rows = first.shape[0]
    pre = first + pltpu.roll(second, rows - 1, 0)
    out = jnp.dot(_gelu_tanh(pre).astype(BF16), w2_ref[...], preferred_element_type=F32)
    for h in range(NSA_KV_HEADS):
        o_ref[0, h] = out[h * groups:(h + 1) * groups].astype(o_ref.dtype)


def _compress(t, pos, w1, w2):
    B, Hkv, groups, half = t.shape
    return pl.pallas_call(
        _compress_body,
        grid=(B,),
        in_specs=[pl.BlockSpec((1, Hkv, groups, half), lambda b: (b, 0, 0, 0)),
                  pl.BlockSpec((2, half), lambda b: (0, 0)),
                  pl.BlockSpec((2 * half, CMP_HIDDEN), lambda b: (0, 0)),
                  pl.BlockSpec((CMP_HIDDEN, HEAD_DIM), lambda b: (0, 0))],
        out_specs=pl.BlockSpec((1, Hkv, groups, HEAD_DIM), lambda b: (b, 0, 0, 0)),
        out_shape=jax.ShapeDtypeStruct((B, Hkv, groups, HEAD_DIM), BF16),
        compiler_params=_cparams("parallel"),
        name="nsa_compress",
    )(t, pos.reshape(2, half), w1.astype(BF16), w2.astype(BF16))


def _swap_halves(x):
    return jnp.concatenate([x[:, HEAD_DIM:], x[:, :HEAD_DIM]], axis=1)


def _group_lanes(x):
    return jnp.concatenate([x] * NSA_GROUP, axis=1)


def _nsa_body(q_ref, kc_ref, vct_ref, ks_ref, vst_ref, kw_ref, vwt_ref, gt_ref, bias_ref, c2s_ref,
              o_ref, selneg_ref, oc_ref, os_ref, m_ref, l_ref, acc_ref):
    i = pl.program_id(2)
    n_cmp = kc_ref.shape[1]
    n_slc = c2s_ref.shape[0]
    per_tile = TILE // SLC_BLOCK
    cols = NSA_GROUP * TILE
    kv_heads = range(2)
    lane = lax.broadcasted_iota(I32, (TILE, LANES), 1)
    lo_half = lane < HEAD_DIM
    qpos = i * TILE + (lax.broadcasted_iota(I32, (n_cmp, cols), 1) & (TILE - 1))
    cmp_valid = CMP_STRIDE * lax.broadcasted_iota(I32, (n_cmp, cols), 0) + (CMP_LEN - 1) <= qpos
    key = lax.broadcasted_iota(I32, (TILE, TILE), 0)
    qry = lax.broadcasted_iota(I32, (TILE, TILE), 1)
    diag_neg = jnp.where(key <= qry, 0.0, NEG_INF)
    qall = q_ref[0] * SCALE
    q4s = []

    for a in kv_heads:
        keep = lo_half if a == 0 else jnp.logical_not(lo_half)
        heads = []
        for g in range(NSA_GROUP):
            cb = a * 2 + g // 2
            x = qall[:, cb * LANES:(cb + 1) * LANES]
            if g % 2 != a:
                x = _swap_halves(x)
            heads.append(jnp.where(keep, x, jnp.zeros_like(x)))
        q4 = jnp.concatenate(heads, axis=0)
        q4s.append(q4)

        s_c = lax.dot_general(kc_ref[0], q4, _NT, preferred_element_type=F32)
        s_c = jnp.where(cmp_valid, s_c, NEG_INF)
        m_c = jnp.max(s_c, axis=0, keepdims=True)
        e_c = jnp.where(cmp_valid, jnp.exp(s_c - m_c), 0.0)
        l_c = jnp.sum(e_c, axis=0, keepdims=True)
        p_c = e_c / jnp.where(l_c > 0.0, l_c, 1.0)
        oc_ref[a] = jnp.dot(vct_ref[0], p_c.astype(BF16), preferred_element_type=F32)

        p_sum = p_c[:, 0:TILE]
        for g in range(1, NSA_GROUP):
            p_sum = p_sum + p_c[:, g * TILE:(g + 1) * TILE]
        p_hi = p_sum.astype(BF16)
        p_lo = (p_sum - p_hi.astype(F32)).astype(BF16)
        imp = (jnp.dot(c2s_ref[...], p_hi, preferred_element_type=F32)
               + jnp.dot(c2s_ref[...], p_lo, preferred_element_type=F32))
        j = lax.broadcasted_iota(I32, imp.shape, 0)
        qb = (i * TILE + lax.broadcasted_iota(I32, imp.shape, 1)) >> int(math.log2(SLC_BLOCK))
        forced = (j == 0) | ((j <= qb) & (j > qb - SLC_LOCAL))
        imp = jnp.where(forced, jnp.inf, jnp.where(j > qb, -jnp.inf, imp))
        rank = _rank_before(imp, n_slc)
        selneg_ref[a] = jnp.where((rank < SLC_TOPN) & (j <= qb), 0.0, NEG_INF)

    def sel_neg(a, t):
        rows = [jnp.broadcast_to(selneg_ref[a, pl.ds(per_tile * t + c, 1), :], (SLC_BLOCK, TILE))
                for c in range(per_tile)]
        return jnp.concatenate(rows, axis=0)

    def attend_all(k_ref, vt_ref, t, add_of):
        off = pl.multiple_of(t * TILE, TILE)
        for a in kv_heads:
            _attend(q4s[a], k_ref[0, pl.ds(off, TILE), :], vt_ref[0, t], add_of(a),
                    m_ref.at[a], l_ref.at[a], acc_ref.at[a])

    def init_all():
        for a in kv_heads:
            _init_state(m_ref.at[a], l_ref.at[a], acc_ref.at[a])

    far_bias = lambda a: bias_ref[1, a, 0:1, :]

    init_all()
    attend_all(ks_ref, vst_ref, i, lambda a: bias_ref[0, a] + _group_lanes(sel_neg(a, i) + diag_neg))

    @pl.when(i >= 1)
    def _():
        attend_all(ks_ref, vst_ref, i - 1, lambda a: bias_ref[1, a] + _group_lanes(sel_neg(a, i - 1)))

    def far(t, carry):
        attend_all(ks_ref, vst_ref, t, lambda a: far_bias(a) + _group_lanes(sel_neg(a, t)))
        return carry

    lax.fori_loop(0, jnp.maximum(i - 1, 0), far, 0)
    for a in kv_heads:
        os_ref[a] = acc_ref[a] / l_ref[a]

    init_all()
    attend_all(kw_ref, vwt_ref, i, lambda a: bias_ref[0, a] + _group_lanes(diag_neg))

    @pl.when(i >= 1)
    def _():
        attend_all(kw_ref, vwt_ref, i - 1, lambda a: bias_ref[1, a])

    @pl.when(i >= 2)
    def _():
        edge_neg = _group_lanes(jnp.where(key > qry, 0.0, NEG_INF))
        attend_all(kw_ref, vwt_ref, i - 2, lambda a: far_bias(a) + edge_neg)

    gates = jax.nn.sigmoid(gt_ref[0, 0])
    pieces = []
    for a in kv_heads:
        o_w = acc_ref[a] / l_ref[a]
        rs = slice(a * HEAD_DIM, (a + 1) * HEAD_DIM)
        for g in range(NSA_GROUP):
            c0 = 3 * (NSA_GROUP * a + g)
            ls = slice(g * TILE, (g + 1) * TILE)
            pieces.append(gates[c0:c0 + 1, :] * oc_ref[a, rs, ls] + gates[c0 + 1:c0 + 2, :] * os_ref[a, rs, ls]
                          + gates[c0 + 2:c0 + 3, :] * o_w[rs, ls])
    o_ref[0] = jnp.concatenate(pieces, axis=0).T.astype(o_ref.dtype)


def _cmp_to_slc(S):
    n_cmp_pad = S // CMP_STRIDE
    n_slc = S // SLC_BLOCK
    ci = np.arange(n_cmp_pad)[:, None] * CMP_STRIDE
    sj = np.arange(n_slc)[None, :] * SLC_BLOCK
    c2s = ((ci < sj + SLC_BLOCK) & (ci + CMP_LEN > sj)).astype(np.float32)
    c2s[(S - CMP_LEN) // CMP_STRIDE + 1:] = 0.0
    return jnp.asarray(c2s.T, BF16)


def _nsa_attention(proj, vt, gate_t, kcmp, vcmp_t, bias):
    B, S, _ = proj.shape
    nq = S // TILE
    n_cmp = kcmp.shape[1]
    n_slc = S // SLC_BLOCK
    qw = 2 * NSA_GROUP * HEAD_DIM
    q_blocks = N_HEADS * HEAD_DIM // LANES
    kv_blocks = NSA_KV_HEADS * HEAD_DIM // LANES

    def k_spec(which):
        base = q_blocks + which * kv_blocks
        return pl.BlockSpec((1, S, LANES), lambda b, p, i: (b, 0, base + p))

    def vt_spec(which):
        base = which * kv_blocks
        return pl.BlockSpec((1, nq, LANES, TILE), lambda b, p, i: (b, 0, base + p, 0))

    state = pltpu.VMEM((2, LANES, NSA_GROUP * TILE), F32)
    stat = pltpu.VMEM((2, 1, NSA_GROUP * TILE), F32)
    return pl.pallas_call(
        _nsa_body,
        grid=(B, 2, nq),
        in_specs=[pl.BlockSpec((1, TILE, qw), lambda b, p, i: (b, i, p)),
                  pl.BlockSpec((1, n_cmp, LANES), lambda b, p, i: (b, 0, p)),
                  pl.BlockSpec((1, LANES, n_cmp), lambda b, p, i: (b, p, 0)),
                  k_spec(2), vt_spec(0), k_spec(3), vt_spec(1),
                  pl.BlockSpec((1, 1, LANES, TILE), lambda b, p, i: (b, i, p, 0)),
                  pl.BlockSpec((2, 2, TILE, NSA_GROUP * TILE), lambda b, p, i: (0, p, 0, 0)),
                  pl.BlockSpec((n_slc, n_cmp), lambda b, p, i: (0, 0))],
        out_specs=pl.BlockSpec((1, TILE, qw), lambda b, p, i: (b, i, p)),
        out_shape=jax.ShapeDtypeStruct((B, S, N_HEADS * HEAD_DIM), BF16),
        scratch_shapes=[pltpu.VMEM((2, n_slc, TILE), F32), state, state, stat, stat, state],
        compiler_params=_cparams("parallel", "parallel", "arbitrary"),
        name="nsa_attention",
    )(proj, kcmp, vcmp_t, proj, vt, proj, vt, gate_t, _heads_on_lanes(bias, NSA_GROUP), _cmp_to_slc(S))


def _split_bf16(x):
    hi = x.astype(BF16)
    return hi, (x - hi.astype(F32)).astype(BF16)


ROUTER_TM = 1024
_ROW_OF_EXPERT = np.arange(N_EXPERTS).reshape(N_GROUPS, EXPERTS_PER_GROUP).T.reshape(-1)


def _router_body(x_ref, w_ref, b_ref, tri_ref, idx_ref, wt_ref, pos_ref, cnt_ref, base_ref):
    @pl.when(pl.program_id(0) == 0)
    def _():
        base_ref[...] = jnp.zeros(base_ref.shape, F32)

    x_hi, x_lo = _split_bf16(x_ref[...])
    w_hi, w_lo = _split_bf16(w_ref[...])
    logits = (lax.dot_general(w_hi, x_hi, _NT, preferred_element_type=F32)
              + lax.dot_general(w_hi, x_lo, _NT, preferred_element_type=F32)
              + lax.dot_general(w_lo, x_hi, _NT, preferred_element_type=F32)) + b_ref[:, 0:1]
    m = jnp.max(logits, axis=0, keepdims=True)
    e = jnp.exp(logits - m)
    probs = e / jnp.sum(e, axis=0, keepdims=True)
    pk = [probs[k * N_GROUPS:(k + 1) * N_GROUPS] for k in range(EXPERTS_PER_GROUP)]
    hi1, lo1 = jnp.maximum(pk[0], pk[1]), jnp.minimum(pk[0], pk[1])
    hi2, lo2 = jnp.maximum(pk[2], pk[3]), jnp.minimum(pk[2], pk[3])
    score = jnp.maximum(hi1, hi2) + jnp.maximum(jnp.minimum(hi1, hi2), jnp.maximum(lo1, lo2))
    grp = lax.broadcasted_iota(I32, score.shape, 0)
    best = jnp.min(jnp.where(score == jnp.max(score, axis=0, keepdims=True), grp, N_GROUPS),
                   axis=0, keepdims=True)
    v = [jnp.sum(jnp.where(grp == best, p, 0.0), axis=0, keepdims=True) for p in pk]
    v1 = jnp.maximum(jnp.maximum(v[0], v[1]), jnp.maximum(v[2], v[3]))
    i1 = jnp.where(v[0] == v1, 0, jnp.where(v[1] == v1, 1, jnp.where(v[2] == v1, 2, 3)))
    rest = [jnp.where(i1 == k, -1.0, v[k]) for k in range(EXPERTS_PER_GROUP)]
    v2 = jnp.maximum(jnp.maximum(rest[0], rest[1]), jnp.maximum(rest[2], rest[3]))
    i2 = jnp.where(rest[0] == v2, 0, jnp.where(rest[1] == v2, 1, jnp.where(rest[2] == v2, 2, 3)))
    tot = v1 + v2
    idx_ref[...] = jnp.concatenate([best * EXPERTS_PER_GROUP + i1, best * EXPERTS_PER_GROUP + i2], axis=0)
    wt_ref[...] = jnp.concatenate([v1 / tot, v2 / tot], axis=0)

    row = lax.broadcasted_iota(I32, logits.shape, 0)
    hot = [jnp.where(row == ik * N_GROUPS + best, 1.0, 0.0) for ik in (i1, i2)]
    both = (hot[0] + hot[1]).astype(BF16)
    run = base_ref[:, 0:1]
    pos = [[], []]
    for c in range(logits.shape[1] // LANES):
        ls = slice(c * LANES, (c + 1) * LANES)
        before = run + jnp.dot(both[:, ls], tri_ref[...], preferred_element_type=F32) - 1.0
        for k in range(2):
            pos[k].append(jnp.sum(hot[k][:, ls] * before, axis=0, keepdims=True))
        run = before[:, LANES - 1:LANES] + 1.0
    pos_ref[...] = jnp.concatenate([jnp.concatenate(pos[0], axis=1), jnp.concatenate(pos[1], axis=1)],
                                   axis=0).astype(I32)
    base_ref[...] = jnp.broadcast_to(run, base_ref.shape)
    cnt_ref[...] = jnp.broadcast_to(run, cnt_ref.shape)


def _router(x, router_w, router_b):
    N, D = x.shape
    w = router_w.T[_ROW_OF_EXPERT]
    b = jnp.broadcast_to(router_b[_ROW_OF_EXPERT][:, None], (N_EXPERTS, LANES))
    tri = jnp.asarray(np.triu(np.ones((LANES, LANES), np.float32)), BF16)
    tm = ROUTER_TM
    tok = lambda dt: jax.ShapeDtypeStruct((2, N), dt)
    tok_spec = pl.BlockSpec((2, tm), lambda i: (0, i))
    idx, wts, pos, cnt = pl.pallas_call(
        _router_body,
        grid=(N // tm,),
        in_specs=[pl.BlockSpec((tm, D), lambda i: (i, 0)),
                  pl.BlockSpec((N_EXPERTS, D), lambda i: (0, 0)),
                  pl.BlockSpec((N_EXPERTS, LANES), lambda i: (0, 0)),
                  pl.BlockSpec((LANES, LANES), lambda i: (0, 0))],
        out_specs=[tok_spec, tok_spec, tok_spec, pl.BlockSpec((N_EXPERTS, LANES), lambda i: (0, 0))],
        out_shape=[tok(I32), tok(F32), tok(I32), jax.ShapeDtypeStruct((N_EXPERTS, LANES), F32)],
        scratch_shapes=[pltpu.VMEM((N_EXPERTS, LANES), F32)],
        compiler_params=_cparams("arbitrary"),
        name="moe_router",
    )(x, w, b, tri)
    counts = cnt[np.argsort(_ROW_OF_EXPERT), 0].astype(I32)
    return idx, wts, pos, counts


def _expert_body(blk_e_ref, n_used_ref, x_ref, wg_ref, wu_ref, wd_ref, o_ref, wg_b, wu_b, wd_b):
    i = pl.program_id(0)

    @pl.when((i == 0) | (blk_e_ref[i] != blk_e_ref[jnp.maximum(i - 1, 0)]))
    def _():
        wg_b[...] = wg_ref[0].astype(BF16)
        wu_b[...] = wu_ref[0].astype(BF16)
        wd_b[...] = wd_ref[0].astype(BF16)

    @pl.when(i < n_used_ref[0])
    def _():
        x = x_ref[...]
        gate = jnp.dot(x, wg_b[...], preferred_element_type=F32)
        up = jnp.dot(x, wu_b[...], preferred_element_type=F32)
        hid = (gate * jax.nn.sigmoid(gate) * up).astype(BF16)
        o_ref[...] = jnp.dot(hid, wd_b[...], preferred_element_type=F32)

    @pl.when(i >= n_used_ref[0])
    def _():
        o_ref[...] = jnp.zeros(o_ref.shape, o_ref.dtype)


def _experts(xs, blk_e, n_used, wg, wu, wd):
    R, D = xs.shape
    n_blk = R // MOE_TB

    def live(i, be, nu):
        return jnp.minimum(i, nu[0] - 1)

    grid_spec = pltpu.PrefetchScalarGridSpec(
        num_scalar_prefetch=2,
        grid=(n_blk,),
        in_specs=[pl.BlockSpec((MOE_TB, D), lambda i, be, nu: (live(i, be, nu), 0)),
                  pl.BlockSpec((1, D, D_EXPERT), lambda i, be, nu: (be[i], 0, 0)),
                  pl.BlockSpec((1, D, D_EXPERT), lambda i, be, nu: (be[i], 0, 0)),
                  pl.BlockSpec((1, D_EXPERT, D), lambda i, be, nu: (be[i], 0, 0))],
        out_specs=pl.BlockSpec((MOE_TB, D), lambda i, be, nu: (i, 0)),
        scratch_shapes=[pltpu.VMEM((D, D_EXPERT), BF16), pltpu.VMEM((D, D_EXPERT), BF16),
                        pltpu.VMEM((D_EXPERT, D), BF16)],
    )
    return pl.pallas_call(
        _expert_body,
        grid_spec=grid_spec,
        out_shape=jax.ShapeDtypeStruct((R, D), F32),
        compiler_params=_cparams("arbitrary"),
        name="moe_experts",
    )(blk_e, n_used, xs, wg, wu, wd)


def _combine_ln_body(x_ref, y0_ref, y1_ref, wt_ref, g_ref, b_ref, o_ref, ob_ref):
    ffn = y0_ref[...] * wt_ref[:, 0:1] + y1_ref[...] * wt_ref[:, HEAD_DIM:HEAD_DIM + 1]
    out = _layer_norm_rows(DEEPNORM_ALPHA * x_ref[...] + ffn, g_ref[...], b_ref[...])
    o_ref[...] = out
    ob_ref[...] = out.astype(BF16)


def _combine_ln(x, y0, y1, wt, g, b):
    M, D = x.shape
    row = pl.BlockSpec((LN_TM, D), lambda i: (i, 0))
    vec = pl.BlockSpec((1, D), lambda i: (0, 0))
    return pl.pallas_call(
        _combine_ln_body,
        grid=(M // LN_TM,),
        in_specs=[row, row, row, pl.BlockSpec((LN_TM, LANES), lambda i: (i, 0)), vec, vec],
        out_specs=[row, row],
        out_shape=[jax.ShapeDtypeStruct((M, D), F32), jax.ShapeDtypeStruct((M, D), BF16)],
        compiler_params=_cparams("parallel"),
        name="moe_combine_ln",
    )(x, y0, y1, wt, g.reshape(1, D), b.reshape(1, D))


def _moe_ln(h, hb, router_w, router_b, wg, wu, wd, g, b):
    N, D = h.shape
    A = 2 * N
    idx, wts, pos, counts = _router(h, router_w, router_b)
    starts = jnp.cumsum(counts) - counts
    padded = (counts + MOE_TB - 1) // MOE_TB * MOE_TB
    pends = jnp.cumsum(padded)
    pstarts = pends - padded
    R = A + N_EXPERTS * MOE_TB
    n_blk = R // MOE_TB
    experts = jnp.arange(N_EXPERTS, dtype=I32)
    dest = pos + jnp.sum(jnp.where(idx[None] == experts[:, None, None], pstarts[:, None, None], 0), axis=0)
    tok = jnp.broadcast_to(jnp.arange(N, dtype=I32)[None, :], (2, N))
    _, tok_sorted = lax.sort_key_val(dest.reshape(A), tok.reshape(A))
    blk_row0 = jnp.arange(n_blk, dtype=I32) * MOE_TB
    blk_e = jnp.minimum(jnp.sum((pends[None, :] <= blk_row0[:, None]).astype(I32), axis=1), N_EXPERTS - 1)
    hot = blk_e[:, None] == experts[None, :]
    compact0 = blk_row0 + jnp.sum(jnp.where(hot, (starts - pstarts)[None, :], 0), axis=1)
    compact = jnp.clip(compact0[:, None] + jnp.arange(MOE_TB, dtype=I32)[None, :], 0, A - 1).reshape(R)
    n_used = (pends[-1:] // MOE_TB).astype(I32)
    xs = hb[tok_sorted[compact]]
    yb = _experts(xs, blk_e, n_used, wg, wu, wd)
    wt = jnp.concatenate([jnp.broadcast_to(wts[k][:, None], (N, HEAD_DIM)) for k in range(2)], axis=1)
    return _combine_ln(h, yb[dest[0]], yb[dest[1]], wt, g, b)


def _moba_layer(h, w_in, w_out, bias, g, b, B, S):
    HD = N_HEADS * HEAD_DIM
    qk = _matmul(h, w_in[:, :2 * HD].astype(BF16), BF16).reshape(B, S, 2 * HD)
    vt = _matmul_t(w_in[:, 2 * HD:].T.astype(BF16), h, B, S, BF16)
    att = _moba_attention(qk, vt, bias)
    return _proj_ln(att.reshape(B * S, HD), w_out.astype(BF16), h, g, b)


def _nsa_layer(h, hb, w_in, w_out, pos_k, pos_v, ck_w1, ck_w2, cv_w1, cv_w2, bias, g, b, B, S):
    HD = N_HEADS * HEAD_DIM
    kvw = NSA_KV_HEADS * HEAD_DIM
    col = lambda k: slice(HD + k * kvw, HD + (k + 1) * kvw)
    w_rows = jnp.concatenate([w_in[:, :HD + 2 * kvw], w_in[:, col(2)], w_in[:, col(4)]], axis=1)
    proj = _matmul(hb, w_rows.astype(BF16), BF16).reshape(B, S, HD + 4 * kvw)
    w_vt = jnp.concatenate([w_in[:, col(3)], w_in[:, col(5)]], axis=1).T
    vt = _matmul_t(w_vt.astype(BF16), hb, B, S, BF16)
    per_pair = 3 * N_HEADS // 2
    wg = w_in[:, HD + 6 * kvw:].reshape(D_MODEL, 2, per_pair)
    wg = jnp.pad(wg, ((0, 0), (0, 0), (0, LANES - per_pair))).reshape(D_MODEL, 2 * LANES).T
    gate_t = _matmul_t(wg.astype(BF16), hb, B, S, F32)

    def grouped(t):
        t = t.reshape(B, S, NSA_KV_HEADS, HEAD_DIM).transpose(0, 2, 1, 3)
        return t.reshape(B, NSA_KV_HEADS, S // CMP_STRIDE, CMP_STRIDE * HEAD_DIM)

    kcmp = _compress(grouped(proj[..., HD:HD + kvw]), pos_k, ck_w1, ck_w2)
    vcmp = _compress(grouped(proj[..., HD + kvw:HD + 2 * kvw]), pos_v, cv_w1, cv_w2)
    n_cmp = kcmp.shape[2]
    kcmp = kcmp.transpose(0, 2, 1, 3).reshape(B, n_cmp, kvw)
    vcmp_t = vcmp.transpose(0, 1, 3, 2).reshape(B, kvw, n_cmp)
    att = _nsa_attention(proj, vt, gate_t, kcmp, vcmp_t, bias)
    return _proj_ln(att.reshape(B * S, HD), w_out.astype(BF16), h, g, b)


def kernel(x, rel_bias, router_w, router_b, ln_g, ln_b, moba_w_in, moba_w_out, nsa_w_in, nsa_w_out,
           nsa_pos_k, nsa_pos_v, nsa_ck_w1, nsa_ck_w2, nsa_cv_w1, nsa_cv_w2,
           moe_w_gate, moe_w_up, moe_w_down):
    B, S, D = x.shape
    bias = _bias_tiles(rel_bias)
    h = x.reshape(B * S, D)
    h, hb = _moba_layer(h, moba_w_in[0], moba_w_out[0], bias, ln_g[0, 0], ln_b[0, 0], B, S)
    h, hb = _moe_ln(h, hb, router_w, router_b, moe_w_gate[0], moe_w_up[0], moe_w_down[0],
                    ln_g[0, 1], ln_b[0, 1])
    h, hb = _nsa_layer(h, hb, nsa_w_in[0], nsa_w_out[0], nsa_pos_k[0], nsa_pos_v[0],
                       nsa_ck_w1[0], nsa_ck_w2[0], nsa_cv_w1[0], nsa_cv_w2[0],
                       bias, ln_g[1, 0], ln_b[1, 0], B, S)
    h, hb = _moe_ln(h, hb, router_w, router_b, moe_w_gate[1], moe_w_up[1], moe_w_down[1],
                    ln_g[1, 1], ln_b[1, 1])
    return h.reshape(B, S, D)
```

```python
import math

import numpy as np
import jax
import jax.numpy as jnp
from jax import lax
from jax.experimental import pallas as pl
from jax.experimental.pallas import tpu as pltpu

F32, BF16, I32 = jnp.float32, jnp.bfloat16, jnp.int32

D_MODEL = 1024
N_HEADS = 16
HEAD_DIM = 64
DEPTH = 2
NEG_INF = -1e30
LN_EPS = 1e-5
MOBA_BLOCK = 256
MOBA_TOPK = 3
NSA_KV_HEADS = 4
NSA_GROUP = N_HEADS // NSA_KV_HEADS
CMP_LEN = 32
CMP_STRIDE = 16
CMP_HIDDEN = 256
SLC_BLOCK = 64
SLC_TOPN = 16
SLC_LOCAL = 2
WINDOW = 512
REL_BUCKETS = 32
REL_MAX_DIST = 128
N_EXPERTS = 32
N_GROUPS = 8
EXPERTS_PER_GROUP = N_EXPERTS // N_GROUPS
D_EXPERT = 512
DEEPNORM_ALPHA = (2 * DEPTH) ** 0.25
SCALE = HEAD_DIM ** -0.5

LANES = 128
TILE = 256
MM_TM = 1024
MM_TN = 1024
LN_TM = 512
MOE_TB = 256
VMEM_LIMIT = 48 * 1024 * 1024

_NT = (((1,), (1,)), ((), ()))


def _cparams(*sem):
    return pltpu.CompilerParams(dimension_semantics=sem, vmem_limit_bytes=VMEM_LIMIT)


def _mm_body(a_ref, b_ref, o_ref):
    o_ref[...] = jnp.dot(a_ref[...].astype(BF16), b_ref[...],
                         preferred_element_type=F32).astype(o_ref.dtype)


def _matmul(a, b, out_dtype):
    M, K = a.shape
    N = b.shape[1]
    tn = min(MM_TN, N)
    return pl.pallas_call(
        _mm_body,
        grid=(M // MM_TM, N // tn),
        in_specs=[pl.BlockSpec((MM_TM, K), lambda i, j: (i, 0)),
                  pl.BlockSpec((K, tn), lambda i, j: (0, j))],
        out_specs=pl.BlockSpec((MM_TM, tn), lambda i, j: (i, j)),
        out_shape=jax.ShapeDtypeStruct((M, N), out_dtype),
        compiler_params=_cparams("parallel", "arbitrary"),
        name="in_proj",
    )(a, b)


def _mm_t_body(w_ref, a_ref, o_ref):
    r = lax.dot_general(w_ref[...], a_ref[...].astype(BF16), _NT, preferred_element_type=F32)
    for t in range(o_ref.shape[1]):
        o_ref[0, t] = r[:, t * TILE:(t + 1) * TILE].astype(o_ref.dtype)


def _matmul_t(w_t, a, B, S, out_dtype):
    Nout, K = w_t.shape
    tn = min(MM_TN, Nout)
    per_seq = S // MM_TM
    sub = MM_TM // TILE
    return pl.pallas_call(
        _mm_t_body,
        grid=(B * per_seq, Nout // tn),
        in_specs=[pl.BlockSpec((tn, K), lambda i, j: (j, 0)),
                  pl.BlockSpec((MM_TM, K), lambda i, j: (i, 0))],
        out_specs=pl.BlockSpec((1, sub, tn, TILE), lambda i, j: (i // per_seq, i % per_seq, j, 0)),
        out_shape=jax.ShapeDtypeStruct((B, S // TILE, Nout, TILE), out_dtype),
        compiler_params=_cparams("parallel", "arbitrary"),
        name="in_proj_t",
    )(w_t, a)


def _layer_norm_rows(z, g, b):
    mu = jnp.mean(z, axis=-1, keepdims=True)
    zc = z - mu
    var = jnp.mean(zc * zc, axis=-1, keepdims=True)
    return zc * lax.rsqrt(var + LN_EPS) * g + b


def _proj_ln_body(a_ref, w_ref, x_ref, g_ref, b_ref, o_ref, ob_ref):
    y = jnp.dot(a_ref[...], w_ref[...], preferred_element_type=F32)
    out = _layer_norm_rows(DEEPNORM_ALPHA * x_ref[...] + y, g_ref[...], b_ref[...])
    o_ref[...] = out
    ob_ref[...] = out.astype(BF16)


def _proj_ln(a, w, x, g, b):
    M, K = a.shape
    D = w.shape[1]
    row = pl.BlockSpec((LN_TM, D), lambda i: (i, 0))
    vec = pl.BlockSpec((1, D), lambda i: (0, 0))
    return pl.pallas_call(
        _proj_ln_body,
        grid=(M // LN_TM,),
        in_specs=[pl.BlockSpec((LN_TM, K), lambda i: (i, 0)),
                  pl.BlockSpec((K, D), lambda i: (0, 0)), row, vec, vec],
        out_specs=[row, row],
        out_shape=[jax.ShapeDtypeStruct((M, D), F32), jax.ShapeDtypeStruct((M, D), BF16)],
        compiler_params=_cparams("parallel"),
        name="out_proj_ln",
    )(a, w, x, g.reshape(1, D), b.reshape(1, D))


def _t5_bucket_np(rel):
    n = np.maximum(rel, 0)
    max_exact = REL_BUCKETS // 2
    nf = np.maximum(n, 1).astype(np.float32)
    large = max_exact + (np.log(nf / np.float32(max_exact))
                         / np.float32(math.log(REL_MAX_DIST / max_exact))
                         * np.float32(REL_BUCKETS - max_exact)).astype(np.int32)
    large = np.minimum(large, REL_BUCKETS - 1)
    return np.where(n < max_exact, n, large).astype(np.int32)


def _bias_body(tbl_ref, bk_ref, o_ref):
    h = pl.program_id(0)
    for dl in range(2):
        bk = bk_ref[dl]
        acc = jnp.zeros((TILE, TILE), F32)
        for b in range(REL_BUCKETS):
            acc = jnp.where(bk == b, tbl_ref[h * REL_BUCKETS + b], acc)
        o_ref[dl, 0] = acc


def _bias_tiles(rel_bias):
    key = np.arange(TILE)[:, None]
    qry = np.arange(TILE)[None, :]
    assert int(_t5_bucket_np(np.array(TILE + 1))) == REL_BUCKETS - 1
    bk = np.stack([_t5_bucket_np(qry - key), _t5_bucket_np(TILE + qry - key)])
    return pl.pallas_call(
        _bias_body,
        grid=(N_HEADS,),
        in_specs=[pl.BlockSpec(memory_space=pltpu.SMEM),
                  pl.BlockSpec((2, TILE, TILE), lambda h: (0, 0, 0))],
        out_specs=pl.BlockSpec((2, 1, TILE, TILE), lambda h: (0, h, 0, 0)),
        out_shape=jax.ShapeDtypeStruct((2, N_HEADS, TILE, TILE), F32),
        name="t5_bias_tiles",
    )(rel_bias.T.reshape(-1), jnp.asarray(bk))


def _heads_on_lanes(bias, per_block):
    two, H, T, _ = bias.shape
    b = bias.reshape(two, H // per_block, per_block, T, T).transpose(0, 1, 3, 2, 4)
    return b.reshape(two, H // per_block, T, per_block * T)


def _init_state(m_ref, l_ref, acc_ref):
    m_ref[...] = jnp.full(m_ref.shape, NEG_INF, F32)
    l_ref[...] = jnp.zeros(l_ref.shape, F32)
    acc_ref[...] = jnp.zeros(acc_ref.shape, F32)


def _attend(q, tiles, m_ref, l_ref, acc_ref):
    scores = [lax.dot_general(k, q, _NT, preferred_element_type=F32) + add for k, _, add in tiles]
    m_prev = m_ref[...]
    m_new = m_prev
    for s in scores:
        m_new = jnp.maximum(m_new, jnp.max(s, axis=0, keepdims=True))
    a = jnp.exp(m_prev - m_new)
    probs = [jnp.exp(s - m_new) for s in scores]
    l_new = a * l_ref[...]
    for p in probs:
        l_new = l_new + jnp.sum(p, axis=0, keepdims=True)
    l_ref[...] = l_new
    vt = jnp.concatenate([v for _, v, _ in tiles], axis=1)
    pv = jnp.dot(vt, jnp.concatenate([p.astype(BF16) for p in probs], axis=0), preferred_element_type=F32)
    acc_ref[...] = a * acc_ref[...] + pv
    m_ref[...] = m_new


def _rank_before(vals, rows):
    idx = lax.broadcasted_iota(I32, vals.shape, 0)
    rank = jnp.zeros(vals.shape, I32)
    for m in range(rows):
        row = vals[m:m + 1, :]
        beats = (row > vals) | ((row == vals) & (idx > m))
        rank = rank + jnp.where(beats, 1, 0)
    return rank


MOBA_STREAMS = 2


def _moba_body(q_ref, k_ref, vt_ref, bias_ref, o_ref, kmean_ref, neg_ref, m_ref, l_ref, acc_ref):
    i = pl.program_id(2)
    nb = k_ref.shape[1] // TILE
    streams = range(MOBA_STREAMS)
    lanes_of = lambda s: slice(s * LANES, (s + 1) * LANES)

    @pl.when(i == 0)
    def _():
        for s in streams:
            for n in range(nb):
                kb = k_ref[0, n * TILE:(n + 1) * TILE, lanes_of(s)].astype(F32)
                kmean_ref[s, n:n + 1, :] = jnp.sum(kb, axis=0, keepdims=True) * (1.0 / TILE)

    lane = lax.broadcasted_iota(I32, (TILE, LANES), 1)
    q2s = []
    for s in streams:
        q = q_ref[0, :, lanes_of(s)] * SCALE
        zero = jnp.zeros_like(q)
        q2 = jnp.concatenate([jnp.where(lane < HEAD_DIM, q, zero),
                              jnp.where(lane >= HEAD_DIM, q, zero)], axis=0)
        km = kmean_ref[s]
        k_hi = km.astype(BF16)
        k_lo = (km - k_hi.astype(F32)).astype(BF16)
        gate = (lax.dot_general(k_hi, q2, _NT, preferred_element_type=F32)
                + lax.dot_general(k_lo, q2, _NT, preferred_element_type=F32))
        blk = lax.broadcasted_iota(I32, gate.shape, 0)
        gate = jnp.where(blk < i, gate, -jnp.inf)
        rank = _rank_before(gate, nb)
        neg_ref[s] = jnp.where((rank < MOBA_TOPK) & (blk < i), 0.0, NEG_INF)
        _init_state(m_ref.at[s], l_ref.at[s], acc_ref.at[s])
        q2s.append(q2)

    def attend_all(group):
        for s in streams:
            tiles = [(k_ref[0, pl.ds(pl.multiple_of(t * TILE, TILE), TILE), lanes_of(s)],
                      vt_ref[0, t, lanes_of(s), :], add_of(s)) for t, add_of in group]
            _attend(q2s[s], tiles, m_ref.at[s], l_ref.at[s], acc_ref.at[s])

    key = lax.broadcasted_iota(I32, (TILE, 2 * TILE), 0)
    qry = lax.broadcasted_iota(I32, (TILE, 2 * TILE), 1) & (TILE - 1)
    causal_neg = jnp.where(key <= qry, 0.0, NEG_INF)
    own = (i, lambda s: bias_ref[0, s] + causal_neg)
    near = (i - 1, lambda s: bias_ref[1, s] + neg_ref[s, pl.ds(i - 1, 1), :])
    far = lambda n: (n, lambda s: bias_ref[1, s, 0:1, :] + neg_ref[s, pl.ds(n, 1), :])

    @pl.when(i == 0)
    def _():
        attend_all([own])

    @pl.when(i >= 1)
    def _():
        attend_all([own, near])

    n_far = jnp.maximum(i - 1, 0)

    def far_pair(j, carry):
        attend_all([far(2 * j), far(2 * j + 1)])
        return carry

    lax.fori_loop(0, n_far >> 1, far_pair, 0)

    @pl.when((n_far & 1) == 1)
    def _():
        attend_all([far(n_far - 1)])

    for s in streams:
        o = acc_ref[s] / l_ref[s]
        o = jnp.concatenate([o[:HEAD_DIM, :TILE], o[HEAD_DIM:, TILE:]], axis=0)
        o_ref[0, :, lanes_of(s)] = o.T.astype(o_ref.dtype)


def _moba_attention(qk, vt, bias):
    B, S, _ = qk.shape
    n_steps = N_HEADS // 2 // MOBA_STREAMS
    nq = S // TILE
    w = MOBA_STREAMS * LANES
    return pl.pallas_call(
        _moba_body,
        grid=(B, n_steps, nq),
        in_specs=[pl.BlockSpec((1, TILE, w), lambda b, p, i: (b, i, p)),
                  pl.BlockSpec((1, S, w), lambda b, p, i: (b, 0, n_steps + p)),
                  pl.BlockSpec((1, nq, w, TILE), lambda b, p, i: (b, 0, p, 0)),
                  pl.BlockSpec((2, MOBA_STREAMS, TILE, 2 * TILE), lambda b, p, i: (0, p, 0, 0))],
        out_specs=pl.BlockSpec((1, TILE, w), lambda b, p, i: (b, i, p)),
        out_shape=jax.ShapeDtypeStruct((B, S, N_HEADS * HEAD_DIM), BF16),
        scratch_shapes=[pltpu.VMEM((MOBA_STREAMS, nq, LANES), F32),
                        pltpu.VMEM((MOBA_STREAMS, nq, 2 * TILE), F32),
                        pltpu.VMEM((MOBA_STREAMS, 1, 2 * TILE), F32),
                        pltpu.VMEM((MOBA_STREAMS, 1, 2 * TILE), F32),
                        pltpu.VMEM((MOBA_STREAMS, LANES, 2 * TILE), F32)],
        compiler_params=_cparams("parallel", "parallel", "arbitrary"),
        name="moba_attention",
    )(qk, qk, vt, _heads_on_lanes(bias, 2))


def _gelu_tanh(x):
    return 0.5 * x * (1.0 + jnp.tanh(math.sqrt(2.0 / math.pi) * (x + 0.044715 * (x * x * x))))


def _compress_body(t_ref, pos_ref, w1_ref, w2_ref, o_ref):
    groups = t_ref.shape[2]
    half = t_ref.shape[3]
    t = t_ref[0].reshape(NSA_KV_HEADS * groups, half).astype(F32)
    first = jnp.dot((t + pos_ref[0:1, :]).astype(BF16), w1_ref[0:half, :], preferred_element_type=F32)
    second = jnp.dot((t + pos_ref[1:2, :]).astype(BF16), w1_ref[half:2 * half, :],
                     preferred_element_type=F32)
    rows = first.shape[0]
    pre = first + pltpu.roll(second, rows - 1, 0)
    out = jnp.dot(_gelu_tanh(pre).astype(BF16), w2_ref[...], preferred_element_type=F32)
    for h in range(NSA_KV_HEADS):
        o_ref[0, h] = out[h * groups:(h + 1) * groups].astype(o_ref.dtype)


def _compress(t, pos, w1, w2):
    B, Hkv, groups, half = t.shape
    return pl.pallas_call(
        _compress_body,
        grid=(B,),
        in_specs=[pl.BlockSpec((1, Hkv, groups, half), lambda b: (b, 0, 0, 0)),
                  pl.BlockSpec((2, half), lambda b: (0, 0)),
                  pl.BlockSpec((2 * half, CMP_HIDDEN), lambda b: (0, 0)),
                  pl.BlockSpec((CMP_HIDDEN, HEAD_DIM), lambda b: (0, 0))],
        out_specs=pl.BlockSpec((1, Hkv, groups, HEAD_DIM), lambda b: (b, 0, 0, 0)),
        out_shape=jax.ShapeDtypeStruct((B, Hkv, groups, HEAD_DIM), BF16),
        compiler_params=_cparams("parallel"),
        name="nsa_compress",
    )(t, pos.reshape(2, half), w1.astype(BF16), w2.astype(BF16))


def _swap_halves(x):
    return jnp.concatenate([x[:, HEAD_DIM:], x[:, :HEAD_DIM]], axis=1)


def _group_lanes(x):
    return jnp.concatenate([x] * NSA_GROUP, axis=1)


def _nsa_body(q_ref, kc_ref, vct_ref, ks_ref, vst_ref, kw_ref, vwt_ref, gt_ref, bias_ref, c2s_ref,
              o_ref, selneg_ref, oc_ref, os_ref, m_ref, l_ref, acc_ref):
    i = pl.program_id(2)
    n_cmp = kc_ref.shape[1]
    n_slc = c2s_ref.shape[0]
    per_tile = TILE // SLC_BLOCK
    cols = NSA_GROUP * TILE
    kv_heads = range(2)
    lane = lax.broadcasted_iota(I32, (TILE, LANES), 1)
    lo_half = lane < HEAD_DIM
    qpos = i * TILE + (lax.broadcasted_iota(I32, (n_cmp, cols), 1) & (TILE - 1))
    cmp_valid = CMP_STRIDE * lax.broadcasted_iota(I32, (n_cmp, cols), 0) + (CMP_LEN - 1) <= qpos
    key = lax.broadcasted_iota(I32, (TILE, TILE), 0)
    qry = lax.broadcasted_iota(I32, (TILE, TILE), 1)
    diag_neg = jnp.where(key <= qry, 0.0, NEG_INF)
    qall = q_ref[0] * SCALE
    q4s = []

    for a in kv_heads:
        keep = lo_half if a == 0 else jnp.logical_not(lo_half)
        heads = []
        for g in range(NSA_GROUP):
            cb = a * 2 + g // 2
            x = qall[:, cb * LANES:(cb + 1) * LANES]
            if g % 2 != a:
                x = _swap_halves(x)
            heads.append(jnp.where(keep, x, jnp.zeros_like(x)))
        q4 = jnp.concatenate(heads, axis=0)
        q4s.append(q4)

        s_c = lax.dot_general(kc_ref[0], q4, _NT, preferred_element_type=F32)
        s_c = jnp.where(cmp_valid, s_c, NEG_INF)
        m_c = jnp.max(s_c, axis=0, keepdims=True)
        e_c = jnp.where(cmp_valid, jnp.exp(s_c - m_c), 0.0)
        l_c = jnp.sum(e_c, axis=0, keepdims=True)
        p_c = e_c / jnp.where(l_c > 0.0, l_c, 1.0)
        oc_ref[a] = jnp.dot(vct_ref[0], p_c.astype(BF16), preferred_element_type=F32)

        p_sum = p_c[:, 0:TILE]
        for g in range(1, NSA_GROUP):
            p_sum = p_sum + p_c[:, g * TILE:(g + 1) * TILE]
        p_hi = p_sum.astype(BF16)
        p_lo = (p_sum - p_hi.astype(F32)).astype(BF16)
        imp = (jnp.dot(c2s_ref[...], p_hi, preferred_element_type=F32)
               + jnp.dot(c2s_ref[...], p_lo, preferred_element_type=F32))
        j = lax.broadcasted_iota(I32, imp.shape, 0)
        qb = (i * TILE + lax.broadcasted_iota(I32, imp.shape, 1)) >> int(math.log2(SLC_BLOCK))
        forced = (j == 0) | ((j <= qb) & (j > qb - SLC_LOCAL))
        imp = jnp.where(forced, jnp.inf, jnp.where(j > qb, -jnp.inf, imp))
        rank = _rank_before(imp, n_slc)
        selneg_ref[a] = jnp.where((rank < SLC_TOPN) & (j <= qb), 0.0, NEG_INF)

    def sel_neg(a, t):
        rows = [jnp.broadcast_to(selneg_ref[a, pl.ds(per_tile * t + c, 1), :], (SLC_BLOCK, TILE))
                for c in range(per_tile)]
        return jnp.concatenate(rows, axis=0)

    def attend_all(k_ref, vt_ref, group):
        for a in kv_heads:
            tiles = [(k_ref[0, pl.ds(pl.multiple_of(t * TILE, TILE), TILE), :], vt_ref[0, t], add_of(a))
                     for t, add_of in group]
            _attend(q4s[a], tiles, m_ref.at[a], l_ref.at[a], acc_ref.at[a])

    def init_all():
        for a in kv_heads:
            _init_state(m_ref.at[a], l_ref.at[a], acc_ref.at[a])

    far_bias = lambda a: bias_ref[1, a, 0:1, :]

    init_all()
    own = (i, lambda a: bias_ref[0, a] + _group_lanes(sel_neg(a, i) + diag_neg))
    near = (i - 1, lambda a: bias_ref[1, a] + _group_lanes(sel_neg(a, i - 1)))
    far = lambda t: (t, lambda a: far_bias(a) + _group_lanes(sel_neg(a, t)))

    @pl.when(i == 0)
    def _():
        attend_all(ks_ref, vst_ref, [own])

    @pl.when(i >= 1)
    def _():
        attend_all(ks_ref, vst_ref, [own, near])

    n_far = jnp.maximum(i - 1, 0)

    def far_pair(jj, carry):
        attend_all(ks_ref, vst_ref, [far(2 * jj), far(2 * jj + 1)])
        return carry

    lax.fori_loop(0, n_far >> 1, far_pair, 0)

    @pl.when((n_far & 1) == 1)
    def _():
        attend_all(ks_ref, vst_ref, [far(n_far - 1)])

    for a in kv_heads:
        os_ref[a] = acc_ref[a] / l_ref[a]

    init_all()
    own_w = (i, lambda a: bias_ref[0, a] + _group_lanes(diag_neg))
    near_w = (i - 1, lambda a: bias_ref[1, a])
    edge_w = (i - 2, lambda a: far_bias(a) + _group_lanes(jnp.where(key > qry, 0.0, NEG_INF)))

    @pl.when(i == 0)
    def _():
        attend_all(kw_ref, vwt_ref, [own_w])

    @pl.when(i == 1)
    def _():
        attend_all(kw_ref, vwt_ref, [own_w, near_w])

    @pl.when(i >= 2)
    def _():
        attend_all(kw_ref, vwt_ref, [own_w, near_w, edge_w])

    gates = jax.nn.sigmoid(gt_ref[0, 0])
    pieces = []
    for a in kv_heads:
        o_w = acc_ref[a] / l_ref[a]
        rs = slice(a * HEAD_DIM, (a + 1) * HEAD_DIM)
        for g in range(NSA_GROUP):
            c0 = 3 * (NSA_GROUP * a + g)
            ls = slice(g * TILE, (g + 1) * TILE)
            pieces.append(gates[c0:c0 + 1, :] * oc_ref[a, rs, ls] + gates[c0 + 1:c0 + 2, :] * os_ref[a, rs, ls]
                          + gates[c0 + 2:c0 + 3, :] * o_w[rs, ls])
    o_ref[0] = jnp.concatenate(pieces, axis=0).T.astype(o_ref.dtype)


def _cmp_to_slc(S):
    n_cmp_pad = S // CMP_STRIDE
    n_slc = S // SLC_BLOCK
    ci = np.arange(n_cmp_pad)[:, None] * CMP_STRIDE
    sj = np.arange(n_slc)[None, :] * SLC_BLOCK
    c2s = ((ci < sj + SLC_BLOCK) & (ci + CMP_LEN > sj)).astype(np.float32)
    c2s[(S - CMP_LEN) // CMP_STRIDE + 1:] = 0.0
    return jnp.asarray(c2s.T, BF16)


def _nsa_attention(proj, vt, gate_t, kcmp, vcmp_t, bias):
    B, S, _ = proj.shape
    nq = S // TILE
    n_cmp = kcmp.shape[1]
    n_slc = S // SLC_BLOCK
    qw = 2 * NSA_GROUP * HEAD_DIM
    q_blocks = N_HEADS * HEAD_DIM // LANES
    kv_blocks = NSA_KV_HEADS * HEAD_DIM // LANES

    def k_spec(which):
        base = q_blocks + which * kv_blocks
        return pl.BlockSpec((1, S, LANES), lambda b, p, i: (b, 0, base + p))

    def vt_spec(which):
        base = which * kv_blocks
        return pl.BlockSpec((1, nq, LANES, TILE), lambda b, p, i: (b, 0, base + p, 0))

    state = pltpu.VMEM((2, LANES, NSA_GROUP * TILE), F32)
    stat = pltpu.VMEM((2, 1, NSA_GROUP * TILE), F32)
    return pl.pallas_call(
        _nsa_body,
        grid=(B, 2, nq),
        in_specs=[pl.BlockSpec((1, TILE, qw), lambda b, p, i: (b, i, p)),
                  pl.BlockSpec((1, n_cmp, LANES), lambda b, p, i: (b, 0, p)),
                  pl.BlockSpec((1, LANES, n_cmp), lambda b, p, i: (b, p, 0)),
                  k_spec(2), vt_spec(0), k_spec(3), vt_spec(1),
                  pl.BlockSpec((1, 1, LANES, TILE), lambda b, p, i: (b, i, p, 0)),
                  pl.BlockSpec((2, 2, TILE, NSA_GROUP * TILE), lambda b, p, i: (0, p, 0, 0)),
                  pl.BlockSpec((n_slc, n_cmp), lambda b, p, i: (0, 0))],
        out_specs=pl.BlockSpec((1, TILE, qw), lambda b, p, i: (b, i, p)),
        out_shape=jax.ShapeDtypeStruct((B, S, N_HEADS * HEAD_DIM), BF16),
        scratch_shapes=[pltpu.VMEM((2, n_slc, TILE), F32), state, state, stat, stat, state],
        compiler_params=_cparams("parallel", "parallel", "arbitrary"),
        name="nsa_attention",
    )(proj, kcmp, vcmp_t, proj, vt, proj, vt, gate_t, _heads_on_lanes(bias, NSA_GROUP), _cmp_to_slc(S))


def _split_bf16(x):
    hi = x.astype(BF16)
    return hi, (x - hi.astype(F32)).astype(BF16)


ROUTER_TM = 1024
_ROW_OF_EXPERT = np.arange(N_EXPERTS).reshape(N_GROUPS, EXPERTS_PER_GROUP).T.reshape(-1)


def _router_body(x_ref, w_ref, b_ref, tri_ref, idx_ref, wt_ref, pos_ref, cnt_ref, base_ref):
    @pl.when(pl.program_id(0) == 0)
    def _():
        base_ref[...] = jnp.zeros(base_ref.shape, F32)

    x_hi, x_lo = _split_bf16(x_ref[...])
    w_hi, w_lo = _split_bf16(w_ref[...])
    logits = (lax.dot_general(w_hi, x_hi, _NT, preferred_element_type=F32)
              + lax.dot_general(w_hi, x_lo, _NT, preferred_element_type=F32)
              + lax.dot_general(w_lo, x_hi, _NT, preferred_element_type=F32)) + b_ref[:, 0:1]
    m = jnp.max(logits, axis=0, keepdims=True)
    e = jnp.exp(logits - m)
    probs = e / jnp.sum(e, axis=0, keepdims=True)
    pk = [probs[k * N_GROUPS:(k + 1) * N_GROUPS] for k in range(EXPERTS_PER_GROUP)]
    hi1, lo1 = jnp.maximum(pk[0], pk[1]), jnp.minimum(pk[0], pk[1])
    hi2, lo2 = jnp.maximum(pk[2], pk[3]), jnp.minimum(pk[2], pk[3])
    score = jnp.maximum(hi1, hi2) + jnp.maximum(jnp.minimum(hi1, hi2), jnp.maximum(lo1, lo2))
    grp = lax.broadcasted_iota(I32, score.shape, 0)
    best = jnp.min(jnp.where(score == jnp.max(score, axis=0, keepdims=True), grp, N_GROUPS),
                   axis=0, keepdims=True)
    v = [jnp.sum(jnp.where(grp == best, p, 0.0), axis=0, keepdims=True) for p in pk]
    v1 = jnp.maximum(jnp.maximum(v[0], v[1]), jnp.maximum(v[2], v[3]))
    i1 = jnp.where(v[0] == v1, 0, jnp.where(v[1] == v1, 1, jnp.where(v[2] == v1, 2, 3)))
    rest = [jnp.where(i1 == k, -1.0, v[k]) for k in range(EXPERTS_PER_GROUP)]
    v2 = jnp.maximum(jnp.maximum(rest[0], rest[1]), jnp.maximum(rest[2], rest[3]))
    i2 = jnp.where(rest[0] == v2, 0, jnp.where(rest[1] == v2, 1, jnp.where(rest[2] == v2, 2, 3)))
    tot = v1 + v2
    idx_ref[...] = jnp.concatenate([best * EXPERTS_PER_GROUP + i1, best * EXPERTS_PER_GROUP + i2], axis=0)
    wt_ref[...] = jnp.concatenate([v1 / tot, v2 / tot], axis=0)

    row = lax.broadcasted_iota(I32, logits.shape, 0)
    hot = [jnp.where(row == ik * N_GROUPS + best, 1.0, 0.0) for ik in (i1, i2)]
    both = (hot[0] + hot[1]).astype(BF16)
    run = base_ref[:, 0:1]
    pos = [[], []]
    for c in range(logits.shape[1] // LANES):
        ls = slice(c * LANES, (c + 1) * LANES)
        before = run + jnp.dot(both[:, ls], tri_ref[...], preferred_element_type=F32) - 1.0
        for k in range(2):
            pos[k].append(jnp.sum(hot[k][:, ls] * before, axis=0, keepdims=True))
        run = before[:, LANES - 1:LANES] + 1.0
    pos_ref[...] = jnp.concatenate([jnp.concatenate(pos[0], axis=1), jnp.concatenate(pos[1], axis=1)],
                                   axis=0).astype(I32)
    base_ref[...] = jnp.broadcast_to(run, base_ref.shape)
    cnt_ref[...] = jnp.broadcast_to(run, cnt_ref.shape)


def _router(x, router_w, router_b):
    N, D = x.shape
    w = router_w.T[_ROW_OF_EXPERT]
    b = jnp.broadcast_to(router_b[_ROW_OF_EXPERT][:, None], (N_EXPERTS, LANES))
    tri = jnp.asarray(np.triu(np.ones((LANES, LANES), np.float32)), BF16)
    tm = ROUTER_TM
    tok = lambda dt: jax.ShapeDtypeStruct((2, N), dt)
    tok_spec = pl.BlockSpec((2, tm), lambda i: (0, i))
    idx, wts, pos, cnt = pl.pallas_call(
        _router_body,
        grid=(N // tm,),
        in_specs=[pl.BlockSpec((tm, D), lambda i: (i, 0)),
                  pl.BlockSpec((N_EXPERTS, D), lambda i: (0, 0)),
                  pl.BlockSpec((N_EXPERTS, LANES), lambda i: (0, 0)),
                  pl.BlockSpec((LANES, LANES), lambda i: (0, 0))],
        out_specs=[tok_spec, tok_spec, tok_spec, pl.BlockSpec((N_EXPERTS, LANES), lambda i: (0, 0))],
        out_shape=[tok(I32), tok(F32), tok(I32), jax.ShapeDtypeStruct((N_EXPERTS, LANES), F32)],
        scratch_shapes=[pltpu.VMEM((N_EXPERTS, LANES), F32)],
        compiler_params=_cparams("arbitrary"),
        name="moe_router",
    )(x, w, b, tri)
    counts = cnt[np.argsort(_ROW_OF_EXPERT), 0].astype(I32)
    return idx, wts, pos, counts


def _expert_body(blk_e_ref, n_used_ref, x_ref, wg_ref, wu_ref, wd_ref, o_ref, wg_b, wu_b, wd_b):
    i = pl.program_id(0)

    @pl.when((i == 0) | (blk_e_ref[i] != blk_e_ref[jnp.maximum(i - 1, 0)]))
    def _():
        wg_b[...] = wg_ref[0, 0].astype(BF16)
        wu_b[...] = wu_ref[0, 0].astype(BF16)
        wd_b[...] = wd_ref[0, 0].astype(BF16)

    @pl.when(i < n_used_ref[0])
    def _():
        x = x_ref[...]
        gate = jnp.dot(x, wg_b[...], preferred_element_type=F32)
        up = jnp.dot(x, wu_b[...], preferred_element_type=F32)
        hid = (gate * jax.nn.sigmoid(gate) * up).astype(BF16)
        o_ref[...] = jnp.dot(hid, wd_b[...], preferred_element_type=F32)

    @pl.when(i >= n_used_ref[0])
    def _():
        o_ref[...] = jnp.zeros(o_ref.shape, o_ref.dtype)


def _experts(xs, blk_e, n_used, wg, wu, wd, layer):
    R, D = xs.shape
    n_blk = R // MOE_TB

    def live(i, be, nu):
        return jnp.minimum(i, nu[0] - 1)

    grid_spec = pltpu.PrefetchScalarGridSpec(
        num_scalar_prefetch=2,
        grid=(n_blk,),
        in_specs=[pl.BlockSpec((MOE_TB, D), lambda i, be, nu: (live(i, be, nu), 0)),
                  pl.BlockSpec((1, 1, D, D_EXPERT), lambda i, be, nu: (layer, be[i], 0, 0)),
                  pl.BlockSpec((1, 1, D, D_EXPERT), lambda i, be, nu: (layer, be[i], 0, 0)),
                  pl.BlockSpec((1, 1, D_EXPERT, D), lambda i, be, nu: (layer, be[i], 0, 0))],
        out_specs=pl.BlockSpec((MOE_TB, D), lambda i, be, nu: (i, 0)),
        scratch_shapes=[pltpu.VMEM((D, D_EXPERT), BF16), pltpu.VMEM((D, D_EXPERT), BF16),
                        pltpu.VMEM((D_EXPERT, D), BF16)],
    )
    return pl.pallas_call(
        _expert_body,
        grid_spec=grid_spec,
        out_shape=jax.ShapeDtypeStruct((R, D), F32),
        compiler_params=_cparams("arbitrary"),
        name="moe_experts",
    )(blk_e, n_used, xs, wg, wu, wd)


def _combine_ln_body(x_ref, y0_ref, y1_ref, wt_ref, g_ref, b_ref, o_ref, ob_ref):
    ffn = y0_ref[...] * wt_ref[:, 0:1] + y1_ref[...] * wt_ref[:, HEAD_DIM:HEAD_DIM + 1]
    out = _layer_norm_rows(DEEPNORM_ALPHA * x_ref[...] + ffn, g_ref[...], b_ref[...])
    o_ref[...] = out
    ob_ref[...] = out.astype(BF16)


def _combine_ln(x, y0, y1, wt, g, b):
    M, D = x.shape
    row = pl.BlockSpec((LN_TM, D), lambda i: (i, 0))
    vec = pl.BlockSpec((1, D), lambda i: (0, 0))
    return pl.pallas_call(
        _combine_ln_body,
        grid=(M // LN_TM,),
        in_specs=[row, row, row, pl.BlockSpec((LN_TM, LANES), lambda i: (i, 0)), vec, vec],
        out_specs=[row, row],
        out_shape=[jax.ShapeDtypeStruct((M, D), F32), jax.ShapeDtypeStruct((M, D), BF16)],
        compiler_params=_cparams("parallel"),
        name="moe_combine_ln",
    )(x, y0, y1, wt, g.reshape(1, D), b.reshape(1, D))


def _moe_ln(h, hb, router_w, router_b, wg, wu, wd, layer, g, b):
    N, D = h.shape
    A = 2 * N
    idx, wts, pos, counts = _router(h, router_w, router_b)
    starts = jnp.cumsum(counts) - counts
    padded = (counts + MOE_TB - 1) // MOE_TB * MOE_TB
    pends = jnp.cumsum(padded)
    pstarts = pends - padded
    R = A + N_EXPERTS * MOE_TB
    n_blk = R // MOE_TB
    experts = jnp.arange(N_EXPERTS, dtype=I32)
    dest = pos + jnp.sum(jnp.where(idx[None] == experts[:, None, None], pstarts[:, None, None], 0), axis=0)
    tok = jnp.broadcast_to(jnp.arange(N, dtype=I32)[None, :], (2, N))
    _, tok_sorted = lax.sort_key_val(dest.reshape(A), tok.reshape(A))
    blk_row0 = jnp.arange(n_blk, dtype=I32) * MOE_TB
    blk_e = jnp.minimum(jnp.sum((pends[None, :] <= blk_row0[:, None]).astype(I32), axis=1), N_EXPERTS - 1)
    hot = blk_e[:, None] == experts[None, :]
    compact0 = blk_row0 + jnp.sum(jnp.where(hot, (starts - pstarts)[None, :], 0), axis=1)
    compact = jnp.clip(compact0[:, None] + jnp.arange(MOE_TB, dtype=I32)[None, :], 0, A - 1).reshape(R)
    n_used = (pends[-1:] // MOE_TB).astype(I32)
    xs = hb[tok_sorted[compact]]
    yb = _experts(xs, blk_e, n_used, wg, wu, wd, layer)
    wt = jnp.concatenate([jnp.broadcast_to(wts[k][:, None], (N, HEAD_DIM)) for k in range(2)], axis=1)
    return _combine_ln(h, yb[dest[0]], yb[dest[1]], wt, g, b)


def _moba_layer(h, w_in, w_out, bias, g, b, B, S):
    HD = N_HEADS * HEAD_DIM
    qk = _matmul(h, w_in[:, :2 * HD].astype(BF16), BF16).reshape(B, S, 2 * HD)
    vt = _matmul_t(w_in[:, 2 * HD:].T.astype(BF16), h, B, S, BF16)
    att = _moba_attention(qk, vt, bias)
    return _proj_ln(att.reshape(B * S, HD), w_out.astype(BF16), h, g, b)


def _nsa_layer(h, hb, w_in, w_out, pos_k, pos_v, ck_w1, ck_w2, cv_w1, cv_w2, bias, g, b, B, S):
    HD = N_HEADS * HEAD_DIM
    kvw = NSA_KV_HEADS * HEAD_DIM
    col = lambda k: slice(HD + k * kvw, HD + (k + 1) * kvw)
    w_rows = jnp.concatenate([w_in[:, :HD + 2 * kvw], w_in[:, col(2)], w_in[:, col(4)]], axis=1)
    proj = _matmul(hb, w_rows.astype(BF16), BF16).reshape(B, S, HD + 4 * kvw)
    w_vt = jnp.concatenate([w_in[:, col(3)], w_in[:, col(5)]], axis=1).T
    vt = _matmul_t(w_vt.astype(BF16), hb, B, S, BF16)
    per_pair = 3 * N_HEADS // 2
    wg = w_in[:, HD + 6 * kvw:].reshape(D_MODEL, 2, per_pair)
    wg = jnp.pad(wg, ((0, 0), (0, 0), (0, LANES - per_pair))).reshape(D_MODEL, 2 * LANES).T
    gate_t = _matmul_t(wg.astype(BF16), hb, B, S, F32)

    def grouped(t):
        t = t.reshape(B, S, NSA_KV_HEADS, HEAD_DIM).transpose(0, 2, 1, 3)
        return t.reshape(B, NSA_KV_HEADS, S // CMP_STRIDE, CMP_STRIDE * HEAD_DIM)

    kcmp = _compress(grouped(proj[..., HD:HD + kvw]), pos_k, ck_w1, ck_w2)
    vcmp = _compress(grouped(proj[..., HD + kvw:HD + 2 * kvw]), pos_v, cv_w1, cv_w2)
    n_cmp = kcmp.shape[2]
    kcmp = kcmp.transpose(0, 2, 1, 3).reshape(B, n_cmp, kvw)
    vcmp_t = vcmp.transpose(0, 1, 3, 2).reshape(B, kvw, n_cmp)
    att = _nsa_attention(proj, vt, gate_t, kcmp, vcmp_t, bias)
    return _proj_ln(att.reshape(B * S, HD), w_out.astype(BF16), h, g, b)


def kernel(x, rel_bias, router_w, router_b, ln_g, ln_b, moba_w_in, moba_w_out, nsa_w_in, nsa_w_out,
           nsa_pos_k, nsa_pos_v, nsa_ck_w1, nsa_ck_w2, nsa_cv_w1, nsa_cv_w2,
           moe_w_gate, moe_w_up, moe_w_down):
    B, S, D = x.shape
    bias = _bias_tiles(rel_bias)
    h = x.reshape(B * S, D)
    h, hb = _moba_layer(h, moba_w_in[0], moba_w_out[0], bias, ln_g[0, 0], ln_b[0, 0], B, S)
    h, hb = _moe_ln(h, hb, router_w, router_b, moe_w_gate, moe_w_up, moe_w_down, 0,
                    ln_g[0, 1], ln_b[0, 1])
    h, hb = _nsa_layer(h, hb, nsa_w_in[0], nsa_w_out[0], nsa_pos_k[0], nsa_pos_v[0],
                       nsa_ck_w1[0], nsa_ck_w2[0], nsa_cv_w1[0], nsa_cv_w2[0],
                       bias, ln_g[1, 0], ln_b[1, 0], B, S)
    h, hb = _moe_ln(h, hb, router_w, router_b, moe_w_gate, moe_w_up, moe_w_down, 1,
                    ln_g[1, 1], ln_b[1, 1])
    return h.reshape(B, S, D)
```

```python
import math

import numpy as np
import jax
import jax.numpy as jnp
from jax import lax
from jax.experimental import pallas as pl
from jax.experimental.pallas import tpu as pltpu

F32, BF16, I32 = jnp.float32, jnp.bfloat16, jnp.int32

D_MODEL = 1024
N_HEADS = 16
HEAD_DIM = 64
DEPTH = 2
NEG_INF = -1e30
LN_EPS = 1e-5
MOBA_BLOCK = 256
MOBA_TOPK = 3
NSA_KV_HEADS = 4
NSA_GROUP = N_HEADS // NSA_KV_HEADS
CMP_LEN = 32
CMP_STRIDE = 16
CMP_HIDDEN = 256
SLC_BLOCK = 64
SLC_TOPN = 16
SLC_LOCAL = 2
WINDOW = 512
REL_BUCKETS = 32
REL_MAX_DIST = 128
N_EXPERTS = 32
N_GROUPS = 8
EXPERTS_PER_GROUP = N_EXPERTS // N_GROUPS
D_EXPERT = 512
DEEPNORM_ALPHA = (2 * DEPTH) ** 0.25
SCALE = HEAD_DIM ** -0.5

LANES = 128
TILE = 256
MM_TM = 1024
MM_TN = 1024
LN_TM = 512
MOE_TB = 256
VMEM_LIMIT = 48 * 1024 * 1024

_NT = (((1,), (1,)), ((), ()))


def _cparams(*sem):
    return pltpu.CompilerParams(dimension_semantics=sem, vmem_limit_bytes=VMEM_LIMIT)


def _mm_body(a_ref, b_ref, o_ref):
    o_ref[...] = jnp.dot(a_ref[...].astype(BF16), b_ref[...],
                         preferred_element_type=F32).astype(o_ref.dtype)


def _matmul(a, b, out_dtype):
    M, K = a.shape
    N = b.shape[1]
    tn = min(MM_TN, N)
    return pl.pallas_call(
        _mm_body,
        grid=(M // MM_TM, N // tn),
        in_specs=[pl.BlockSpec((MM_TM, K), lambda i, j: (i, 0)),
                  pl.BlockSpec((K, tn), lambda i, j: (0, j))],
        out_specs=pl.BlockSpec((MM_TM, tn), lambda i, j: (i, j)),
        out_shape=jax.ShapeDtypeStruct((M, N), out_dtype),
        compiler_params=_cparams("parallel", "arbitrary"),
        name="in_proj",
    )(a, b)


def _mm_t_body(w_ref, a_ref, o_ref):
    r = lax.dot_general(w_ref[...], a_ref[...].astype(BF16), _NT, preferred_element_type=F32)
    for t in range(o_ref.shape[1]):
        o_ref[0, t] = r[:, t * TILE:(t + 1) * TILE].astype(o_ref.dtype)


def _matmul_t(w_t, a, B, S, out_dtype):
    Nout, K = w_t.shape
    tn = min(MM_TN, Nout)
    per_seq = S // MM_TM
    sub = MM_TM // TILE
    return pl.pallas_call(
        _mm_t_body,
        grid=(B * per_seq, Nout // tn),
        in_specs=[pl.BlockSpec((tn, K), lambda i, j: (j, 0)),
                  pl.BlockSpec((MM_TM, K), lambda i, j: (i, 0))],
        out_specs=pl.BlockSpec((1, sub, tn, TILE), lambda i, j: (i // per_seq, i % per_seq, j, 0)),
        out_shape=jax.ShapeDtypeStruct((B, S // TILE, Nout, TILE), out_dtype),
        compiler_params=_cparams("parallel", "arbitrary"),
        name="in_proj_t",
    )(w_t, a)


def _layer_norm_rows(z, g, b):
    mu = jnp.mean(z, axis=-1, keepdims=True)
    zc = z - mu
    var = jnp.mean(zc * zc, axis=-1, keepdims=True)
    return zc * lax.rsqrt(var + LN_EPS) * g + b


def _proj_ln_body(a_ref, w_ref, x_ref, g_ref, b_ref, o_ref, ob_ref):
    y = jnp.dot(a_ref[...], w_ref[...], preferred_element_type=F32)
    out = _layer_norm_rows(DEEPNORM_ALPHA * x_ref[...] + y, g_ref[...], b_ref[...])
    o_ref[...] = out
    ob_ref[...] = out.astype(BF16)


def _proj_ln(a, w, x, g, b):
    M, K = a.shape
    D = w.shape[1]
    row = pl.BlockSpec((LN_TM, D), lambda i: (i, 0))
    vec = pl.BlockSpec((1, D), lambda i: (0, 0))
    return pl.pallas_call(
        _proj_ln_body,
        grid=(M // LN_TM,),
        in_specs=[pl.BlockSpec((LN_TM, K), lambda i: (i, 0)),
                  pl.BlockSpec((K, D), lambda i: (0, 0)), row, vec, vec],
        out_specs=[row, row],
        out_shape=[jax.ShapeDtypeStruct((M, D), F32), jax.ShapeDtypeStruct((M, D), BF16)],
        compiler_params=_cparams("parallel"),
        name="out_proj_ln",
    )(a, w, x, g.reshape(1, D), b.reshape(1, D))


def _t5_bucket_np(rel):
    n = np.maximum(rel, 0)
    max_exact = REL_BUCKETS // 2
    nf = np.maximum(n, 1).astype(np.float32)
    large = max_exact + (np.log(nf / np.float32(max_exact))
                         / np.float32(math.log(REL_MAX_DIST / max_exact))
                         * np.float32(REL_BUCKETS - max_exact)).astype(np.int32)
    large = np.minimum(large, REL_BUCKETS - 1)
    return np.where(n < max_exact, n, large).astype(np.int32)


def _bias_body(tbl_ref, bk_ref, o_ref):
    h = pl.program_id(0)
    for dl in range(2):
        bk = bk_ref[dl]
        acc = jnp.zeros((TILE, TILE), F32)
        for b in range(REL_BUCKETS):
            acc = jnp.where(bk == b, tbl_ref[h * REL_BUCKETS + b], acc)
        o_ref[dl, 0] = acc


def _bias_tiles(rel_bias):
    key = np.arange(TILE)[:, None]
    qry = np.arange(TILE)[None, :]
    assert int(_t5_bucket_np(np.array(TILE + 1))) == REL_BUCKETS - 1
    bk = np.stack([_t5_bucket_np(qry - key), _t5_bucket_np(TILE + qry - key)])
    return pl.pallas_call(
        _bias_body,
        grid=(N_HEADS,),
        in_specs=[pl.BlockSpec(memory_space=pltpu.SMEM),
                  pl.BlockSpec((2, TILE, TILE), lambda h: (0, 0, 0))],
        out_specs=pl.BlockSpec((2, 1, TILE, TILE), lambda h: (0, h, 0, 0)),
        out_shape=jax.ShapeDtypeStruct((2, N_HEADS, TILE, TILE), F32),
        name="t5_bias_tiles",
    )(rel_bias.T.reshape(-1), jnp.asarray(bk))


def _heads_on_lanes(bias, per_block):
    two, H, T, _ = bias.shape
    b = bias.reshape(two, H // per_block, per_block, T, T).transpose(0, 1, 3, 2, 4)
    return b.reshape(two, H // per_block, T, per_block * T)


def _init_state(m_ref, l_ref, acc_ref):
    m_ref[...] = jnp.full(m_ref.shape, NEG_INF, F32)
    l_ref[...] = jnp.zeros(l_ref.shape, F32)
    acc_ref[...] = jnp.zeros(acc_ref.shape, F32)


def _attend(q, tiles, m_ref, l_ref, acc_ref):
    scores = [lax.dot_general(k, q, _NT, preferred_element_type=F32) + add for k, _, add in tiles]
    m_prev = m_ref[...]
    m_new = m_prev
    for s in scores:
        m_new = jnp.maximum(m_new, jnp.max(s, axis=0, keepdims=True))
    a = jnp.exp(m_prev - m_new)
    probs = [jnp.exp(s - m_new) for s in scores]
    l_new = a * l_ref[...]
    for p in probs:
        l_new = l_new + jnp.sum(p, axis=0, keepdims=True)
    l_ref[...] = l_new
    vt = jnp.concatenate([v for _, v, _ in tiles], axis=1)
    pv = jnp.dot(vt, jnp.concatenate([p.astype(BF16) for p in probs], axis=0), preferred_element_type=F32)
    acc_ref[...] = a * acc_ref[...] + pv
    m_ref[...] = m_new


def _rank_before(vals, rows):
    idx = lax.broadcasted_iota(I32, vals.shape, 0)
    rank = jnp.zeros(vals.shape, I32)
    for m in range(rows):
        row = vals[m:m + 1, :]
        beats = (row > vals) | ((row == vals) & (idx > m))
        rank = rank + jnp.where(beats, 1, 0)
    return rank


MOBA_STREAMS = 2


def _softmax_pv(scores, adds, vts, m_ref, l_ref, acc_ref):
    m_prev = m_ref[...]
    m_new = m_prev
    for s, add in zip(scores, adds):
        m_new = jnp.maximum(m_new, jnp.max(s, axis=0, keepdims=True) + add)
    a = jnp.exp(m_prev - m_new)
    probs = [jnp.exp(s - (m_new - add)) for s, add in zip(scores, adds)]
    l_new = a * l_ref[...]
    for p in probs:
        l_new = l_new + jnp.sum(p, axis=0, keepdims=True)
    l_ref[...] = l_new
    pv = jnp.dot(jnp.concatenate(vts, axis=1), jnp.concatenate([p.astype(BF16) for p in probs], axis=0),
                 preferred_element_type=F32)
    acc_ref[...] = a * acc_ref[...] + pv
    m_ref[...] = m_new


def _moba_body(q_ref, k_ref, vt_ref, bias_ref, o_ref, kmean_ref, radd_ref, sa_ref, sb_ref, m_ref, l_ref, acc_ref):
    i = pl.program_id(2)
    nb = k_ref.shape[1] // TILE
    streams = range(MOBA_STREAMS)
    lanes_of = lambda s: slice(s * LANES, (s + 1) * LANES)

    @pl.when(i == 0)
    def _():
        for s in streams:
            for n in range(nb):
                kb = k_ref[0, n * TILE:(n + 1) * TILE, lanes_of(s)].astype(F32)
                kmean_ref[s, n:n + 1, :] = jnp.sum(kb, axis=0, keepdims=True) * (1.0 / TILE)

    n_far = jnp.maximum(i - 1, 0)
    n_far_groups = (n_far + 1) >> 1
    last = nb - 1

    def key_tile(t, s):
        return k_ref[0, pl.ds(pl.multiple_of(t * TILE, TILE), TILE), lanes_of(s)]

    def far_tiles(j):
        return 2 * j, jnp.minimum(2 * j + 1, last)

    lane = lax.broadcasted_iota(I32, (TILE, LANES), 1)
    key = lax.broadcasted_iota(I32, (TILE, 2 * TILE), 0)
    qry = lax.broadcasted_iota(I32, (TILE, 2 * TILE), 1) & (TILE - 1)
    causal_neg = jnp.where(key <= qry, 0.0, NEG_INF)
    t_near = jnp.maximum(i - 1, 0)
    q2s = []
    for s in streams:
        q = q_ref[0, :, lanes_of(s)] * SCALE
        zero = jnp.zeros_like(q)
        q2 = jnp.concatenate([jnp.where(lane < HEAD_DIM, q, zero),
                              jnp.where(lane >= HEAD_DIM, q, zero)], axis=0)
        q2s.append(q2)
        sa_ref[s, 0] = (lax.dot_general(key_tile(i, s), q2, _NT, preferred_element_type=F32)
                        + (bias_ref[0, s] + causal_neg))
        sa_ref[s, 1] = lax.dot_general(key_tile(t_near, s), q2, _NT, preferred_element_type=F32) + bias_ref[1, s]
        km = kmean_ref[s]
        k_hi = km.astype(BF16)
        k_lo = (km - k_hi.astype(F32)).astype(BF16)
        gate = (lax.dot_general(k_hi, q2, _NT, preferred_element_type=F32)
                + lax.dot_general(k_lo, q2, _NT, preferred_element_type=F32))
        blk = lax.broadcasted_iota(I32, gate.shape, 0)
        gate = jnp.where(blk < i, gate, -jnp.inf)
        rank = _rank_before(gate, nb)
        neg = jnp.where((rank < MOBA_TOPK) & (blk < i), 0.0, NEG_INF)
        far_bias = bias_ref[1, s, 0:1, :]
        near_row = jnp.full((1, 2 * TILE), NEG_INF, F32)
        radd_ref[s, 0:1, :] = jnp.zeros((1, 2 * TILE), F32)
        for n in range(nb):
            row = neg[n:n + 1, :]
            near_row = jnp.where(n == i - 1, row, near_row)
            radd_ref[s, 2 + n:3 + n, :] = jnp.where(n < n_far, far_bias + row, NEG_INF)
        radd_ref[s, 1:2, :] = near_row
        _init_state(m_ref.at[s], l_ref.at[s], acc_ref.at[s])

    def update(j, s, buf):
        first = j == 0
        t0 = jnp.where(first, i, 2 * j - 2)
        t1 = jnp.where(first, t_near, jnp.minimum(2 * j - 1, last))
        _softmax_pv([buf[s, 0], buf[s, 1]],
                    [radd_ref[s, pl.ds(2 * j, 1), :], radd_ref[s, pl.ds(2 * j + 1, 1), :]],
                    [vt_ref[0, t0, lanes_of(s), :], vt_ref[0, t1, lanes_of(s), :]],
                    m_ref.at[s], l_ref.at[s], acc_ref.at[s])

    def step(j, src, dst):
        for s in streams:
            ta, tb = far_tiles(j)
            dst[s, 0] = lax.dot_general(key_tile(ta, s), q2s[s], _NT, preferred_element_type=F32)
            dst[s, 1] = lax.dot_general(key_tile(tb, s), q2s[s], _NT, preferred_element_type=F32)
            update(j, s, src)

    def two_steps(jj, carry):
        step(2 * jj, sa_ref, sb_ref)
        step(2 * jj + 1, sb_ref, sa_ref)
        return carry

    lax.fori_loop(0, n_far_groups >> 1, two_steps, 0)

    @pl.when((n_far_groups & 1) == 1)
    def _():
        step(n_far_groups - 1, sa_ref, sb_ref)
        for s in streams:
            update(n_far_groups, s, sb_ref)

    @pl.when((n_far_groups & 1) == 0)
    def _():
        for s in streams:
            update(n_far_groups, s, sa_ref)

    for s in streams:
        o = acc_ref[s] / l_ref[s]
        o = jnp.concatenate([o[:HEAD_DIM, :TILE], o[HEAD_DIM:, TILE:]], axis=0)
        o_ref[0, :, lanes_of(s)] = o.T.astype(o_ref.dtype)


def _moba_attention(qk, vt, bias):
    B, S, _ = qk.shape
    n_steps = N_HEADS // 2 // MOBA_STREAMS
    nq = S // TILE
    w = MOBA_STREAMS * LANES
    return pl.pallas_call(
        _moba_body,
        grid=(B, n_steps, nq),
        in_specs=[pl.BlockSpec((1, TILE, w), lambda b, p, i: (b, i, p)),
                  pl.BlockSpec((1, S, w), lambda b, p, i: (b, 0, n_steps + p)),
                  pl.BlockSpec((1, nq, w, TILE), lambda b, p, i: (b, 0, p, 0)),
                  pl.BlockSpec((2, MOBA_STREAMS, TILE, 2 * TILE), lambda b, p, i: (0, p, 0, 0))],
        out_specs=pl.BlockSpec((1, TILE, w), lambda b, p, i: (b, i, p)),
        out_shape=jax.ShapeDtypeStruct((B, S, N_HEADS * HEAD_DIM), BF16),
        scratch_shapes=[pltpu.VMEM((MOBA_STREAMS, nq, LANES), F32),
                        pltpu.VMEM((MOBA_STREAMS, 2 + nq, 2 * TILE), F32),
                        pltpu.VMEM((MOBA_STREAMS, 2, TILE, 2 * TILE), F32),
                        pltpu.VMEM((MOBA_STREAMS, 2, TILE, 2 * TILE), F32),
                        pltpu.VMEM((MOBA_STREAMS, 1, 2 * TILE), F32),
                        pltpu.VMEM((MOBA_STREAMS, 1, 2 * TILE), F32),
                        pltpu.VMEM((MOBA_STREAMS, LANES, 2 * TILE), F32)],
        compiler_params=_cparams("parallel", "parallel", "arbitrary"),
        name="moba_attention",
    )(qk, qk, vt, _heads_on_lanes(bias, 2))


def _gelu_tanh(x):
    return 0.5 * x * (1.0 + jnp.tanh(math.sqrt(2.0 / math.pi) * (x + 0.044715 * (x * x * x))))


def _compress_body(t_ref, pos_ref, w1_ref, w2_ref, o_ref):
    groups = t_ref.shape[2]
    half = t_ref.shape[3]
    t = t_ref[0].reshape(NSA_KV_HEADS * groups, half).astype(F32)
    first = jnp.dot((t + pos_ref[0:1, :]).astype(BF16), w1_ref[0:half, :], preferred_element_type=F32)
    second = jnp.dot((t + pos_ref[1:2, :]).astype(BF16), w1_ref[half:2 * half, :],
                     preferred_element_type=F32)
    rows = first.shape[0]
    pre = first + pltpu.roll(second, rows - 1, 0)
    out = jnp.dot(_gelu_tanh(pre).astype(BF16), w2_ref[...], preferred_element_type=F32)
    for h in range(NSA_KV_HEADS):
        o_ref[0, h] = out[h * groups:(h + 1) * groups].astype(o_ref.dtype)


def _compress(t, pos, w1, w2):
    B, Hkv, groups, half = t.shape
    return pl.pallas_call(
        _compress_body,
        grid=(B,),
        in_specs=[pl.BlockSpec((1, Hkv, groups, half), lambda b: (b, 0, 0, 0)),
                  pl.BlockSpec((2, half), lambda b: (0, 0)),
                  pl.BlockSpec((2 * half, CMP_HIDDEN), lambda b: (0, 0)),
                  pl.BlockSpec((CMP_HIDDEN, HEAD_DIM), lambda b: (0, 0))],
        out_specs=pl.BlockSpec((1, Hkv, groups, HEAD_DIM), lambda b: (b, 0, 0, 0)),
        out_shape=jax.ShapeDtypeStruct((B, Hkv, groups, HEAD_DIM), BF16),
        compiler_params=_cparams("parallel"),
        name="nsa_compress",
    )(t, pos.reshape(2, half), w1.astype(BF16), w2.astype(BF16))


def _swap_halves(x):
    return jnp.concatenate([x[:, HEAD_DIM:], x[:, :HEAD_DIM]], axis=1)


def _group_lanes(x):
    return jnp.concatenate([x] * NSA_GROUP, axis=1)


def _nsa_body(q_ref, kc_ref, vct_ref, ks_ref, vst_ref, kw_ref, vwt_ref, gt_ref, bias_ref, c2s_ref,
              o_ref, selneg_ref, oc_ref, os_ref, m_ref, l_ref, acc_ref):
    i = pl.program_id(2)
    n_cmp = kc_ref.shape[1]
    n_slc = c2s_ref.shape[0]
    per_tile = TILE // SLC_BLOCK
    cols = NSA_GROUP * TILE
    kv_heads = range(2)
    lane = lax.broadcasted_iota(I32, (TILE, LANES), 1)
    lo_half = lane < HEAD_DIM
    qpos = i * TILE + (lax.broadcasted_iota(I32, (n_cmp, cols), 1) & (TILE - 1))
    cmp_valid = CMP_STRIDE * lax.broadcasted_iota(I32, (n_cmp, cols), 0) + (CMP_LEN - 1) <= qpos
    key = lax.broadcasted_iota(I32, (TILE, TILE), 0)
    qry = lax.broadcasted_iota(I32, (TILE, TILE), 1)
    diag_neg = jnp.where(key <= qry, 0.0, NEG_INF)
    qall = q_ref[0] * SCALE
    q4s = []

    for a in kv_heads:
        keep = lo_half if a == 0 else jnp.logical_not(lo_half)
        heads = []
        for g in range(NSA_GROUP):
            cb = a * 2 + g // 2
            x = qall[:, cb * LANES:(cb + 1) * LANES]
            if g % 2 != a:
                x = _swap_halves(x)
            heads.append(jnp.where(keep, x, jnp.zeros_like(x)))
        q4 = jnp.concatenate(heads, axis=0)
        q4s.append(q4)

        s_c = lax.dot_general(kc_ref[0], q4, _NT, preferred_element_type=F32)
        s_c = jnp.where(cmp_valid, s_c, NEG_INF)
        m_c = jnp.max(s_c, axis=0, keepdims=True)
        e_c = jnp.where(cmp_valid, jnp.exp(s_c - m_c), 0.0)
        l_c = jnp.sum(e_c, axis=0, keepdims=True)
        p_c = e_c / jnp.where(l_c > 0.0, l_c, 1.0)
        oc_ref[a] = jnp.dot(vct_ref[0], p_c.astype(BF16), preferred_element_type=F32)

        p_sum = p_c[:, 0:TILE]
        for g in range(1, NSA_GROUP):
            p_sum = p_sum + p_c[:, g * TILE:(g + 1) * TILE]
        p_hi = p_sum.astype(BF16)
        p_lo = (p_sum - p_hi.astype(F32)).astype(BF16)
        imp = (jnp.dot(c2s_ref[...], p_hi, preferred_element_type=F32)
               + jnp.dot(c2s_ref[...], p_lo, preferred_element_type=F32))
        j = lax.broadcasted_iota(I32, imp.shape, 0)
        qb = (i * TILE + lax.broadcasted_iota(I32, imp.shape, 1)) >> int(math.log2(SLC_BLOCK))
        forced = (j == 0) | ((j <= qb) & (j > qb - SLC_LOCAL))
        imp = jnp.where(forced, jnp.inf, jnp.where(j > qb, -jnp.inf, imp))
        rank = _rank_before(imp, n_slc)
        selneg_ref[a] = jnp.where((rank < SLC_TOPN) & (j <= qb), 0.0, NEG_INF)

    def sel_neg(a, t):
        rows = [jnp.broadcast_to(selneg_ref[a, pl.ds(per_tile * t + c, 1), :], (SLC_BLOCK, TILE))
                for c in range(per_tile)]
        return jnp.concatenate(rows, axis=0)

    def attend_all(k_ref, vt_ref, group):
        for a in kv_heads:
            tiles = [(k_ref[0, pl.ds(pl.multiple_of(t * TILE, TILE), TILE), :], vt_ref[0, t], add_of(a))
                     for t, add_of in group]
            _attend(q4s[a], tiles, m_ref.at[a], l_ref.at[a], acc_ref.at[a])

    def init_all():
        for a in kv_heads:
            _init_state(m_ref.at[a], l_ref.at[a], acc_ref.at[a])

    far_bias = lambda a: bias_ref[1, a, 0:1, :]

    init_all()
    own = (i, lambda a: bias_ref[0, a] + _group_lanes(sel_neg(a, i) + diag_neg))
    near = (i - 1, lambda a: bias_ref[1, a] + _group_lanes(sel_neg(a, i - 1)))
    far = lambda t: (t, lambda a: far_bias(a) + _group_lanes(sel_neg(a, t)))

    @pl.when(i == 0)
    def _():
        attend_all(ks_ref, vst_ref, [own])

    @pl.when(i >= 1)
    def _():
        attend_all(ks_ref, vst_ref, [own, near])

    n_far = jnp.maximum(i - 1, 0)

    def far_pair(jj, carry):
        attend_all(ks_ref, vst_ref, [far(2 * jj), far(2 * jj + 1)])
        return carry

    lax.fori_loop(0, n_far >> 1, far_pair, 0)

    @pl.when((n_far & 1) == 1)
    def _():
        attend_all(ks_ref, vst_ref, [far(n_far - 1)])

    for a in kv_heads:
        os_ref[a] = acc_ref[a] / l_ref[a]

    init_all()
    own_w = (i, lambda a: bias_ref[0, a] + _group_lanes(diag_neg))
    near_w = (i - 1, lambda a: bias_ref[1, a])
    edge_w = (i - 2, lambda a: far_bias(a) + _group_lanes(jnp.where(key > qry, 0.0, NEG_INF)))

    @pl.when(i == 0)
    def _():
        attend_all(kw_ref, vwt_ref, [own_w])

    @pl.when(i == 1)
    def _():
        attend_all(kw_ref, vwt_ref, [own_w, near_w])

    @pl.when(i >= 2)
    def _():
        attend_all(kw_ref, vwt_ref, [own_w, near_w, edge_w])

    gates = jax.nn.sigmoid(gt_ref[0, 0])
    pieces = []
    for a in kv_heads:
        o_w = acc_ref[a] / l_ref[a]
        rs = slice(a * HEAD_DIM, (a + 1) * HEAD_DIM)
        for g in range(NSA_GROUP):
            c0 = 3 * (NSA_GROUP * a + g)
            ls = slice(g * TILE, (g + 1) * TILE)
            pieces.append(gates[c0:c0 + 1, :] * oc_ref[a, rs, ls] + gates[c0 + 1:c0 + 2, :] * os_ref[a, rs, ls]
                          + gates[c0 + 2:c0 + 3, :] * o_w[rs, ls])
    o_ref[0] = jnp.concatenate(pieces, axis=0).T.astype(o_ref.dtype)


def _cmp_to_slc(S):
    n_cmp_pad = S // CMP_STRIDE
    n_slc = S // SLC_BLOCK
    ci = np.arange(n_cmp_pad)[:, None] * CMP_STRIDE
    sj = np.arange(n_slc)[None, :] * SLC_BLOCK
    c2s = ((ci < sj + SLC_BLOCK) & (ci + CMP_LEN > sj)).astype(np.float32)
    c2s[(S - CMP_LEN) // CMP_STRIDE + 1:] = 0.0
    return jnp.asarray(c2s.T, BF16)


def _nsa_attention(proj, vt, gate_t, kcmp, vcmp_t, bias):
    B, S, _ = proj.shape
    nq = S // TILE
    n_cmp = kcmp.shape[1]
    n_slc = S // SLC_BLOCK
    qw = 2 * NSA_GROUP * HEAD_DIM
    q_blocks = N_HEADS * HEAD_DIM // LANES
    kv_blocks = NSA_KV_HEADS * HEAD_DIM // LANES

    def k_spec(which):
        base = q_blocks + which * kv_blocks
        return pl.BlockSpec((1, S, LANES), lambda b, p, i: (b, 0, base + p))

    def vt_spec(which):
        base = which * kv_blocks
        return pl.BlockSpec((1, nq, LANES, TILE), lambda b, p, i: (b, 0, base + p, 0))

    state = pltpu.VMEM((2, LANES, NSA_GROUP * TILE), F32)
    stat = pltpu.VMEM((2, 1, NSA_GROUP * TILE), F32)
    return pl.pallas_call(
        _nsa_body,
        grid=(B, 2, nq),
        in_specs=[pl.BlockSpec((1, TILE, qw), lambda b, p, i: (b, i, p)),
                  pl.BlockSpec((1, n_cmp, LANES), lambda b, p, i: (b, 0, p)),
                  pl.BlockSpec((1, LANES, n_cmp), lambda b, p, i: (b, p, 0)),
                  k_spec(2), vt_spec(0), k_spec(3), vt_spec(1),
                  pl.BlockSpec((1, 1, LANES, TILE), lambda b, p, i: (b, i, p, 0)),
                  pl.BlockSpec((2, 2, TILE, NSA_GROUP * TILE), lambda b, p, i: (0, p, 0, 0)),
                  pl.BlockSpec((n_slc, n_cmp), lambda b, p, i: (0, 0))],
        out_specs=pl.BlockSpec((1, TILE, qw), lambda b, p, i: (b, i, p)),
        out_shape=jax.ShapeDtypeStruct((B, S, N_HEADS * HEAD_DIM), BF16),
        scratch_shapes=[pltpu.VMEM((2, n_slc, TILE), F32), state, state, stat, stat, state],
        compiler_params=_cparams("parallel", "parallel", "arbitrary"),
        name="nsa_attention",
    )(proj, kcmp, vcmp_t, proj, vt, proj, vt, gate_t, _heads_on_lanes(bias, NSA_GROUP), _cmp_to_slc(S))


def _split_bf16(x):
    hi = x.astype(BF16)
    return hi, (x - hi.astype(F32)).astype(BF16)


ROUTER_TM = 1024
_ROW_OF_EXPERT = np.arange(N_EXPERTS).reshape(N_GROUPS, EXPERTS_PER_GROUP).T.reshape(-1)


def _router_body(x_ref, w_ref, b_ref, tri_ref, idx_ref, wt_ref, pos_ref, cnt_ref, base_ref):
    @pl.when(pl.program_id(0) == 0)
    def _():
        base_ref[...] = jnp.zeros(base_ref.shape, F32)

    x_hi, x_lo = _split_bf16(x_ref[...])
    w_hi, w_lo = _split_bf16(w_ref[...])
    logits = (lax.dot_general(w_hi, x_hi, _NT, preferred_element_type=F32)
              + lax.dot_general(w_hi, x_lo, _NT, preferred_element_type=F32)
              + lax.dot_general(w_lo, x_hi, _NT, preferred_element_type=F32)) + b_ref[:, 0:1]
    m = jnp.max(logits, axis=0, keepdims=True)
    e = jnp.exp(logits - m)
    probs = e / jnp.sum(e, axis=0, keepdims=True)
    pk = [probs[k * N_GROUPS:(k + 1) * N_GROUPS] for k in range(EXPERTS_PER_GROUP)]
    hi1, lo1 = jnp.maximum(pk[0], pk[1]), jnp.minimum(pk[0], pk[1])
    hi2, lo2 = jnp.maximum(pk[2], pk[3]), jnp.minimum(pk[2], pk[3])
    score = jnp.maximum(hi1, hi2) + jnp.maximum(jnp.minimum(hi1, hi2), jnp.maximum(lo1, lo2))
    grp = lax.broadcasted_iota(I32, score.shape, 0)
    best = jnp.min(jnp.where(score == jnp.max(score, axis=0, keepdims=True), grp, N_GROUPS),
                   axis=0, keepdims=True)
    v = [jnp.sum(jnp.where(grp == best, p, 0.0), axis=0, keepdims=True) for p in pk]
    v1 = jnp.maximum(jnp.maximum(v[0], v[1]), jnp.maximum(v[2], v[3]))
    i1 = jnp.where(v[0] == v1, 0, jnp.where(v[1] == v1, 1, jnp.where(v[2] == v1, 2, 3)))
    rest = [jnp.where(i1 == k, -1.0, v[k]) for k in range(EXPERTS_PER_GROUP)]
    v2 = jnp.maximum(jnp.maximum(rest[0], rest[1]), jnp.maximum(rest[2], rest[3]))
    i2 = jnp.where(rest[0] == v2, 0, jnp.where(rest[1] == v2, 1, jnp.where(rest[2] == v2, 2, 3)))
    tot = v1 + v2
    idx_ref[...] = jnp.concatenate([best * EXPERTS_PER_GROUP + i1, best * EXPERTS_PER_GROUP + i2], axis=0)
    wt_ref[...] = jnp.concatenate([v1 / tot, v2 / tot], axis=0)

    row = lax.broadcasted_iota(I32, logits.shape, 0)
    hot = [jnp.where(row == ik * N_GROUPS + best, 1.0, 0.0) for ik in (i1, i2)]
    both = (hot[0] + hot[1]).astype(BF16)
    run = base_ref[:, 0:1]
    pos = [[], []]
    for c in range(logits.shape[1] // LANES):
        ls = slice(c * LANES, (c + 1) * LANES)
        before = run + jnp.dot(both[:, ls], tri_ref[...], preferred_element_type=F32) - 1.0
        for k in range(2):
            pos[k].append(jnp.sum(hot[k][:, ls] * before, axis=0, keepdims=True))
        run = before[:, LANES - 1:LANES] + 1.0
    pos_ref[...] = jnp.concatenate([jnp.concatenate(pos[0], axis=1), jnp.concatenate(pos[1], axis=1)],
                                   axis=0).astype(I32)
    base_ref[...] = jnp.broadcast_to(run, base_ref.shape)
    cnt_ref[...] = jnp.broadcast_to(run, cnt_ref.shape)


def _router(x, router_w, router_b):
    N, D = x.shape
    w = router_w.T[_ROW_OF_EXPERT]
    b = jnp.broadcast_to(router_b[_ROW_OF_EXPERT][:, None], (N_EXPERTS, LANES))
    tri = jnp.asarray(np.triu(np.ones((LANES, LANES), np.float32)), BF16)
    tm = ROUTER_TM
    tok = lambda dt: jax.ShapeDtypeStruct((2, N), dt)
    tok_spec = pl.BlockSpec((2, tm), lambda i: (0, i))
    idx, wts, pos, cnt = pl.pallas_call(
        _router_body,
        grid=(N // tm,),
        in_specs=[pl.BlockSpec((tm, D), lambda i: (i, 0)),
                  pl.BlockSpec((N_EXPERTS, D), lambda i: (0, 0)),
                  pl.BlockSpec((N_EXPERTS, LANES), lambda i: (0, 0)),
                  pl.BlockSpec((LANES, LANES), lambda i: (0, 0))],
        out_specs=[tok_spec, tok_spec, tok_spec, pl.BlockSpec((N_EXPERTS, LANES), lambda i: (0, 0))],
        out_shape=[tok(I32), tok(F32), tok(I32), jax.ShapeDtypeStruct((N_EXPERTS, LANES), F32)],
        scratch_shapes=[pltpu.VMEM((N_EXPERTS, LANES), F32)],
        compiler_params=_cparams("arbitrary"),
        name="moe_router",
    )(x, w, b, tri)
    counts = cnt[np.argsort(_ROW_OF_EXPERT), 0].astype(I32)
    return idx, wts, pos, counts


def _expert_body(blk_e_ref, n_used_ref, x_ref, wg_ref, wu_ref, wd_ref, o_ref, wg_b, wu_b, wd_b):
    i = pl.program_id(0)

    @pl.when((i == 0) | (blk_e_ref[i] != blk_e_ref[jnp.maximum(i - 1, 0)]))
    def _():
        wg_b[...] = wg_ref[0, 0].astype(BF16)
        wu_b[...] = wu_ref[0, 0].astype(BF16)
        wd_b[...] = wd_ref[0, 0].astype(BF16)

    @pl.when(i < n_used_ref[0])
    def _():
        x = x_ref[...]
        gate = jnp.dot(x, wg_b[...], preferred_element_type=F32)
        up = jnp.dot(x, wu_b[...], preferred_element_type=F32)
        hid = (gate * jax.nn.sigmoid(gate) * up).astype(BF16)
        o_ref[...] = jnp.dot(hid, wd_b[...], preferred_element_type=F32)

    @pl.when(i >= n_used_ref[0])
    def _():
        o_ref[...] = jnp.zeros(o_ref.shape, o_ref.dtype)


def _experts(xs, blk_e, n_used, wg, wu, wd, layer):
    R, D = xs.shape
    n_blk = R // MOE_TB

    def live(i, be, nu):
        return jnp.minimum(i, nu[0] - 1)

    grid_spec = pltpu.PrefetchScalarGridSpec(
        num_scalar_prefetch=2,
        grid=(n_blk,),
        in_specs=[pl.BlockSpec((MOE_TB, D), lambda i, be, nu: (live(i, be, nu), 0)),
                  pl.BlockSpec((1, 1, D, D_EXPERT), lambda i, be, nu: (layer, be[i], 0, 0)),
                  pl.BlockSpec((1, 1, D, D_EXPERT), lambda i, be, nu: (layer, be[i], 0, 0)),
                  pl.BlockSpec((1, 1, D_EXPERT, D), lambda i, be, nu: (layer, be[i], 0, 0))],
        out_specs=pl.BlockSpec((MOE_TB, D), lambda i, be, nu: (i, 0)),
        scratch_shapes=[pltpu.VMEM((D, D_EXPERT), BF16), pltpu.VMEM((D, D_EXPERT), BF16),
                        pltpu.VMEM((D_EXPERT, D), BF16)],
    )
    return pl.pallas_call(
        _expert_body,
        grid_spec=grid_spec,
        out_shape=jax.ShapeDtypeStruct((R, D), F32),
        compiler_params=_cparams("arbitrary"),
        name="moe_experts",
    )(blk_e, n_used, xs, wg, wu, wd)


def _combine_ln_body(x_ref, y0_ref, y1_ref, wt_ref, g_ref, b_ref, o_ref, ob_ref):
    ffn = y0_ref[...] * wt_ref[:, 0:1] + y1_ref[...] * wt_ref[:, HEAD_DIM:HEAD_DIM + 1]
    out = _layer_norm_rows(DEEPNORM_ALPHA * x_ref[...] + ffn, g_ref[...], b_ref[...])
    o_ref[...] = out
    ob_ref[...] = out.astype(BF16)


def _combine_ln(x, y0, y1, wt, g, b):
    M, D = x.shape
    row = pl.BlockSpec((LN_TM, D), lambda i: (i, 0))
    vec = pl.BlockSpec((1, D), lambda i: (0, 0))
    return pl.pallas_call(
        _combine_ln_body,
        grid=(M // LN_TM,),
        in_specs=[row, row, row, pl.BlockSpec((LN_TM, LANES), lambda i: (i, 0)), vec, vec],
        out_specs=[row, row],
        out_shape=[jax.ShapeDtypeStruct((M, D), F32), jax.ShapeDtypeStruct((M, D), BF16)],
        compiler_params=_cparams("parallel"),
        name="moe_combine_ln",
    )(x, y0, y1, wt, g.reshape(1, D), b.reshape(1, D))


def _moe_ln(h, hb, router_w, router_b, wg, wu, wd, layer, g, b):
    N, D = h.shape
    A = 2 * N
    idx, wts, pos, counts = _router(h, router_w, router_b)
    starts = jnp.cumsum(counts) - counts
    padded = (counts + MOE_TB - 1) // MOE_TB * MOE_TB
    pends = jnp.cumsum(padded)
    pstarts = pends - padded
    R = A + N_EXPERTS * MOE_TB
    n_blk = R // MOE_TB
    experts = jnp.arange(N_EXPERTS, dtype=I32)
    dest = pos + jnp.sum(jnp.where(idx[None] == experts[:, None, None], pstarts[:, None, None], 0), axis=0)
    tok = jnp.broadcast_to(jnp.arange(N, dtype=I32)[None, :], (2, N))
    _, tok_sorted = lax.sort_key_val(dest.reshape(A), tok.reshape(A))
    blk_row0 = jnp.arange(n_blk, dtype=I32) * MOE_TB
    blk_e = jnp.minimum(jnp.sum((pends[None, :] <= blk_row0[:, None]).astype(I32), axis=1), N_EXPERTS - 1)
    hot = blk_e[:, None] == experts[None, :]
    compact0 = blk_row0 + jnp.sum(jnp.where(hot, (starts - pstarts)[None, :], 0), axis=1)
    compact = jnp.clip(compact0[:, None] + jnp.arange(MOE_TB, dtype=I32)[None, :], 0, A - 1).reshape(R)
    n_used = (pends[-1:] // MOE_TB).astype(I32)
    xs = hb[tok_sorted[compact]]
    yb = _experts(xs, blk_e, n_used, wg, wu, wd, layer)
    wt = jnp.concatenate([jnp.broadcast_to(wts[k][:, None], (N, HEAD_DIM)) for k in range(2)], axis=1)
    return _combine_ln(h, yb[dest[0]], yb[dest[1]], wt, g, b)


def _moba_layer(h, w_in, w_out, bias, g, b, B, S):
    HD = N_HEADS * HEAD_DIM
    qk = _matmul(h, w_in[:, :2 * HD].astype(BF16), BF16).reshape(B, S, 2 * HD)
    vt = _matmul_t(w_in[:, 2 * HD:].T.astype(BF16), h, B, S, BF16)
    att = _moba_attention(qk, vt, bias)
    return _proj_ln(att.reshape(B * S, HD), w_out.astype(BF16), h, g, b)


def _nsa_layer(h, hb, w_in, w_out, pos_k, pos_v, ck_w1, ck_w2, cv_w1, cv_w2, bias, g, b, B, S):
    HD = N_HEADS * HEAD_DIM
    kvw = NSA_KV_HEADS * HEAD_DIM
    col = lambda k: slice(HD + k * kvw, HD + (k + 1) * kvw)
    w_rows = jnp.concatenate([w_in[:, :HD + 2 * kvw], w_in[:, col(2)], w_in[:, col(4)]], axis=1)
    proj = _matmul(hb, w_rows.astype(BF16), BF16).reshape(B, S, HD + 4 * kvw)
    w_vt = jnp.concatenate([w_in[:, col(3)], w_in[:, col(5)]], axis=1).T
    vt = _matmul_t(w_vt.astype(BF16), hb, B, S, BF16)
    per_pair = 3 * N_HEADS // 2
    wg = w_in[:, HD + 6 * kvw:].reshape(D_MODEL, 2, per_pair)
    wg = jnp.pad(wg, ((0, 0), (0, 0), (0, LANES - per_pair))).reshape(D_MODEL, 2 * LANES).T
    gate_t = _matmul_t(wg.astype(BF16), hb, B, S, F32)

    def grouped(t):
        t = t.reshape(B, S, NSA_KV_HEADS, HEAD_DIM).transpose(0, 2, 1, 3)
        return t.reshape(B, NSA_KV_HEADS, S // CMP_STRIDE, CMP_STRIDE * HEAD_DIM)

    kcmp = _compress(grouped(proj[..., HD:HD + kvw]), pos_k, ck_w1, ck_w2)
    vcmp = _compress(grouped(proj[..., HD + kvw:HD + 2 * kvw]), pos_v, cv_w1, cv_w2)
    n_cmp = kcmp.shape[2]
    kcmp = kcmp.transpose(0, 2, 1, 3).reshape(B, n_cmp, kvw)
    vcmp_t = vcmp.transpose(0, 1, 3, 2).reshape(B, kvw, n_cmp)
    att = _nsa_attention(proj, vt, gate_t, kcmp, vcmp_t, bias)
    return _proj_ln(att.reshape(B * S, HD), w_out.astype(BF16), h, g, b)


def kernel(x, rel_bias, router_w, router_b, ln_g, ln_b, moba_w_in, moba_w_out, nsa_w_in, nsa_w_out,
           nsa_pos_k, nsa_pos_v, nsa_ck_w1, nsa_ck_w2, nsa_cv_w1, nsa_cv_w2,
           moe_w_gate, moe_w_up, moe_w_down):
    B, S, D = x.shape
    bias = _bias_tiles(rel_bias)
    h = x.reshape(B * S, D)
    h, hb = _moba_layer(h, moba_w_in[0], moba_w_out[0], bias, ln_g[0, 0], ln_b[0, 0], B, S)
    h, hb = _moe_ln(h, hb, router_w, router_b, moe_w_gate, moe_w_up, moe_w_down, 0,
                    ln_g[0, 1], ln_b[0, 1])
    h, hb = _nsa_layer(h, hb, nsa_w_in[0], nsa_w_out[0], nsa_pos_k[0], nsa_pos_v[0],
                       nsa_ck_w1[0], nsa_ck_w2[0], nsa_cv_w1[0], nsa_cv_w2[0],
                       bias, ln_g[1, 0], ln_b[1, 0], B, S)
    h, hb = _moe_ln(h, hb, router_w, router_b, moe_w_gate, moe_w_up, moe_w_down, 1,
                    ln_g[1, 1], ln_b[1, 1])
    return h.reshape(B, S, D)
```

```python
import math

import numpy as np
import jax
import jax.numpy as jnp
from jax import lax
from jax.experimental import pallas as pl
from jax.experimental.pallas import tpu as pltpu

F32, BF16, I32 = jnp.float32, jnp.bfloat16, jnp.int32

D_MODEL = 1024
N_HEADS = 16
HEAD_DIM = 64
DEPTH = 2
NEG_INF = -1e30
LN_EPS = 1e-5
MOBA_BLOCK = 256
MOBA_TOPK = 3
NSA_KV_HEADS = 4
NSA_GROUP = N_HEADS // NSA_KV_HEADS
CMP_LEN = 32
CMP_STRIDE = 16
CMP_HIDDEN = 256
SLC_BLOCK = 64
SLC_TOPN = 16
SLC_LOCAL = 2
WINDOW = 512
REL_BUCKETS = 32
REL_MAX_DIST = 128
N_EXPERTS = 32
N_GROUPS = 8
EXPERTS_PER_GROUP = N_EXPERTS // N_GROUPS
D_EXPERT = 512
DEEPNORM_ALPHA = (2 * DEPTH) ** 0.25
SCALE = HEAD_DIM ** -0.5

LANES = 128
TILE = 256
MM_TM = 1024
MM_TN = 1024
LN_TM = 512
MOE_TB = 256
VMEM_LIMIT = 48 * 1024 * 1024

_NT = (((1,), (1,)), ((), ()))


def _cparams(*sem):
    return pltpu.CompilerParams(dimension_semantics=sem, vmem_limit_bytes=VMEM_LIMIT)


def _mm_body(a_ref, b_ref, o_ref):
    o_ref[...] = jnp.dot(a_ref[...].astype(BF16), b_ref[...],
                         preferred_element_type=F32).astype(o_ref.dtype)


def _matmul(a, b, out_dtype):
    M, K = a.shape
    N = b.shape[1]
    tn = min(MM_TN, N)
    return pl.pallas_call(
        _mm_body,
        grid=(M // MM_TM, N // tn),
        in_specs=[pl.BlockSpec((MM_TM, K), lambda i, j: (i, 0)),
                  pl.BlockSpec((K, tn), lambda i, j: (0, j))],
        out_specs=pl.BlockSpec((MM_TM, tn), lambda i, j: (i, j)),
        out_shape=jax.ShapeDtypeStruct((M, N), out_dtype),
        compiler_params=_cparams("parallel", "arbitrary"),
        name="in_proj",
    )(a, b)


def _mm_t_body(w_ref, a_ref, o_ref):
    r = lax.dot_general(w_ref[...], a_ref[...].astype(BF16), _NT, preferred_element_type=F32)
    for t in range(o_ref.shape[1]):
        o_ref[0, t] = r[:, t * TILE:(t + 1) * TILE].astype(o_ref.dtype)


def _matmul_t(w_t, a, B, S, out_dtype):
    Nout, K = w_t.shape
    tn = min(MM_TN, Nout)
    per_seq = S // MM_TM
    sub = MM_TM // TILE
    return pl.pallas_call(
        _mm_t_body,
        grid=(B * per_seq, Nout // tn),
        in_specs=[pl.BlockSpec((tn, K), lambda i, j: (j, 0)),
                  pl.BlockSpec((MM_TM, K), lambda i, j: (i, 0))],
        out_specs=pl.BlockSpec((1, sub, tn, TILE), lambda i, j: (i // per_seq, i % per_seq, j, 0)),
        out_shape=jax.ShapeDtypeStruct((B, S // TILE, Nout, TILE), out_dtype),
        compiler_params=_cparams("parallel", "arbitrary"),
        name="in_proj_t",
    )(w_t, a)


def _layer_norm_rows(z, g, b):
    mu = jnp.mean(z, axis=-1, keepdims=True)
    zc = z - mu
    var = jnp.mean(zc * zc, axis=-1, keepdims=True)
    return zc * lax.rsqrt(var + LN_EPS) * g + b


def _proj_ln_body(a_ref, w_ref, x_ref, g_ref, b_ref, o_ref, ob_ref):
    y = jnp.dot(a_ref[...], w_ref[...], preferred_element_type=F32)
    out = _layer_norm_rows(DEEPNORM_ALPHA * x_ref[...] + y, g_ref[...], b_ref[...])
    o_ref[...] = out
    ob_ref[...] = out.astype(BF16)


def _proj_ln(a, w, x, g, b):
    M, K = a.shape
    D = w.shape[1]
    row = pl.BlockSpec((LN_TM, D), lambda i: (i, 0))
    vec = pl.BlockSpec((1, D), lambda i: (0, 0))
    return pl.pallas_call(
        _proj_ln_body,
        grid=(M // LN_TM,),
        in_specs=[pl.BlockSpec((LN_TM, K), lambda i: (i, 0)),
                  pl.BlockSpec((K, D), lambda i: (0, 0)), row, vec, vec],
        out_specs=[row, row],
        out_shape=[jax.ShapeDtypeStruct((M, D), F32), jax.ShapeDtypeStruct((M, D), BF16)],
        compiler_params=_cparams("parallel"),
        name="out_proj_ln",
    )(a, w, x, g.reshape(1, D), b.reshape(1, D))


def _t5_bucket_np(rel):
    n = np.maximum(rel, 0)
    max_exact = REL_BUCKETS // 2
    nf = np.maximum(n, 1).astype(np.float32)
    large = max_exact + (np.log(nf / np.float32(max_exact))
                         / np.float32(math.log(REL_MAX_DIST / max_exact))
                         * np.float32(REL_BUCKETS - max_exact)).astype(np.int32)
    large = np.minimum(large, REL_BUCKETS - 1)
    return np.where(n < max_exact, n, large).astype(np.int32)


def _bias_body(tbl_ref, bk_ref, o_ref):
    h = pl.program_id(0)
    for dl in range(2):
        bk = bk_ref[dl]
        acc = jnp.zeros((TILE, TILE), F32)
        for b in range(REL_BUCKETS):
            acc = jnp.where(bk == b, tbl_ref[h * REL_BUCKETS + b], acc)
        o_ref[dl, 0] = acc


def _bias_tiles(rel_bias):
    key = np.arange(TILE)[:, None]
    qry = np.arange(TILE)[None, :]
    assert int(_t5_bucket_np(np.array(TILE + 1))) == REL_BUCKETS - 1
    bk = np.stack([_t5_bucket_np(qry - key), _t5_bucket_np(TILE + qry - key)])
    return pl.pallas_call(
        _bias_body,
        grid=(N_HEADS,),
        in_specs=[pl.BlockSpec(memory_space=pltpu.SMEM),
                  pl.BlockSpec((2, TILE, TILE), lambda h: (0, 0, 0))],
        out_specs=pl.BlockSpec((2, 1, TILE, TILE), lambda h: (0, h, 0, 0)),
        out_shape=jax.ShapeDtypeStruct((2, N_HEADS, TILE, TILE), F32),
        name="t5_bias_tiles",
    )(rel_bias.T.reshape(-1), jnp.asarray(bk))


def _heads_on_lanes(bias, per_block):
    two, H, T, _ = bias.shape
    b = bias.reshape(two, H // per_block, per_block, T, T).transpose(0, 1, 3, 2, 4)
    return b.reshape(two, H // per_block, T, per_block * T)


def _init_state(m_ref, l_ref, acc_ref):
    m_ref[...] = jnp.full(m_ref.shape, NEG_INF, F32)
    l_ref[...] = jnp.zeros(l_ref.shape, F32)
    acc_ref[...] = jnp.zeros(acc_ref.shape, F32)


def _rank_before(vals, rows):
    idx = lax.broadcasted_iota(I32, vals.shape, 0)
    rank = jnp.zeros(vals.shape, I32)
    for m in range(rows):
        row = vals[m:m + 1, :]
        beats = (row > vals) | ((row == vals) & (idx > m))
        rank = rank + jnp.where(beats, 1, 0)
    return rank


MOBA_STREAMS = 2


def _softmax_pv(scores, adds, vts, m_ref, l_ref, acc_ref):
    m_prev = m_ref[...]
    m_new = m_prev
    for s, add in zip(scores, adds):
        m_new = jnp.maximum(m_new, jnp.max(s, axis=0, keepdims=True) + add)
    a = jnp.exp(m_prev - m_new)
    probs = [jnp.exp(s - (m_new - add)) for s, add in zip(scores, adds)]
    l_new = a * l_ref[...]
    for p in probs:
        l_new = l_new + jnp.sum(p, axis=0, keepdims=True)
    l_ref[...] = l_new
    pv = jnp.dot(jnp.concatenate(vts, axis=1), jnp.concatenate([p.astype(BF16) for p in probs], axis=0),
                 preferred_element_type=F32)
    acc_ref[...] = a * acc_ref[...] + pv
    m_ref[...] = m_new


def _moba_body(q_ref, k_ref, vt_ref, bias_ref, o_ref, kmean_ref, radd_ref, sa_ref, sb_ref, m_ref, l_ref, acc_ref):
    i = pl.program_id(2)
    nb = k_ref.shape[1] // TILE
    streams = range(MOBA_STREAMS)
    lanes_of = lambda s: slice(s * LANES, (s + 1) * LANES)

    @pl.when(i == 0)
    def _():
        for s in streams:
            for n in range(nb):
                kb = k_ref[0, n * TILE:(n + 1) * TILE, lanes_of(s)].astype(F32)
                kmean_ref[s, n:n + 1, :] = jnp.sum(kb, axis=0, keepdims=True) * (1.0 / TILE)

    n_far = jnp.maximum(i - 1, 0)
    n_far_groups = (n_far + 1) >> 1
    last = nb - 1

    def key_tile(t, s):
        return k_ref[0, pl.ds(pl.multiple_of(t * TILE, TILE), TILE), lanes_of(s)]

    def far_tiles(j):
        return 2 * j, jnp.minimum(2 * j + 1, last)

    lane = lax.broadcasted_iota(I32, (TILE, LANES), 1)
    key = lax.broadcasted_iota(I32, (TILE, 2 * TILE), 0)
    qry = lax.broadcasted_iota(I32, (TILE, 2 * TILE), 1) & (TILE - 1)
    causal_neg = jnp.where(key <= qry, 0.0, NEG_INF)
    t_near = jnp.maximum(i - 1, 0)
    q2s = []
    for s in streams:
        q = q_ref[0, :, lanes_of(s)] * SCALE
        zero = jnp.zeros_like(q)
        q2 = jnp.concatenate([jnp.where(lane < HEAD_DIM, q, zero),
                              jnp.where(lane >= HEAD_DIM, q, zero)], axis=0)
        q2s.append(q2)
        sa_ref[s, 0] = (lax.dot_general(key_tile(i, s), q2, _NT, preferred_element_type=F32)
                        + (bias_ref[0, s] + causal_neg))
        sa_ref[s, 1] = lax.dot_general(key_tile(t_near, s), q2, _NT, preferred_element_type=F32) + bias_ref[1, s]
        km = kmean_ref[s]
        k_hi = km.astype(BF16)
        k_lo = (km - k_hi.astype(F32)).astype(BF16)
        gate = (lax.dot_general(k_hi, q2, _NT, preferred_element_type=F32)
                + lax.dot_general(k_lo, q2, _NT, preferred_element_type=F32))
        blk = lax.broadcasted_iota(I32, gate.shape, 0)
        gate = jnp.where(blk < i, gate, -jnp.inf)
        rank = _rank_before(gate, nb)
        neg = jnp.where((rank < MOBA_TOPK) & (blk < i), 0.0, NEG_INF)
        far_bias = bias_ref[1, s, 0:1, :]
        near_row = jnp.full((1, 2 * TILE), NEG_INF, F32)
        radd_ref[s, 0:1, :] = jnp.zeros((1, 2 * TILE), F32)
        for n in range(nb):
            row = neg[n:n + 1, :]
            near_row = jnp.where(n == i - 1, row, near_row)
            radd_ref[s, 2 + n:3 + n, :] = jnp.where(n < n_far, far_bias + row, NEG_INF)
        radd_ref[s, 1:2, :] = near_row
        _init_state(m_ref.at[s], l_ref.at[s], acc_ref.at[s])

    def update(j, s, buf):
        first = j == 0
        t0 = jnp.where(first, i, 2 * j - 2)
        t1 = jnp.where(first, t_near, jnp.minimum(2 * j - 1, last))
        _softmax_pv([buf[s, 0], buf[s, 1]],
                    [radd_ref[s, pl.ds(2 * j, 1), :], radd_ref[s, pl.ds(2 * j + 1, 1), :]],
                    [vt_ref[0, t0, lanes_of(s), :], vt_ref[0, t1, lanes_of(s), :]],
                    m_ref.at[s], l_ref.at[s], acc_ref.at[s])

    def step(j, src, dst):
        for s in streams:
            ta, tb = far_tiles(j)
            dst[s, 0] = lax.dot_general(key_tile(ta, s), q2s[s], _NT, preferred_element_type=F32)
            dst[s, 1] = lax.dot_general(key_tile(tb, s), q2s[s], _NT, preferred_element_type=F32)
            update(j, s, src)

    def two_steps(jj, carry):
        step(2 * jj, sa_ref, sb_ref)
        step(2 * jj + 1, sb_ref, sa_ref)
        return carry

    lax.fori_loop(0, n_far_groups >> 1, two_steps, 0)

    @pl.when((n_far_groups & 1) == 1)
    def _():
        step(n_far_groups - 1, sa_ref, sb_ref)
        for s in streams:
            update(n_far_groups, s, sb_ref)

    @pl.when((n_far_groups & 1) == 0)
    def _():
        for s in streams:
            update(n_far_groups, s, sa_ref)

    for s in streams:
        o = acc_ref[s] / l_ref[s]
        o = jnp.concatenate([o[:HEAD_DIM, :TILE], o[HEAD_DIM:, TILE:]], axis=0)
        o_ref[0, :, lanes_of(s)] = o.T.astype(o_ref.dtype)


def _moba_attention(qk, vt, bias):
    B, S, _ = qk.shape
    n_steps = N_HEADS // 2 // MOBA_STREAMS
    nq = S // TILE
    w = MOBA_STREAMS * LANES
    return pl.pallas_call(
        _moba_body,
        grid=(B, n_steps, nq),
        in_specs=[pl.BlockSpec((1, TILE, w), lambda b, p, i: (b, i, p)),
                  pl.BlockSpec((1, S, w), lambda b, p, i: (b, 0, n_steps + p)),
                  pl.BlockSpec((1, nq, w, TILE), lambda b, p, i: (b, 0, p, 0)),
                  pl.BlockSpec((2, MOBA_STREAMS, TILE, 2 * TILE), lambda b, p, i: (0, p, 0, 0))],
        out_specs=pl.BlockSpec((1, TILE, w), lambda b, p, i: (b, i, p)),
        out_shape=jax.ShapeDtypeStruct((B, S, N_HEADS * HEAD_DIM), BF16),
        scratch_shapes=[pltpu.VMEM((MOBA_STREAMS, nq, LANES), F32),
                        pltpu.VMEM((MOBA_STREAMS, 2 + nq, 2 * TILE), F32),
                        pltpu.VMEM((MOBA_STREAMS, 2, TILE, 2 * TILE), F32),
                        pltpu.VMEM((MOBA_STREAMS, 2, TILE, 2 * TILE), F32),
                        pltpu.VMEM((MOBA_STREAMS, 1, 2 * TILE), F32),
                        pltpu.VMEM((MOBA_STREAMS, 1, 2 * TILE), F32),
                        pltpu.VMEM((MOBA_STREAMS, LANES, 2 * TILE), F32)],
        compiler_params=_cparams("parallel", "parallel", "arbitrary"),
        name="moba_attention",
    )(qk, qk, vt, _heads_on_lanes(bias, 2))


def _gelu_tanh(x):
    return 0.5 * x * (1.0 + jnp.tanh(math.sqrt(2.0 / math.pi) * (x + 0.044715 * (x * x * x))))


def _compress_body(t_ref, pos_ref, w1_ref, w2_ref, o_ref):
    groups = t_ref.shape[2]
    half = t_ref.shape[3]
    t = t_ref[0].reshape(NSA_KV_HEADS * groups, half).astype(F32)
    first = jnp.dot((t + pos_ref[0:1, :]).astype(BF16), w1_ref[0:half, :], preferred_element_type=F32)
    second = jnp.dot((t + pos_ref[1:2, :]).astype(BF16), w1_ref[half:2 * half, :],
                     preferred_element_type=F32)
    rows = first.shape[0]
    pre = first + pltpu.roll(second, rows - 1, 0)
    out = jnp.dot(_gelu_tanh(pre).astype(BF16), w2_ref[...], preferred_element_type=F32)
    for h in range(NSA_KV_HEADS):
        o_ref[0, h] = out[h * groups:(h + 1) * groups].astype(o_ref.dtype)


def _compress(t, pos, w1, w2):
    B, Hkv, groups, half = t.shape
    return pl.pallas_call(
        _compress_body,
        grid=(B,),
        in_specs=[pl.BlockSpec((1, Hkv, groups, half), lambda b: (b, 0, 0, 0)),
                  pl.BlockSpec((2, half), lambda b: (0, 0)),
                  pl.BlockSpec((2 * half, CMP_HIDDEN), lambda b: (0, 0)),
                  pl.BlockSpec((CMP_HIDDEN, HEAD_DIM), lambda b: (0, 0))],
        out_specs=pl.BlockSpec((1, Hkv, groups, HEAD_DIM), lambda b: (b, 0, 0, 0)),
        out_shape=jax.ShapeDtypeStruct((B, Hkv, groups, HEAD_DIM), BF16),
        compiler_params=_cparams("parallel"),
        name="nsa_compress",
    )(t, pos.reshape(2, half), w1.astype(BF16), w2.astype(BF16))


def _swap_halves(x):
    return jnp.concatenate([x[:, HEAD_DIM:], x[:, :HEAD_DIM]], axis=1)


def _group_lanes(x):
    return jnp.concatenate([x] * NSA_GROUP, axis=1)


def _nsa_body(q_ref, kc_ref, vct_ref, ks_ref, vst_ref, kw_ref, vwt_ref, gt_ref, bias_ref, c2s_ref,
              o_ref, selneg_ref, radd_ref, sa_ref, sb_ref, sw_ref, oc_ref, os_ref, m_ref, l_ref, acc_ref):
    i = pl.program_id(2)
    nb = ks_ref.shape[1] // TILE
    n_cmp = kc_ref.shape[1]
    n_slc = c2s_ref.shape[0]
    per_tile = TILE // SLC_BLOCK
    cols = NSA_GROUP * TILE
    kv_heads = range(2)
    n_far = jnp.maximum(i - 1, 0)
    n_far_groups = (n_far + 1) >> 1
    last = nb - 1
    t_near = jnp.maximum(i - 1, 0)
    t_edge = jnp.maximum(i - 2, 0)

    def key_tile(k_ref, t):
        return k_ref[0, pl.ds(pl.multiple_of(t * TILE, TILE), TILE), :]

    def scores_of(k_ref, t, a):
        return lax.dot_general(key_tile(k_ref, t), q4s[a], _NT, preferred_element_type=F32)

    lane = lax.broadcasted_iota(I32, (TILE, LANES), 1)
    lo_half = lane < HEAD_DIM
    qpos = i * TILE + (lax.broadcasted_iota(I32, (n_cmp, cols), 1) & (TILE - 1))
    cmp_valid = CMP_STRIDE * lax.broadcasted_iota(I32, (n_cmp, cols), 0) + (CMP_LEN - 1) <= qpos
    key = lax.broadcasted_iota(I32, (TILE, TILE), 0)
    qry = lax.broadcasted_iota(I32, (TILE, TILE), 1)
    diag_neg = _group_lanes(jnp.where(key <= qry, 0.0, NEG_INF))
    edge_neg = _group_lanes(jnp.where(key > qry, 0.0, NEG_INF))
    qall = q_ref[0] * SCALE
    q4s = []

    for a in kv_heads:
        keep = lo_half if a == 0 else jnp.logical_not(lo_half)
        heads = []
        for g in range(NSA_GROUP):
            cb = a * 2 + g // 2
            x = qall[:, cb * LANES:(cb + 1) * LANES]
            if g % 2 != a:
                x = _swap_halves(x)
            heads.append(jnp.where(keep, x, jnp.zeros_like(x)))
        q4s.append(jnp.concatenate(heads, axis=0))
        far_bias = bias_ref[1, a, 0:1, :]

        sa_ref[a, 0] = scores_of(ks_ref, i, a) + (bias_ref[0, a] + diag_neg)
        sa_ref[a, 1] = scores_of(ks_ref, t_near, a) + bias_ref[1, a]
        sw_ref[a, 0] = scores_of(kw_ref, i, a) + (bias_ref[0, a] + diag_neg)
        sw_ref[a, 1] = scores_of(kw_ref, t_near, a) + bias_ref[1, a]
        sw_ref[a, 2] = scores_of(kw_ref, t_edge, a) + (far_bias + edge_neg)

        s_c = lax.dot_general(kc_ref[0], q4s[a], _NT, preferred_element_type=F32)
        s_c = jnp.where(cmp_valid, s_c, NEG_INF)
        m_c = jnp.max(s_c, axis=0, keepdims=True)
        e_c = jnp.where(cmp_valid, jnp.exp(s_c - m_c), 0.0)
        l_c = jnp.sum(e_c, axis=0, keepdims=True)
        p_c = e_c / jnp.where(l_c > 0.0, l_c, 1.0)
        oc_ref[a] = jnp.dot(vct_ref[0], p_c.astype(BF16), preferred_element_type=F32)

        p_sum = p_c[:, 0:TILE]
        for g in range(1, NSA_GROUP):
            p_sum = p_sum + p_c[:, g * TILE:(g + 1) * TILE]
        p_hi = p_sum.astype(BF16)
        p_lo = (p_sum - p_hi.astype(F32)).astype(BF16)
        imp = (jnp.dot(c2s_ref[...], p_hi, preferred_element_type=F32)
               + jnp.dot(c2s_ref[...], p_lo, preferred_element_type=F32))
        j = lax.broadcasted_iota(I32, imp.shape, 0)
        qb = (i * TILE + lax.broadcasted_iota(I32, imp.shape, 1)) >> int(math.log2(SLC_BLOCK))
        forced = (j == 0) | ((j <= qb) & (j > qb - SLC_LOCAL))
        imp = jnp.where(forced, jnp.inf, jnp.where(j > qb, -jnp.inf, imp))
        rank = _rank_before(imp, n_slc)
        selneg = jnp.where((rank < SLC_TOPN) & (j <= qb), 0.0, NEG_INF)
        selneg_ref[a] = selneg
        for c in range(per_tile):
            radd_ref[a, c:c + 1, :] = _group_lanes(selneg_ref[a, pl.ds(per_tile * i + c, 1), :])
            near_row = _group_lanes(selneg_ref[a, pl.ds(per_tile * t_near + c, 1), :])
            radd_ref[a, per_tile + c:per_tile + c + 1, :] = jnp.where(i >= 1, near_row, NEG_INF)
        for blk in range(n_slc):
            row = far_bias + _group_lanes(selneg[blk:blk + 1, :])
            r = 2 * per_tile + blk
            radd_ref[a, r:r + 1, :] = jnp.where(blk // per_tile < n_far, row, NEG_INF)
        _init_state(m_ref.at[a], l_ref.at[a], acc_ref.at[a])

    def update(jg, a, buf):
        first = jg == 0
        t0 = jnp.where(first, i, 2 * jg - 2)
        t1 = jnp.where(first, t_near, jnp.minimum(2 * jg - 1, last))
        scores, adds = [], []
        for t in range(2):
            for c in range(per_tile):
                scores.append(buf[a, t, c * SLC_BLOCK:(c + 1) * SLC_BLOCK, :])
                adds.append(radd_ref[a, pl.ds(2 * per_tile * jg + per_tile * t + c, 1), :])
        _softmax_pv(scores, adds, [vst_ref[0, t0], vst_ref[0, t1]], m_ref.at[a], l_ref.at[a], acc_ref.at[a])

    def step(jg, src, dst):
        for a in kv_heads:
            dst[a, 0] = scores_of(ks_ref, 2 * jg, a)
            dst[a, 1] = scores_of(ks_ref, jnp.minimum(2 * jg + 1, last), a)
            update(jg, a, src)

    def two_steps(jj, carry):
        step(2 * jj, sa_ref, sb_ref)
        step(2 * jj + 1, sb_ref, sa_ref)
        return carry

    lax.fori_loop(0, n_far_groups >> 1, two_steps, 0)

    @pl.when((n_far_groups & 1) == 1)
    def _():
        step(n_far_groups - 1, sa_ref, sb_ref)
        for a in kv_heads:
            update(n_far_groups, a, sb_ref)

    @pl.when((n_far_groups & 1) == 0)
    def _():
        for a in kv_heads:
            update(n_far_groups, a, sa_ref)

    gates = jax.nn.sigmoid(gt_ref[0, 0])
    zero_row = jnp.zeros((1, cols), F32)
    pieces = []
    for a in kv_heads:
        os_ref[a] = acc_ref[a] / l_ref[a]
        _init_state(m_ref.at[a], l_ref.at[a], acc_ref.at[a])
        _softmax_pv([sw_ref[a, 0], sw_ref[a, 1], sw_ref[a, 2]],
                    [zero_row, zero_row + jnp.where(i >= 1, 0.0, NEG_INF), zero_row + jnp.where(i >= 2, 0.0, NEG_INF)],
                    [vwt_ref[0, i], vwt_ref[0, t_near], vwt_ref[0, t_edge]],
                    m_ref.at[a], l_ref.at[a], acc_ref.at[a])
        o_w = acc_ref[a] / l_ref[a]
        rs = slice(a * HEAD_DIM, (a + 1) * HEAD_DIM)
        for g in range(NSA_GROUP):
            c0 = 3 * (NSA_GROUP * a + g)
            ls = slice(g * TILE, (g + 1) * TILE)
            pieces.append(gates[c0:c0 + 1, :] * oc_ref[a, rs, ls] + gates[c0 + 1:c0 + 2, :] * os_ref[a, rs, ls]
                          + gates[c0 + 2:c0 + 3, :] * o_w[rs, ls])
    o_ref[0] = jnp.concatenate(pieces, axis=0).T.astype(o_ref.dtype)


def _cmp_to_slc(S):
    n_cmp_pad = S // CMP_STRIDE
    n_slc = S // SLC_BLOCK
    ci = np.arange(n_cmp_pad)[:, None] * CMP_STRIDE
    sj = np.arange(n_slc)[None, :] * SLC_BLOCK
    c2s = ((ci < sj + SLC_BLOCK) & (ci + CMP_LEN > sj)).astype(np.float32)
    c2s[(S - CMP_LEN) // CMP_STRIDE + 1:] = 0.0
    return jnp.asarray(c2s.T, BF16)


def _nsa_attention(proj, vt, gate_t, kcmp, vcmp_t, bias):
    B, S, _ = proj.shape
    nq = S // TILE
    n_cmp = kcmp.shape[1]
    n_slc = S // SLC_BLOCK
    qw = 2 * NSA_GROUP * HEAD_DIM
    q_blocks = N_HEADS * HEAD_DIM // LANES
    kv_blocks = NSA_KV_HEADS * HEAD_DIM // LANES

    def k_spec(which):
        base = q_blocks + which * kv_blocks
        return pl.BlockSpec((1, S, LANES), lambda b, p, i: (b, 0, base + p))

    def vt_spec(which):
        base = which * kv_blocks
        return pl.BlockSpec((1, nq, LANES, TILE), lambda b, p, i: (b, 0, base + p, 0))

    state = pltpu.VMEM((2, LANES, NSA_GROUP * TILE), F32)
    stat = pltpu.VMEM((2, 1, NSA_GROUP * TILE), F32)
    return pl.pallas_call(
        _nsa_body,
        grid=(B, 2, nq),
        in_specs=[pl.BlockSpec((1, TILE, qw), lambda b, p, i: (b, i, p)),
                  pl.BlockSpec((1, n_cmp, LANES), lambda b, p, i: (b, 0, p)),
                  pl.BlockSpec((1, LANES, n_cmp), lambda b, p, i: (b, p, 0)),
                  k_spec(2), vt_spec(0), k_spec(3), vt_spec(1),
                  pl.BlockSpec((1, 1, LANES, TILE), lambda b, p, i: (b, i, p, 0)),
                  pl.BlockSpec((2, 2, TILE, NSA_GROUP * TILE), lambda b, p, i: (0, p, 0, 0)),
                  pl.BlockSpec((n_slc, n_cmp), lambda b, p, i: (0, 0))],
        out_specs=pl.BlockSpec((1, TILE, qw), lambda b, p, i: (b, i, p)),
        out_shape=jax.ShapeDtypeStruct((B, S, N_HEADS * HEAD_DIM), BF16),
        scratch_shapes=[pltpu.VMEM((2, n_slc, TILE), F32),
                        pltpu.VMEM((2, 2 * (TILE // SLC_BLOCK) + n_slc, NSA_GROUP * TILE), F32),
                        pltpu.VMEM((2, 2, TILE, NSA_GROUP * TILE), F32),
                        pltpu.VMEM((2, 2, TILE, NSA_GROUP * TILE), F32),
                        pltpu.VMEM((2, 3, TILE, NSA_GROUP * TILE), F32),
                        state, state, stat, stat, state],
        compiler_params=_cparams("parallel", "parallel", "arbitrary"),
        name="nsa_attention",
    )(proj, kcmp, vcmp_t, proj, vt, proj, vt, gate_t, _heads_on_lanes(bias, NSA_GROUP), _cmp_to_slc(S))


def _split_bf16(x):
    hi = x.astype(BF16)
    return hi, (x - hi.astype(F32)).astype(BF16)


ROUTER_TM = 1024
_ROW_OF_EXPERT = np.arange(N_EXPERTS).reshape(N_GROUPS, EXPERTS_PER_GROUP).T.reshape(-1)


def _router_body(x_ref, w_ref, b_ref, tri_ref, idx_ref, wt_ref, pos_ref, cnt_ref, base_ref):
    @pl.when(pl.program_id(0) == 0)
    def _():
        base_ref[...] = jnp.zeros(base_ref.shape, F32)

    x_hi, x_lo = _split_bf16(x_ref[...])
    w_hi, w_lo = _split_bf16(w_ref[...])
    logits = (lax.dot_general(w_hi, x_hi, _NT, preferred_element_type=F32)
              + lax.dot_general(w_hi, x_lo, _NT, preferred_element_type=F32)
              + lax.dot_general(w_lo, x_hi, _NT, preferred_element_type=F32)) + b_ref[:, 0:1]
    m = jnp.max(logits, axis=0, keepdims=True)
    e = jnp.exp(logits - m)
    probs = e / jnp.sum(e, axis=0, keepdims=True)
    pk = [probs[k * N_GROUPS:(k + 1) * N_GROUPS] for k in range(EXPERTS_PER_GROUP)]
    hi1, lo1 = jnp.maximum(pk[0], pk[1]), jnp.minimum(pk[0], pk[1])
    hi2, lo2 = jnp.maximum(pk[2], pk[3]), jnp.minimum(pk[2], pk[3])
    score = jnp.maximum(hi1, hi2) + jnp.maximum(jnp.minimum(hi1, hi2), jnp.maximum(lo1, lo2))
    grp = lax.broadcasted_iota(I32, score.shape, 0)
    best = jnp.min(jnp.where(score == jnp.max(score, axis=0, keepdims=True), grp, N_GROUPS),
                   axis=0, keepdims=True)
    v = [jnp.sum(jnp.where(grp == best, p, 0.0), axis=0, keepdims=True) for p in pk]
    v1 = jnp.maximum(jnp.maximum(v[0], v[1]), jnp.maximum(v[2], v[3]))
    i1 = jnp.where(v[0] == v1, 0, jnp.where(v[1] == v1, 1, jnp.where(v[2] == v1, 2, 3)))
    rest = [jnp.where(i1 == k, -1.0, v[k]) for k in range(EXPERTS_PER_GROUP)]
    v2 = jnp.maximum(jnp.maximum(rest[0], rest[1]), jnp.maximum(rest[2], rest[3]))
    i2 = jnp.where(rest[0] == v2, 0, jnp.where(rest[1] == v2, 1, jnp.where(rest[2] == v2, 2, 3)))
    tot = v1 + v2
    idx_ref[...] = jnp.concatenate([best * EXPERTS_PER_GROUP + i1, best * EXPERTS_PER_GROUP + i2], axis=0)
    wt_ref[...] = jnp.concatenate([v1 / tot, v2 / tot], axis=0)

    row = lax.broadcasted_iota(I32, logits.shape, 0)
    hot = [jnp.where(row == ik * N_GROUPS + best, 1.0, 0.0) for ik in (i1, i2)]
    both = (hot[0] + hot[1]).astype(BF16)
    run = base_ref[:, 0:1]
    pos = [[], []]
    for c in range(logits.shape[1] // LANES):
        ls = slice(c * LANES, (c + 1) * LANES)
        before = run + jnp.dot(both[:, ls], tri_ref[...], preferred_element_type=F32) - 1.0
        for k in range(2):
            pos[k].append(jnp.sum(hot[k][:, ls] * before, axis=0, keepdims=True))
        run = before[:, LANES - 1:LANES] + 1.0
    pos_ref[...] = jnp.concatenate([jnp.concatenate(pos[0], axis=1), jnp.concatenate(pos[1], axis=1)],
                                   axis=0).astype(I32)
    base_ref[...] = jnp.broadcast_to(run, base_ref.shape)
    cnt_ref[...] = jnp.broadcast_to(run, cnt_ref.shape)


def _router(x, router_w, router_b):
    N, D = x.shape
    w = router_w.T[_ROW_OF_EXPERT]
    b = jnp.broadcast_to(router_b[_ROW_OF_EXPERT][:, None], (N_EXPERTS, LANES))
    tri = jnp.asarray(np.triu(np.ones((LANES, LANES), np.float32)), BF16)
    tm = ROUTER_TM
    tok = lambda dt: jax.ShapeDtypeStruct((2, N), dt)
    tok_spec = pl.BlockSpec((2, tm), lambda i: (0, i))
    idx, wts, pos, cnt = pl.pallas_call(
        _router_body,
        grid=(N // tm,),
        in_specs=[pl.BlockSpec((tm, D), lambda i: (i, 0)),
                  pl.BlockSpec((N_EXPERTS, D), lambda i: (0, 0)),
                  pl.BlockSpec((N_EXPERTS, LANES), lambda i: (0, 0)),
                  pl.BlockSpec((LANES, LANES), lambda i: (0, 0))],
        out_specs=[tok_spec, tok_spec, tok_spec, pl.BlockSpec((N_EXPERTS, LANES), lambda i: (0, 0))],
        out_shape=[tok(I32), tok(F32), tok(I32), jax.ShapeDtypeStruct((N_EXPERTS, LANES), F32)],
        scratch_shapes=[pltpu.VMEM((N_EXPERTS, LANES), F32)],
        compiler_params=_cparams("arbitrary"),
        name="moe_router",
    )(x, w, b, tri)
    counts = cnt[np.argsort(_ROW_OF_EXPERT), 0].astype(I32)
    return idx, wts, pos, counts


def _expert_body(blk_e_ref, n_used_ref, x_ref, wg_ref, wu_ref, wd_ref, o_ref, wg_b, wu_b, wd_b):
    i = pl.program_id(0)

    @pl.when((i == 0) | (blk_e_ref[i] != blk_e_ref[jnp.maximum(i - 1, 0)]))
    def _():
        wg_b[...] = wg_ref[0, 0].astype(BF16)
        wu_b[...] = wu_ref[0, 0].astype(BF16)
        wd_b[...] = wd_ref[0, 0].astype(BF16)

    @pl.when(i < n_used_ref[0])
    def _():
        x = x_ref[...].astype(BF16)
        gate = jnp.dot(x, wg_b[...], preferred_element_type=F32)
        up = jnp.dot(x, wu_b[...], preferred_element_type=F32)
        hid = (gate * jax.nn.sigmoid(gate) * up).astype(BF16)
        o_ref[...] = jnp.dot(hid, wd_b[...], preferred_element_type=F32)

    @pl.when(i >= n_used_ref[0])
    def _():
        o_ref[...] = jnp.zeros(o_ref.shape, o_ref.dtype)


def _experts(xs, blk_e, n_used, wg, wu, wd, layer):
    R, D = xs.shape
    n_blk = R // MOE_TB

    def live(i, be, nu):
        return jnp.minimum(i, nu[0] - 1)

    grid_spec = pltpu.PrefetchScalarGridSpec(
        num_scalar_prefetch=2,
        grid=(n_blk,),
        in_specs=[pl.BlockSpec((MOE_TB, D), lambda i, be, nu: (live(i, be, nu), 0)),
                  pl.BlockSpec((1, 1, D, D_EXPERT), lambda i, be, nu: (layer, be[i], 0, 0)),
                  pl.BlockSpec((1, 1, D, D_EXPERT), lambda i, be, nu: (layer, be[i], 0, 0)),
                  pl.BlockSpec((1, 1, D_EXPERT, D), lambda i, be, nu: (layer, be[i], 0, 0))],
        out_specs=pl.BlockSpec((MOE_TB, D), lambda i, be, nu: (i, 0)),
        scratch_shapes=[pltpu.VMEM((D, D_EXPERT), BF16), pltpu.VMEM((D, D_EXPERT), BF16),
                        pltpu.VMEM((D_EXPERT, D), BF16)],
    )
    return pl.pallas_call(
        _expert_body,
        grid_spec=grid_spec,
        out_shape=jax.ShapeDtypeStruct((R, D), F32),
        compiler_params=_cparams("arbitrary"),
        name="moe_experts",
    )(blk_e, n_used, xs, wg, wu, wd)


def _combine_ln_body(x_ref, y0_ref, y1_ref, wt_ref, g_ref, b_ref, o_ref, ob_ref):
    ffn = y0_ref[...] * wt_ref[:, 0:1] + y1_ref[...] * wt_ref[:, HEAD_DIM:HEAD_DIM + 1]
    out = _layer_norm_rows(DEEPNORM_ALPHA * x_ref[...] + ffn, g_ref[...], b_ref[...])
    o_ref[...] = out
    ob_ref[...] = out.astype(BF16)


def _combine_ln(x, y0, y1, wt, g, b):
    M, D = x.shape
    row = pl.BlockSpec((LN_TM, D), lambda i: (i, 0))
    vec = pl.BlockSpec((1, D), lambda i: (0, 0))
    return pl.pallas_call(
        _combine_ln_body,
        grid=(M // LN_TM,),
        in_specs=[row, row, row, pl.BlockSpec((LN_TM, LANES), lambda i: (i, 0)), vec, vec],
        out_specs=[row, row],
        out_shape=[jax.ShapeDtypeStruct((M, D), F32), jax.ShapeDtypeStruct((M, D), BF16)],
        compiler_params=_cparams("parallel"),
        name="moe_combine_ln",
    )(x, y0, y1, wt, g.reshape(1, D), b.reshape(1, D))


def _moe_ln(h, router_w, router_b, wg, wu, wd, layer, g, b):
    N, D = h.shape
    A = 2 * N
    idx, wts, pos, counts = _router(h, router_w, router_b)
    starts = jnp.cumsum(counts) - counts
    padded = (counts + MOE_TB - 1) // MOE_TB * MOE_TB
    pends = jnp.cumsum(padded)
    pstarts = pends - padded
    R = A + N_EXPERTS * MOE_TB
    n_blk = R // MOE_TB
    experts = jnp.arange(N_EXPERTS, dtype=I32)
    dest = pos + jnp.sum(jnp.where(idx[None] == experts[:, None, None], pstarts[:, None, None], 0), axis=0)
    tok = jnp.broadcast_to(jnp.arange(N, dtype=I32)[None, :], (2, N))
    _, tok_sorted = lax.sort_key_val(dest.reshape(A), tok.reshape(A))
    blk_row0 = jnp.arange(n_blk, dtype=I32) * MOE_TB
    blk_e = jnp.minimum(jnp.sum((pends[None, :] <= blk_row0[:, None]).astype(I32), axis=1), N_EXPERTS - 1)
    hot = blk_e[:, None] == experts[None, :]
    compact0 = blk_row0 + jnp.sum(jnp.where(hot, (starts - pstarts)[None, :], 0), axis=1)
    compact = jnp.clip(compact0[:, None] + jnp.arange(MOE_TB, dtype=I32)[None, :], 0, A - 1).reshape(R)
    n_used = (pends[-1:] // MOE_TB).astype(I32)
    xs = h[tok_sorted[compact]]
    yb = _experts(xs, blk_e, n_used, wg, wu, wd, layer)
    wt = jnp.concatenate([jnp.broadcast_to(wts[k][:, None], (N, HEAD_DIM)) for k in range(2)], axis=1)
    return _combine_ln(h, yb[dest[0]], yb[dest[1]], wt, g, b)


def _moba_layer(h, w_in, w_out, bias, g, b, B, S):
    HD = N_HEADS * HEAD_DIM
    qk = _matmul(h, w_in[:, :2 * HD].astype(BF16), BF16).reshape(B, S, 2 * HD)
    vt = _matmul_t(w_in[:, 2 * HD:].T.astype(BF16), h, B, S, BF16)
    att = _moba_attention(qk, vt, bias)
    return _proj_ln(att.reshape(B * S, HD), w_out.astype(BF16), h, g, b)


def _nsa_layer(h, hb, w_in, w_out, pos_k, pos_v, ck_w1, ck_w2, cv_w1, cv_w2, bias, g, b, B, S):
    HD = N_HEADS * HEAD_DIM
    kvw = NSA_KV_HEADS * HEAD_DIM
    col = lambda k: slice(HD + k * kvw, HD + (k + 1) * kvw)
    w_rows = jnp.concatenate([w_in[:, :HD + 2 * kvw], w_in[:, col(2)], w_in[:, col(4)]], axis=1)
    proj = _matmul(hb, w_rows.astype(BF16), BF16).reshape(B, S, HD + 4 * kvw)
    w_vt = jnp.concatenate([w_in[:, col(3)], w_in[:, col(5)]], axis=1).T
    vt = _matmul_t(w_vt.astype(BF16), hb, B, S, BF16)
    per_pair = 3 * N_HEADS // 2
    wg = w_in[:, HD + 6 * kvw:].reshape(D_MODEL, 2, per_pair)
    wg = jnp.pad(wg, ((0, 0), (0, 0), (0, LANES - per_pair))).reshape(D_MODEL, 2 * LANES).T
    gate_t = _matmul_t(wg.astype(BF16), hb, B, S, F32)

    def grouped(t):
        t = t.reshape(B, S, NSA_KV_HEADS, HEAD_DIM).transpose(0, 2, 1, 3)
        return t.reshape(B, NSA_KV_HEADS, S // CMP_STRIDE, CMP_STRIDE * HEAD_DIM)

    kcmp = _compress(grouped(proj[..., HD:HD + kvw]), pos_k, ck_w1, ck_w2)
    vcmp = _compress(grouped(proj[..., HD + kvw:HD + 2 * kvw]), pos_v, cv_w1, cv_w2)
    n_cmp = kcmp.shape[2]
    kcmp = kcmp.transpose(0, 2, 1, 3).reshape(B, n_cmp, kvw)
    vcmp_t = vcmp.transpose(0, 1, 3, 2).reshape(B, kvw, n_cmp)
    att = _nsa_attention(proj, vt, gate_t, kcmp, vcmp_t, bias)
    return _proj_ln(att.reshape(B * S, HD), w_out.astype(BF16), h, g, b)


def kernel(x, rel_bias, router_w, router_b, ln_g, ln_b, moba_w_in, moba_w_out, nsa_w_in, nsa_w_out,
           nsa_pos_k, nsa_pos_v, nsa_ck_w1, nsa_ck_w2, nsa_cv_w1, nsa_cv_w2,
           moe_w_gate, moe_w_up, moe_w_down):
    B, S, D = x.shape
    bias = _bias_tiles(rel_bias)
    h = x.reshape(B * S, D)
    h, hb = _moba_layer(h, moba_w_in[0], moba_w_out[0], bias, ln_g[0, 0], ln_b[0, 0], B, S)
    h, hb = _moe_ln(h, router_w, router_b, moe_w_gate, moe_w_up, moe_w_down, 0,
                    ln_g[0, 1], ln_b[0, 1])
    h, hb = _nsa_layer(h, hb, nsa_w_in[0], nsa_w_out[0], nsa_pos_k[0], nsa_pos_v[0],
                       nsa_ck_w1[0], nsa_ck_w2[0], nsa_cv_w1[0], nsa_cv_w2[0],
                       bias, ln_g[1, 0], ln_b[1, 0], B, S)
    h, hb = _moe_ln(h, router_w, router_b, moe_w_gate, moe_w_up, moe_w_down, 1,
                    ln_g[1, 1], ln_b[1, 1])
    return h.reshape(B, S, D)
```

```python
import math

import numpy as np
import jax
import jax.numpy as jnp
from jax import lax
from jax.experimental import pallas as pl
from jax.experimental.pallas import tpu as pltpu

F32, BF16, I32 = jnp.float32, jnp.bfloat16, jnp.int32

D_MODEL = 1024
N_HEADS = 16
HEAD_DIM = 64
DEPTH = 2
NEG_INF = -1e30
LN_EPS = 1e-5
MOBA_BLOCK = 256
MOBA_TOPK = 3
NSA_KV_HEADS = 4
NSA_GROUP = N_HEADS // NSA_KV_HEADS
CMP_LEN = 32
CMP_STRIDE = 16
CMP_HIDDEN = 256
SLC_BLOCK = 64
SLC_TOPN = 16
SLC_LOCAL = 2
WINDOW = 512
REL_BUCKETS = 32
REL_MAX_DIST = 128
N_EXPERTS = 32
N_GROUPS = 8
EXPERTS_PER_GROUP = N_EXPERTS // N_GROUPS
D_EXPERT = 512
DEEPNORM_ALPHA = (2 * DEPTH) ** 0.25
LOG2E = math.log2(math.e)
Q_SCALE = HEAD_DIM ** -0.5 * LOG2E

LANES = 128
SUBLANES = 8
TILE = 256
MM_TM = 1024
MM_TN = 1024
LN_TM = 512
MOE_TB = 256
VMEM_LIMIT = 48 * 1024 * 1024

_NT = (((1,), (1,)), ((), ()))


def _cparams(*sem):
    return pltpu.CompilerParams(dimension_semantics=sem, vmem_limit_bytes=VMEM_LIMIT)


def _mm_body(a_ref, b_ref, c_ref, o_ref):
    acc = jnp.dot(a_ref[...].astype(BF16), b_ref[...], preferred_element_type=F32)
    o_ref[...] = (acc * c_ref[...]).astype(o_ref.dtype)


def _matmul(a, b, col_scale, out_dtype):
    M, K = a.shape
    N = b.shape[1]
    tn = min(MM_TN, N)
    return pl.pallas_call(
        _mm_body,
        grid=(M // MM_TM, N // tn),
        in_specs=[pl.BlockSpec((MM_TM, K), lambda i, j: (i, 0)),
                  pl.BlockSpec((K, tn), lambda i, j: (0, j)),
                  pl.BlockSpec((1, tn), lambda i, j: (0, j))],
        out_specs=pl.BlockSpec((MM_TM, tn), lambda i, j: (i, j)),
        out_shape=jax.ShapeDtypeStruct((M, N), out_dtype),
        compiler_params=_cparams("parallel", "arbitrary"),
        name="in_proj",
    )(a, b, col_scale.reshape(1, N))


def _query_scale(n_query_cols, n_cols):
    return jnp.where(jnp.arange(n_cols) < n_query_cols, Q_SCALE, 1.0).astype(F32)


def _mm_t_body(w_ref, a_ref, o_ref):
    r = lax.dot_general(w_ref[...], a_ref[...].astype(BF16), _NT, preferred_element_type=F32)
    for t in range(o_ref.shape[1]):
        o_ref[0, t] = r[:, t * TILE:(t + 1) * TILE].astype(o_ref.dtype)


def _matmul_t(w_t, a, B, S, out_dtype):
    Nout, K = w_t.shape
    tn = min(MM_TN, Nout)
    per_seq = S // MM_TM
    sub = MM_TM // TILE
    return pl.pallas_call(
        _mm_t_body,
        grid=(B * per_seq, Nout // tn),
        in_specs=[pl.BlockSpec((tn, K), lambda i, j: (j, 0)),
                  pl.BlockSpec((MM_TM, K), lambda i, j: (i, 0))],
        out_specs=pl.BlockSpec((1, sub, tn, TILE), lambda i, j: (i // per_seq, i % per_seq, j, 0)),
        out_shape=jax.ShapeDtypeStruct((B, S // TILE, Nout, TILE), out_dtype),
        compiler_params=_cparams("parallel", "arbitrary"),
        name="in_proj_t",
    )(w_t, a)


def _layer_norm_rows(z, g, b):
    mu = jnp.mean(z, axis=-1, keepdims=True)
    zc = z - mu
    var = jnp.mean(zc * zc, axis=-1, keepdims=True)
    return zc * lax.rsqrt(var + LN_EPS) * g + b


def _proj_ln_body(a_ref, w_ref, x_ref, g_ref, b_ref, o_ref, ob_ref):
    y = jnp.dot(a_ref[...], w_ref[...], preferred_element_type=F32)
    out = _layer_norm_rows(DEEPNORM_ALPHA * x_ref[...] + y, g_ref[...], b_ref[...])
    o_ref[...] = out
    ob_ref[...] = out.astype(BF16)


def _proj_ln(a, w, x, g, b):
    M, K = a.shape
    D = w.shape[1]
    row = pl.BlockSpec((LN_TM, D), lambda i: (i, 0))
    vec = pl.BlockSpec((1, D), lambda i: (0, 0))
    return pl.pallas_call(
        _proj_ln_body,
        grid=(M // LN_TM,),
        in_specs=[pl.BlockSpec((LN_TM, K), lambda i: (i, 0)),
                  pl.BlockSpec((K, D), lambda i: (0, 0)), row, vec, vec],
        out_specs=[row, row],
        out_shape=[jax.ShapeDtypeStruct((M, D), F32), jax.ShapeDtypeStruct((M, D), BF16)],
        compiler_params=_cparams("parallel"),
        name="out_proj_ln",
    )(a, w, x, g.reshape(1, D), b.reshape(1, D))


def _t5_bucket_np(rel):
    n = np.maximum(rel, 0)
    max_exact = REL_BUCKETS // 2
    nf = np.maximum(n, 1).astype(np.float32)
    large = max_exact + (np.log(nf / np.float32(max_exact))
                         / np.float32(math.log(REL_MAX_DIST / max_exact))
                         * np.float32(REL_BUCKETS - max_exact)).astype(np.int32)
    large = np.minimum(large, REL_BUCKETS - 1)
    return np.where(n < max_exact, n, large).astype(np.int32)


def _bias_body(tbl_ref, bk_ref, o_ref):
    h = pl.program_id(0)
    for dl in range(2):
        bk = bk_ref[dl]
        acc = jnp.zeros((TILE, TILE), F32)
        for b in range(REL_BUCKETS):
            acc = jnp.where(bk == b, tbl_ref[h * REL_BUCKETS + b], acc)
        o_ref[dl, 0] = acc * LOG2E


def _bias_tiles(rel_bias):
    key = np.arange(TILE)[:, None]
    qry = np.arange(TILE)[None, :]
    assert int(_t5_bucket_np(np.array(TILE + 1))) == REL_BUCKETS - 1
    bk = np.stack([_t5_bucket_np(qry - key), _t5_bucket_np(TILE + qry - key)])
    return pl.pallas_call(
        _bias_body,
        grid=(N_HEADS,),
        in_specs=[pl.BlockSpec(memory_space=pltpu.SMEM),
                  pl.BlockSpec((2, TILE, TILE), lambda h: (0, 0, 0))],
        out_specs=pl.BlockSpec((2, 1, TILE, TILE), lambda h: (0, h, 0, 0)),
        out_shape=jax.ShapeDtypeStruct((2, N_HEADS, TILE, TILE), F32),
        name="t5_bias_tiles",
    )(rel_bias.T.reshape(-1), jnp.asarray(bk))


def _heads_on_lanes(bias, per_block):
    two, H, T, _ = bias.shape
    b = bias.reshape(two, H // per_block, per_block, T, T).transpose(0, 1, 3, 2, 4)
    return b.reshape(two, H // per_block, T, per_block * T)


def _init_state(m_ref, l_ref, acc_ref):
    m_ref[...] = jnp.full(m_ref.shape, NEG_INF, F32)
    l_ref[...] = jnp.zeros(l_ref.shape, F32)
    acc_ref[...] = jnp.zeros(acc_ref.shape, F32)


def _rank_before(vals, rows):
    idx = lax.broadcasted_iota(I32, vals.shape, 0)
    rank = jnp.zeros(vals.shape, I32)
    for m in range(rows):
        row = vals[m:m + 1, :]
        beats = (row > vals) | ((row == vals) & (idx > m))
        rank = rank + jnp.where(beats, 1, 0)
    return rank


MOBA_STREAMS = 2


def _softmax_pv(scores, adds, vts, m_ref, l_ref, acc_ref):
    def fold(x, op):
        return op(x.reshape(x.shape[0] // SUBLANES, SUBLANES, x.shape[1]), axis=0)

    m_prev = m_ref[...]
    m_part = None
    for s, add in zip(scores, adds):
        part = fold(s, jnp.max) + add
        m_part = part if m_part is None else jnp.maximum(m_part, part)
    m_new = jnp.maximum(m_prev, jnp.max(m_part, axis=0, keepdims=True))
    a = jnp.exp2(m_prev - m_new)
    probs = [jnp.exp2(s - (m_new - add)) for s, add in zip(scores, adds)]
    l_part = fold(probs[0], jnp.sum)
    for p in probs[1:]:
        l_part = l_part + fold(p, jnp.sum)
    l_ref[...] = a * l_ref[...] + jnp.sum(l_part, axis=0, keepdims=True)
    pv = jnp.dot(jnp.concatenate(vts, axis=1), jnp.concatenate([p.astype(BF16) for p in probs], axis=0),
                 preferred_element_type=F32)
    acc_ref[...] = a * acc_ref[...] + pv
    m_ref[...] = m_new


def _moba_body(q_ref, k_ref, vt_ref, bias_ref, o_ref, kmean_ref, radd_ref, sa_ref, sb_ref, m_ref, l_ref, acc_ref):
    i = pl.program_id(2)
    nb = k_ref.shape[1] // TILE
    streams = range(MOBA_STREAMS)
    lanes_of = lambda s: slice(s * LANES, (s + 1) * LANES)

    @pl.when(i == 0)
    def _():
        for s in streams:
            for n in range(nb):
                kb = k_ref[0, n * TILE:(n + 1) * TILE, lanes_of(s)].astype(F32)
                kmean_ref[s, n:n + 1, :] = jnp.sum(kb, axis=0, keepdims=True) * (1.0 / TILE)

    n_far = jnp.maximum(i - 1, 0)
    n_far_groups = (n_far + 1) >> 1
    last = nb - 1

    def key_tile(t, s):
        return k_ref[0, pl.ds(pl.multiple_of(t * TILE, TILE), TILE), lanes_of(s)]

    def far_tiles(j):
        return 2 * j, jnp.minimum(2 * j + 1, last)

    lane = lax.broadcasted_iota(I32, (TILE, LANES), 1)
    key = lax.broadcasted_iota(I32, (TILE, 2 * TILE), 0)
    qry = lax.broadcasted_iota(I32, (TILE, 2 * TILE), 1) & (TILE - 1)
    causal_neg = jnp.where(key <= qry, 0.0, NEG_INF)
    t_near = jnp.maximum(i - 1, 0)
    q2s = []
    for s in streams:
        q = q_ref[0, :, lanes_of(s)]
        zero = jnp.zeros_like(q)
        q2 = jnp.concatenate([jnp.where(lane < HEAD_DIM, q, zero),
                              jnp.where(lane >= HEAD_DIM, q, zero)], axis=0)
        q2s.append(q2)
        sa_ref[s, 0] = (lax.dot_general(key_tile(i, s), q2, _NT, preferred_element_type=F32)
                        + (bias_ref[0, s] + causal_neg))
        sa_ref[s, 1] = lax.dot_general(key_tile(t_near, s), q2, _NT, preferred_element_type=F32) + bias_ref[1, s]
        km = kmean_ref[s]
        k_hi = km.astype(BF16)
        k_lo = (km - k_hi.astype(F32)).astype(BF16)
        gate = (lax.dot_general(k_hi, q2, _NT, preferred_element_type=F32)
                + lax.dot_general(k_lo, q2, _NT, preferred_element_type=F32))
        blk = lax.broadcasted_iota(I32, gate.shape, 0)
        gate = jnp.where(blk < i, gate, -jnp.inf)
        rank = _rank_before(gate, nb)
        neg = jnp.where((rank < MOBA_TOPK) & (blk < i), 0.0, NEG_INF)
        far_bias = bias_ref[1, s, 0:1, :]
        near_row = jnp.full((1, 2 * TILE), NEG_INF, F32)
        radd_ref[s, 0:1, :] = jnp.zeros((1, 2 * TILE), F32)
        for n in range(nb):
            row = neg[n:n + 1, :]
            near_row = jnp.where(n == i - 1, row, near_row)
            radd_ref[s, 2 + n:3 + n, :] = jnp.where(n < n_far, far_bias + row, NEG_INF)
        radd_ref[s, 1:2, :] = near_row
        _init_state(m_ref.at[s], l_ref.at[s], acc_ref.at[s])

    def update(j, s, buf):
        first = j == 0
        t0 = jnp.where(first, i, 2 * j - 2)
        t1 = jnp.where(first, t_near, jnp.minimum(2 * j - 1, last))
        _softmax_pv([buf[s, 0], buf[s, 1]],
                    [radd_ref[s, pl.ds(2 * j, 1), :], radd_ref[s, pl.ds(2 * j + 1, 1), :]],
                    [vt_ref[0, t0, lanes_of(s), :], vt_ref[0, t1, lanes_of(s), :]],
                    m_ref.at[s], l_ref.at[s], acc_ref.at[s])

    def step(j, src, dst):
        for s in streams:
            ta, tb = far_tiles(j)
            dst[s, 0] = lax.dot_general(key_tile(ta, s), q2s[s], _NT, preferred_element_type=F32)
            dst[s, 1] = lax.dot_general(key_tile(tb, s), q2s[s], _NT, preferred_element_type=F32)
            update(j, s, src)

    def two_steps(jj, carry):
        step(2 * jj, sa_ref, sb_ref)
        step(2 * jj + 1, sb_ref, sa_ref)
        return carry

    lax.fori_loop(0, n_far_groups >> 1, two_steps, 0)

    @pl.when((n_far_groups & 1) == 1)
    def _():
        step(n_far_groups - 1, sa_ref, sb_ref)
        for s in streams:
            update(n_far_groups, s, sb_ref)

    @pl.when((n_far_groups & 1) == 0)
    def _():
        for s in streams:
            update(n_far_groups, s, sa_ref)

    for s in streams:
        o = acc_ref[s] / l_ref[s]
        o = jnp.concatenate([o[:HEAD_DIM, :TILE], o[HEAD_DIM:, TILE:]], axis=0)
        o_ref[0, :, lanes_of(s)] = o.T.astype(o_ref.dtype)


def _moba_attention(qk, vt, bias):
    B, S, _ = qk.shape
    n_steps = N_HEADS // 2 // MOBA_STREAMS
    nq = S // TILE
    w = MOBA_STREAMS * LANES
    return pl.pallas_call(
        _moba_body,
        grid=(B, n_steps, nq),
        in_specs=[pl.BlockSpec((1, TILE, w), lambda b, p, i: (b, i, p)),
                  pl.BlockSpec((1, S, w), lambda b, p, i: (b, 0, n_steps + p)),
                  pl.BlockSpec((1, nq, w, TILE), lambda b, p, i: (b, 0, p, 0)),
                  pl.BlockSpec((2, MOBA_STREAMS, TILE, 2 * TILE), lambda b, p, i: (0, p, 0, 0))],
        out_specs=pl.BlockSpec((1, TILE, w), lambda b, p, i: (b, i, p)),
        out_shape=jax.ShapeDtypeStruct((B, S, N_HEADS * HEAD_DIM), BF16),
        scratch_shapes=[pltpu.VMEM((MOBA_STREAMS, nq, LANES), F32),
                        pltpu.VMEM((MOBA_STREAMS, 2 + nq, 2 * TILE), F32),
                        pltpu.VMEM((MOBA_STREAMS, 2, TILE, 2 * TILE), F32),
                        pltpu.VMEM((MOBA_STREAMS, 2, TILE, 2 * TILE), F32),
                        pltpu.VMEM((MOBA_STREAMS, 1, 2 * TILE), F32),
                        pltpu.VMEM((MOBA_STREAMS, 1, 2 * TILE), F32),
                        pltpu.VMEM((MOBA_STREAMS, LANES, 2 * TILE), F32)],
        compiler_params=_cparams("parallel", "parallel", "arbitrary"),
        name="moba_attention",
    )(qk, qk, vt, _heads_on_lanes(bias, 2))


def _gelu_tanh(x):
    return 0.5 * x * (1.0 + jnp.tanh(math.sqrt(2.0 / math.pi) * (x + 0.044715 * (x * x * x))))


def _compress_body(t_ref, pos_ref, w1_ref, w2_ref, o_ref):
    groups = t_ref.shape[2]
    half = t_ref.shape[3]
    t = t_ref[0].reshape(NSA_KV_HEADS * groups, half).astype(F32)
    first = jnp.dot((t + pos_ref[0:1, :]).astype(BF16), w1_ref[0:half, :], preferred_element_type=F32)
    second = jnp.dot((t + pos_ref[1:2, :]).astype(BF16), w1_ref[half:2 * half, :],
                     preferred_element_type=F32)
    rows = first.shape[0]
    pre = first + pltpu.roll(second, rows - 1, 0)
    out = jnp.dot(_gelu_tanh(pre).astype(BF16), w2_ref[...], preferred_element_type=F32)
    for h in range(NSA_KV_HEADS):
        o_ref[0, h] = out[h * groups:(h + 1) * groups].astype(o_ref.dtype)


def _compress(t, pos, w1, w2):
    B, Hkv, groups, half = t.shape
    return pl.pallas_call(
        _compress_body,
        grid=(B,),
        in_specs=[pl.BlockSpec((1, Hkv, groups, half), lambda b: (b, 0, 0, 0)),
                  pl.BlockSpec((2, half), lambda b: (0, 0)),
                  pl.BlockSpec((2 * half, CMP_HIDDEN), lambda b: (0, 0)),
                  pl.BlockSpec((CMP_HIDDEN, HEAD_DIM), lambda b: (0, 0))],
        out_specs=pl.BlockSpec((1, Hkv, groups, HEAD_DIM), lambda b: (b, 0, 0, 0)),
        out_shape=jax.ShapeDtypeStruct((B, Hkv, groups, HEAD_DIM), BF16),
        compiler_params=_cparams("parallel"),
        name="nsa_compress",
    )(t, pos.reshape(2, half), w1.astype(BF16), w2.astype(BF16))


def _swap_halves(x):
    return jnp.concatenate([x[:, HEAD_DIM:], x[:, :HEAD_DIM]], axis=1)


def _group_lanes(x):
    return jnp.concatenate([x] * NSA_GROUP, axis=1)


def _nsa_body(q_ref, kc_ref, vct_ref, ks_ref, vst_ref, kw_ref, vwt_ref, gt_ref, bias_ref, c2s_ref,
              o_ref, selneg_ref, radd_ref, sa_ref, sb_ref, sw_ref, oc_ref, os_ref, m_ref, l_ref, acc_ref):
    i = pl.program_id(2)
    nb = ks_ref.shape[1] // TILE
    n_cmp = kc_ref.shape[1]
    n_slc = c2s_ref.shape[0]
    per_tile = TILE // SLC_BLOCK
    cols = NSA_GROUP * TILE
    kv_heads = range(2)
    n_far = jnp.maximum(i - 1, 0)
    n_far_groups = (n_far + 1) >> 1
    last = nb - 1
    t_near = jnp.maximum(i - 1, 0)
    t_edge = jnp.maximum(i - 2, 0)

    def key_tile(k_ref, t):
        return k_ref[0, pl.ds(pl.multiple_of(t * TILE, TILE), TILE), :]

    def scores_of(k_ref, t, a):
        return lax.dot_general(key_tile(k_ref, t), q4s[a], _NT, preferred_element_type=F32)

    lane = lax.broadcasted_iota(I32, (TILE, LANES), 1)
    lo_half = lane < HEAD_DIM
    qpos = i * TILE + (lax.broadcasted_iota(I32, (n_cmp, cols), 1) & (TILE - 1))
    cmp_valid = CMP_STRIDE * lax.broadcasted_iota(I32, (n_cmp, cols), 0) + (CMP_LEN - 1) <= qpos
    key = lax.broadcasted_iota(I32, (TILE, TILE), 0)
    qry = lax.broadcasted_iota(I32, (TILE, TILE), 1)
    diag_neg = _group_lanes(jnp.where(key <= qry, 0.0, NEG_INF))
    edge_neg = _group_lanes(jnp.where(key > qry, 0.0, NEG_INF))
    qall = q_ref[0]
    q4s = []

    for a in kv_heads:
        keep = lo_half if a == 0 else jnp.logical_not(lo_half)
        heads = []
        for g in range(NSA_GROUP):
            cb = a * 2 + g // 2
            x = qall[:, cb * LANES:(cb + 1) * LANES]
            if g % 2 != a:
                x = _swap_halves(x)
            heads.append(jnp.where(keep, x, jnp.zeros_like(x)))
        q4s.append(jnp.concatenate(heads, axis=0))
        far_bias = bias_ref[1, a, 0:1, :]

        sa_ref[a, 0] = scores_of(ks_ref, i, a) + (bias_ref[0, a] + diag_neg)
        sa_ref[a, 1] = scores_of(ks_ref, t_near, a) + bias_ref[1, a]
        sw_ref[a, 0] = scores_of(kw_ref, i, a) + (bias_ref[0, a] + diag_neg)
        sw_ref[a, 1] = scores_of(kw_ref, t_near, a) + bias_ref[1, a]
        sw_ref[a, 2] = scores_of(kw_ref, t_edge, a) + (far_bias + edge_neg)

        s_c = lax.dot_general(kc_ref[0], q4s[a], _NT, preferred_element_type=F32)
        s_c = jnp.where(cmp_valid, s_c, NEG_INF)
        m_c = jnp.max(s_c, axis=0, keepdims=True)
        e_c = jnp.where(cmp_valid, jnp.exp2(s_c - m_c), 0.0)
        l_c = jnp.sum(e_c, axis=0, keepdims=True)
        p_c = e_c / jnp.where(l_c > 0.0, l_c, 1.0)
        oc_ref[a] = jnp.dot(vct_ref[0], p_c.astype(BF16), preferred_element_type=F32)

        p_sum = p_c[:, 0:TILE]
        for g in range(1, NSA_GROUP):
            p_sum = p_sum + p_c[:, g * TILE:(g + 1) * TILE]
        p_hi = p_sum.astype(BF16)
        p_lo = (p_sum - p_hi.astype(F32)).astype(BF16)
        imp = (jnp.dot(c2s_ref[...], p_hi, preferred_element_type=F32)
               + jnp.dot(c2s_ref[...], p_lo, preferred_element_type=F32))
        j = lax.broadcasted_iota(I32, imp.shape, 0)
        qb = (i * TILE + lax.broadcasted_iota(I32, imp.shape, 1)) >> int(math.log2(SLC_BLOCK))
        forced = (j == 0) | ((j <= qb) & (j > qb - SLC_LOCAL))
        imp = jnp.where(forced, jnp.inf, jnp.where(j > qb, -jnp.inf, imp))
        rank = _rank_before(imp, n_slc)
        selneg = jnp.where((rank < SLC_TOPN) & (j <= qb), 0.0, NEG_INF)
        selneg_ref[a] = selneg
        for c in range(per_tile):
            radd_ref[a, c:c + 1, :] = _group_lanes(selneg_ref[a, pl.ds(per_tile * i + c, 1), :])
            near_row = _group_lanes(selneg_ref[a, pl.ds(per_tile * t_near + c, 1), :])
            radd_ref[a, per_tile + c:per_tile + c + 1, :] = jnp.where(i >= 1, near_row, NEG_INF)
        for blk in range(n_slc):
            row = far_bias + _group_lanes(selneg[blk:blk + 1, :])
            r = 2 * per_tile + blk
            radd_ref[a, r:r + 1, :] = jnp.where(blk // per_tile < n_far, row, NEG_INF)
        _init_state(m_ref.at[a], l_ref.at[a], acc_ref.at[a])

    def update(jg, a, buf):
        first = jg == 0
        t0 = jnp.where(first, i, 2 * jg - 2)
        t1 = jnp.where(first, t_near, jnp.minimum(2 * jg - 1, last))
        scores, adds = [], []
        for t in range(2):
            for c in range(per_tile):
                scores.append(buf[a, t, c * SLC_BLOCK:(c + 1) * SLC_BLOCK, :])
                adds.append(radd_ref[a, pl.ds(2 * per_tile * jg + per_tile * t + c, 1), :])
        _softmax_pv(scores, adds, [vst_ref[0, t0], vst_ref[0, t1]], m_ref.at[a], l_ref.at[a], acc_ref.at[a])

    def step(jg, src, dst):
        for a in kv_heads:
            dst[a, 0] = scores_of(ks_ref, 2 * jg, a)
            dst[a, 1] = scores_of(ks_ref, jnp.minimum(2 * jg + 1, last), a)
            update(jg, a, src)

    def two_steps(jj, carry):
        step(2 * jj, sa_ref, sb_ref)
        step(2 * jj + 1, sb_ref, sa_ref)
        return carry

    lax.fori_loop(0, n_far_groups >> 1, two_steps, 0)

    @pl.when((n_far_groups & 1) == 1)
    def _():
        step(n_far_groups - 1, sa_ref, sb_ref)
        for a in kv_heads:
            update(n_far_groups, a, sb_ref)

    @pl.when((n_far_groups & 1) == 0)
    def _():
        for a in kv_heads:
            update(n_far_groups, a, sa_ref)

    gates = jax.nn.sigmoid(gt_ref[0, 0])
    zero_row = jnp.zeros((1, cols), F32)
    pieces = []
    for a in kv_heads:
        os_ref[a] = acc_ref[a] / l_ref[a]
        _init_state(m_ref.at[a], l_ref.at[a], acc_ref.at[a])
        _softmax_pv([sw_ref[a, 0], sw_ref[a, 1], sw_ref[a, 2]],
                    [zero_row, zero_row + jnp.where(i >= 1, 0.0, NEG_INF), zero_row + jnp.where(i >= 2, 0.0, NEG_INF)],
                    [vwt_ref[0, i], vwt_ref[0, t_near], vwt_ref[0, t_edge]],
                    m_ref.at[a], l_ref.at[a], acc_ref.at[a])
        o_w = acc_ref[a] / l_ref[a]
        rs = slice(a * HEAD_DIM, (a + 1) * HEAD_DIM)
        for g in range(NSA_GROUP):
            c0 = 3 * (NSA_GROUP * a + g)
            ls = slice(g * TILE, (g + 1) * TILE)
            pieces.append(gates[c0:c0 + 1, :] * oc_ref[a, rs, ls] + gates[c0 + 1:c0 + 2, :] * os_ref[a, rs, ls]
                          + gates[c0 + 2:c0 + 3, :] * o_w[rs, ls])
    o_ref[0] = jnp.concatenate(pieces, axis=0).T.astype(o_ref.dtype)


def _cmp_to_slc(S):
    n_cmp_pad = S // CMP_STRIDE
    n_slc = S // SLC_BLOCK
    ci = np.arange(n_cmp_pad)[:, None] * CMP_STRIDE
    sj = np.arange(n_slc)[None, :] * SLC_BLOCK
    c2s = ((ci < sj + SLC_BLOCK) & (ci + CMP_LEN > sj)).astype(np.float32)
    c2s[(S - CMP_LEN) // CMP_STRIDE + 1:] = 0.0
    return jnp.asarray(c2s.T, BF16)


def _nsa_attention(proj, vt, gate_t, kcmp, vcmp_t, bias):
    B, S, _ = proj.shape
    nq = S // TILE
    n_cmp = kcmp.shape[1]
    n_slc = S // SLC_BLOCK
    qw = 2 * NSA_GROUP * HEAD_DIM
    q_blocks = N_HEADS * HEAD_DIM // LANES
    kv_blocks = NSA_KV_HEADS * HEAD_DIM // LANES

    def k_spec(which):
        base = q_blocks + which * kv_blocks
        return pl.BlockSpec((1, S, LANES), lambda b, p, i: (b, 0, base + p))

    def vt_spec(which):
        base = which * kv_blocks
        return pl.BlockSpec((1, nq, LANES, TILE), lambda b, p, i: (b, 0, base + p, 0))

    state = pltpu.VMEM((2, LANES, NSA_GROUP * TILE), F32)
    stat = pltpu.VMEM((2, 1, NSA_GROUP * TILE), F32)
    return pl.pallas_call(
        _nsa_body,
        grid=(B, 2, nq),
        in_specs=[pl.BlockSpec((1, TILE, qw), lambda b, p, i: (b, i, p)),
                  pl.BlockSpec((1, n_cmp, LANES), lambda b, p, i: (b, 0, p)),
                  pl.BlockSpec((1, LANES, n_cmp), lambda b, p, i: (b, p, 0)),
                  k_spec(2), vt_spec(0), k_spec(3), vt_spec(1),
                  pl.BlockSpec((1, 1, LANES, TILE), lambda b, p, i: (b, i, p, 0)),
                  pl.BlockSpec((2, 2, TILE, NSA_GROUP * TILE), lambda b, p, i: (0, p, 0, 0)),
                  pl.BlockSpec((n_slc, n_cmp), lambda b, p, i: (0, 0))],
        out_specs=pl.BlockSpec((1, TILE, qw), lambda b, p, i: (b, i, p)),
        out_shape=jax.ShapeDtypeStruct((B, S, N_HEADS * HEAD_DIM), BF16),
        scratch_shapes=[pltpu.VMEM((2, n_slc, TILE), F32),
                        pltpu.VMEM((2, 2 * (TILE // SLC_BLOCK) + n_slc, NSA_GROUP * TILE), F32),
                        pltpu.VMEM((2, 2, TILE, NSA_GROUP * TILE), F32),
                        pltpu.VMEM((2, 2, TILE, NSA_GROUP * TILE), F32),
                        pltpu.VMEM((2, 3, TILE, NSA_GROUP * TILE), F32),
                        state, state, stat, stat, state],
        compiler_params=_cparams("parallel", "parallel", "arbitrary"),
        name="nsa_attention",
    )(proj, kcmp, vcmp_t, proj, vt, proj, vt, gate_t, _heads_on_lanes(bias, NSA_GROUP), _cmp_to_slc(S))


def _split_bf16(x):
    hi = x.astype(BF16)
    return hi, (x - hi.astype(F32)).astype(BF16)


ROUTER_TM = 1024
_ROW_OF_EXPERT = np.arange(N_EXPERTS).reshape(N_GROUPS, EXPERTS_PER_GROUP).T.reshape(-1)


def _router_body(x_ref, w_ref, b_ref, tri_ref, idx_ref, wt_ref, pos_ref, cnt_ref, base_ref):
    @pl.when(pl.program_id(0) == 0)
    def _():
        base_ref[...] = jnp.zeros(base_ref.shape, F32)

    x_hi, x_lo = _split_bf16(x_ref[...])
    w_hi, w_lo = _split_bf16(w_ref[...])
    logits = (lax.dot_general(w_hi, x_hi, _NT, preferred_element_type=F32)
              + lax.dot_general(w_hi, x_lo, _NT, preferred_element_type=F32)
              + lax.dot_general(w_lo, x_hi, _NT, preferred_element_type=F32)) + b_ref[:, 0:1]
    m = jnp.max(logits, axis=0, keepdims=True)
    e = jnp.exp(logits - m)
    probs = e / jnp.sum(e, axis=0, keepdims=True)
    pk = [probs[k * N_GROUPS:(k + 1) * N_GROUPS] for k in range(EXPERTS_PER_GROUP)]
    hi1, lo1 = jnp.maximum(pk[0], pk[1]), jnp.minimum(pk[0], pk[1])
    hi2, lo2 = jnp.maximum(pk[2], pk[3]), jnp.minimum(pk[2], pk[3])
    score = jnp.maximum(hi1, hi2) + jnp.maximum(jnp.minimum(hi1, hi2), jnp.maximum(lo1, lo2))
    grp = lax.broadcasted_iota(I32, score.shape, 0)
    best = jnp.min(jnp.where(score == jnp.max(score, axis=0, keepdims=True), grp, N_GROUPS),
                   axis=0, keepdims=True)
    v = [jnp.sum(jnp.where(grp == best, p, 0.0), axis=0, keepdims=True) for p in pk]
    v1 = jnp.maximum(jnp.maximum(v[0], v[1]), jnp.maximum(v[2], v[3]))
    i1 = jnp.where(v[0] == v1, 0, jnp.where(v[1] == v1, 1, jnp.where(v[2] == v1, 2, 3)))
    rest = [jnp.where(i1 == k, -1.0, v[k]) for k in range(EXPERTS_PER_GROUP)]
    v2 = jnp.maximum(jnp.maximum(rest[0], rest[1]), jnp.maximum(rest[2], rest[3]))
    i2 = jnp.where(rest[0] == v2, 0, jnp.where(rest[1] == v2, 1, jnp.where(rest[2] == v2, 2, 3)))
    tot = v1 + v2
    idx_ref[...] = jnp.concatenate([best * EXPERTS_PER_GROUP + i1, best * EXPERTS_PER_GROUP + i2], axis=0)
    wt_ref[...] = jnp.concatenate([v1 / tot, v2 / tot], axis=0)

    row = lax.broadcasted_iota(I32, logits.shape, 0)
    hot = [jnp.where(row == ik * N_GROUPS + best, 1.0, 0.0) for ik in (i1, i2)]
    both = (hot[0] + hot[1]).astype(BF16)
    run = base_ref[:, 0:1]
    pos = [[], []]
    for c in range(logits.shape[1] // LANES):
        ls = slice(c * LANES, (c + 1) * LANES)
        before = run + jnp.dot(both[:, ls], tri_ref[...], preferred_element_type=F32) - 1.0
        for k in range(2):
            pos[k].append(jnp.sum(hot[k][:, ls] * before, axis=0, keepdims=True))
        run = before[:, LANES - 1:LANES] + 1.0
    pos_ref[...] = jnp.concatenate([jnp.concatenate(pos[0], axis=1), jnp.concatenate(pos[1], axis=1)],
                                   axis=0).astype(I32)
    base_ref[...] = jnp.broadcast_to(run, base_ref.shape)
    cnt_ref[...] = jnp.broadcast_to(run, cnt_ref.shape)


def _router(x, router_w, router_b):
    N, D = x.shape
    w = router_w.T[_ROW_OF_EXPERT]
    b = jnp.broadcast_to(router_b[_ROW_OF_EXPERT][:, None], (N_EXPERTS, LANES))
    tri = jnp.asarray(np.triu(np.ones((LANES, LANES), np.float32)), BF16)
    tm = ROUTER_TM
    tok = lambda dt: jax.ShapeDtypeStruct((2, N), dt)
    tok_spec = pl.BlockSpec((2, tm), lambda i: (0, i))
    idx, wts, pos, cnt = pl.pallas_call(
        _router_body,
        grid=(N // tm,),
        in_specs=[pl.BlockSpec((tm, D), lambda i: (i, 0)),
                  pl.BlockSpec((N_EXPERTS, D), lambda i: (0, 0)),
                  pl.BlockSpec((N_EXPERTS, LANES), lambda i: (0, 0)),
                  pl.BlockSpec((LANES, LANES), lambda i: (0, 0))],
        out_specs=[tok_spec, tok_spec, tok_spec, pl.BlockSpec((N_EXPERTS, LANES), lambda i: (0, 0))],
        out_shape=[tok(I32), tok(F32), tok(I32), jax.ShapeDtypeStruct((N_EXPERTS, LANES), F32)],
        scratch_shapes=[pltpu.VMEM((N_EXPERTS, LANES), F32)],
        compiler_params=_cparams("arbitrary"),
        name="moe_router",
    )(x, w, b, tri)
    counts = cnt[np.argsort(_ROW_OF_EXPERT), 0].astype(I32)
    return idx, wts, pos, counts


def _expert_body(blk_e_ref, n_used_ref, x_ref, wg_ref, wu_ref, wd_ref, o_ref, wg_b, wu_b, wd_b):
    i = pl.program_id(0)

    @pl.when((i == 0) | (blk_e_ref[i] != blk_e_ref[jnp.maximum(i - 1, 0)]))
    def _():
        wg_b[...] = wg_ref[0, 0].astype(BF16)
        wu_b[...] = wu_ref[0, 0].astype(BF16)
        wd_b[...] = wd_ref[0, 0].astype(BF16)

    @pl.when(i < n_used_ref[0])
    def _():
        x = x_ref[...].astype(BF16)
        gate = jnp.dot(x, wg_b[...], preferred_element_type=F32)
        up = jnp.dot(x, wu_b[...], preferred_element_type=F32)
        hid = (gate * jax.nn.sigmoid(gate) * up).astype(BF16)
        o_ref[...] = jnp.dot(hid, wd_b[...], preferred_element_type=F32)

    @pl.when(i >= n_used_ref[0])
    def _():
        o_ref[...] = jnp.zeros(o_ref.shape, o_ref.dtype)


def _experts(xs, blk_e, n_used, wg, wu, wd, layer):
    R, D = xs.shape
    n_blk = R // MOE_TB

    def live(i, be, nu):
        return jnp.minimum(i, nu[0] - 1)

    grid_spec = pltpu.PrefetchScalarGridSpec(
        num_scalar_prefetch=2,
        grid=(n_blk,),
        in_specs=[pl.BlockSpec((MOE_TB, D), lambda i, be, nu: (live(i, be, nu), 0)),
                  pl.BlockSpec((1, 1, D, D_EXPERT), lambda i, be, nu: (layer, be[i], 0, 0)),
                  pl.BlockSpec((1, 1, D, D_EXPERT), lambda i, be, nu: (layer, be[i], 0, 0)),
                  pl.BlockSpec((1, 1, D_EXPERT, D), lambda i, be, nu: (layer, be[i], 0, 0))],
        out_specs=pl.BlockSpec((MOE_TB, D), lambda i, be, nu: (i, 0)),
        scratch_shapes=[pltpu.VMEM((D, D_EXPERT), BF16), pltpu.VMEM((D, D_EXPERT), BF16),
                        pltpu.VMEM((D_EXPERT, D), BF16)],
    )
    return pl.pallas_call(
        _expert_body,
        grid_spec=grid_spec,
        out_shape=jax.ShapeDtypeStruct((R, D), F32),
        compiler_params=_cparams("arbitrary"),
        name="moe_experts",
    )(blk_e, n_used, xs, wg, wu, wd)


def _combine_ln_body(x_ref, y0_ref, y1_ref, wt_ref, g_ref, b_ref, o_ref, ob_ref):
    ffn = y0_ref[...] * wt_ref[:, 0:1] + y1_ref[...] * wt_ref[:, HEAD_DIM:HEAD_DIM + 1]
    out = _layer_norm_rows(DEEPNORM_ALPHA * x_ref[...] + ffn, g_ref[...], b_ref[...])
    o_ref[...] = out
    ob_ref[...] = out.astype(BF16)


def _combine_ln(x, y0, y1, wt, g, b):
    M, D = x.shape
    row = pl.BlockSpec((LN_TM, D), lambda i: (i, 0))
    vec = pl.BlockSpec((1, D), lambda i: (0, 0))
    return pl.pallas_call(
        _combine_ln_body,
        grid=(M // LN_TM,),
        in_specs=[row, row, row, pl.BlockSpec((LN_TM, LANES), lambda i: (i, 0)), vec, vec],
        out_specs=[row, row],
        out_shape=[jax.ShapeDtypeStruct((M, D), F32), jax.ShapeDtypeStruct((M, D), BF16)],
        compiler_params=_cparams("parallel"),
        name="moe_combine_ln",
    )(x, y0, y1, wt, g.reshape(1, D), b.reshape(1, D))


def _moe_ln(h, router_w, router_b, wg, wu, wd, layer, g, b):
    N, D = h.shape
    A = 2 * N
    idx, wts, pos, counts = _router(h, router_w, router_b)
    starts = jnp.cumsum(counts) - counts
    padded = (counts + MOE_TB - 1) // MOE_TB * MOE_TB
    pends = jnp.cumsum(padded)
    pstarts = pends - padded
    R = A + N_EXPERTS * MOE_TB
    n_blk = R // MOE_TB
    experts = jnp.arange(N_EXPERTS, dtype=I32)
    dest = pos + jnp.sum(jnp.where(idx[None] == experts[:, None, None], pstarts[:, None, None], 0), axis=0)
    tok = jnp.broadcast_to(jnp.arange(N, dtype=I32)[None, :], (2, N))
    _, tok_sorted = lax.sort_key_val(dest.reshape(A), tok.reshape(A))
    blk_row0 = jnp.arange(n_blk, dtype=I32) * MOE_TB
    blk_e = jnp.minimum(jnp.sum((pends[None, :] <= blk_row0[:, None]).astype(I32), axis=1), N_EXPERTS - 1)
    hot = blk_e[:, None] == experts[None, :]
    compact0 = blk_row0 + jnp.sum(jnp.where(hot, (starts - pstarts)[None, :], 0), axis=1)
    compact = jnp.clip(compact0[:, None] + jnp.arange(MOE_TB, dtype=I32)[None, :], 0, A - 1).reshape(R)
    n_used = (pends[-1:] // MOE_TB).astype(I32)
    xs = h[tok_sorted[compact]]
    yb = _experts(xs, blk_e, n_used, wg, wu, wd, layer)
    wt = jnp.concatenate([jnp.broadcast_to(wts[k][:, None], (N, HEAD_DIM)) for k in range(2)], axis=1)
    return _combine_ln(h, yb[dest[0]], yb[dest[1]], wt, g, b)


def _moba_layer(h, w_in, w_out, bias, g, b, B, S):
    HD = N_HEADS * HEAD_DIM
    qk = _matmul(h, w_in[:, :2 * HD].astype(BF16), _query_scale(HD, 2 * HD), BF16).reshape(B, S, 2 * HD)
    vt = _matmul_t(w_in[:, 2 * HD:].T.astype(BF16), h, B, S, BF16)
    att = _moba_attention(qk, vt, bias)
    return _proj_ln(att.reshape(B * S, HD), w_out.astype(BF16), h, g, b)


def _nsa_layer(h, hb, w_in, w_out, pos_k, pos_v, ck_w1, ck_w2, cv_w1, cv_w2, bias, g, b, B, S):
    HD = N_HEADS * HEAD_DIM
    kvw = NSA_KV_HEADS * HEAD_DIM
    col = lambda k: slice(HD + k * kvw, HD + (k + 1) * kvw)
    w_rows = jnp.concatenate([w_in[:, :HD + 2 * kvw], w_in[:, col(2)], w_in[:, col(4)]], axis=1)
    proj = _matmul(hb, w_rows.astype(BF16), _query_scale(HD, HD + 4 * kvw), BF16).reshape(B, S, HD + 4 * kvw)
    w_vt = jnp.concatenate([w_in[:, col(3)], w_in[:, col(5)]], axis=1).T
    vt = _matmul_t(w_vt.astype(BF16), hb, B, S, BF16)
    per_pair = 3 * N_HEADS // 2
    wg = w_in[:, HD + 6 * kvw:].reshape(D_MODEL, 2, per_pair)
    wg = jnp.pad(wg, ((0, 0), (0, 0), (0, LANES - per_pair))).reshape(D_MODEL, 2 * LANES).T
    gate_t = _matmul_t(wg.astype(BF16), hb, B, S, F32)

    def grouped(t):
        t = t.reshape(B, S, NSA_KV_HEADS, HEAD_DIM).transpose(0, 2, 1, 3)
        return t.reshape(B, NSA_KV_HEADS, S // CMP_STRIDE, CMP_STRIDE * HEAD_DIM)

    kcmp = _compress(grouped(proj[..., HD:HD + kvw]), pos_k, ck_w1, ck_w2)
    vcmp = _compress(grouped(proj[..., HD + kvw:HD + 2 * kvw]), pos_v, cv_w1, cv_w2)
    n_cmp = kcmp.shape[2]
    kcmp = kcmp.transpose(0, 2, 1, 3).reshape(B, n_cmp, kvw)
    vcmp_t = vcmp.transpose(0, 1, 3, 2).reshape(B, kvw, n_cmp)
    att = _nsa_attention(proj, vt, gate_t, kcmp, vcmp_t, bias)
    return _proj_ln(att.reshape(B * S, HD), w_out.astype(BF16), h, g, b)


def kernel(x, rel_bias, router_w, router_b, ln_g, ln_b, moba_w_in, moba_w_out, nsa_w_in, nsa_w_out,
           nsa_pos_k, nsa_pos_v, nsa_ck_w1, nsa_ck_w2, nsa_cv_w1, nsa_cv_w2,
           moe_w_gate, moe_w_up, moe_w_down):
    B, S, D = x.shape
    bias = _bias_tiles(rel_bias)
    h = x.reshape(B * S, D)
    h, hb = _moba_layer(h, moba_w_in[0], moba_w_out[0], bias, ln_g[0, 0], ln_b[0, 0], B, S)
    h, hb = _moe_ln(h, router_w, router_b, moe_w_gate, moe_w_up, moe_w_down, 0,
                    ln_g[0, 1], ln_b[0, 1])
    h, hb = _nsa_layer(h, hb, nsa_w_in[0], nsa_w_out[0], nsa_pos_k[0], nsa_pos_v[0],
                       nsa_ck_w1[0], nsa_ck_w2[0], nsa_cv_w1[0], nsa_cv_w2[0],
                       bias, ln_g[1, 0], ln_b[1, 0], B, S)
    h, hb = _moe_ln(h, router_w, router_b, moe_w_gate, moe_w_up, moe_w_down, 1,
                    ln_g[1, 1], ln_b[1, 1])
    return h.reshape(B, S, D)
```

```python
import math

import numpy as np
import jax
import jax.numpy as jnp
from jax import lax
from jax.experimental import pallas as pl
from jax.experimental.pallas import tpu as pltpu

F32, BF16, I32 = jnp.float32, jnp.bfloat16, jnp.int32

D_MODEL = 1024
N_HEADS = 16
HEAD_DIM = 64
DEPTH = 2
NEG_INF = -1e30
LN_EPS = 1e-5
MOBA_BLOCK = 256
MOBA_TOPK = 3
NSA_KV_HEADS = 4
NSA_GROUP = N_HEADS // NSA_KV_HEADS
CMP_LEN = 32
CMP_STRIDE = 16
CMP_HIDDEN = 256
SLC_BLOCK = 64
SLC_TOPN = 16
SLC_LOCAL = 2
WINDOW = 512
REL_BUCKETS = 32
REL_MAX_DIST = 128
N_EXPERTS = 32
N_GROUPS = 8
EXPERTS_PER_GROUP = N_EXPERTS // N_GROUPS
D_EXPERT = 512
DEEPNORM_ALPHA = (2 * DEPTH) ** 0.25
LOG2E = math.log2(math.e)
Q_SCALE = HEAD_DIM ** -0.5 * LOG2E

LANES = 128
SUBLANES = 8
TILE = 256
MM_TM = 1024
MM_TN = 1024
LN_TM = 512
MOE_TB = 256
VMEM_LIMIT = 48 * 1024 * 1024

_NT = (((1,), (1,)), ((), ()))


def _cparams(*sem):
    return pltpu.CompilerParams(dimension_semantics=sem, vmem_limit_bytes=VMEM_LIMIT)


def _mm_body(a_ref, b_ref, c_ref, o_ref):
    acc = jnp.dot(a_ref[...].astype(BF16), b_ref[...], preferred_element_type=F32)
    o_ref[...] = (acc * c_ref[...]).astype(o_ref.dtype)


def _matmul(a, b, col_scale, out_dtype):
    M, K = a.shape
    N = b.shape[1]
    tn = min(MM_TN, N)
    return pl.pallas_call(
        _mm_body,
        grid=(M // MM_TM, N // tn),
        in_specs=[pl.BlockSpec((MM_TM, K), lambda i, j: (i, 0)),
                  pl.BlockSpec((K, tn), lambda i, j: (0, j)),
                  pl.BlockSpec((1, tn), lambda i, j: (0, j))],
        out_specs=pl.BlockSpec((MM_TM, tn), lambda i, j: (i, j)),
        out_shape=jax.ShapeDtypeStruct((M, N), out_dtype),
        compiler_params=_cparams("parallel", "arbitrary"),
        name="in_proj",
    )(a, b, col_scale.reshape(1, N))


def _query_scale(n_query_cols, n_cols):
    return jnp.where(jnp.arange(n_cols) < n_query_cols, Q_SCALE, 1.0).astype(F32)


def _mm_t_body(w_ref, a_ref, o_ref):
    r = lax.dot_general(w_ref[...], a_ref[...].astype(BF16), _NT, preferred_element_type=F32)
    for t in range(o_ref.shape[1]):
        o_ref[0, t] = r[:, t * TILE:(t + 1) * TILE].astype(o_ref.dtype)


def _matmul_t(w_t, a, B, S, out_dtype):
    Nout, K = w_t.shape
    tn = min(MM_TN, Nout)
    per_seq = S // MM_TM
    sub = MM_TM // TILE
    return pl.pallas_call(
        _mm_t_body,
        grid=(B * per_seq, Nout // tn),
        in_specs=[pl.BlockSpec((tn, K), lambda i, j: (j, 0)),
                  pl.BlockSpec((MM_TM, K), lambda i, j: (i, 0))],
        out_specs=pl.BlockSpec((1, sub, tn, TILE), lambda i, j: (i // per_seq, i % per_seq, j, 0)),
        out_shape=jax.ShapeDtypeStruct((B, S // TILE, Nout, TILE), out_dtype),
        compiler_params=_cparams("parallel", "arbitrary"),
        name="in_proj_t",
    )(w_t, a)


def _layer_norm_rows(z, g, b):
    mu = jnp.mean(z, axis=-1, keepdims=True)
    zc = z - mu
    var = jnp.mean(zc * zc, axis=-1, keepdims=True)
    return zc * lax.rsqrt(var + LN_EPS) * g + b


def _proj_ln_body(a_ref, w_ref, x_ref, g_ref, b_ref, o_ref, ob_ref):
    y = jnp.dot(a_ref[...], w_ref[...], preferred_element_type=F32)
    out = _layer_norm_rows(DEEPNORM_ALPHA * x_ref[...] + y, g_ref[...], b_ref[...])
    o_ref[...] = out
    ob_ref[...] = out.astype(BF16)


def _proj_ln(a, w, x, g, b):
    M, K = a.shape
    D = w.shape[1]
    row = pl.BlockSpec((LN_TM, D), lambda i: (i, 0))
    vec = pl.BlockSpec((1, D), lambda i: (0, 0))
    return pl.pallas_call(
        _proj_ln_body,
        grid=(M // LN_TM,),
        in_specs=[pl.BlockSpec((LN_TM, K), lambda i: (i, 0)),
                  pl.BlockSpec((K, D), lambda i: (0, 0)), row, vec, vec],
        out_specs=[row, row],
        out_shape=[jax.ShapeDtypeStruct((M, D), F32), jax.ShapeDtypeStruct((M, D), BF16)],
        compiler_params=_cparams("parallel"),
        name="out_proj_ln",
    )(a, w, x, g.reshape(1, D), b.reshape(1, D))


def _t5_bucket_np(rel):
    n = np.maximum(rel, 0)
    max_exact = REL_BUCKETS // 2
    nf = np.maximum(n, 1).astype(np.float32)
    large = max_exact + (np.log(nf / np.float32(max_exact))
                         / np.float32(math.log(REL_MAX_DIST / max_exact))
                         * np.float32(REL_BUCKETS - max_exact)).astype(np.int32)
    large = np.minimum(large, REL_BUCKETS - 1)
    return np.where(n < max_exact, n, large).astype(np.int32)


def _bias_body(tbl_ref, bk_ref, o_ref):
    h = pl.program_id(0)
    for dl in range(2):
        bk = bk_ref[dl]
        acc = jnp.zeros((TILE, TILE), F32)
        for b in range(REL_BUCKETS):
            acc = jnp.where(bk == b, tbl_ref[h * REL_BUCKETS + b], acc)
        o_ref[dl, 0] = acc * LOG2E


def _bias_tiles(rel_bias):
    key = np.arange(TILE)[:, None]
    qry = np.arange(TILE)[None, :]
    assert int(_t5_bucket_np(np.array(TILE + 1))) == REL_BUCKETS - 1
    bk = np.stack([_t5_bucket_np(qry - key), _t5_bucket_np(TILE + qry - key)])
    return pl.pallas_call(
        _bias_body,
        grid=(N_HEADS,),
        in_specs=[pl.BlockSpec(memory_space=pltpu.SMEM),
                  pl.BlockSpec((2, TILE, TILE), lambda h: (0, 0, 0))],
        out_specs=pl.BlockSpec((2, 1, TILE, TILE), lambda h: (0, h, 0, 0)),
        out_shape=jax.ShapeDtypeStruct((2, N_HEADS, TILE, TILE), F32),
        name="t5_bias_tiles",
    )(rel_bias.T.reshape(-1), jnp.asarray(bk))


def _heads_on_lanes(bias, per_block):
    two, H, T, _ = bias.shape
    b = bias.reshape(two, H // per_block, per_block, T, T).transpose(0, 1, 3, 2, 4)
    return b.reshape(two, H // per_block, T, per_block * T)


def _init_state(m_ref, l_ref, acc_ref):
    m_ref[...] = jnp.full(m_ref.shape, NEG_INF, F32)
    l_ref[...] = jnp.zeros(l_ref.shape, F32)
    acc_ref[...] = jnp.zeros(acc_ref.shape, F32)


def _rank_before(vals, rows):
    idx = lax.broadcasted_iota(I32, vals.shape, 0)
    rank = jnp.zeros(vals.shape, I32)
    for m in range(rows):
        row = vals[m:m + 1, :]
        beats = (row > vals) | ((row == vals) & (idx > m))
        rank = rank + jnp.where(beats, 1, 0)
    return rank


MOBA_STREAMS = 2


def _softmax_pv(scores, adds, vts, m_ref, l_ref, acc_ref):
    def fold(x, op):
        return op(x.reshape(x.shape[0] // SUBLANES, SUBLANES, x.shape[1]), axis=0)

    m_prev = m_ref[...]
    m_part = None
    for s, add in zip(scores, adds):
        part = fold(s, jnp.max) + add
        m_part = part if m_part is None else jnp.maximum(m_part, part)
    m_new = jnp.maximum(m_prev, jnp.max(m_part, axis=0, keepdims=True))
    a = jnp.exp2(m_prev - m_new)
    probs = [jnp.exp2(s - (m_new - add)) for s, add in zip(scores, adds)]
    l_part = fold(probs[0], jnp.sum)
    for p in probs[1:]:
        l_part = l_part + fold(p, jnp.sum)
    l_ref[...] = a * l_ref[...] + jnp.sum(l_part, axis=0, keepdims=True)
    pv = jnp.dot(jnp.concatenate(vts, axis=1), jnp.concatenate([p.astype(BF16) for p in probs], axis=0),
                 preferred_element_type=F32)
    acc_ref[...] = a * acc_ref[...] + pv
    m_ref[...] = m_new


def _moba_body(q_ref, k_ref, vt_ref, bias_ref, o_ref, kmean_ref, radd_ref, sa_ref, sb_ref, m_ref, l_ref, acc_ref):
    i = pl.program_id(2)
    nb = k_ref.shape[1] // TILE
    streams = range(MOBA_STREAMS)
    lanes_of = lambda s: slice(s * LANES, (s + 1) * LANES)

    @pl.when(i == 0)
    def _():
        for s in streams:
            for n in range(nb):
                kb = k_ref[0, n * TILE:(n + 1) * TILE, lanes_of(s)].astype(F32)
                kmean_ref[s, n:n + 1, :] = jnp.sum(kb, axis=0, keepdims=True) * (1.0 / TILE)

    n_far = jnp.maximum(i - 1, 0)
    n_far_groups = (n_far + 1) >> 1
    last = nb - 1

    def key_tile(t, s):
        return k_ref[0, pl.ds(pl.multiple_of(t * TILE, TILE), TILE), lanes_of(s)]

    def far_tiles(j):
        return 2 * j, jnp.minimum(2 * j + 1, last)

    lane = lax.broadcasted_iota(I32, (TILE, LANES), 1)
    key = lax.broadcasted_iota(I32, (TILE, 2 * TILE), 0)
    qry = lax.broadcasted_iota(I32, (TILE, 2 * TILE), 1) & (TILE - 1)
    causal_neg = jnp.where(key <= qry, 0.0, NEG_INF)
    t_near = jnp.maximum(i - 1, 0)
    q2s = []
    for s in streams:
        q = q_ref[0, :, lanes_of(s)]
        zero = jnp.zeros_like(q)
        q2 = jnp.concatenate([jnp.where(lane < HEAD_DIM, q, zero),
                              jnp.where(lane >= HEAD_DIM, q, zero)], axis=0)
        q2s.append(q2)
        sa_ref[s, 0] = (lax.dot_general(key_tile(i, s), q2, _NT, preferred_element_type=F32)
                        + (bias_ref[0, s] + causal_neg))
        sa_ref[s, 1] = lax.dot_general(key_tile(t_near, s), q2, _NT, preferred_element_type=F32) + bias_ref[1, s]
        km = kmean_ref[s]
        k_hi = km.astype(BF16)
        k_lo = (km - k_hi.astype(F32)).astype(BF16)
        gate = (lax.dot_general(k_hi, q2, _NT, preferred_element_type=F32)
                + lax.dot_general(k_lo, q2, _NT, preferred_element_type=F32))
        blk = lax.broadcasted_iota(I32, gate.shape, 0)
        gate = jnp.where(blk < i, gate, -jnp.inf)
        rank = _rank_before(gate, nb)
        neg = jnp.where((rank < MOBA_TOPK) & (blk < i), 0.0, NEG_INF)
        far_bias = bias_ref[1, s, 0:1, :]
        near_row = jnp.full((1, 2 * TILE), NEG_INF, F32)
        radd_ref[s, 0:1, :] = jnp.zeros((1, 2 * TILE), F32)
        for n in range(nb):
            row = neg[n:n + 1, :]
            near_row = jnp.where(n == i - 1, row, near_row)
            radd_ref[s, 2 + n:3 + n, :] = jnp.where(n < n_far, far_bias + row, NEG_INF)
        radd_ref[s, 1:2, :] = near_row
        _init_state(m_ref.at[s], l_ref.at[s], acc_ref.at[s])

    def update(j, s, buf):
        first = j == 0
        t0 = jnp.where(first, i, 2 * j - 2)
        t1 = jnp.where(first, t_near, jnp.minimum(2 * j - 1, last))
        _softmax_pv([buf[s, 0], buf[s, 1]],
                    [radd_ref[s, pl.ds(2 * j, 1), :], radd_ref[s, pl.ds(2 * j + 1, 1), :]],
                    [vt_ref[0, t0, lanes_of(s), :], vt_ref[0, t1, lanes_of(s), :]],
                    m_ref.at[s], l_ref.at[s], acc_ref.at[s])

    def step(j, src, dst):
        for s in streams:
            ta, tb = far_tiles(j)
            dst[s, 0] = lax.dot_general(key_tile(ta, s), q2s[s], _NT, preferred_element_type=F32)
            dst[s, 1] = lax.dot_general(key_tile(tb, s), q2s[s], _NT, preferred_element_type=F32)
            update(j, s, src)

    def two_steps(jj, carry):
        step(2 * jj, sa_ref, sb_ref)
        step(2 * jj + 1, sb_ref, sa_ref)
        return carry

    lax.fori_loop(0, n_far_groups >> 1, two_steps, 0)

    @pl.when((n_far_groups & 1) == 1)
    def _():
        step(n_far_groups - 1, sa_ref, sb_ref)
        for s in streams:
            update(n_far_groups, s, sb_ref)

    @pl.when((n_far_groups & 1) == 0)
    def _():
        for s in streams:
            update(n_far_groups, s, sa_ref)

    for s in streams:
        o = acc_ref[s] / l_ref[s]
        o = jnp.concatenate([o[:HEAD_DIM, :TILE], o[HEAD_DIM:, TILE:]], axis=0)
        o_ref[0, :, lanes_of(s)] = o.T.astype(o_ref.dtype)


def _moba_attention(qk, vt, bias):
    B, S, _ = qk.shape
    n_steps = N_HEADS // 2 // MOBA_STREAMS
    nq = S // TILE
    w = MOBA_STREAMS * LANES
    return pl.pallas_call(
        _moba_body,
        grid=(B, n_steps, nq),
        in_specs=[pl.BlockSpec((1, TILE, w), lambda b, p, i: (b, i, p)),
                  pl.BlockSpec((1, S, w), lambda b, p, i: (b, 0, n_steps + p)),
                  pl.BlockSpec((1, nq, w, TILE), lambda b, p, i: (b, 0, p, 0)),
                  pl.BlockSpec((2, MOBA_STREAMS, TILE, 2 * TILE), lambda b, p, i: (0, p, 0, 0))],
        out_specs=pl.BlockSpec((1, TILE, w), lambda b, p, i: (b, i, p)),
        out_shape=jax.ShapeDtypeStruct((B, S, N_HEADS * HEAD_DIM), BF16),
        scratch_shapes=[pltpu.VMEM((MOBA_STREAMS, nq, LANES), F32),
                        pltpu.VMEM((MOBA_STREAMS, 2 + nq, 2 * TILE), F32),
                        pltpu.VMEM((MOBA_STREAMS, 2, TILE, 2 * TILE), F32),
                        pltpu.VMEM((MOBA_STREAMS, 2, TILE, 2 * TILE), F32),
                        pltpu.VMEM((MOBA_STREAMS, 1, 2 * TILE), F32),
                        pltpu.VMEM((MOBA_STREAMS, 1, 2 * TILE), F32),
                        pltpu.VMEM((MOBA_STREAMS, LANES, 2 * TILE), F32)],
        compiler_params=_cparams("parallel", "parallel", "arbitrary"),
        name="moba_attention",
    )(qk, qk, vt, _heads_on_lanes(bias, 2))


def _gelu_tanh(x):
    return 0.5 * x * (1.0 + jnp.tanh(math.sqrt(2.0 / math.pi) * (x + 0.044715 * (x * x * x))))


def _compress_body(t_ref, pos_ref, w1_ref, w2_ref, o_ref):
    groups = t_ref.shape[2]
    half = t_ref.shape[3]
    t = t_ref[0].reshape(NSA_KV_HEADS * groups, half).astype(F32)
    first = jnp.dot((t + pos_ref[0:1, :]).astype(BF16), w1_ref[0:half, :], preferred_element_type=F32)
    second = jnp.dot((t + pos_ref[1:2, :]).astype(BF16), w1_ref[half:2 * half, :],
                     preferred_element_type=F32)
    rows = first.shape[0]
    pre = first + pltpu.roll(second, rows - 1, 0)
    out = jnp.dot(_gelu_tanh(pre).astype(BF16), w2_ref[...], preferred_element_type=F32)
    for h in range(NSA_KV_HEADS):
        o_ref[0, h] = out[h * groups:(h + 1) * groups].astype(o_ref.dtype)


def _compress(t, pos, w1, w2):
    B, Hkv, groups, half = t.shape
    return pl.pallas_call(
        _compress_body,
        grid=(B,),
        in_specs=[pl.BlockSpec((1, Hkv, groups, half), lambda b: (b, 0, 0, 0)),
                  pl.BlockSpec((2, half), lambda b: (0, 0)),
                  pl.BlockSpec((2 * half, CMP_HIDDEN), lambda b: (0, 0)),
                  pl.BlockSpec((CMP_HIDDEN, HEAD_DIM), lambda b: (0, 0))],
        out_specs=pl.BlockSpec((1, Hkv, groups, HEAD_DIM), lambda b: (b, 0, 0, 0)),
        out_shape=jax.ShapeDtypeStruct((B, Hkv, groups, HEAD_DIM), BF16),
        compiler_params=_cparams("parallel"),
        name="nsa_compress",
    )(t, pos.reshape(2, half), w1.astype(BF16), w2.astype(BF16))


def _swap_halves(x):
    return jnp.concatenate([x[:, HEAD_DIM:], x[:, :HEAD_DIM]], axis=1)


def _group_lanes(x):
    return jnp.concatenate([x] * NSA_GROUP, axis=1)


def _nsa_body(q_ref, kc_ref, vct_ref, ks_ref, vst_ref, kw_ref, vwt_ref, gt_ref, bias_ref, c2s_ref,
              o_ref, selneg_ref, radd_ref, sa_ref, sb_ref, sw_ref, oc_ref, os_ref, m_ref, l_ref, acc_ref):
    i = pl.program_id(2)
    nb = ks_ref.shape[1] // TILE
    n_cmp = kc_ref.shape[1]
    n_slc = c2s_ref.shape[0]
    per_tile = TILE // SLC_BLOCK
    cols = NSA_GROUP * TILE
    kv_heads = range(2)
    n_far = jnp.maximum(i - 1, 0)
    n_far_groups = (n_far + 1) >> 1
    last = nb - 1
    t_near = jnp.maximum(i - 1, 0)
    t_edge = jnp.maximum(i - 2, 0)

    def key_tile(k_ref, t):
        return k_ref[0, pl.ds(pl.multiple_of(t * TILE, TILE), TILE), :]

    def scores_of(k_ref, t, a):
        return lax.dot_general(key_tile(k_ref, t), q4s[a], _NT, preferred_element_type=F32)

    lane = lax.broadcasted_iota(I32, (TILE, LANES), 1)
    lo_half = lane < HEAD_DIM
    qpos = i * TILE + (lax.broadcasted_iota(I32, (n_cmp, cols), 1) & (TILE - 1))
    cmp_valid = CMP_STRIDE * lax.broadcasted_iota(I32, (n_cmp, cols), 0) + (CMP_LEN - 1) <= qpos
    key = lax.broadcasted_iota(I32, (TILE, TILE), 0)
    qry = lax.broadcasted_iota(I32, (TILE, TILE), 1)
    diag_neg = _group_lanes(jnp.where(key <= qry, 0.0, NEG_INF))
    edge_neg = _group_lanes(jnp.where(key > qry, 0.0, NEG_INF))
    qall = q_ref[0]
    q4s = []

    for a in kv_heads:
        keep = lo_half if a == 0 else jnp.logical_not(lo_half)
        heads = []
        for g in range(NSA_GROUP):
            cb = a * 2 + g // 2
            x = qall[:, cb * LANES:(cb + 1) * LANES]
            if g % 2 != a:
                x = _swap_halves(x)
            heads.append(jnp.where(keep, x, jnp.zeros_like(x)))
        q4s.append(jnp.concatenate(heads, axis=0))
        far_bias = bias_ref[1, a, 0:1, :]

        sa_ref[a, 0] = scores_of(ks_ref, i, a) + (bias_ref[0, a] + diag_neg)
        sa_ref[a, 1] = scores_of(ks_ref, t_near, a) + bias_ref[1, a]
        sw_ref[a, 0] = scores_of(kw_ref, i, a) + (bias_ref[0, a] + diag_neg)
        sw_ref[a, 1] = scores_of(kw_ref, t_near, a) + bias_ref[1, a]
        sw_ref[a, 2] = scores_of(kw_ref, t_edge, a) + (far_bias + edge_neg)

        s_c = lax.dot_general(kc_ref[0], q4s[a], _NT, preferred_element_type=F32)
        s_c = jnp.where(cmp_valid, s_c, NEG_INF)
        m_c = jnp.max(s_c, axis=0, keepdims=True)
        e_c = jnp.where(cmp_valid, jnp.exp2(s_c - m_c), 0.0)
        l_c = jnp.sum(e_c, axis=0, keepdims=True)
        p_c = e_c / jnp.where(l_c > 0.0, l_c, 1.0)
        oc_ref[a] = jnp.dot(vct_ref[0], p_c.astype(BF16), preferred_element_type=F32)

        p_sum = p_c[:, 0:TILE]
        for g in range(1, NSA_GROUP):
            p_sum = p_sum + p_c[:, g * TILE:(g + 1) * TILE]
        p_hi = p_sum.astype(BF16)
        p_lo = (p_sum - p_hi.astype(F32)).astype(BF16)
        imp = (jnp.dot(c2s_ref[...], p_hi, preferred_element_type=F32)
               + jnp.dot(c2s_ref[...], p_lo, preferred_element_type=F32))
        j = lax.broadcasted_iota(I32, imp.shape, 0)
        qb = (i * TILE + lax.broadcasted_iota(I32, imp.shape, 1)) >> int(math.log2(SLC_BLOCK))
        forced = (j == 0) | ((j <= qb) & (j > qb - SLC_LOCAL))
        imp = jnp.where(forced, jnp.inf, jnp.where(j > qb, -jnp.inf, imp))
        rank = _rank_before(imp, n_slc)
        selneg = jnp.where((rank < SLC_TOPN) & (j <= qb), 0.0, NEG_INF)
        selneg_ref[a] = selneg
        for c in range(per_tile):
            radd_ref[a, c:c + 1, :] = _group_lanes(selneg_ref[a, pl.ds(per_tile * i + c, 1), :])
            near_row = _group_lanes(selneg_ref[a, pl.ds(per_tile * t_near + c, 1), :])
            radd_ref[a, per_tile + c:per_tile + c + 1, :] = jnp.where(i >= 1, near_row, NEG_INF)
        for blk in range(n_slc):
            row = far_bias + _group_lanes(selneg[blk:blk + 1, :])
            r = 2 * per_tile + blk
            radd_ref[a, r:r + 1, :] = jnp.where(blk // per_tile < n_far, row, NEG_INF)
        _init_state(m_ref.at[a], l_ref.at[a], acc_ref.at[a])

    def update(jg, a, buf):
        first = jg == 0
        t0 = jnp.where(first, i, 2 * jg - 2)
        t1 = jnp.where(first, t_near, jnp.minimum(2 * jg - 1, last))
        scores, adds = [], []
        for t in range(2):
            for c in range(per_tile):
                scores.append(buf[a, t, c * SLC_BLOCK:(c + 1) * SLC_BLOCK, :])
                adds.append(radd_ref[a, pl.ds(2 * per_tile * jg + per_tile * t + c, 1), :])
        _softmax_pv(scores, adds, [vst_ref[0, t0], vst_ref[0, t1]], m_ref.at[a], l_ref.at[a], acc_ref.at[a])

    def step(jg, src, dst):
        for a in kv_heads:
            dst[a, 0] = scores_of(ks_ref, 2 * jg, a)
            dst[a, 1] = scores_of(ks_ref, jnp.minimum(2 * jg + 1, last), a)
            update(jg, a, src)

    def two_steps(jj, carry):
        step(2 * jj, sa_ref, sb_ref)
        step(2 * jj + 1, sb_ref, sa_ref)
        return carry

    lax.fori_loop(0, n_far_groups >> 1, two_steps, 0)

    @pl.when((n_far_groups & 1) == 1)
    def _():
        step(n_far_groups - 1, sa_ref, sb_ref)
        for a in kv_heads:
            update(n_far_groups, a, sb_ref)

    @pl.when((n_far_groups & 1) == 0)
    def _():
        for a in kv_heads:
            update(n_far_groups, a, sa_ref)

    gates = jax.nn.sigmoid(gt_ref[0, 0])
    zero_row = jnp.zeros((1, cols), F32)
    pieces = []
    for a in kv_heads:
        os_ref[a] = acc_ref[a] / l_ref[a]
        _init_state(m_ref.at[a], l_ref.at[a], acc_ref.at[a])
        _softmax_pv([sw_ref[a, 0], sw_ref[a, 1], sw_ref[a, 2]],
                    [zero_row, zero_row + jnp.where(i >= 1, 0.0, NEG_INF), zero_row + jnp.where(i >= 2, 0.0, NEG_INF)],
                    [vwt_ref[0, i], vwt_ref[0, t_near], vwt_ref[0, t_edge]],
                    m_ref.at[a], l_ref.at[a], acc_ref.at[a])
        o_w = acc_ref[a] / l_ref[a]
        rs = slice(a * HEAD_DIM, (a + 1) * HEAD_DIM)
        for g in range(NSA_GROUP):
            c0 = 3 * (NSA_GROUP * a + g)
            ls = slice(g * TILE, (g + 1) * TILE)
            pieces.append(gates[c0:c0 + 1, :] * oc_ref[a, rs, ls] + gates[c0 + 1:c0 + 2, :] * os_ref[a, rs, ls]
                          + gates[c0 + 2:c0 + 3, :] * o_w[rs, ls])
    o_ref[0] = jnp.concatenate(pieces, axis=0).T.astype(o_ref.dtype)


def _cmp_to_slc(S):
    n_cmp_pad = S // CMP_STRIDE
    n_slc = S // SLC_BLOCK
    ci = np.arange(n_cmp_pad)[:, None] * CMP_STRIDE
    sj = np.arange(n_slc)[None, :] * SLC_BLOCK
    c2s = ((ci < sj + SLC_BLOCK) & (ci + CMP_LEN > sj)).astype(np.float32)
    c2s[(S - CMP_LEN) // CMP_STRIDE + 1:] = 0.0
    return jnp.asarray(c2s.T, BF16)


def _nsa_attention(proj, vt, gate_t, kcmp, vcmp_t, bias):
    B, S, _ = proj.shape
    nq = S // TILE
    n_cmp = kcmp.shape[1]
    n_slc = S // SLC_BLOCK
    qw = 2 * NSA_GROUP * HEAD_DIM
    q_blocks = N_HEADS * HEAD_DIM // LANES
    kv_blocks = NSA_KV_HEADS * HEAD_DIM // LANES

    def k_spec(which):
        base = q_blocks + which * kv_blocks
        return pl.BlockSpec((1, S, LANES), lambda b, p, i: (b, 0, base + p))

    def vt_spec(which):
        base = which * kv_blocks
        return pl.BlockSpec((1, nq, LANES, TILE), lambda b, p, i: (b, 0, base + p, 0))

    state = pltpu.VMEM((2, LANES, NSA_GROUP * TILE), F32)
    stat = pltpu.VMEM((2, 1, NSA_GROUP * TILE), F32)
    return pl.pallas_call(
        _nsa_body,
        grid=(B, 2, nq),
        in_specs=[pl.BlockSpec((1, TILE, qw), lambda b, p, i: (b, i, p)),
                  pl.BlockSpec((1, n_cmp, LANES), lambda b, p, i: (b, 0, p)),
                  pl.BlockSpec((1, LANES, n_cmp), lambda b, p, i: (b, p, 0)),
                  k_spec(2), vt_spec(0), k_spec(3), vt_spec(1),
                  pl.BlockSpec((1, 1, LANES, TILE), lambda b, p, i: (b, i, p, 0)),
                  pl.BlockSpec((2, 2, TILE, NSA_GROUP * TILE), lambda b, p, i: (0, p, 0, 0)),
                  pl.BlockSpec((n_slc, n_cmp), lambda b, p, i: (0, 0))],
        out_specs=pl.BlockSpec((1, TILE, qw), lambda b, p, i: (b, i, p)),
        out_shape=jax.ShapeDtypeStruct((B, S, N_HEADS * HEAD_DIM), BF16),
        scratch_shapes=[pltpu.VMEM((2, n_slc, TILE), F32),
                        pltpu.VMEM((2, 2 * (TILE // SLC_BLOCK) + n_slc, NSA_GROUP * TILE), F32),
                        pltpu.VMEM((2, 2, TILE, NSA_GROUP * TILE), F32),
                        pltpu.VMEM((2, 2, TILE, NSA_GROUP * TILE), F32),
                        pltpu.VMEM((2, 3, TILE, NSA_GROUP * TILE), F32),
                        state, state, stat, stat, state],
        compiler_params=_cparams("parallel", "parallel", "arbitrary"),
        name="nsa_attention",
    )(proj, kcmp, vcmp_t, proj, vt, proj, vt, gate_t, _heads_on_lanes(bias, NSA_GROUP), _cmp_to_slc(S))


def _split_bf16(x):
    hi = x.astype(BF16)
    return hi, (x - hi.astype(F32)).astype(BF16)


ROUTER_TM = 1024
_ROW_OF_EXPERT = np.arange(N_EXPERTS).reshape(N_GROUPS, EXPERTS_PER_GROUP).T.reshape(-1)


def _router_body(x_ref, w_ref, b_ref, tri_ref, idx_ref, wt_ref, pos_ref, cnt_ref, base_ref):
    @pl.when(pl.program_id(0) == 0)
    def _():
        base_ref[...] = jnp.zeros(base_ref.shape, F32)

    x_hi, x_lo = _split_bf16(x_ref[...])
    w_hi, w_lo = _split_bf16(w_ref[...])
    logits = (lax.dot_general(w_hi, x_hi, _NT, preferred_element_type=F32)
              + lax.dot_general(w_hi, x_lo, _NT, preferred_element_type=F32)
              + lax.dot_general(w_lo, x_hi, _NT, preferred_element_type=F32)) + b_ref[:, 0:1]
    m = jnp.max(logits, axis=0, keepdims=True)
    e = jnp.exp(logits - m)
    probs = e / jnp.sum(e, axis=0, keepdims=True)
    pk = [probs[k * N_GROUPS:(k + 1) * N_GROUPS] for k in range(EXPERTS_PER_GROUP)]
    hi1, lo1 = jnp.maximum(pk[0], pk[1]), jnp.minimum(pk[0], pk[1])
    hi2, lo2 = jnp.maximum(pk[2], pk[3]), jnp.minimum(pk[2], pk[3])
    score = jnp.maximum(hi1, hi2) + jnp.maximum(jnp.minimum(hi1, hi2), jnp.maximum(lo1, lo2))
    grp = lax.broadcasted_iota(I32, score.shape, 0)
    best = jnp.min(jnp.where(score == jnp.max(score, axis=0, keepdims=True), grp, N_GROUPS),
                   axis=0, keepdims=True)
    v = [jnp.sum(jnp.where(grp == best, p, 0.0), axis=0, keepdims=True) for p in pk]
    v1 = jnp.maximum(jnp.maximum(v[0], v[1]), jnp.maximum(v[2], v[3]))
    i1 = jnp.where(v[0] == v1, 0, jnp.where(v[1] == v1, 1, jnp.where(v[2] == v1, 2, 3)))
    rest = [jnp.where(i1 == k, -1.0, v[k]) for k in range(EXPERTS_PER_GROUP)]
    v2 = jnp.maximum(jnp.maximum(rest[0], rest[1]), jnp.maximum(rest[2], rest[3]))
    i2 = jnp.where(rest[0] == v2, 0, jnp.where(rest[1] == v2, 1, jnp.where(rest[2] == v2, 2, 3)))
    tot = v1 + v2
    idx_ref[...] = jnp.concatenate([best * EXPERTS_PER_GROUP + i1, best * EXPERTS_PER_GROUP + i2], axis=0)
    wt_ref[...] = jnp.concatenate([v1 / tot, v2 / tot], axis=0)

    row = lax.broadcasted_iota(I32, logits.shape, 0)
    hot = [jnp.where(row == ik * N_GROUPS + best, 1.0, 0.0) for ik in (i1, i2)]
    both = (hot[0] + hot[1]).astype(BF16)
    run = base_ref[:, 0:1]
    pos = [[], []]
    for c in range(logits.shape[1] // LANES):
        ls = slice(c * LANES, (c + 1) * LANES)
        before = run + jnp.dot(both[:, ls], tri_ref[...], preferred_element_type=F32) - 1.0
        for k in range(2):
            pos[k].append(jnp.sum(hot[k][:, ls] * before, axis=0, keepdims=True))
        run = before[:, LANES - 1:LANES] + 1.0
    pos_ref[...] = jnp.concatenate([jnp.concatenate(pos[0], axis=1), jnp.concatenate(pos[1], axis=1)],
                                   axis=0).astype(I32)
    base_ref[...] = jnp.broadcast_to(run, base_ref.shape)
    cnt_ref[...] = jnp.broadcast_to(run, cnt_ref.shape)


def _router(x, router_w, router_b):
    N, D = x.shape
    w = router_w.T[_ROW_OF_EXPERT]
    b = jnp.broadcast_to(router_b[_ROW_OF_EXPERT][:, None], (N_EXPERTS, LANES))
    tri = jnp.asarray(np.triu(np.ones((LANES, LANES), np.float32)), BF16)
    tm = ROUTER_TM
    tok = lambda dt: jax.ShapeDtypeStruct((2, N), dt)
    tok_spec = pl.BlockSpec((2, tm), lambda i: (0, i))
    idx, wts, pos, cnt = pl.pallas_call(
        _router_body,
        grid=(N // tm,),
        in_specs=[pl.BlockSpec((tm, D), lambda i: (i, 0)),
                  pl.BlockSpec((N_EXPERTS, D), lambda i: (0, 0)),
                  pl.BlockSpec((N_EXPERTS, LANES), lambda i: (0, 0)),
                  pl.BlockSpec((LANES, LANES), lambda i: (0, 0))],
        out_specs=[tok_spec, tok_spec, tok_spec, pl.BlockSpec((N_EXPERTS, LANES), lambda i: (0, 0))],
        out_shape=[tok(I32), tok(F32), tok(I32), jax.ShapeDtypeStruct((N_EXPERTS, LANES), F32)],
        scratch_shapes=[pltpu.VMEM((N_EXPERTS, LANES), F32)],
        compiler_params=_cparams("arbitrary"),
        name="moe_router",
    )(x, w, b, tri)
    counts = cnt[np.argsort(_ROW_OF_EXPERT), 0].astype(I32)
    return idx, wts, pos, counts


def _expert_body(blk_e_ref, n_used_ref, x_ref, wg_ref, wu_ref, wd_ref, o_ref, wg_b, wu_b, wd_b):
    i = pl.program_id(0)

    @pl.when((i == 0) | (blk_e_ref[i] != blk_e_ref[jnp.maximum(i - 1, 0)]))
    def _():
        wg_b[...] = wg_ref[0, 0].astype(BF16)
        wu_b[...] = wu_ref[0, 0].astype(BF16)
        wd_b[...] = wd_ref[0, 0].astype(BF16)

    @pl.when(i < n_used_ref[0])
    def _():
        x = x_ref[...]
        gate = jnp.dot(x, wg_b[...], preferred_element_type=F32)
        up = jnp.dot(x, wu_b[...], preferred_element_type=F32)
        hid = (gate * jax.nn.sigmoid(gate) * up).astype(BF16)
        o_ref[...] = jnp.dot(hid, wd_b[...], preferred_element_type=F32).astype(o_ref.dtype)

    @pl.when(i >= n_used_ref[0])
    def _():
        o_ref[...] = jnp.zeros(o_ref.shape, o_ref.dtype)


def _experts(xs, blk_e, n_used, wg, wu, wd, layer):
    R, D = xs.shape
    n_blk = R // MOE_TB

    def live(i, be, nu):
        return jnp.minimum(i, nu[0] - 1)

    grid_spec = pltpu.PrefetchScalarGridSpec(
        num_scalar_prefetch=2,
        grid=(n_blk,),
        in_specs=[pl.BlockSpec((MOE_TB, D), lambda i, be, nu: (live(i, be, nu), 0)),
                  pl.BlockSpec((1, 1, D, D_EXPERT), lambda i, be, nu: (layer, be[i], 0, 0)),
                  pl.BlockSpec((1, 1, D, D_EXPERT), lambda i, be, nu: (layer, be[i], 0, 0)),
                  pl.BlockSpec((1, 1, D_EXPERT, D), lambda i, be, nu: (layer, be[i], 0, 0))],
        out_specs=pl.BlockSpec((MOE_TB, D), lambda i, be, nu: (i, 0)),
        scratch_shapes=[pltpu.VMEM((D, D_EXPERT), BF16), pltpu.VMEM((D, D_EXPERT), BF16),
                        pltpu.VMEM((D_EXPERT, D), BF16)],
    )
    return pl.pallas_call(
        _expert_body,
        grid_spec=grid_spec,
        out_shape=jax.ShapeDtypeStruct((R, D), BF16),
        compiler_params=_cparams("arbitrary"),
        name="moe_experts",
    )(blk_e, n_used, xs, wg, wu, wd)


def _combine_ln_body(x_ref, y0_ref, y1_ref, wt_ref, g_ref, b_ref, o_ref, ob_ref):
    ffn = y0_ref[...] * wt_ref[:, 0:1] + y1_ref[...] * wt_ref[:, HEAD_DIM:HEAD_DIM + 1]
    out = _layer_norm_rows(DEEPNORM_ALPHA * x_ref[...] + ffn, g_ref[...], b_ref[...])
    o_ref[...] = out
    ob_ref[...] = out.astype(BF16)


def _combine_ln(x, y0, y1, wt, g, b):
    M, D = x.shape
    row = pl.BlockSpec((LN_TM, D), lambda i: (i, 0))
    vec = pl.BlockSpec((1, D), lambda i: (0, 0))
    return pl.pallas_call(
        _combine_ln_body,
        grid=(M // LN_TM,),
        in_specs=[row, row, row, pl.BlockSpec((LN_TM, LANES), lambda i: (i, 0)), vec, vec],
        out_specs=[row, row],
        out_shape=[jax.ShapeDtypeStruct((M, D), F32), jax.ShapeDtypeStruct((M, D), BF16)],
        compiler_params=_cparams("parallel"),
        name="moe_combine_ln",
    )(x, y0, y1, wt, g.reshape(1, D), b.reshape(1, D))


def _moe_ln(h, hb, router_w, router_b, wg, wu, wd, layer, g, b):
    N, D = h.shape
    A = 2 * N
    idx, wts, pos, counts = _router(h, router_w, router_b)
    starts = jnp.cumsum(counts) - counts
    padded = (counts + MOE_TB - 1) // MOE_TB * MOE_TB
    pends = jnp.cumsum(padded)
    pstarts = pends - padded
    R = A + N_EXPERTS * MOE_TB
    n_blk = R // MOE_TB
    experts = jnp.arange(N_EXPERTS, dtype=I32)
    dest = pos + jnp.sum(jnp.where(idx[None] == experts[:, None, None], pstarts[:, None, None], 0), axis=0)
    tok = jnp.broadcast_to(jnp.arange(N, dtype=I32)[None, :], (2, N))
    _, tok_sorted = lax.sort_key_val(dest.reshape(A), tok.reshape(A))
    blk_row0 = jnp.arange(n_blk, dtype=I32) * MOE_TB
    blk_e = jnp.minimum(jnp.sum((pends[None, :] <= blk_row0[:, None]).astype(I32), axis=1), N_EXPERTS - 1)
    hot = blk_e[:, None] == experts[None, :]
    compact0 = blk_row0 + jnp.sum(jnp.where(hot, (starts - pstarts)[None, :], 0), axis=1)
    compact = jnp.remainder(compact0[:, None] + jnp.arange(MOE_TB, dtype=I32)[None, :], A).reshape(R)
    n_used = (pends[-1:] // MOE_TB).astype(I32)
    xs = hb[tok_sorted[compact]]
    yb = _experts(xs, blk_e, n_used, wg, wu, wd, layer)
    wt = jnp.concatenate([jnp.broadcast_to(wts[k][:, None], (N, HEAD_DIM)) for k in range(2)], axis=1)
    return _combine_ln(h, yb[dest[0]], yb[dest[1]], wt, g, b)


def _moba_layer(h, w_in, w_out, bias, g, b, B, S):
    HD = N_HEADS * HEAD_DIM
    qk = _matmul(h, w_in[:, :2 * HD].astype(BF16), _query_scale(HD, 2 * HD), BF16).reshape(B, S, 2 * HD)
    vt = _matmul_t(w_in[:, 2 * HD:].T.astype(BF16), h, B, S, BF16)
    att = _moba_attention(qk, vt, bias)
    return _proj_ln(att.reshape(B * S, HD), w_out.astype(BF16), h, g, b)


def _nsa_layer(h, hb, w_in, w_out, pos_k, pos_v, ck_w1, ck_w2, cv_w1, cv_w2, bias, g, b, B, S):
    HD = N_HEADS * HEAD_DIM
    kvw = NSA_KV_HEADS * HEAD_DIM
    col = lambda k: slice(HD + k * kvw, HD + (k + 1) * kvw)
    w_rows = jnp.concatenate([w_in[:, :HD + 2 * kvw], w_in[:, col(2)], w_in[:, col(4)]], axis=1)
    proj = _matmul(hb, w_rows.astype(BF16), _query_scale(HD, HD + 4 * kvw), BF16).reshape(B, S, HD + 4 * kvw)
    w_vt = jnp.concatenate([w_in[:, col(3)], w_in[:, col(5)]], axis=1).T
    vt = _matmul_t(w_vt.astype(BF16), hb, B, S, BF16)
    per_pair = 3 * N_HEADS // 2
    wg = w_in[:, HD + 6 * kvw:].reshape(D_MODEL, 2, per_pair)
    wg = jnp.pad(wg, ((0, 0), (0, 0), (0, LANES - per_pair))).reshape(D_MODEL, 2 * LANES).T
    gate_t = _matmul_t(wg.astype(BF16), hb, B, S, F32)

    def grouped(t):
        t = t.reshape(B, S, NSA_KV_HEADS, HEAD_DIM).transpose(0, 2, 1, 3)
        return t.reshape(B, NSA_KV_HEADS, S // CMP_STRIDE, CMP_STRIDE * HEAD_DIM)

    kcmp = _compress(grouped(proj[..., HD:HD + kvw]), pos_k, ck_w1, ck_w2)
    vcmp = _compress(grouped(proj[..., HD + kvw:HD + 2 * kvw]), pos_v, cv_w1, cv_w2)
    n_cmp = kcmp.shape[2]
    kcmp = kcmp.transpose(0, 2, 1, 3).reshape(B, n_cmp, kvw)
    vcmp_t = vcmp.transpose(0, 1, 3, 2).reshape(B, kvw, n_cmp)
    att = _nsa_attention(proj, vt, gate_t, kcmp, vcmp_t, bias)
    return _proj_ln(att.reshape(B * S, HD), w_out.astype(BF16), h, g, b)


def kernel(x, rel_bias, router_w, router_b, ln_g, ln_b, moba_w_in, moba_w_out, nsa_w_in, nsa_w_out,
           nsa_pos_k, nsa_pos_v, nsa_ck_w1, nsa_ck_w2, nsa_cv_w1, nsa_cv_w2,
           moe_w_gate, moe_w_up, moe_w_down):
    B, S, D = x.shape
    bias = _bias_tiles(rel_bias)
    h = x.reshape(B * S, D)
    h, hb = _moba_layer(h, moba_w_in[0], moba_w_out[0], bias, ln_g[0, 0], ln_b[0, 0], B, S)
    h, hb = _moe_ln(h, hb, router_w, router_b, moe_w_gate, moe_w_up, moe_w_down, 0,
                    ln_g[0, 1], ln_b[0, 1])
    h, hb = _nsa_layer(h, hb, nsa_w_in[0], nsa_w_out[0], nsa_pos_k[0], nsa_pos_v[0],
                       nsa_ck_w1[0], nsa_ck_w2[0], nsa_cv_w1[0], nsa_cv_w2[0],
                       bias, ln_g[1, 0], ln_b[1, 0], B, S)
    h, hb = _moe_ln(h, hb, router_w, router_b, moe_w_gate, moe_w_up, moe_w_down, 1,
                    ln_g[1, 1], ln_b[1, 1])
    return h.reshape(B, S, D)
```

```python
import math

import numpy as np
import jax
import jax.numpy as jnp
from jax import lax
from jax.experimental import pallas as pl
from jax.experimental.pallas import tpu as pltpu

F32, BF16, I32 = jnp.float32, jnp.bfloat16, jnp.int32

D_MODEL = 1024
N_HEADS = 16
HEAD_DIM = 64
DEPTH = 2
NEG_INF = -1e30
LN_EPS = 1e-5
MOBA_BLOCK = 256
MOBA_TOPK = 3
NSA_KV_HEADS = 4
NSA_GROUP = N_HEADS // NSA_KV_HEADS
CMP_LEN = 32
CMP_STRIDE = 16
CMP_HIDDEN = 256
SLC_BLOCK = 64
SLC_TOPN = 16
SLC_LOCAL = 2
WINDOW = 512
REL_BUCKETS = 32
REL_MAX_DIST = 128
N_EXPERTS = 32
N_GROUPS = 8
EXPERTS_PER_GROUP = N_EXPERTS // N_GROUPS
D_EXPERT = 512
DEEPNORM_ALPHA = (2 * DEPTH) ** 0.25
LOG2E = math.log2(math.e)
Q_SCALE = HEAD_DIM ** -0.5 * LOG2E

LANES = 128
SUBLANES = 8
TILE = 256
MM_TM = 1024
MM_TN = 1024
LN_TM = 512
MOE_TB = 512
VMEM_LIMIT = 48 * 1024 * 1024

_NT = (((1,), (1,)), ((), ()))


def _cparams(*sem):
    return pltpu.CompilerParams(dimension_semantics=sem, vmem_limit_bytes=VMEM_LIMIT)


def _mm_body(a_ref, b_ref, c_ref, o_ref):
    acc = jnp.dot(a_ref[...].astype(BF16), b_ref[...], preferred_element_type=F32)
    o_ref[...] = (acc * c_ref[...]).astype(o_ref.dtype)


def _matmul(a, b, col_scale, out_dtype):
    M, K = a.shape
    N = b.shape[1]
    tn = min(MM_TN, N)
    return pl.pallas_call(
        _mm_body,
        grid=(M // MM_TM, N // tn),
        in_specs=[pl.BlockSpec((MM_TM, K), lambda i, j: (i, 0)),
                  pl.BlockSpec((K, tn), lambda i, j: (0, j)),
                  pl.BlockSpec((1, tn), lambda i, j: (0, j))],
        out_specs=pl.BlockSpec((MM_TM, tn), lambda i, j: (i, j)),
        out_shape=jax.ShapeDtypeStruct((M, N), out_dtype),
        compiler_params=_cparams("parallel", "arbitrary"),
        name="in_proj",
    )(a, b, col_scale.reshape(1, N))


def _query_scale(n_query_cols, n_cols):
    return jnp.where(jnp.arange(n_cols) < n_query_cols, Q_SCALE, 1.0).astype(F32)


def _mm_t_body(w_ref, a_ref, o_ref):
    r = lax.dot_general(w_ref[...], a_ref[...].astype(BF16), _NT, preferred_element_type=F32)
    for t in range(o_ref.shape[1]):
        o_ref[0, t] = r[:, t * TILE:(t + 1) * TILE].astype(o_ref.dtype)


def _matmul_t(w_t, a, B, S, out_dtype):
    Nout, K = w_t.shape
    tn = min(MM_TN, Nout)
    per_seq = S // MM_TM
    sub = MM_TM // TILE
    return pl.pallas_call(
        _mm_t_body,
        grid=(B * per_seq, Nout // tn),
        in_specs=[pl.BlockSpec((tn, K), lambda i, j: (j, 0)),
                  pl.BlockSpec((MM_TM, K), lambda i, j: (i, 0))],
        out_specs=pl.BlockSpec((1, sub, tn, TILE), lambda i, j: (i // per_seq, i % per_seq, j, 0)),
        out_shape=jax.ShapeDtypeStruct((B, S // TILE, Nout, TILE), out_dtype),
        compiler_params=_cparams("parallel", "arbitrary"),
        name="in_proj_t",
    )(w_t, a)


def _layer_norm_rows(z, g, b):
    mu = jnp.mean(z, axis=-1, keepdims=True)
    zc = z - mu
    var = jnp.mean(zc * zc, axis=-1, keepdims=True)
    return zc * lax.rsqrt(var + LN_EPS) * g + b


def _proj_ln_body(a_ref, w_ref, x_ref, g_ref, b_ref, o_ref, ob_ref):
    y = jnp.dot(a_ref[...], w_ref[...], preferred_element_type=F32)
    out = _layer_norm_rows(DEEPNORM_ALPHA * x_ref[...] + y, g_ref[...], b_ref[...])
    o_ref[...] = out
    ob_ref[...] = out.astype(BF16)


def _proj_ln(a, w, x, g, b):
    M, K = a.shape
    D = w.shape[1]
    row = pl.BlockSpec((LN_TM, D), lambda i: (i, 0))
    vec = pl.BlockSpec((1, D), lambda i: (0, 0))
    return pl.pallas_call(
        _proj_ln_body,
        grid=(M // LN_TM,),
        in_specs=[pl.BlockSpec((LN_TM, K), lambda i: (i, 0)),
                  pl.BlockSpec((K, D), lambda i: (0, 0)), row, vec, vec],
        out_specs=[row, row],
        out_shape=[jax.ShapeDtypeStruct((M, D), F32), jax.ShapeDtypeStruct((M, D), BF16)],
        compiler_params=_cparams("parallel"),
        name="out_proj_ln",
    )(a, w, x, g.reshape(1, D), b.reshape(1, D))


def _t5_bucket_np(rel):
    n = np.maximum(rel, 0)
    max_exact = REL_BUCKETS // 2
    nf = np.maximum(n, 1).astype(np.float32)
    large = max_exact + (np.log(nf / np.float32(max_exact))
                         / np.float32(math.log(REL_MAX_DIST / max_exact))
                         * np.float32(REL_BUCKETS - max_exact)).astype(np.int32)
    large = np.minimum(large, REL_BUCKETS - 1)
    return np.where(n < max_exact, n, large).astype(np.int32)


def _bias_body(tbl_ref, bk_ref, o_ref):
    h = pl.program_id(0)
    for dl in range(2):
        bk = bk_ref[dl]
        acc = jnp.zeros((TILE, TILE), F32)
        for b in range(REL_BUCKETS):
            acc = jnp.where(bk == b, tbl_ref[h * REL_BUCKETS + b], acc)
        o_ref[dl, 0] = acc * LOG2E


def _bias_tiles(rel_bias):
    key = np.arange(TILE)[:, None]
    qry = np.arange(TILE)[None, :]
    assert int(_t5_bucket_np(np.array(TILE + 1))) == REL_BUCKETS - 1
    bk = np.stack([_t5_bucket_np(qry - key), _t5_bucket_np(TILE + qry - key)])
    return pl.pallas_call(
        _bias_body,
        grid=(N_HEADS,),
        in_specs=[pl.BlockSpec(memory_space=pltpu.SMEM),
                  pl.BlockSpec((2, TILE, TILE), lambda h: (0, 0, 0))],
        out_specs=pl.BlockSpec((2, 1, TILE, TILE), lambda h: (0, h, 0, 0)),
        out_shape=jax.ShapeDtypeStruct((2, N_HEADS, TILE, TILE), F32),
        name="t5_bias_tiles",
    )(rel_bias.T.reshape(-1), jnp.asarray(bk))


def _heads_on_lanes(bias, per_block):
    two, H, T, _ = bias.shape
    b = bias.reshape(two, H // per_block, per_block, T, T).transpose(0, 1, 3, 2, 4)
    return b.reshape(two, H // per_block, T, per_block * T)


def _init_state(m_ref, l_ref, acc_ref):
    m_ref[...] = jnp.full(m_ref.shape, NEG_INF, F32)
    l_ref[...] = jnp.zeros(l_ref.shape, F32)
    acc_ref[...] = jnp.zeros(acc_ref.shape, F32)


def _rank_before(vals, rows):
    idx = lax.broadcasted_iota(I32, vals.shape, 0)
    rank = jnp.zeros(vals.shape, I32)
    for m in range(rows):
        row = vals[m:m + 1, :]
        beats = (row > vals) | ((row == vals) & (idx > m))
        rank = rank + jnp.where(beats, 1, 0)
    return rank


MOBA_STREAMS = 4


def _softmax_pv(scores, adds, vts, heads, m_ref, l_ref, acc_ref):
    def fold(x, op):
        return op(x.reshape(x.shape[0] // SUBLANES, SUBLANES, x.shape[1]), axis=0)

    m_prev = m_ref[...]
    m_part = None
    for s, add in zip(scores, adds):
        part = fold(s, jnp.max) + add
        m_part = part if m_part is None else jnp.maximum(m_part, part)
    m_new = jnp.maximum(m_prev, jnp.max(m_part, axis=0, keepdims=True))
    a = jnp.exp2(m_prev - m_new)
    probs = [jnp.exp2(s - (m_new - add)) for s, add in zip(scores, adds)]
    l_part = fold(probs[0], jnp.sum)
    for p in probs[1:]:
        l_part = l_part + fold(p, jnp.sum)
    l_ref[...] = a * l_ref[...] + jnp.sum(l_part, axis=0, keepdims=True)
    vt = jnp.concatenate(vts, axis=1)
    pb = jnp.concatenate([p.astype(BF16) for p in probs], axis=0)
    pv = jnp.concatenate([jnp.dot(vt[rows], pb[:, cols], preferred_element_type=F32) for rows, cols in heads],
                         axis=1)
    acc_ref[...] = a * acc_ref[...] + pv
    m_ref[...] = m_new


def _moba_body(q_ref, k_ref, vt_ref, bias_ref, o_ref, kmean_ref, radd_ref, sa_ref, sb_ref, m_ref, l_ref, acc_ref):
    i = pl.program_id(2)
    nb = k_ref.shape[1] // TILE
    streams = range(MOBA_STREAMS)
    lanes_of = lambda s: slice(s * LANES, (s + 1) * LANES)

    @pl.when(i == 0)
    def _():
        for s in streams:
            for n in range(nb):
                kb = k_ref[0, n * TILE:(n + 1) * TILE, lanes_of(s)].astype(F32)
                kmean_ref[s, n:n + 1, :] = jnp.sum(kb, axis=0, keepdims=True) * (1.0 / TILE)

    n_far = jnp.maximum(i - 1, 0)
    n_far_groups = (n_far + 1) >> 1
    last = nb - 1

    def key_tile(t, s):
        return k_ref[0, pl.ds(pl.multiple_of(t * TILE, TILE), TILE), lanes_of(s)]

    def far_tiles(j):
        return 2 * j, jnp.minimum(2 * j + 1, last)

    lane = lax.broadcasted_iota(I32, (TILE, LANES), 1)
    key = lax.broadcasted_iota(I32, (TILE, 2 * TILE), 0)
    qry = lax.broadcasted_iota(I32, (TILE, 2 * TILE), 1) & (TILE - 1)
    causal_neg = jnp.where(key <= qry, 0.0, NEG_INF)
    t_near = jnp.maximum(i - 1, 0)
    q2s = []
    for s in streams:
        q = q_ref[0, :, lanes_of(s)]
        zero = jnp.zeros_like(q)
        q2 = jnp.concatenate([jnp.where(lane < HEAD_DIM, q, zero),
                              jnp.where(lane >= HEAD_DIM, q, zero)], axis=0)
        q2s.append(q2)
        sa_ref[s, 0] = (lax.dot_general(key_tile(i, s), q2, _NT, preferred_element_type=F32)
                        + (bias_ref[0, s] + causal_neg))
        sa_ref[s, 1] = lax.dot_general(key_tile(t_near, s), q2, _NT, preferred_element_type=F32) + bias_ref[1, s]
        km = kmean_ref[s]
        k_hi = km.astype(BF16)
        k_lo = (km - k_hi.astype(F32)).astype(BF16)
        gate = (lax.dot_general(k_hi, q2, _NT, preferred_element_type=F32)
                + lax.dot_general(k_lo, q2, _NT, preferred_element_type=F32))
        blk = lax.broadcasted_iota(I32, gate.shape, 0)
        gate = jnp.where(blk < i, gate, -jnp.inf)
        rank = _rank_before(gate, nb)
        neg = jnp.where((rank < MOBA_TOPK) & (blk < i), 0.0, NEG_INF)
        far_bias = bias_ref[1, s, 0:1, :]
        near_row = jnp.full((1, 2 * TILE), NEG_INF, F32)
        radd_ref[s, 0:1, :] = jnp.zeros((1, 2 * TILE), F32)
        for n in range(nb):
            row = neg[n:n + 1, :]
            near_row = jnp.where(n == i - 1, row, near_row)
            radd_ref[s, 2 + n:3 + n, :] = jnp.where(n < n_far, far_bias + row, NEG_INF)
        radd_ref[s, 1:2, :] = near_row
        _init_state(m_ref.at[s], l_ref.at[s], acc_ref.at[s])

    pair_heads = [(slice(h * HEAD_DIM, (h + 1) * HEAD_DIM), slice(h * TILE, (h + 1) * TILE)) for h in range(2)]

    def update(j, s, buf):
        first = j == 0
        t0 = jnp.where(first, i, 2 * j - 2)
        t1 = jnp.where(first, t_near, jnp.minimum(2 * j - 1, last))
        _softmax_pv([buf[s, 0], buf[s, 1]],
                    [radd_ref[s, pl.ds(2 * j, 1), :], radd_ref[s, pl.ds(2 * j + 1, 1), :]],
                    [vt_ref[0, t0, lanes_of(s), :], vt_ref[0, t1, lanes_of(s), :]], pair_heads,
                    m_ref.at[s], l_ref.at[s], acc_ref.at[s])

    def step(j, src, dst):
        for s in streams:
            ta, tb = far_tiles(j)
            dst[s, 0] = lax.dot_general(key_tile(ta, s), q2s[s], _NT, preferred_element_type=F32)
            dst[s, 1] = lax.dot_general(key_tile(tb, s), q2s[s], _NT, preferred_element_type=F32)
            update(j, s, src)

    def two_steps(jj, carry):
        step(2 * jj, sa_ref, sb_ref)
        step(2 * jj + 1, sb_ref, sa_ref)
        return carry

    lax.fori_loop(0, n_far_groups >> 1, two_steps, 0)

    @pl.when((n_far_groups & 1) == 1)
    def _():
        step(n_far_groups - 1, sa_ref, sb_ref)
        for s in streams:
            update(n_far_groups, s, sb_ref)

    @pl.when((n_far_groups & 1) == 0)
    def _():
        for s in streams:
            update(n_far_groups, s, sa_ref)

    for s in streams:
        o = acc_ref[s] / l_ref[s]
        o = jnp.concatenate([o[:, :TILE], o[:, TILE:]], axis=0)
        o_ref[0, :, lanes_of(s)] = o.T.astype(o_ref.dtype)


def _moba_attention(qk, vt, bias):
    B, S, _ = qk.shape
    n_steps = N_HEADS // 2 // MOBA_STREAMS
    nq = S // TILE
    w = MOBA_STREAMS * LANES
    return pl.pallas_call(
        _moba_body,
        grid=(B, n_steps, nq),
        in_specs=[pl.BlockSpec((1, TILE, w), lambda b, p, i: (b, i, p)),
                  pl.BlockSpec((1, S, w), lambda b, p, i: (b, 0, n_steps + p)),
                  pl.BlockSpec((1, nq, w, TILE), lambda b, p, i: (b, 0, p, 0)),
                  pl.BlockSpec((2, MOBA_STREAMS, TILE, 2 * TILE), lambda b, p, i: (0, p, 0, 0))],
        out_specs=pl.BlockSpec((1, TILE, w), lambda b, p, i: (b, i, p)),
        out_shape=jax.ShapeDtypeStruct((B, S, N_HEADS * HEAD_DIM), BF16),
        scratch_shapes=[pltpu.VMEM((MOBA_STREAMS, nq, LANES), F32),
                        pltpu.VMEM((MOBA_STREAMS, 2 + nq, 2 * TILE), F32),
                        pltpu.VMEM((MOBA_STREAMS, 2, TILE, 2 * TILE), F32),
                        pltpu.VMEM((MOBA_STREAMS, 2, TILE, 2 * TILE), F32),
                        pltpu.VMEM((MOBA_STREAMS, 1, 2 * TILE), F32),
                        pltpu.VMEM((MOBA_STREAMS, 1, 2 * TILE), F32),
                        pltpu.VMEM((MOBA_STREAMS, HEAD_DIM, 2 * TILE), F32)],
        compiler_params=_cparams("parallel", "parallel", "arbitrary"),
        name="moba_attention",
    )(qk, qk, vt, _heads_on_lanes(bias, 2))


def _gelu_tanh(x):
    return 0.5 * x * (1.0 + jnp.tanh(math.sqrt(2.0 / math.pi) * (x + 0.044715 * (x * x * x))))


def _compress_body(t_ref, pos_ref, w1_ref, w2_ref, o_ref):
    groups = t_ref.shape[2]
    half = t_ref.shape[3]
    t = t_ref[0].reshape(NSA_KV_HEADS * groups, half).astype(F32)
    first = jnp.dot((t + pos_ref[0:1, :]).astype(BF16), w1_ref[0:half, :], preferred_element_type=F32)
    second = jnp.dot((t + pos_ref[1:2, :]).astype(BF16), w1_ref[half:2 * half, :],
                     preferred_element_type=F32)
    rows = first.shape[0]
    pre = first + pltpu.roll(second, rows - 1, 0)
    out = jnp.dot(_gelu_tanh(pre).astype(BF16), w2_ref[...], preferred_element_type=F32)
    for h in range(NSA_KV_HEADS):
        o_ref[0, h] = out[h * groups:(h + 1) * groups].astype(o_ref.dtype)


def _compress(t, pos, w1, w2):
    B, Hkv, groups, half = t.shape
    return pl.pallas_call(
        _compress_body,
        grid=(B,),
        in_specs=[pl.BlockSpec((1, Hkv, groups, half), lambda b: (b, 0, 0, 0)),
                  pl.BlockSpec((2, half), lambda b: (0, 0)),
                  pl.BlockSpec((2 * half, CMP_HIDDEN), lambda b: (0, 0)),
                  pl.BlockSpec((CMP_HIDDEN, HEAD_DIM), lambda b: (0, 0))],
        out_specs=pl.BlockSpec((1, Hkv, groups, HEAD_DIM), lambda b: (b, 0, 0, 0)),
        out_shape=jax.ShapeDtypeStruct((B, Hkv, groups, HEAD_DIM), BF16),
        compiler_params=_cparams("parallel"),
        name="nsa_compress",
    )(t, pos.reshape(2, half), w1.astype(BF16), w2.astype(BF16))


def _swap_halves(x):
    return jnp.concatenate([x[:, HEAD_DIM:], x[:, :HEAD_DIM]], axis=1)


def _group_lanes(x):
    return jnp.concatenate([x] * NSA_GROUP, axis=1)


def _nsa_body(q_ref, kc_ref, vct_ref, ks_ref, vst_ref, kw_ref, vwt_ref, gt_ref, bias_ref, c2s_ref,
              o_ref, selneg_ref, radd_ref, sa_ref, sb_ref, sw_ref, oc_ref, os_ref, m_ref, l_ref, acc_ref):
    i = pl.program_id(2)
    nb = ks_ref.shape[1] // TILE
    n_cmp = kc_ref.shape[1]
    n_slc = c2s_ref.shape[0]
    per_tile = TILE // SLC_BLOCK
    cols = NSA_GROUP * TILE
    kv_heads = range(2)
    n_far = jnp.maximum(i - 1, 0)
    n_far_groups = (n_far + 1) >> 1
    last = nb - 1
    t_near = jnp.maximum(i - 1, 0)
    t_edge = jnp.maximum(i - 2, 0)

    dims_of = lambda a: slice(a * HEAD_DIM, (a + 1) * HEAD_DIM)

    def key_tile(k_ref, t):
        return k_ref[0, pl.ds(pl.multiple_of(t * TILE, TILE), TILE), :]

    def scores_of(k_ref, t, a):
        return lax.dot_general(key_tile(k_ref, t), q4s[a], _NT, preferred_element_type=F32)

    lane = lax.broadcasted_iota(I32, (TILE, LANES), 1)
    lo_half = lane < HEAD_DIM
    qpos = i * TILE + (lax.broadcasted_iota(I32, (n_cmp, cols), 1) & (TILE - 1))
    cmp_valid = CMP_STRIDE * lax.broadcasted_iota(I32, (n_cmp, cols), 0) + (CMP_LEN - 1) <= qpos
    key = lax.broadcasted_iota(I32, (TILE, TILE), 0)
    qry = lax.broadcasted_iota(I32, (TILE, TILE), 1)
    diag_neg = _group_lanes(jnp.where(key <= qry, 0.0, NEG_INF))
    edge_neg = _group_lanes(jnp.where(key > qry, 0.0, NEG_INF))
    qall = q_ref[0]
    q4s = []

    for a in kv_heads:
        keep = lo_half if a == 0 else jnp.logical_not(lo_half)
        heads = []
        for g in range(NSA_GROUP):
            cb = a * 2 + g // 2
            x = qall[:, cb * LANES:(cb + 1) * LANES]
            if g % 2 != a:
                x = _swap_halves(x)
            heads.append(jnp.where(keep, x, jnp.zeros_like(x)))
        q4s.append(jnp.concatenate(heads, axis=0))
        far_bias = bias_ref[1, a, 0:1, :]

        sa_ref[a, 0] = scores_of(ks_ref, i, a) + (bias_ref[0, a] + diag_neg)
        sa_ref[a, 1] = scores_of(ks_ref, t_near, a) + bias_ref[1, a]
        sw_ref[a, 0] = scores_of(kw_ref, i, a) + (bias_ref[0, a] + diag_neg)
        sw_ref[a, 1] = scores_of(kw_ref, t_near, a) + bias_ref[1, a]
        sw_ref[a, 2] = scores_of(kw_ref, t_edge, a) + (far_bias + edge_neg)

        s_c = lax.dot_general(kc_ref[0], q4s[a], _NT, preferred_element_type=F32)
        s_c = jnp.where(cmp_valid, s_c, NEG_INF)
        m_c = jnp.max(s_c, axis=0, keepdims=True)
        e_c = jnp.where(cmp_valid, jnp.exp2(s_c - m_c), 0.0)
        l_c = jnp.sum(e_c, axis=0, keepdims=True)
        p_c = e_c / jnp.where(l_c > 0.0, l_c, 1.0)
        oc_ref[a] = jnp.dot(vct_ref[0, dims_of(a), :], p_c.astype(BF16), preferred_element_type=F32)

        p_sum = p_c[:, 0:TILE]
        for g in range(1, NSA_GROUP):
            p_sum = p_sum + p_c[:, g * TILE:(g + 1) * TILE]
        p_hi = p_sum.astype(BF16)
        p_lo = (p_sum - p_hi.astype(F32)).astype(BF16)
        imp = (jnp.dot(c2s_ref[...], p_hi, preferred_element_type=F32)
               + jnp.dot(c2s_ref[...], p_lo, preferred_element_type=F32))
        j = lax.broadcasted_iota(I32, imp.shape, 0)
        qb = (i * TILE + lax.broadcasted_iota(I32, imp.shape, 1)) >> int(math.log2(SLC_BLOCK))
        forced = (j == 0) | ((j <= qb) & (j > qb - SLC_LOCAL))
        imp = jnp.where(forced, jnp.inf, jnp.where(j > qb, -jnp.inf, imp))
        rank = _rank_before(imp, n_slc)
        selneg = jnp.where((rank < SLC_TOPN) & (j <= qb), 0.0, NEG_INF)
        selneg_ref[a] = selneg
        for c in range(per_tile):
            radd_ref[a, c:c + 1, :] = _group_lanes(selneg_ref[a, pl.ds(per_tile * i + c, 1), :])
            near_row = _group_lanes(selneg_ref[a, pl.ds(per_tile * t_near + c, 1), :])
            radd_ref[a, per_tile + c:per_tile + c + 1, :] = jnp.where(i >= 1, near_row, NEG_INF)
        for blk in range(n_slc):
            row = far_bias + _group_lanes(selneg[blk:blk + 1, :])
            r = 2 * per_tile + blk
            radd_ref[a, r:r + 1, :] = jnp.where(blk // per_tile < n_far, row, NEG_INF)
        _init_state(m_ref.at[a], l_ref.at[a], acc_ref.at[a])

    def update(jg, a, buf):
        first = jg == 0
        t0 = jnp.where(first, i, 2 * jg - 2)
        t1 = jnp.where(first, t_near, jnp.minimum(2 * jg - 1, last))
        scores, adds = [], []
        for t in range(2):
            for c in range(per_tile):
                scores.append(buf[a, t, c * SLC_BLOCK:(c + 1) * SLC_BLOCK, :])
                adds.append(radd_ref[a, pl.ds(2 * per_tile * jg + per_tile * t + c, 1), :])
        _softmax_pv(scores, adds, [vst_ref[0, t0], vst_ref[0, t1]], [(dims_of(a), slice(None))],
                    m_ref.at[a], l_ref.at[a], acc_ref.at[a])

    def step(jg, src, dst):
        for a in kv_heads:
            dst[a, 0] = scores_of(ks_ref, 2 * jg, a)
            dst[a, 1] = scores_of(ks_ref, jnp.minimum(2 * jg + 1, last), a)
            update(jg, a, src)

    def two_steps(jj, carry):
        step(2 * jj, sa_ref, sb_ref)
        step(2 * jj + 1, sb_ref, sa_ref)
        return carry

    lax.fori_loop(0, n_far_groups >> 1, two_steps, 0)

    @pl.when((n_far_groups & 1) == 1)
    def _():
        step(n_far_groups - 1, sa_ref, sb_ref)
        for a in kv_heads:
            update(n_far_groups, a, sb_ref)

    @pl.when((n_far_groups & 1) == 0)
    def _():
        for a in kv_heads:
            update(n_far_groups, a, sa_ref)

    gates = jax.nn.sigmoid(gt_ref[0, 0])
    zero_row = jnp.zeros((1, cols), F32)
    pieces = []
    for a in kv_heads:
        os_ref[a] = acc_ref[a] / l_ref[a]
        _init_state(m_ref.at[a], l_ref.at[a], acc_ref.at[a])
        _softmax_pv([sw_ref[a, 0], sw_ref[a, 1], sw_ref[a, 2]],
                    [zero_row, zero_row + jnp.where(i >= 1, 0.0, NEG_INF), zero_row + jnp.where(i >= 2, 0.0, NEG_INF)],
                    [vwt_ref[0, i], vwt_ref[0, t_near], vwt_ref[0, t_edge]], [(dims_of(a), slice(None))],
                    m_ref.at[a], l_ref.at[a], acc_ref.at[a])
        o_w = acc_ref[a] / l_ref[a]
        for g in range(NSA_GROUP):
            c0 = 3 * (NSA_GROUP * a + g)
            ls = slice(g * TILE, (g + 1) * TILE)
            pieces.append(gates[c0:c0 + 1, :] * oc_ref[a, :, ls] + gates[c0 + 1:c0 + 2, :] * os_ref[a, :, ls]
                          + gates[c0 + 2:c0 + 3, :] * o_w[:, ls])
    o_ref[0] = jnp.concatenate(pieces, axis=0).T.astype(o_ref.dtype)


def _cmp_to_slc(S):
    n_cmp_pad = S // CMP_STRIDE
    n_slc = S // SLC_BLOCK
    ci = np.arange(n_cmp_pad)[:, None] * CMP_STRIDE
    sj = np.arange(n_slc)[None, :] * SLC_BLOCK
    c2s = ((ci < sj + SLC_BLOCK) & (ci + CMP_LEN > sj)).astype(np.float32)
    c2s[(S - CMP_LEN) // CMP_STRIDE + 1:] = 0.0
    return jnp.asarray(c2s.T, BF16)


def _nsa_attention(proj, vt, gate_t, kcmp, vcmp_t, bias):
    B, S, _ = proj.shape
    nq = S // TILE
    n_cmp = kcmp.shape[1]
    n_slc = S // SLC_BLOCK
    qw = 2 * NSA_GROUP * HEAD_DIM
    q_blocks = N_HEADS * HEAD_DIM // LANES
    kv_blocks = NSA_KV_HEADS * HEAD_DIM // LANES

    def k_spec(which):
        base = q_blocks + which * kv_blocks
        return pl.BlockSpec((1, S, LANES), lambda b, p, i: (b, 0, base + p))

    def vt_spec(which):
        base = which * kv_blocks
        return pl.BlockSpec((1, nq, LANES, TILE), lambda b, p, i: (b, 0, base + p, 0))

    state = pltpu.VMEM((2, HEAD_DIM, NSA_GROUP * TILE), F32)
    stat = pltpu.VMEM((2, 1, NSA_GROUP * TILE), F32)
    return pl.pallas_call(
        _nsa_body,
        grid=(B, 2, nq),
        in_specs=[pl.BlockSpec((1, TILE, qw), lambda b, p, i: (b, i, p)),
                  pl.BlockSpec((1, n_cmp, LANES), lambda b, p, i: (b, 0, p)),
                  pl.BlockSpec((1, LANES, n_cmp), lambda b, p, i: (b, p, 0)),
                  k_spec(2), vt_spec(0), k_spec(3), vt_spec(1),
                  pl.BlockSpec((1, 1, LANES, TILE), lambda b, p, i: (b, i, p, 0)),
                  pl.BlockSpec((2, 2, TILE, NSA_GROUP * TILE), lambda b, p, i: (0, p, 0, 0)),
                  pl.BlockSpec((n_slc, n_cmp), lambda b, p, i: (0, 0))],
        out_specs=pl.BlockSpec((1, TILE, qw), lambda b, p, i: (b, i, p)),
        out_shape=jax.ShapeDtypeStruct((B, S, N_HEADS * HEAD_DIM), BF16),
        scratch_shapes=[pltpu.VMEM((2, n_slc, TILE), F32),
                        pltpu.VMEM((2, 2 * (TILE // SLC_BLOCK) + n_slc, NSA_GROUP * TILE), F32),
                        pltpu.VMEM((2, 2, TILE, NSA_GROUP * TILE), F32),
                        pltpu.VMEM((2, 2, TILE, NSA_GROUP * TILE), F32),
                        pltpu.VMEM((2, 3, TILE, NSA_GROUP * TILE), F32),
                        state, state, stat, stat, state],
        compiler_params=_cparams("parallel", "parallel", "arbitrary"),
        name="nsa_attention",
    )(proj, kcmp, vcmp_t, proj, vt, proj, vt, gate_t, _heads_on_lanes(bias, NSA_GROUP), _cmp_to_slc(S))


def _split_bf16(x):
    hi = x.astype(BF16)
    return hi, (x - hi.astype(F32)).astype(BF16)


ROUTER_TM = 1024
_ROW_OF_EXPERT = np.arange(N_EXPERTS).reshape(N_GROUPS, EXPERTS_PER_GROUP).T.reshape(-1)


def _router_body(x_ref, w_ref, b_ref, tri_ref, idx_ref, wt_ref, pos_ref, cnt_ref, base_ref):
    @pl.when(pl.program_id(0) == 0)
    def _():
        base_ref[...] = jnp.zeros(base_ref.shape, F32)

    x_hi, x_lo = _split_bf16(x_ref[...])
    w_hi, w_lo = _split_bf16(w_ref[...])
    logits = (lax.dot_general(w_hi, x_hi, _NT, preferred_element_type=F32)
              + lax.dot_general(w_hi, x_lo, _NT, preferred_element_type=F32)
              + lax.dot_general(w_lo, x_hi, _NT, preferred_element_type=F32)) + b_ref[:, 0:1]
    m = jnp.max(logits, axis=0, keepdims=True)
    e = jnp.exp(logits - m)
    probs = e / jnp.sum(e, axis=0, keepdims=True)
    pk = [probs[k * N_GROUPS:(k + 1) * N_GROUPS] for k in range(EXPERTS_PER_GROUP)]
    hi1, lo1 = jnp.maximum(pk[0], pk[1]), jnp.minimum(pk[0], pk[1])
    hi2, lo2 = jnp.maximum(pk[2], pk[3]), jnp.minimum(pk[2], pk[3])
    score = jnp.maximum(hi1, hi2) + jnp.maximum(jnp.minimum(hi1, hi2), jnp.maximum(lo1, lo2))
    grp = lax.broadcasted_iota(I32, score.shape, 0)
    best = jnp.min(jnp.where(score == jnp.max(score, axis=0, keepdims=True), grp, N_GROUPS),
                   axis=0, keepdims=True)
    v = [jnp.sum(jnp.where(grp == best, p, 0.0), axis=0, keepdims=True) for p in pk]
    v1 = jnp.maximum(jnp.maximum(v[0], v[1]), jnp.maximum(v[2], v[3]))
    i1 = jnp.where(v[0] == v1, 0, jnp.where(v[1] == v1, 1, jnp.where(v[2] == v1, 2, 3)))
    rest = [jnp.where(i1 == k, -1.0, v[k]) for k in range(EXPERTS_PER_GROUP)]
    v2 = jnp.maximum(jnp.maximum(rest[0], rest[1]), jnp.maximum(rest[2], rest[3]))
    i2 = jnp.where(rest[0] == v2, 0, jnp.where(rest[1] == v2, 1, jnp.where(rest[2] == v2, 2, 3)))
    tot = v1 + v2
    idx_ref[...] = jnp.concatenate([best * EXPERTS_PER_GROUP + i1, best * EXPERTS_PER_GROUP + i2], axis=0)
    wt_ref[...] = jnp.concatenate([v1 / tot, v2 / tot], axis=0)

    row = lax.broadcasted_iota(I32, logits.shape, 0)
    hot = [jnp.where(row == ik * N_GROUPS + best, 1.0, 0.0) for ik in (i1, i2)]
    both = (hot[0] + hot[1]).astype(BF16)
    run = base_ref[:, 0:1]
    pos = [[], []]
    for c in range(logits.shape[1] // LANES):
        ls = slice(c * LANES, (c + 1) * LANES)
        before = run + jnp.dot(both[:, ls], tri_ref[...], preferred_element_type=F32) - 1.0
        for k in range(2):
            pos[k].append(jnp.sum(hot[k][:, ls] * before, axis=0, keepdims=True))
        run = before[:, LANES - 1:LANES] + 1.0
    pos_ref[...] = jnp.concatenate([jnp.concatenate(pos[0], axis=1), jnp.concatenate(pos[1], axis=1)],
                                   axis=0).astype(I32)
    base_ref[...] = jnp.broadcast_to(run, base_ref.shape)
    cnt_ref[...] = jnp.broadcast_to(run, cnt_ref.shape)


def _router(x, router_w, router_b):
    N, D = x.shape
    w = router_w.T[_ROW_OF_EXPERT]
    b = jnp.broadcast_to(router_b[_ROW_OF_EXPERT][:, None], (N_EXPERTS, LANES))
    tri = jnp.asarray(np.triu(np.ones((LANES, LANES), np.float32)), BF16)
    tm = ROUTER_TM
    tok = lambda dt: jax.ShapeDtypeStruct((2, N), dt)
    tok_spec = pl.BlockSpec((2, tm), lambda i: (0, i))
    idx, wts, pos, cnt = pl.pallas_call(
        _router_body,
        grid=(N // tm,),
        in_specs=[pl.BlockSpec((tm, D), lambda i: (i, 0)),
                  pl.BlockSpec((N_EXPERTS, D), lambda i: (0, 0)),
                  pl.BlockSpec((N_EXPERTS, LANES), lambda i: (0, 0)),
                  pl.BlockSpec((LANES, LANES), lambda i: (0, 0))],
        out_specs=[tok_spec, tok_spec, tok_spec, pl.BlockSpec((N_EXPERTS, LANES), lambda i: (0, 0))],
        out_shape=[tok(I32), tok(F32), tok(I32), jax.ShapeDtypeStruct((N_EXPERTS, LANES), F32)],
        scratch_shapes=[pltpu.VMEM((N_EXPERTS, LANES), F32)],
        compiler_params=_cparams("arbitrary"),
        name="moe_router",
    )(x, w, b, tri)
    counts = cnt[np.argsort(_ROW_OF_EXPERT), 0].astype(I32)
    return idx, wts, pos, counts


def _expert_body(blk_e_ref, n_used_ref, x_ref, wg_ref, wu_ref, wd_ref, o_ref, wg_b, wu_b, wd_b):
    i = pl.program_id(0)

    @pl.when((i == 0) | (blk_e_ref[i] != blk_e_ref[jnp.maximum(i - 1, 0)]))
    def _():
        wg_b[...] = wg_ref[0, 0].astype(BF16)
        wu_b[...] = wu_ref[0, 0].astype(BF16)
        wd_b[...] = wd_ref[0, 0].astype(BF16)

    @pl.when(i < n_used_ref[0])
    def _():
        x = x_ref[...]
        gate = jnp.dot(x, wg_b[...], preferred_element_type=F32)
        up = jnp.dot(x, wu_b[...], preferred_element_type=F32)
        hid = (gate * jax.nn.sigmoid(gate) * up).astype(BF16)
        o_ref[...] = jnp.dot(hid, wd_b[...], preferred_element_type=F32).astype(o_ref.dtype)

    @pl.when(i >= n_used_ref[0])
    def _():
        o_ref[...] = jnp.zeros(o_ref.shape, o_ref.dtype)


def _experts(xs, blk_e, n_used, wg, wu, wd, layer):
    R, D = xs.shape
    n_blk = R // MOE_TB

    def live(i, be, nu):
        return jnp.minimum(i, nu[0] - 1)

    grid_spec = pltpu.PrefetchScalarGridSpec(
        num_scalar_prefetch=2,
        grid=(n_blk,),
        in_specs=[pl.BlockSpec((MOE_TB, D), lambda i, be, nu: (live(i, be, nu), 0)),
                  pl.BlockSpec((1, 1, D, D_EXPERT), lambda i, be, nu: (layer, be[i], 0, 0)),
                  pl.BlockSpec((1, 1, D, D_EXPERT), lambda i, be, nu: (layer, be[i], 0, 0)),
                  pl.BlockSpec((1, 1, D_EXPERT, D), lambda i, be, nu: (layer, be[i], 0, 0))],
        out_specs=pl.BlockSpec((MOE_TB, D), lambda i, be, nu: (i, 0)),
        scratch_shapes=[pltpu.VMEM((D, D_EXPERT), BF16), pltpu.VMEM((D, D_EXPERT), BF16),
                        pltpu.VMEM((D_EXPERT, D), BF16)],
    )
    return pl.pallas_call(
        _expert_body,
        grid_spec=grid_spec,
        out_shape=jax.ShapeDtypeStruct((R, D), BF16),
        compiler_params=_cparams("arbitrary"),
        name="moe_experts",
    )(blk_e, n_used, xs, wg, wu, wd)


def _combine_ln_body(x_ref, y0_ref, y1_ref, wt_ref, g_ref, b_ref, o_ref, ob_ref):
    ffn = y0_ref[...] * wt_ref[:, 0:1] + y1_ref[...] * wt_ref[:, HEAD_DIM:HEAD_DIM + 1]
    out = _layer_norm_rows(DEEPNORM_ALPHA * x_ref[...] + ffn, g_ref[...], b_ref[...])
    o_ref[...] = out
    ob_ref[...] = out.astype(BF16)


def _combine_ln(x, y0, y1, wt, g, b):
    M, D = x.shape
    row = pl.BlockSpec((LN_TM, D), lambda i: (i, 0))
    vec = pl.BlockSpec((1, D), lambda i: (0, 0))
    return pl.pallas_call(
        _combine_ln_body,
        grid=(M // LN_TM,),
        in_specs=[row, row, row, pl.BlockSpec((LN_TM, LANES), lambda i: (i, 0)), vec, vec],
        out_specs=[row, row],
        out_shape=[jax.ShapeDtypeStruct((M, D), F32), jax.ShapeDtypeStruct((M, D), BF16)],
        compiler_params=_cparams("parallel"),
        name="moe_combine_ln",
    )(x, y0, y1, wt, g.reshape(1, D), b.reshape(1, D))


def _moe_ln(h, hb, router_w, router_b, wg, wu, wd, layer, g, b):
    N, D = h.shape
    A = 2 * N
    idx, wts, pos, counts = _router(h, router_w, router_b)
    starts = jnp.cumsum(counts) - counts
    padded = (counts + MOE_TB - 1) // MOE_TB * MOE_TB
    pends = jnp.cumsum(padded)
    pstarts = pends - padded
    R = A + N_EXPERTS * MOE_TB
    n_blk = R // MOE_TB
    experts = jnp.arange(N_EXPERTS, dtype=I32)
    dest = pos + jnp.sum(jnp.where(idx[None] == experts[:, None, None], pstarts[:, None, None], 0), axis=0)
    tok = jnp.broadcast_to(jnp.arange(N, dtype=I32)[None, :], (2, N))
    _, tok_sorted = lax.sort_key_val(dest.reshape(A), tok.reshape(A))
    blk_row0 = jnp.arange(n_blk, dtype=I32) * MOE_TB
    blk_e = jnp.minimum(jnp.sum((pends[None, :] <= blk_row0[:, None]).astype(I32), axis=1), N_EXPERTS - 1)
    hot = blk_e[:, None] == experts[None, :]
    compact0 = blk_row0 + jnp.sum(jnp.where(hot, (starts - pstarts)[None, :], 0), axis=1)
    compact = jnp.remainder(compact0[:, None] + jnp.arange(MOE_TB, dtype=I32)[None, :], A).reshape(R)
    n_used = (pends[-1:] // MOE_TB).astype(I32)
    xs = hb[tok_sorted[compact]]
    yb = _experts(xs, blk_e, n_used, wg, wu, wd, layer)
    wt = jnp.concatenate([jnp.broadcast_to(wts[k][:, None], (N, HEAD_DIM)) for k in range(2)], axis=1)
    return _combine_ln(h, yb[dest[0]], yb[dest[1]], wt, g, b)


def _moba_layer(h, w_in, w_out, bias, g, b, B, S):
    HD = N_HEADS * HEAD_DIM
    qk = _matmul(h, w_in[:, :2 * HD].astype(BF16), _query_scale(HD, 2 * HD), BF16).reshape(B, S, 2 * HD)
    vt = _matmul_t(w_in[:, 2 * HD:].T.astype(BF16), h, B, S, BF16)
    att = _moba_attention(qk, vt, bias)
    return _proj_ln(att.reshape(B * S, HD), w_out.astype(BF16), h, g, b)


def _nsa_layer(h, hb, w_in, w_out, pos_k, pos_v, ck_w1, ck_w2, cv_w1, cv_w2, bias, g, b, B, S):
    HD = N_HEADS * HEAD_DIM
    kvw = NSA_KV_HEADS * HEAD_DIM
    col = lambda k: slice(HD + k * kvw, HD + (k + 1) * kvw)
    w_rows = jnp.concatenate([w_in[:, :HD + 2 * kvw], w_in[:, col(2)], w_in[:, col(4)]], axis=1)
    proj = _matmul(hb, w_rows.astype(BF16), _query_scale(HD, HD + 4 * kvw), BF16).reshape(B, S, HD + 4 * kvw)
    w_vt = jnp.concatenate([w_in[:, col(3)], w_in[:, col(5)]], axis=1).T
    vt = _matmul_t(w_vt.astype(BF16), hb, B, S, BF16)
    per_pair = 3 * N_HEADS // 2
    wg = w_in[:, HD + 6 * kvw:].reshape(D_MODEL, 2, per_pair)
    wg = jnp.pad(wg, ((0, 0), (0, 0), (0, LANES - per_pair))).reshape(D_MODEL, 2 * LANES).T
    gate_t = _matmul_t(wg.astype(BF16), hb, B, S, F32)

    def grouped(t):
        t = t.reshape(B, S, NSA_KV_HEADS, HEAD_DIM).transpose(0, 2, 1, 3)
        return t.reshape(B, NSA_KV_HEADS, S // CMP_STRIDE, CMP_STRIDE * HEAD_DIM)

    kcmp = _compress(grouped(proj[..., HD:HD + kvw]), pos_k, ck_w1, ck_w2)
    vcmp = _compress(grouped(proj[..., HD + kvw:HD + 2 * kvw]), pos_v, cv_w1, cv_w2)
    n_cmp = kcmp.shape[2]
    kcmp = kcmp.transpose(0, 2, 1, 3).reshape(B, n_cmp, kvw)
    vcmp_t = vcmp.transpose(0, 1, 3, 2).reshape(B, kvw, n_cmp)
    att = _nsa_attention(proj, vt, gate_t, kcmp, vcmp_t, bias)
    return _proj_ln(att.reshape(B * S, HD), w_out.astype(BF16), h, g, b)


def kernel(x, rel_bias, router_w, router_b, ln_g, ln_b, moba_w_in, moba_w_out, nsa_w_in, nsa_w_out,
           nsa_pos_k, nsa_pos_v, nsa_ck_w1, nsa_ck_w2, nsa_cv_w1, nsa_cv_w2,
           moe_w_gate, moe_w_up, moe_w_down):
    B, S, D = x.shape
    bias = _bias_tiles(rel_bias)
    h = x.reshape(B * S, D)
    h, hb = _moba_layer(h, moba_w_in[0], moba_w_out[0], bias, ln_g[0, 0], ln_b[0, 0], B, S)
    h, hb = _moe_ln(h, hb, router_w, router_b, moe_w_gate, moe_w_up, moe_w_down, 0,
                    ln_g[0, 1], ln_b[0, 1])
    h, hb = _nsa_layer(h, hb, nsa_w_in[0], nsa_w_out[0], nsa_pos_k[0], nsa_pos_v[0],
                       nsa_ck_w1[0], nsa_ck_w2[0], nsa_cv_w1[0], nsa_cv_w2[0],
                       bias, ln_g[1, 0], ln_b[1, 0], B, S)
    h, hb = _moe_ln(h, hb, router_w, router_b, moe_w_gate, moe_w_up, moe_w_down, 1,
                    ln_g[1, 1], ln_b[1, 1])
    return h.reshape(B, S, D)
```

```python
import math

import numpy as np
import jax
import jax.numpy as jnp
from jax import lax
from jax.experimental import pallas as pl
from jax.experimental.pallas import tpu as pltpu

F32, BF16, I32 = jnp.float32, jnp.bfloat16, jnp.int32

D_MODEL = 1024
N_HEADS = 16
HEAD_DIM = 64
DEPTH = 2
NEG_INF = -1e30
LN_EPS = 1e-5
MOBA_BLOCK = 256
MOBA_TOPK = 3
NSA_KV_HEADS = 4
NSA_GROUP = N_HEADS // NSA_KV_HEADS
CMP_LEN = 32
CMP_STRIDE = 16
CMP_HIDDEN = 256
SLC_BLOCK = 64
SLC_TOPN = 16
SLC_LOCAL = 2
WINDOW = 512
REL_BUCKETS = 32
REL_MAX_DIST = 128
N_EXPERTS = 32
N_GROUPS = 8
EXPERTS_PER_GROUP = N_EXPERTS // N_GROUPS
D_EXPERT = 512
DEEPNORM_ALPHA = (2 * DEPTH) ** 0.25
LOG2E = math.log2(math.e)
Q_SCALE = HEAD_DIM ** -0.5 * LOG2E

LANES = 128
SUBLANES = 8
TILE = 256
MM_TM = 1024
MM_TN = 1024
LN_TM = 1024
MOE_TB = 512
VMEM_LIMIT = 48 * 1024 * 1024

_NT = (((1,), (1,)), ((), ()))


def _cparams(*sem):
    return pltpu.CompilerParams(dimension_semantics=sem, vmem_limit_bytes=VMEM_LIMIT)


def _mm_body(a_ref, b_ref, c_ref, o_ref):
    acc = jnp.dot(a_ref[...].astype(BF16), b_ref[...], preferred_element_type=F32)
    o_ref[...] = (acc * c_ref[...]).astype(o_ref.dtype)


def _matmul(a, b, col_scale, out_dtype):
    M, K = a.shape
    N = b.shape[1]
    tn = min(MM_TN, N)
    return pl.pallas_call(
        _mm_body,
        grid=(M // MM_TM, N // tn),
        in_specs=[pl.BlockSpec((MM_TM, K), lambda i, j: (i, 0)),
                  pl.BlockSpec((K, tn), lambda i, j: (0, j)),
                  pl.BlockSpec((1, tn), lambda i, j: (0, j))],
        out_specs=pl.BlockSpec((MM_TM, tn), lambda i, j: (i, j)),
        out_shape=jax.ShapeDtypeStruct((M, N), out_dtype),
        compiler_params=_cparams("parallel", "arbitrary"),
        name="in_proj",
    )(a, b, col_scale.reshape(1, N))


def _query_scale(n_query_cols, n_cols):
    return jnp.where(jnp.arange(n_cols) < n_query_cols, Q_SCALE, 1.0).astype(F32)


def _mm_t_body(w_ref, a_ref, o_ref):
    r = lax.dot_general(w_ref[...], a_ref[...].astype(BF16), _NT, preferred_element_type=F32)
    for t in range(o_ref.shape[1]):
        o_ref[0, t] = r[:, t * TILE:(t + 1) * TILE].astype(o_ref.dtype)


def _matmul_t(w_t, a, B, S, out_dtype):
    Nout, K = w_t.shape
    tn = min(MM_TN, Nout)
    per_seq = S // MM_TM
    sub = MM_TM // TILE
    return pl.pallas_call(
        _mm_t_body,
        grid=(B * per_seq, Nout // tn),
        in_specs=[pl.BlockSpec((tn, K), lambda i, j: (j, 0)),
                  pl.BlockSpec((MM_TM, K), lambda i, j: (i, 0))],
        out_specs=pl.BlockSpec((1, sub, tn, TILE), lambda i, j: (i // per_seq, i % per_seq, j, 0)),
        out_shape=jax.ShapeDtypeStruct((B, S // TILE, Nout, TILE), out_dtype),
        compiler_params=_cparams("parallel", "arbitrary"),
        name="in_proj_t",
    )(w_t, a)


def _layer_norm_rows(z, g, b):
    mu = jnp.mean(z, axis=-1, keepdims=True)
    zc = z - mu
    var = jnp.mean(zc * zc, axis=-1, keepdims=True)
    return zc * lax.rsqrt(var + LN_EPS) * g + b


def _t5_bucket_np(rel):
    n = np.maximum(rel, 0)
    max_exact = REL_BUCKETS // 2
    nf = np.maximum(n, 1).astype(np.float32)
    large = max_exact + (np.log(nf / np.float32(max_exact))
                         / np.float32(math.log(REL_MAX_DIST / max_exact))
                         * np.float32(REL_BUCKETS - max_exact)).astype(np.int32)
    large = np.minimum(large, REL_BUCKETS - 1)
    return np.where(n < max_exact, n, large).astype(np.int32)


def _bias_body(tbl_ref, bk_ref, o_ref):
    h = pl.program_id(0)
    for dl in range(2):
        bk = bk_ref[dl]
        acc = jnp.zeros((TILE, TILE), F32)
        for b in range(REL_BUCKETS):
            acc = jnp.where(bk == b, tbl_ref[h * REL_BUCKETS + b], acc)
        o_ref[dl, 0] = acc * LOG2E


def _bias_tiles(rel_bias):
    key = np.arange(TILE)[:, None]
    qry = np.arange(TILE)[None, :]
    assert int(_t5_bucket_np(np.array(TILE + 1))) == REL_BUCKETS - 1
    bk = np.stack([_t5_bucket_np(qry - key), _t5_bucket_np(TILE + qry - key)])
    return pl.pallas_call(
        _bias_body,
        grid=(N_HEADS,),
        in_specs=[pl.BlockSpec(memory_space=pltpu.SMEM),
                  pl.BlockSpec((2, TILE, TILE), lambda h: (0, 0, 0))],
        out_specs=pl.BlockSpec((2, 1, TILE, TILE), lambda h: (0, h, 0, 0)),
        out_shape=jax.ShapeDtypeStruct((2, N_HEADS, TILE, TILE), F32),
        name="t5_bias_tiles",
    )(rel_bias.T.reshape(-1), jnp.asarray(bk))


def _heads_on_lanes(bias, per_block):
    two, H, T, _ = bias.shape
    b = bias.reshape(two, H // per_block, per_block, T, T).transpose(0, 1, 3, 2, 4)
    return b.reshape(two, H // per_block, T, per_block * T)


def _init_state(m_ref, l_ref, acc_ref):
    m_ref[...] = jnp.full(m_ref.shape, NEG_INF, F32)
    l_ref[...] = jnp.zeros(l_ref.shape, F32)
    acc_ref[...] = jnp.zeros(acc_ref.shape, F32)


def _rank_before(vals, rows):
    idx = lax.broadcasted_iota(I32, vals.shape, 0)
    rank = jnp.zeros(vals.shape, I32)
    for m in range(rows):
        row = vals[m:m + 1, :]
        beats = (row > vals) | ((row == vals) & (idx > m))
        rank = rank + jnp.where(beats, 1, 0)
    return rank


MOBA_STREAMS = 4


def _softmax_pv(scores, adds, vts, heads, m_ref, l_ref, acc_ref):
    def fold(x, op):
        return op(x.reshape(x.shape[0] // SUBLANES, SUBLANES, x.shape[1]), axis=0)

    m_prev = m_ref[...]
    m_part = None
    for s, add in zip(scores, adds):
        part = fold(s, jnp.max) + add
        m_part = part if m_part is None else jnp.maximum(m_part, part)
    m_new = jnp.maximum(m_prev, jnp.max(m_part, axis=0, keepdims=True))
    a = jnp.exp2(m_prev - m_new)
    probs = [jnp.exp2(s - (m_new - add)) for s, add in zip(scores, adds)]
    l_part = fold(probs[0], jnp.sum)
    for p in probs[1:]:
        l_part = l_part + fold(p, jnp.sum)
    l_ref[...] = a * l_ref[...] + jnp.sum(l_part, axis=0, keepdims=True)
    vt = jnp.concatenate(vts, axis=1)
    pb = jnp.concatenate([p.astype(BF16) for p in probs], axis=0)
    pv = jnp.concatenate([jnp.dot(vt[rows], pb[:, cols], preferred_element_type=F32) for rows, cols in heads],
                         axis=1)
    acc_ref[...] = a * acc_ref[...] + pv
    m_ref[...] = m_new


def _moba_body(q_ref, k_ref, vt_ref, bias_ref, o_ref, kmean_ref, radd_ref, sa_ref, sb_ref, m_ref, l_ref, acc_ref):
    i = pl.program_id(2)
    nb = k_ref.shape[1] // TILE
    streams = range(MOBA_STREAMS)
    lanes_of = lambda s: slice(s * LANES, (s + 1) * LANES)

    @pl.when(i == 0)
    def _():
        for s in streams:
            for n in range(nb):
                kb = k_ref[0, n * TILE:(n + 1) * TILE, lanes_of(s)].astype(F32)
                kmean_ref[s, n:n + 1, :] = jnp.sum(kb, axis=0, keepdims=True) * (1.0 / TILE)

    n_far = jnp.maximum(i - 1, 0)
    n_far_groups = (n_far + 1) >> 1
    last = nb - 1

    def key_tile(t, s):
        return k_ref[0, pl.ds(pl.multiple_of(t * TILE, TILE), TILE), lanes_of(s)]

    def far_tiles(j):
        return 2 * j, jnp.minimum(2 * j + 1, last)

    lane = lax.broadcasted_iota(I32, (TILE, LANES), 1)
    key = lax.broadcasted_iota(I32, (TILE, 2 * TILE), 0)
    qry = lax.broadcasted_iota(I32, (TILE, 2 * TILE), 1) & (TILE - 1)
    causal_neg = jnp.where(key <= qry, 0.0, NEG_INF)
    t_near = jnp.maximum(i - 1, 0)
    q2s = []
    for s in streams:
        q = q_ref[0, :, lanes_of(s)]
        zero = jnp.zeros_like(q)
        q2 = jnp.concatenate([jnp.where(lane < HEAD_DIM, q, zero),
                              jnp.where(lane >= HEAD_DIM, q, zero)], axis=0)
        q2s.append(q2)
        sa_ref[s, 0] = (lax.dot_general(key_tile(i, s), q2, _NT, preferred_element_type=F32)
                        + (bias_ref[0, s] + causal_neg))
        sa_ref[s, 1] = lax.dot_general(key_tile(t_near, s), q2, _NT, preferred_element_type=F32) + bias_ref[1, s]
        km = kmean_ref[s]
        k_hi = km.astype(BF16)
        k_lo = (km - k_hi.astype(F32)).astype(BF16)
        gate = (lax.dot_general(k_hi, q2, _NT, preferred_element_type=F32)
                + lax.dot_general(k_lo, q2, _NT, preferred_element_type=F32))
        blk = lax.broadcasted_iota(I32, gate.shape, 0)
        gate = jnp.where(blk < i, gate, -jnp.inf)
        rank = _rank_before(gate, nb)
        neg = jnp.where((rank < MOBA_TOPK) & (blk < i), 0.0, NEG_INF)
        far_bias = bias_ref[1, s, 0:1, :]
        near_row = jnp.full((1, 2 * TILE), NEG_INF, F32)
        radd_ref[s, 0:1, :] = jnp.zeros((1, 2 * TILE), F32)
        for n in range(nb):
            row = neg[n:n + 1, :]
            near_row = jnp.where(n == i - 1, row, near_row)
            radd_ref[s, 2 + n:3 + n, :] = jnp.where(n < n_far, far_bias + row, NEG_INF)
        radd_ref[s, 1:2, :] = near_row
        _init_state(m_ref.at[s], l_ref.at[s], acc_ref.at[s])

    pair_heads = [(slice(h * HEAD_DIM, (h + 1) * HEAD_DIM), slice(h * TILE, (h + 1) * TILE)) for h in range(2)]

    def update(j, s, buf):
        first = j == 0
        t0 = jnp.where(first, i, 2 * j - 2)
        t1 = jnp.where(first, t_near, jnp.minimum(2 * j - 1, last))
        _softmax_pv([buf[s, 0], buf[s, 1]],
                    [radd_ref[s, pl.ds(2 * j, 1), :], radd_ref[s, pl.ds(2 * j + 1, 1), :]],
                    [vt_ref[0, t0, lanes_of(s), :], vt_ref[0, t1, lanes_of(s), :]], pair_heads,
                    m_ref.at[s], l_ref.at[s], acc_ref.at[s])

    def step(j, src, dst):
        for s in streams:
            ta, tb = far_tiles(j)
            dst[s, 0] = lax.dot_general(key_tile(ta, s), q2s[s], _NT, preferred_element_type=F32)
            dst[s, 1] = lax.dot_general(key_tile(tb, s), q2s[s], _NT, preferred_element_type=F32)
            update(j, s, src)

    def two_steps(jj, carry):
        step(2 * jj, sa_ref, sb_ref)
        step(2 * jj + 1, sb_ref, sa_ref)
        return carry

    lax.fori_loop(0, n_far_groups >> 1, two_steps, 0)

    @pl.when((n_far_groups & 1) == 1)
    def _():
        step(n_far_groups - 1, sa_ref, sb_ref)
        for s in streams:
            update(n_far_groups, s, sb_ref)

    @pl.when((n_far_groups & 1) == 0)
    def _():
        for s in streams:
            update(n_far_groups, s, sa_ref)

    for s in streams:
        o = acc_ref[s] / l_ref[s]
        o = jnp.concatenate([o[:, :TILE], o[:, TILE:]], axis=0)
        o_ref[0, :, lanes_of(s)] = o.T.astype(o_ref.dtype)


def _moba_attention(qk, vt, bias):
    B, S, _ = qk.shape
    n_steps = N_HEADS // 2 // MOBA_STREAMS
    nq = S // TILE
    w = MOBA_STREAMS * LANES
    return pl.pallas_call(
        _moba_body,
        grid=(B, n_steps, nq),
        in_specs=[pl.BlockSpec((1, TILE, w), lambda b, p, i: (b, i, p)),
                  pl.BlockSpec((1, S, w), lambda b, p, i: (b, 0, n_steps + p)),
                  pl.BlockSpec((1, nq, w, TILE), lambda b, p, i: (b, 0, p, 0)),
                  pl.BlockSpec((2, MOBA_STREAMS, TILE, 2 * TILE), lambda b, p, i: (0, p, 0, 0))],
        out_specs=pl.BlockSpec((1, TILE, w), lambda b, p, i: (b, i, p)),
        out_shape=jax.ShapeDtypeStruct((B, S, N_HEADS * HEAD_DIM), BF16),
        scratch_shapes=[pltpu.VMEM((MOBA_STREAMS, nq, LANES), F32),
                        pltpu.VMEM((MOBA_STREAMS, 2 + nq, 2 * TILE), F32),
                        pltpu.VMEM((MOBA_STREAMS, 2, TILE, 2 * TILE), F32),
                        pltpu.VMEM((MOBA_STREAMS, 2, TILE, 2 * TILE), F32),
                        pltpu.VMEM((MOBA_STREAMS, 1, 2 * TILE), F32),
                        pltpu.VMEM((MOBA_STREAMS, 1, 2 * TILE), F32),
                        pltpu.VMEM((MOBA_STREAMS, HEAD_DIM, 2 * TILE), F32)],
        compiler_params=_cparams("parallel", "parallel", "arbitrary"),
        name="moba_attention",
    )(qk, qk, vt, _heads_on_lanes(bias, 2))


def _gelu_tanh(x):
    return 0.5 * x * (1.0 + jnp.tanh(math.sqrt(2.0 / math.pi) * (x + 0.044715 * (x * x * x))))


def _compress_body(t_ref, pos_ref, w1_ref, w2_ref, o_ref):
    groups = t_ref.shape[2]
    half = t_ref.shape[3]
    t = t_ref[0].reshape(NSA_KV_HEADS * groups, half).astype(F32)
    first = jnp.dot((t + pos_ref[0:1, :]).astype(BF16), w1_ref[0:half, :], preferred_element_type=F32)
    second = jnp.dot((t + pos_ref[1:2, :]).astype(BF16), w1_ref[half:2 * half, :],
                     preferred_element_type=F32)
    rows = first.shape[0]
    pre = first + pltpu.roll(second, rows - 1, 0)
    out = jnp.dot(_gelu_tanh(pre).astype(BF16), w2_ref[...], preferred_element_type=F32)
    for h in range(NSA_KV_HEADS):
        o_ref[0, h] = out[h * groups:(h + 1) * groups].astype(o_ref.dtype)


def _compress(t, pos, w1, w2):
    B, Hkv, groups, half = t.shape
    return pl.pallas_call(
        _compress_body,
        grid=(B,),
        in_specs=[pl.BlockSpec((1, Hkv, groups, half), lambda b: (b, 0, 0, 0)),
                  pl.BlockSpec((2, half), lambda b: (0, 0)),
                  pl.BlockSpec((2 * half, CMP_HIDDEN), lambda b: (0, 0)),
                  pl.BlockSpec((CMP_HIDDEN, HEAD_DIM), lambda b: (0, 0))],
        out_specs=pl.BlockSpec((1, Hkv, groups, HEAD_DIM), lambda b: (b, 0, 0, 0)),
        out_shape=jax.ShapeDtypeStruct((B, Hkv, groups, HEAD_DIM), BF16),
        compiler_params=_cparams("parallel"),
        name="nsa_compress",
    )(t, pos.reshape(2, half), w1.astype(BF16), w2.astype(BF16))


def _swap_halves(x):
    return jnp.concatenate([x[:, HEAD_DIM:], x[:, :HEAD_DIM]], axis=1)


def _group_lanes(x):
    return jnp.concatenate([x] * NSA_GROUP, axis=1)


def _nsa_body(q_ref, kc_ref, vct_ref, ks_ref, vst_ref, kw_ref, vwt_ref, gt_ref, bias_ref, c2s_ref,
              o_ref, selneg_ref, radd_ref, sa_ref, sb_ref, sw_ref, oc_ref, os_ref, m_ref, l_ref, acc_ref):
    i = pl.program_id(2)
    nb = ks_ref.shape[1] // TILE
    n_cmp = kc_ref.shape[1]
    n_slc = c2s_ref.shape[0]
    per_tile = TILE // SLC_BLOCK
    cols = NSA_GROUP * TILE
    kv_heads = range(2)
    n_far = jnp.maximum(i - 1, 0)
    n_far_groups = (n_far + 1) >> 1
    last = nb - 1
    t_near = jnp.maximum(i - 1, 0)
    t_edge = jnp.maximum(i - 2, 0)

    dims_of = lambda a: slice(a * HEAD_DIM, (a + 1) * HEAD_DIM)

    def key_tile(k_ref, t):
        return k_ref[0, pl.ds(pl.multiple_of(t * TILE, TILE), TILE), :]

    def scores_of(k_ref, t, a):
        return lax.dot_general(key_tile(k_ref, t), q4s[a], _NT, preferred_element_type=F32)

    lane = lax.broadcasted_iota(I32, (TILE, LANES), 1)
    lo_half = lane < HEAD_DIM
    qpos = i * TILE + (lax.broadcasted_iota(I32, (n_cmp, cols), 1) & (TILE - 1))
    cmp_valid = CMP_STRIDE * lax.broadcasted_iota(I32, (n_cmp, cols), 0) + (CMP_LEN - 1) <= qpos
    key = lax.broadcasted_iota(I32, (TILE, TILE), 0)
    qry = lax.broadcasted_iota(I32, (TILE, TILE), 1)
    diag_neg = _group_lanes(jnp.where(key <= qry, 0.0, NEG_INF))
    edge_neg = _group_lanes(jnp.where(key > qry, 0.0, NEG_INF))
    qall = q_ref[0]
    q4s = []

    for a in kv_heads:
        keep = lo_half if a == 0 else jnp.logical_not(lo_half)
        heads = []
        for g in range(NSA_GROUP):
            cb = a * 2 + g // 2
            x = qall[:, cb * LANES:(cb + 1) * LANES]
            if g % 2 != a:
                x = _swap_halves(x)
            heads.append(jnp.where(keep, x, jnp.zeros_like(x)))
        q4s.append(jnp.concatenate(heads, axis=0))
        far_bias = bias_ref[1, a, 0:1, :]

        sa_ref[a, 0] = scores_of(ks_ref, i, a) + (bias_ref[0, a] + diag_neg)
        sa_ref[a, 1] = scores_of(ks_ref, t_near, a) + bias_ref[1, a]
        sw_ref[a, 0] = scores_of(kw_ref, i, a) + (bias_ref[0, a] + diag_neg)
        sw_ref[a, 1] = scores_of(kw_ref, t_near, a) + bias_ref[1, a]
        sw_ref[a, 2] = scores_of(kw_ref, t_edge, a) + (far_bias + edge_neg)

        s_c = lax.dot_general(kc_ref[0], q4s[a], _NT, preferred_element_type=F32)
        s_c = jnp.where(cmp_valid, s_c, NEG_INF)
        m_c = jnp.max(s_c, axis=0, keepdims=True)
        e_c = jnp.where(cmp_valid, jnp.exp2(s_c - m_c), 0.0)
        l_c = jnp.sum(e_c, axis=0, keepdims=True)
        p_c = e_c / jnp.where(l_c > 0.0, l_c, 1.0)
        oc_ref[a] = jnp.dot(vct_ref[0, dims_of(a), :], p_c.astype(BF16), preferred_element_type=F32)

        p_sum = p_c[:, 0:TILE]
        for g in range(1, NSA_GROUP):
            p_sum = p_sum + p_c[:, g * TILE:(g + 1) * TILE]
        p_hi = p_sum.astype(BF16)
        p_lo = (p_sum - p_hi.astype(F32)).astype(BF16)
        imp = (jnp.dot(c2s_ref[...], p_hi, preferred_element_type=F32)
               + jnp.dot(c2s_ref[...], p_lo, preferred_element_type=F32))
        j = lax.broadcasted_iota(I32, imp.shape, 0)
        qb = (i * TILE + lax.broadcasted_iota(I32, imp.shape, 1)) >> int(math.log2(SLC_BLOCK))
        forced = (j == 0) | ((j <= qb) & (j > qb - SLC_LOCAL))
        imp = jnp.where(forced, jnp.inf, jnp.where(j > qb, -jnp.inf, imp))
        rank = _rank_before(imp, n_slc)
        selneg = jnp.where((rank < SLC_TOPN) & (j <= qb), 0.0, NEG_INF)
        selneg_ref[a] = selneg
        for c in range(per_tile):
            radd_ref[a, c:c + 1, :] = _group_lanes(selneg_ref[a, pl.ds(per_tile * i + c, 1), :])
            near_row = _group_lanes(selneg_ref[a, pl.ds(per_tile * t_near + c, 1), :])
            radd_ref[a, per_tile + c:per_tile + c + 1, :] = jnp.where(i >= 1, near_row, NEG_INF)
        for blk in range(n_slc):
            row = far_bias + _group_lanes(selneg[blk:blk + 1, :])
            r = 2 * per_tile + blk
            radd_ref[a, r:r + 1, :] = jnp.where(blk // per_tile < n_far, row, NEG_INF)
        _init_state(m_ref.at[a], l_ref.at[a], acc_ref.at[a])

    def update(jg, a, buf):
        first = jg == 0
        t0 = jnp.where(first, i, 2 * jg - 2)
        t1 = jnp.where(first, t_near, jnp.minimum(2 * jg - 1, last))
        scores, adds = [], []
        for t in range(2):
            for c in range(per_tile):
                scores.append(buf[a, t, c * SLC_BLOCK:(c + 1) * SLC_BLOCK, :])
                adds.append(radd_ref[a, pl.ds(2 * per_tile * jg + per_tile * t + c, 1), :])
        _softmax_pv(scores, adds, [vst_ref[0, t0], vst_ref[0, t1]], [(dims_of(a), slice(None))],
                    m_ref.at[a], l_ref.at[a], acc_ref.at[a])

    def step(jg, src, dst):
        for a in kv_heads:
            dst[a, 0] = scores_of(ks_ref, 2 * jg, a)
            dst[a, 1] = scores_of(ks_ref, jnp.minimum(2 * jg + 1, last), a)
            update(jg, a, src)

    def two_steps(jj, carry):
        step(2 * jj, sa_ref, sb_ref)
        step(2 * jj + 1, sb_ref, sa_ref)
        return carry

    lax.fori_loop(0, n_far_groups >> 1, two_steps, 0)

    @pl.when((n_far_groups & 1) == 1)
    def _():
        step(n_far_groups - 1, sa_ref, sb_ref)
        for a in kv_heads:
            update(n_far_groups, a, sb_ref)

    @pl.when((n_far_groups & 1) == 0)
    def _():
        for a in kv_heads:
            update(n_far_groups, a, sa_ref)

    gates = jax.nn.sigmoid(gt_ref[0, 0])
    zero_row = jnp.zeros((1, cols), F32)
    pieces = []
    for a in kv_heads:
        os_ref[a] = acc_ref[a] / l_ref[a]
        _init_state(m_ref.at[a], l_ref.at[a], acc_ref.at[a])
        _softmax_pv([sw_ref[a, 0], sw_ref[a, 1], sw_ref[a, 2]],
                    [zero_row, zero_row + jnp.where(i >= 1, 0.0, NEG_INF), zero_row + jnp.where(i >= 2, 0.0, NEG_INF)],
                    [vwt_ref[0, i], vwt_ref[0, t_near], vwt_ref[0, t_edge]], [(dims_of(a), slice(None))],
                    m_ref.at[a], l_ref.at[a], acc_ref.at[a])
        o_w = acc_ref[a] / l_ref[a]
        for g in range(NSA_GROUP):
            c0 = 3 * (NSA_GROUP * a + g)
            ls = slice(g * TILE, (g + 1) * TILE)
            pieces.append(gates[c0:c0 + 1, :] * oc_ref[a, :, ls] + gates[c0 + 1:c0 + 2, :] * os_ref[a, :, ls]
                          + gates[c0 + 2:c0 + 3, :] * o_w[:, ls])
    o_ref[0] = jnp.concatenate(pieces, axis=0).T.astype(o_ref.dtype)


def _cmp_to_slc(S):
    n_cmp_pad = S // CMP_STRIDE
    n_slc = S // SLC_BLOCK
    ci = np.arange(n_cmp_pad)[:, None] * CMP_STRIDE
    sj = np.arange(n_slc)[None, :] * SLC_BLOCK
    c2s = ((ci < sj + SLC_BLOCK) & (ci + CMP_LEN > sj)).astype(np.float32)
    c2s[(S - CMP_LEN) // CMP_STRIDE + 1:] = 0.0
    return jnp.asarray(c2s.T, BF16)


def _nsa_attention(proj, vt, gate_t, kcmp, vcmp_t, bias):
    B, S, _ = proj.shape
    nq = S // TILE
    n_cmp = kcmp.shape[1]
    n_slc = S // SLC_BLOCK
    qw = 2 * NSA_GROUP * HEAD_DIM
    q_blocks = N_HEADS * HEAD_DIM // LANES
    kv_blocks = NSA_KV_HEADS * HEAD_DIM // LANES

    def k_spec(which):
        base = q_blocks + which * kv_blocks
        return pl.BlockSpec((1, S, LANES), lambda b, p, i: (b, 0, base + p))

    def vt_spec(which):
        base = which * kv_blocks
        return pl.BlockSpec((1, nq, LANES, TILE), lambda b, p, i: (b, 0, base + p, 0))

    state = pltpu.VMEM((2, HEAD_DIM, NSA_GROUP * TILE), F32)
    stat = pltpu.VMEM((2, 1, NSA_GROUP * TILE), F32)
    return pl.pallas_call(
        _nsa_body,
        grid=(B, 2, nq),
        in_specs=[pl.BlockSpec((1, TILE, qw), lambda b, p, i: (b, i, p)),
                  pl.BlockSpec((1, n_cmp, LANES), lambda b, p, i: (b, 0, p)),
                  pl.BlockSpec((1, LANES, n_cmp), lambda b, p, i: (b, p, 0)),
                  k_spec(2), vt_spec(0), k_spec(3), vt_spec(1),
                  pl.BlockSpec((1, 1, LANES, TILE), lambda b, p, i: (b, i, p, 0)),
                  pl.BlockSpec((2, 2, TILE, NSA_GROUP * TILE), lambda b, p, i: (0, p, 0, 0)),
                  pl.BlockSpec((n_slc, n_cmp), lambda b, p, i: (0, 0))],
        out_specs=pl.BlockSpec((1, TILE, qw), lambda b, p, i: (b, i, p)),
        out_shape=jax.ShapeDtypeStruct((B, S, N_HEADS * HEAD_DIM), BF16),
        scratch_shapes=[pltpu.VMEM((2, n_slc, TILE), F32),
                        pltpu.VMEM((2, 2 * (TILE // SLC_BLOCK) + n_slc, NSA_GROUP * TILE), F32),
                        pltpu.VMEM((2, 2, TILE, NSA_GROUP * TILE), F32),
                        pltpu.VMEM((2, 2, TILE, NSA_GROUP * TILE), F32),
                        pltpu.VMEM((2, 3, TILE, NSA_GROUP * TILE), F32),
                        state, state, stat, stat, state],
        compiler_params=_cparams("parallel", "parallel", "arbitrary"),
        name="nsa_attention",
    )(proj, kcmp, vcmp_t, proj, vt, proj, vt, gate_t, _heads_on_lanes(bias, NSA_GROUP), _cmp_to_slc(S))


def _split_bf16(x):
    hi = x.astype(BF16)
    return hi, (x - hi.astype(F32)).astype(BF16)


_ROW_OF_EXPERT = np.arange(N_EXPERTS).reshape(N_GROUPS, EXPERTS_PER_GROUP).T.reshape(-1)


def _route(x, w_ref, b_ref, tri_ref, idx_ref, wt_ref, pos_ref, cnt_ref, base_ref):
    @pl.when(pl.program_id(0) == 0)
    def _():
        base_ref[...] = jnp.zeros(base_ref.shape, F32)

    x_hi, x_lo = _split_bf16(x)
    w_hi, w_lo = _split_bf16(w_ref[...])
    logits = (lax.dot_general(w_hi, x_hi, _NT, preferred_element_type=F32)
              + lax.dot_general(w_hi, x_lo, _NT, preferred_element_type=F32)
              + lax.dot_general(w_lo, x_hi, _NT, preferred_element_type=F32)) + b_ref[:, 0:1]
    m = jnp.max(logits, axis=0, keepdims=True)
    e = jnp.exp(logits - m)
    probs = e / jnp.sum(e, axis=0, keepdims=True)
    pk = [probs[k * N_GROUPS:(k + 1) * N_GROUPS] for k in range(EXPERTS_PER_GROUP)]
    hi1, lo1 = jnp.maximum(pk[0], pk[1]), jnp.minimum(pk[0], pk[1])
    hi2, lo2 = jnp.maximum(pk[2], pk[3]), jnp.minimum(pk[2], pk[3])
    score = jnp.maximum(hi1, hi2) + jnp.maximum(jnp.minimum(hi1, hi2), jnp.maximum(lo1, lo2))
    grp = lax.broadcasted_iota(I32, score.shape, 0)
    best = jnp.min(jnp.where(score == jnp.max(score, axis=0, keepdims=True), grp, N_GROUPS),
                   axis=0, keepdims=True)
    v = [jnp.sum(jnp.where(grp == best, p, 0.0), axis=0, keepdims=True) for p in pk]
    v1 = jnp.maximum(jnp.maximum(v[0], v[1]), jnp.maximum(v[2], v[3]))
    i1 = jnp.where(v[0] == v1, 0, jnp.where(v[1] == v1, 1, jnp.where(v[2] == v1, 2, 3)))
    rest = [jnp.where(i1 == k, -1.0, v[k]) for k in range(EXPERTS_PER_GROUP)]
    v2 = jnp.maximum(jnp.maximum(rest[0], rest[1]), jnp.maximum(rest[2], rest[3]))
    i2 = jnp.where(rest[0] == v2, 0, jnp.where(rest[1] == v2, 1, jnp.where(rest[2] == v2, 2, 3)))
    tot = v1 + v2
    idx_ref[...] = jnp.concatenate([best * EXPERTS_PER_GROUP + i1, best * EXPERTS_PER_GROUP + i2], axis=0)
    wt_ref[...] = jnp.concatenate([v1 / tot, v2 / tot], axis=0)

    row = lax.broadcasted_iota(I32, logits.shape, 0)
    hot = [jnp.where(row == ik * N_GROUPS + best, 1.0, 0.0) for ik in (i1, i2)]
    both = (hot[0] + hot[1]).astype(BF16)
    run = base_ref[:, 0:1]
    pos = [[], []]
    for c in range(logits.shape[1] // LANES):
        ls = slice(c * LANES, (c + 1) * LANES)
        before = run + jnp.dot(both[:, ls], tri_ref[...], preferred_element_type=F32) - 1.0
        for k in range(2):
            pos[k].append(jnp.sum(hot[k][:, ls] * before, axis=0, keepdims=True))
        run = before[:, LANES - 1:LANES] + 1.0
    pos_ref[...] = jnp.concatenate([jnp.concatenate(pos[0], axis=1), jnp.concatenate(pos[1], axis=1)],
                                   axis=0).astype(I32)
    base_ref[...] = jnp.broadcast_to(run, base_ref.shape)
    cnt_ref[...] = jnp.broadcast_to(run, cnt_ref.shape)


def _router_operands(router_w, router_b):
    w = router_w.T[_ROW_OF_EXPERT]
    b = jnp.broadcast_to(router_b[_ROW_OF_EXPERT][:, None], (N_EXPERTS, LANES))
    tri = jnp.asarray(np.triu(np.ones((LANES, LANES), np.float32)), BF16)
    return w, b, tri


def _proj_ln_route_body(a_ref, w_ref, x_ref, g_ref, b_ref, rw_ref, rb_ref, tri_ref,
                        o_ref, ob_ref, idx_ref, wt_ref, pos_ref, cnt_ref, base_ref):
    y = jnp.dot(a_ref[...], w_ref[...], preferred_element_type=F32)
    out = _layer_norm_rows(DEEPNORM_ALPHA * x_ref[...] + y, g_ref[...], b_ref[...])
    o_ref[...] = out
    ob_ref[...] = out.astype(BF16)
    _route(out, rw_ref, rb_ref, tri_ref, idx_ref, wt_ref, pos_ref, cnt_ref, base_ref)


def _proj_ln_route(a, w, x, g, b, router_w, router_b):
    M, K = a.shape
    D = w.shape[1]
    row = pl.BlockSpec((LN_TM, D), lambda i: (i, 0))
    vec = pl.BlockSpec((1, D), lambda i: (0, 0))
    whole = lambda shape: pl.BlockSpec(shape, lambda i: (0, 0))
    tok = lambda dt: jax.ShapeDtypeStruct((2, M), dt)
    tok_spec = pl.BlockSpec((2, LN_TM), lambda i: (0, i))
    h, hb, idx, wts, pos, cnt = pl.pallas_call(
        _proj_ln_route_body,
        grid=(M // LN_TM,),
        in_specs=[pl.BlockSpec((LN_TM, K), lambda i: (i, 0)), whole((K, D)), row, vec, vec,
                  whole((N_EXPERTS, D)), whole((N_EXPERTS, LANES)), whole((LANES, LANES))],
        out_specs=[row, row, tok_spec, tok_spec, tok_spec, whole((N_EXPERTS, LANES))],
        out_shape=[jax.ShapeDtypeStruct((M, D), F32), jax.ShapeDtypeStruct((M, D), BF16),
                   tok(I32), tok(F32), tok(I32), jax.ShapeDtypeStruct((N_EXPERTS, LANES), F32)],
        scratch_shapes=[pltpu.VMEM((N_EXPERTS, LANES), F32)],
        compiler_params=_cparams("arbitrary"),
        name="out_proj_ln_route",
    )(a, w, x, g.reshape(1, D), b.reshape(1, D), *_router_operands(router_w, router_b))
    counts = cnt[np.argsort(_ROW_OF_EXPERT), 0].astype(I32)
    return h, hb, (idx, wts, pos, counts)


def _expert_body(blk_e_ref, n_used_ref, x_ref, wg_ref, wu_ref, wd_ref, o_ref, wg_b, wu_b, wd_b):
    i = pl.program_id(0)

    @pl.when((i == 0) | (blk_e_ref[i] != blk_e_ref[jnp.maximum(i - 1, 0)]))
    def _():
        wg_b[...] = wg_ref[0, 0].astype(BF16)
        wu_b[...] = wu_ref[0, 0].astype(BF16)
        wd_b[...] = wd_ref[0, 0].astype(BF16)

    @pl.when(i < n_used_ref[0])
    def _():
        x = x_ref[...]
        gate = jnp.dot(x, wg_b[...], preferred_element_type=F32)
        up = jnp.dot(x, wu_b[...], preferred_element_type=F32)
        hid = (gate * jax.nn.sigmoid(gate) * up).astype(BF16)
        o_ref[...] = jnp.dot(hid, wd_b[...], preferred_element_type=F32).astype(o_ref.dtype)

    @pl.when(i >= n_used_ref[0])
    def _():
        o_ref[...] = jnp.zeros(o_ref.shape, o_ref.dtype)


def _experts(xs, blk_e, n_used, wg, wu, wd, layer):
    R, D = xs.shape
    n_blk = R // MOE_TB

    def live(i, be, nu):
        return jnp.minimum(i, nu[0] - 1)

    grid_spec = pltpu.PrefetchScalarGridSpec(
        num_scalar_prefetch=2,
        grid=(n_blk,),
        in_specs=[pl.BlockSpec((MOE_TB, D), lambda i, be, nu: (live(i, be, nu), 0)),
                  pl.BlockSpec((1, 1, D, D_EXPERT), lambda i, be, nu: (layer, be[i], 0, 0)),
                  pl.BlockSpec((1, 1, D, D_EXPERT), lambda i, be, nu: (layer, be[i], 0, 0)),
                  pl.BlockSpec((1, 1, D_EXPERT, D), lambda i, be, nu: (layer, be[i], 0, 0))],
        out_specs=pl.BlockSpec((MOE_TB, D), lambda i, be, nu: (i, 0)),
        scratch_shapes=[pltpu.VMEM((D, D_EXPERT), BF16), pltpu.VMEM((D, D_EXPERT), BF16),
                        pltpu.VMEM((D_EXPERT, D), BF16)],
    )
    return pl.pallas_call(
        _expert_body,
        grid_spec=grid_spec,
        out_shape=jax.ShapeDtypeStruct((R, D), BF16),
        compiler_params=_cparams("arbitrary"),
        name="moe_experts",
    )(blk_e, n_used, xs, wg, wu, wd)


def _combine_ln_body(x_ref, y0_ref, y1_ref, wt_ref, g_ref, b_ref, o_ref, ob_ref):
    ffn = y0_ref[...] * wt_ref[:, 0:1] + y1_ref[...] * wt_ref[:, HEAD_DIM:HEAD_DIM + 1]
    out = _layer_norm_rows(DEEPNORM_ALPHA * x_ref[...] + ffn, g_ref[...], b_ref[...])
    o_ref[...] = out
    ob_ref[...] = out.astype(BF16)


def _combine_ln(x, y0, y1, wt, g, b):
    M, D = x.shape
    row = pl.BlockSpec((LN_TM, D), lambda i: (i, 0))
    vec = pl.BlockSpec((1, D), lambda i: (0, 0))
    return pl.pallas_call(
        _combine_ln_body,
        grid=(M // LN_TM,),
        in_specs=[row, row, row, pl.BlockSpec((LN_TM, LANES), lambda i: (i, 0)), vec, vec],
        out_specs=[row, row],
        out_shape=[jax.ShapeDtypeStruct((M, D), F32), jax.ShapeDtypeStruct((M, D), BF16)],
        compiler_params=_cparams("parallel"),
        name="moe_combine_ln",
    )(x, y0, y1, wt, g.reshape(1, D), b.reshape(1, D))


def _moe_ln(h, hb, routing, wg, wu, wd, layer, g, b):
    N, D = h.shape
    A = 2 * N
    idx, wts, pos, counts = routing
    starts = jnp.cumsum(counts) - counts
    padded = (counts + MOE_TB - 1) // MOE_TB * MOE_TB
    pends = jnp.cumsum(padded)
    pstarts = pends - padded
    R = A + N_EXPERTS * MOE_TB
    n_blk = R // MOE_TB
    experts = jnp.arange(N_EXPERTS, dtype=I32)
    dest = pos + jnp.sum(jnp.where(idx[None] == experts[:, None, None], pstarts[:, None, None], 0), axis=0)
    tok = jnp.broadcast_to(jnp.arange(N, dtype=I32)[None, :], (2, N))
    _, tok_sorted = lax.sort_key_val(dest.reshape(A), tok.reshape(A))
    blk_row0 = jnp.arange(n_blk, dtype=I32) * MOE_TB
    blk_e = jnp.minimum(jnp.sum((pends[None, :] <= blk_row0[:, None]).astype(I32), axis=1), N_EXPERTS - 1)
    hot = blk_e[:, None] == experts[None, :]
    compact0 = blk_row0 + jnp.sum(jnp.where(hot, (starts - pstarts)[None, :], 0), axis=1)
    compact = jnp.remainder(compact0[:, None] + jnp.arange(MOE_TB, dtype=I32)[None, :], A).reshape(R)
    n_used = (pends[-1:] // MOE_TB).astype(I32)
    xs = hb[tok_sorted[compact]]
    yb = _experts(xs, blk_e, n_used, wg, wu, wd, layer)
    wt = jnp.concatenate([jnp.broadcast_to(wts[k][:, None], (N, HEAD_DIM)) for k in range(2)], axis=1)
    return _combine_ln(h, yb[dest[0]], yb[dest[1]], wt, g, b)


def _moba_layer(h, w_in, w_out, bias, g, b, router_w, router_b, B, S):
    HD = N_HEADS * HEAD_DIM
    qk = _matmul(h, w_in[:, :2 * HD].astype(BF16), _query_scale(HD, 2 * HD), BF16).reshape(B, S, 2 * HD)
    vt = _matmul_t(w_in[:, 2 * HD:].T.astype(BF16), h, B, S, BF16)
    att = _moba_attention(qk, vt, bias)
    return _proj_ln_route(att.reshape(B * S, HD), w_out.astype(BF16), h, g, b, router_w, router_b)


def _nsa_layer(h, hb, w_in, w_out, pos_k, pos_v, ck_w1, ck_w2, cv_w1, cv_w2, bias, g, b, router_w, router_b, B, S):
    HD = N_HEADS * HEAD_DIM
    kvw = NSA_KV_HEADS * HEAD_DIM
    col = lambda k: slice(HD + k * kvw, HD + (k + 1) * kvw)
    w_rows = jnp.concatenate([w_in[:, :HD + 2 * kvw], w_in[:, col(2)], w_in[:, col(4)]], axis=1)
    proj = _matmul(hb, w_rows.astype(BF16), _query_scale(HD, HD + 4 * kvw), BF16).reshape(B, S, HD + 4 * kvw)
    w_vt = jnp.concatenate([w_in[:, col(3)], w_in[:, col(5)]], axis=1).T
    vt = _matmul_t(w_vt.astype(BF16), hb, B, S, BF16)
    per_pair = 3 * N_HEADS // 2
    wg = w_in[:, HD + 6 * kvw:].reshape(D_MODEL, 2, per_pair)
    wg = jnp.pad(wg, ((0, 0), (0, 0), (0, LANES - per_pair))).reshape(D_MODEL, 2 * LANES).T
    gate_t = _matmul_t(wg.astype(BF16), hb, B, S, F32)

    def grouped(t):
        t = t.reshape(B, S, NSA_KV_HEADS, HEAD_DIM).transpose(0, 2, 1, 3)
        return t.reshape(B, NSA_KV_HEADS, S // CMP_STRIDE, CMP_STRIDE * HEAD_DIM)

    kcmp = _compress(grouped(proj[..., HD:HD + kvw]), pos_k, ck_w1, ck_w2)
    vcmp = _compress(grouped(proj[..., HD + kvw:HD + 2 * kvw]), pos_v, cv_w1, cv_w2)
    n_cmp = kcmp.shape[2]
    kcmp = kcmp.transpose(0, 2, 1, 3).reshape(B, n_cmp, kvw)
    vcmp_t = vcmp.transpose(0, 1, 3, 2).reshape(B, kvw, n_cmp)
    att = _nsa_attention(proj, vt, gate_t, kcmp, vcmp_t, bias)
    return _proj_ln_route(att.reshape(B * S, HD), w_out.astype(BF16), h, g, b, router_w, router_b)


def kernel(x, rel_bias, router_w, router_b, ln_g, ln_b, moba_w_in, moba_w_out, nsa_w_in, nsa_w_out,
           nsa_pos_k, nsa_pos_v, nsa_ck_w1, nsa_ck_w2, nsa_cv_w1, nsa_cv_w2,
           moe_w_gate, moe_w_up, moe_w_down):
    B, S, D = x.shape
    bias = _bias_tiles(rel_bias)
    h = x.reshape(B * S, D)
    h, hb, routing = _moba_layer(h, moba_w_in[0], moba_w_out[0], bias, ln_g[0, 0], ln_b[0, 0],
                                 router_w, router_b, B, S)
    h, hb = _moe_ln(h, hb, routing, moe_w_gate, moe_w_up, moe_w_down, 0, ln_g[0, 1], ln_b[0, 1])
    h, hb, routing = _nsa_layer(h, hb, nsa_w_in[0], nsa_w_out[0], nsa_pos_k[0], nsa_pos_v[0],
                                nsa_ck_w1[0], nsa_ck_w2[0], nsa_cv_w1[0], nsa_cv_w2[0],
                                bias, ln_g[1, 0], ln_b[1, 0], router_w, router_b, B, S)
    h, hb = _moe_ln(h, hb, routing, moe_w_gate, moe_w_up, moe_w_down, 1, ln_g[1, 1], ln_b[1, 1])
    return h.reshape(B, S, D)
```

```python
import math

import numpy as np
import jax
import jax.numpy as jnp
from jax import lax
from jax.experimental import pallas as pl
from jax.experimental.pallas import tpu as pltpu

F32, BF16, I32 = jnp.float32, jnp.bfloat16, jnp.int32

D_MODEL = 1024
N_HEADS = 16
HEAD_DIM = 64
DEPTH = 2
NEG_INF = -1e30
LN_EPS = 1e-5
MOBA_BLOCK = 256
MOBA_TOPK = 3
NSA_KV_HEADS = 4
NSA_GROUP = N_HEADS // NSA_KV_HEADS
CMP_LEN = 32
CMP_STRIDE = 16
CMP_HIDDEN = 256
SLC_BLOCK = 64
SLC_TOPN = 16
SLC_LOCAL = 2
WINDOW = 512
REL_BUCKETS = 32
REL_MAX_DIST = 128
N_EXPERTS = 32
N_GROUPS = 8
EXPERTS_PER_GROUP = N_EXPERTS // N_GROUPS
D_EXPERT = 512
DEEPNORM_ALPHA = (2 * DEPTH) ** 0.25
LOG2E = math.log2(math.e)
Q_SCALE = HEAD_DIM ** -0.5 * LOG2E

LANES = 128
SUBLANES = 8
TILE = 256
MM_TM = 1024
MM_TN = 1024
LN_TM = 1024
MOE_TB = 512
VMEM_LIMIT = 48 * 1024 * 1024

_NT = (((1,), (1,)), ((), ()))


def _cparams(*sem):
    return pltpu.CompilerParams(dimension_semantics=sem, vmem_limit_bytes=VMEM_LIMIT)


def _mm_body(a_ref, b_ref, c_ref, o_ref):
    acc = jnp.dot(a_ref[...].astype(BF16), b_ref[...], preferred_element_type=F32)
    o_ref[...] = (acc * c_ref[...]).astype(o_ref.dtype)


def _matmul(a, b, col_scale, out_dtype):
    M, K = a.shape
    N = b.shape[1]
    tn = min(MM_TN, N)
    return pl.pallas_call(
        _mm_body,
        grid=(M // MM_TM, N // tn),
        in_specs=[pl.BlockSpec((MM_TM, K), lambda i, j: (i, 0)),
                  pl.BlockSpec((K, tn), lambda i, j: (0, j)),
                  pl.BlockSpec((1, tn), lambda i, j: (0, j))],
        out_specs=pl.BlockSpec((MM_TM, tn), lambda i, j: (i, j)),
        out_shape=jax.ShapeDtypeStruct((M, N), out_dtype),
        compiler_params=_cparams("parallel", "arbitrary"),
        name="in_proj",
    )(a, b, col_scale.reshape(1, N))


def _query_scale(n_query_cols, n_cols):
    return jnp.where(jnp.arange(n_cols) < n_query_cols, Q_SCALE, 1.0).astype(F32)


def _mm_t_body(w_ref, a_ref, o_ref):
    r = lax.dot_general(w_ref[...], a_ref[...].astype(BF16), _NT, preferred_element_type=F32)
    for t in range(o_ref.shape[1]):
        o_ref[0, t] = r[:, t * TILE:(t + 1) * TILE].astype(o_ref.dtype)


def _matmul_t(w_t, a, B, S, out_dtype):
    Nout, K = w_t.shape
    tn = min(MM_TN, Nout)
    per_seq = S // MM_TM
    sub = MM_TM // TILE
    return pl.pallas_call(
        _mm_t_body,
        grid=(B * per_seq, Nout // tn),
        in_specs=[pl.BlockSpec((tn, K), lambda i, j: (j, 0)),
                  pl.BlockSpec((MM_TM, K), lambda i, j: (i, 0))],
        out_specs=pl.BlockSpec((1, sub, tn, TILE), lambda i, j: (i // per_seq, i % per_seq, j, 0)),
        out_shape=jax.ShapeDtypeStruct((B, S // TILE, Nout, TILE), out_dtype),
        compiler_params=_cparams("parallel", "arbitrary"),
        name="in_proj_t",
    )(w_t, a)


def _layer_norm_rows(z, g, b):
    mu = jnp.mean(z, axis=-1, keepdims=True)
    zc = z - mu
    var = jnp.mean(zc * zc, axis=-1, keepdims=True)
    return zc * lax.rsqrt(var + LN_EPS) * g + b


def _t5_bucket_np(rel):
    n = np.maximum(rel, 0)
    max_exact = REL_BUCKETS // 2
    nf = np.maximum(n, 1).astype(np.float32)
    large = max_exact + (np.log(nf / np.float32(max_exact))
                         / np.float32(math.log(REL_MAX_DIST / max_exact))
                         * np.float32(REL_BUCKETS - max_exact)).astype(np.int32)
    large = np.minimum(large, REL_BUCKETS - 1)
    return np.where(n < max_exact, n, large).astype(np.int32)


def _bias_body(tbl_ref, bk_ref, o_ref):
    h = pl.program_id(0)
    for dl in range(2):
        bk = bk_ref[dl]
        acc = jnp.zeros((TILE, TILE), F32)
        for b in range(REL_BUCKETS):
            acc = jnp.where(bk == b, tbl_ref[h * REL_BUCKETS + b], acc)
        o_ref[dl, 0] = acc * LOG2E


def _bias_tiles(rel_bias):
    key = np.arange(TILE)[:, None]
    qry = np.arange(TILE)[None, :]
    assert int(_t5_bucket_np(np.array(TILE + 1))) == REL_BUCKETS - 1
    bk = np.stack([_t5_bucket_np(qry - key), _t5_bucket_np(TILE + qry - key)])
    return pl.pallas_call(
        _bias_body,
        grid=(N_HEADS,),
        in_specs=[pl.BlockSpec(memory_space=pltpu.SMEM),
                  pl.BlockSpec((2, TILE, TILE), lambda h: (0, 0, 0))],
        out_specs=pl.BlockSpec((2, 1, TILE, TILE), lambda h: (0, h, 0, 0)),
        out_shape=jax.ShapeDtypeStruct((2, N_HEADS, TILE, TILE), F32),
        name="t5_bias_tiles",
    )(rel_bias.T.reshape(-1), jnp.asarray(bk))


def _heads_on_lanes(bias, per_block):
    two, H, T, _ = bias.shape
    b = bias.reshape(two, H // per_block, per_block, T, T).transpose(0, 1, 3, 2, 4)
    return b.reshape(two, H // per_block, T, per_block * T)


def _init_state(m_ref, l_ref, acc_ref):
    m_ref[...] = jnp.full(m_ref.shape, NEG_INF, F32)
    l_ref[...] = jnp.zeros(l_ref.shape, F32)
    acc_ref[...] = jnp.zeros(acc_ref.shape, F32)


def _rank_before(vals, rows):
    idx = lax.broadcasted_iota(I32, vals.shape, 0)
    rank = jnp.zeros(vals.shape, I32)
    for m in range(rows):
        row = vals[m:m + 1, :]
        beats = (row > vals) | ((row == vals) & (idx > m))
        rank = rank + jnp.where(beats, 1, 0)
    return rank


MOBA_STREAMS = 4


def _softmax_pv(scores, adds, vts, heads, m_ref, l_ref, acc_ref):
    def fold(x, op):
        return op(x.reshape(x.shape[0] // SUBLANES, SUBLANES, x.shape[1]), axis=0)

    m_prev = m_ref[...]
    m_part = None
    for s, add in zip(scores, adds):
        part = fold(s, jnp.max) + add
        m_part = part if m_part is None else jnp.maximum(m_part, part)
    m_new = jnp.maximum(m_prev, jnp.max(m_part, axis=0, keepdims=True))
    a = jnp.exp2(m_prev - m_new)
    probs = [jnp.exp2(s - (m_new - add)) for s, add in zip(scores, adds)]
    l_part = fold(probs[0], jnp.sum)
    for p in probs[1:]:
        l_part = l_part + fold(p, jnp.sum)
    l_ref[...] = a * l_ref[...] + jnp.sum(l_part, axis=0, keepdims=True)
    vt = jnp.concatenate(vts, axis=1)
    pb = jnp.concatenate([p.astype(BF16) for p in probs], axis=0)
    pv = jnp.concatenate([jnp.dot(vt[rows], pb[:, cols], preferred_element_type=F32) for rows, cols in heads],
                         axis=1)
    acc_ref[...] = a * acc_ref[...] + pv
    m_ref[...] = m_new


def _moba_body(q_ref, k_ref, vt_ref, bias_ref, o_ref, kmean_ref, radd_ref, sa_ref, sb_ref, m_ref, l_ref, acc_ref):
    i = pl.program_id(2)
    nb = k_ref.shape[1] // TILE
    streams = range(MOBA_STREAMS)
    lanes_of = lambda s: slice(s * LANES, (s + 1) * LANES)

    @pl.when(i == 0)
    def _():
        for s in streams:
            for n in range(nb):
                kb = k_ref[0, n * TILE:(n + 1) * TILE, lanes_of(s)].astype(F32)
                kmean_ref[s, n:n + 1, :] = jnp.sum(kb, axis=0, keepdims=True) * (1.0 / TILE)

    n_far = jnp.maximum(i - 1, 0)
    n_far_groups = (n_far + 1) >> 1
    last = nb - 1

    def key_tile(t, s):
        return k_ref[0, pl.ds(pl.multiple_of(t * TILE, TILE), TILE), lanes_of(s)]

    def far_tiles(j):
        return 2 * j, jnp.minimum(2 * j + 1, last)

    lane = lax.broadcasted_iota(I32, (TILE, LANES), 1)
    key = lax.broadcasted_iota(I32, (TILE, 2 * TILE), 0)
    qry = lax.broadcasted_iota(I32, (TILE, 2 * TILE), 1) & (TILE - 1)
    causal_neg = jnp.where(key <= qry, 0.0, NEG_INF)
    t_near = jnp.maximum(i - 1, 0)
    q2s = []
    for s in streams:
        q = q_ref[0, :, lanes_of(s)]
        zero = jnp.zeros_like(q)
        q2 = jnp.concatenate([jnp.where(lane < HEAD_DIM, q, zero),
                              jnp.where(lane >= HEAD_DIM, q, zero)], axis=0)
        q2s.append(q2)
        sa_ref[s, 0] = (lax.dot_general(key_tile(i, s), q2, _NT, preferred_element_type=F32)
                        + (bias_ref[0, s] + causal_neg))
        sa_ref[s, 1] = lax.dot_general(key_tile(t_near, s), q2, _NT, preferred_element_type=F32) + bias_ref[1, s]
        km = kmean_ref[s]
        k_hi = km.astype(BF16)
        k_lo = (km - k_hi.astype(F32)).astype(BF16)
        gate = (lax.dot_general(k_hi, q2, _NT, preferred_element_type=F32)
                + lax.dot_general(k_lo, q2, _NT, preferred_element_type=F32))
        blk = lax.broadcasted_iota(I32, gate.shape, 0)
        gate = jnp.where(blk < i, gate, -jnp.inf)
        rank = _rank_before(gate, nb)
        neg = jnp.where((rank < MOBA_TOPK) & (blk < i), 0.0, NEG_INF)
        far_bias = bias_ref[1, s, 0:1, :]
        near_row = jnp.full((1, 2 * TILE), NEG_INF, F32)
        radd_ref[s, 0:1, :] = jnp.zeros((1, 2 * TILE), F32)
        for n in range(nb):
            row = neg[n:n + 1, :]
            near_row = jnp.where(n == i - 1, row, near_row)
            radd_ref[s, 2 + n:3 + n, :] = jnp.where(n < n_far, far_bias + row, NEG_INF)
        radd_ref[s, 1:2, :] = near_row
        _init_state(m_ref.at[s], l_ref.at[s], acc_ref.at[s])

    pair_heads = [(slice(h * HEAD_DIM, (h + 1) * HEAD_DIM), slice(h * TILE, (h + 1) * TILE)) for h in range(2)]

    def update(j, s, buf, n_tiles=2):
        first = j == 0
        tiles = [jnp.where(first, i, 2 * j - 2), jnp.where(first, t_near, jnp.minimum(2 * j - 1, last))][:n_tiles]
        _softmax_pv([buf[s, t] for t in range(n_tiles)],
                    [radd_ref[s, pl.ds(2 * j + t, 1), :] for t in range(n_tiles)],
                    [vt_ref[0, kt, lanes_of(s), :] for kt in tiles], pair_heads,
                    m_ref.at[s], l_ref.at[s], acc_ref.at[s])

    def last_update(buf):
        two = jnp.where(n_far_groups == 0, i >= 1, (n_far & 1) == 0)
        for n_tiles, cond in ((2, two), (1, jnp.logical_not(two))):
            @pl.when(cond)
            def _():
                for s in streams:
                    update(n_far_groups, s, buf, n_tiles)

    def step(j, src, dst):
        for s in streams:
            ta, tb = far_tiles(j)
            dst[s, 0] = lax.dot_general(key_tile(ta, s), q2s[s], _NT, preferred_element_type=F32)
            dst[s, 1] = lax.dot_general(key_tile(tb, s), q2s[s], _NT, preferred_element_type=F32)
            update(j, s, src)

    def two_steps(jj, carry):
        step(2 * jj, sa_ref, sb_ref)
        step(2 * jj + 1, sb_ref, sa_ref)
        return carry

    lax.fori_loop(0, n_far_groups >> 1, two_steps, 0)

    @pl.when((n_far_groups & 1) == 1)
    def _():
        step(n_far_groups - 1, sa_ref, sb_ref)
        last_update(sb_ref)

    @pl.when((n_far_groups & 1) == 0)
    def _():
        last_update(sa_ref)

    for s in streams:
        o = acc_ref[s] / l_ref[s]
        o = jnp.concatenate([o[:, :TILE], o[:, TILE:]], axis=0)
        o_ref[0, :, lanes_of(s)] = o.T.astype(o_ref.dtype)


def _moba_attention(qk, vt, bias):
    B, S, _ = qk.shape
    n_steps = N_HEADS // 2 // MOBA_STREAMS
    nq = S // TILE
    w = MOBA_STREAMS * LANES
    return pl.pallas_call(
        _moba_body,
        grid=(B, n_steps, nq),
        in_specs=[pl.BlockSpec((1, TILE, w), lambda b, p, i: (b, i, p)),
                  pl.BlockSpec((1, S, w), lambda b, p, i: (b, 0, n_steps + p)),
                  pl.BlockSpec((1, nq, w, TILE), lambda b, p, i: (b, 0, p, 0)),
                  pl.BlockSpec((2, MOBA_STREAMS, TILE, 2 * TILE), lambda b, p, i: (0, p, 0, 0))],
        out_specs=pl.BlockSpec((1, TILE, w), lambda b, p, i: (b, i, p)),
        out_shape=jax.ShapeDtypeStruct((B, S, N_HEADS * HEAD_DIM), BF16),
        scratch_shapes=[pltpu.VMEM((MOBA_STREAMS, nq, LANES), F32),
                        pltpu.VMEM((MOBA_STREAMS, 2 + nq, 2 * TILE), F32),
                        pltpu.VMEM((MOBA_STREAMS, 2, TILE, 2 * TILE), F32),
                        pltpu.VMEM((MOBA_STREAMS, 2, TILE, 2 * TILE), F32),
                        pltpu.VMEM((MOBA_STREAMS, 1, 2 * TILE), F32),
                        pltpu.VMEM((MOBA_STREAMS, 1, 2 * TILE), F32),
                        pltpu.VMEM((MOBA_STREAMS, HEAD_DIM, 2 * TILE), F32)],
        compiler_params=_cparams("parallel", "parallel", "arbitrary"),
        name="moba_attention",
    )(qk, qk, vt, _heads_on_lanes(bias, 2))


def _gelu_tanh(x):
    return 0.5 * x * (1.0 + jnp.tanh(math.sqrt(2.0 / math.pi) * (x + 0.044715 * (x * x * x))))


def _compress_body(t_ref, pos_ref, w1_ref, w2_ref, o_ref):
    groups = t_ref.shape[2]
    half = t_ref.shape[3]
    t = t_ref[0].reshape(NSA_KV_HEADS * groups, half).astype(F32)
    first = jnp.dot((t + pos_ref[0:1, :]).astype(BF16), w1_ref[0:half, :], preferred_element_type=F32)
    second = jnp.dot((t + pos_ref[1:2, :]).astype(BF16), w1_ref[half:2 * half, :],
                     preferred_element_type=F32)
    rows = first.shape[0]
    pre = first + pltpu.roll(second, rows - 1, 0)
    out = jnp.dot(_gelu_tanh(pre).astype(BF16), w2_ref[...], preferred_element_type=F32)
    for h in range(NSA_KV_HEADS):
        o_ref[0, h] = out[h * groups:(h + 1) * groups].astype(o_ref.dtype)


def _compress(t, pos, w1, w2):
    B, Hkv, groups, half = t.shape
    return pl.pallas_call(
        _compress_body,
        grid=(B,),
        in_specs=[pl.BlockSpec((1, Hkv, groups, half), lambda b: (b, 0, 0, 0)),
                  pl.BlockSpec((2, half), lambda b: (0, 0)),
                  pl.BlockSpec((2 * half, CMP_HIDDEN), lambda b: (0, 0)),
                  pl.BlockSpec((CMP_HIDDEN, HEAD_DIM), lambda b: (0, 0))],
        out_specs=pl.BlockSpec((1, Hkv, groups, HEAD_DIM), lambda b: (b, 0, 0, 0)),
        out_shape=jax.ShapeDtypeStruct((B, Hkv, groups, HEAD_DIM), BF16),
        compiler_params=_cparams("parallel"),
        name="nsa_compress",
    )(t, pos.reshape(2, half), w1.astype(BF16), w2.astype(BF16))


def _swap_halves(x):
    return jnp.concatenate([x[:, HEAD_DIM:], x[:, :HEAD_DIM]], axis=1)


def _group_lanes(x):
    return jnp.concatenate([x] * NSA_GROUP, axis=1)


def _nsa_body(q_ref, kc_ref, vct_ref, ks_ref, vst_ref, kw_ref, vwt_ref, gt_ref, bias_ref, c2s_ref,
              o_ref, selneg_ref, radd_ref, sa_ref, sb_ref, sw_ref, oc_ref, os_ref, m_ref, l_ref, acc_ref):
    i = pl.program_id(2)
    nb = ks_ref.shape[1] // TILE
    n_cmp = kc_ref.shape[1]
    n_slc = c2s_ref.shape[0]
    per_tile = TILE // SLC_BLOCK
    cols = NSA_GROUP * TILE
    kv_heads = range(2)
    n_far = jnp.maximum(i - 1, 0)
    n_far_groups = (n_far + 1) >> 1
    last = nb - 1
    t_near = jnp.maximum(i - 1, 0)
    t_edge = jnp.maximum(i - 2, 0)

    dims_of = lambda a: slice(a * HEAD_DIM, (a + 1) * HEAD_DIM)

    def key_tile(k_ref, t):
        return k_ref[0, pl.ds(pl.multiple_of(t * TILE, TILE), TILE), :]

    def scores_of(k_ref, t, a):
        return lax.dot_general(key_tile(k_ref, t), q4s[a], _NT, preferred_element_type=F32)

    lane = lax.broadcasted_iota(I32, (TILE, LANES), 1)
    lo_half = lane < HEAD_DIM
    qpos = i * TILE + (lax.broadcasted_iota(I32, (n_cmp, cols), 1) & (TILE - 1))
    cmp_valid = CMP_STRIDE * lax.broadcasted_iota(I32, (n_cmp, cols), 0) + (CMP_LEN - 1) <= qpos
    key = lax.broadcasted_iota(I32, (TILE, TILE), 0)
    qry = lax.broadcasted_iota(I32, (TILE, TILE), 1)
    diag_neg = _group_lanes(jnp.where(key <= qry, 0.0, NEG_INF))
    edge_neg = _group_lanes(jnp.where(key > qry, 0.0, NEG_INF))
    qall = q_ref[0]
    q4s = []

    for a in kv_heads:
        keep = lo_half if a == 0 else jnp.logical_not(lo_half)
        heads = []
        for g in range(NSA_GROUP):
            cb = a * 2 + g // 2
            x = qall[:, cb * LANES:(cb + 1) * LANES]
            if g % 2 != a:
                x = _swap_halves(x)
            heads.append(jnp.where(keep, x, jnp.zeros_like(x)))
        q4s.append(jnp.concatenate(heads, axis=0))
        far_bias = bias_ref[1, a, 0:1, :]

        sa_ref[a, 0] = scores_of(ks_ref, i, a) + (bias_ref[0, a] + diag_neg)
        sa_ref[a, 1] = scores_of(ks_ref, t_near, a) + bias_ref[1, a]
        sw_ref[a, 0] = scores_of(kw_ref, i, a) + (bias_ref[0, a] + diag_neg)
        sw_ref[a, 1] = scores_of(kw_ref, t_near, a) + bias_ref[1, a]
        sw_ref[a, 2] = scores_of(kw_ref, t_edge, a) + (far_bias + edge_neg)

        s_c = lax.dot_general(kc_ref[0], q4s[a], _NT, preferred_element_type=F32)
        s_c = jnp.where(cmp_valid, s_c, NEG_INF)
        m_c = jnp.max(s_c, axis=0, keepdims=True)
        e_c = jnp.where(cmp_valid, jnp.exp2(s_c - m_c), 0.0)
        l_c = jnp.sum(e_c, axis=0, keepdims=True)
        p_c = e_c / jnp.where(l_c > 0.0, l_c, 1.0)
        oc_ref[a] = jnp.dot(vct_ref[0, dims_of(a), :], p_c.astype(BF16), preferred_element_type=F32)

        p_sum = p_c[:, 0:TILE]
        for g in range(1, NSA_GROUP):
            p_sum = p_sum + p_c[:, g * TILE:(g + 1) * TILE]
        p_hi = p_sum.astype(BF16)
        p_lo = (p_sum - p_hi.astype(F32)).astype(BF16)
        imp = (jnp.dot(c2s_ref[...], p_hi, preferred_element_type=F32)
               + jnp.dot(c2s_ref[...], p_lo, preferred_element_type=F32))
        j = lax.broadcasted_iota(I32, imp.shape, 0)
        qb = (i * TILE + lax.broadcasted_iota(I32, imp.shape, 1)) >> int(math.log2(SLC_BLOCK))
        forced = (j == 0) | ((j <= qb) & (j > qb - SLC_LOCAL))
        imp = jnp.where(forced, jnp.inf, jnp.where(j > qb, -jnp.inf, imp))
        rank = _rank_before(imp, n_slc)
        selneg = jnp.where((rank < SLC_TOPN) & (j <= qb), 0.0, NEG_INF)
        selneg_ref[a] = selneg
        for c in range(per_tile):
            radd_ref[a, c:c + 1, :] = _group_lanes(selneg_ref[a, pl.ds(per_tile * i + c, 1), :])
            near_row = _group_lanes(selneg_ref[a, pl.ds(per_tile * t_near + c, 1), :])
            radd_ref[a, per_tile + c:per_tile + c + 1, :] = jnp.where(i >= 1, near_row, NEG_INF)
        for blk in range(n_slc):
            row = far_bias + _group_lanes(selneg[blk:blk + 1, :])
            r = 2 * per_tile + blk
            radd_ref[a, r:r + 1, :] = jnp.where(blk // per_tile < n_far, row, NEG_INF)
        _init_state(m_ref.at[a], l_ref.at[a], acc_ref.at[a])

    def update(jg, a, buf, n_tiles=2):
        first = jg == 0
        tiles = [jnp.where(first, i, 2 * jg - 2), jnp.where(first, t_near, jnp.minimum(2 * jg - 1, last))][:n_tiles]
        scores, adds = [], []
        for t in range(n_tiles):
            for c in range(per_tile):
                scores.append(buf[a, t, c * SLC_BLOCK:(c + 1) * SLC_BLOCK, :])
                adds.append(radd_ref[a, pl.ds(2 * per_tile * jg + per_tile * t + c, 1), :])
        _softmax_pv(scores, adds, [vst_ref[0, kt] for kt in tiles], [(dims_of(a), slice(None))],
                    m_ref.at[a], l_ref.at[a], acc_ref.at[a])

    def last_update(buf):
        two = jnp.where(n_far_groups == 0, i >= 1, (n_far & 1) == 0)
        for n_tiles, cond in ((2, two), (1, jnp.logical_not(two))):
            @pl.when(cond)
            def _():
                for a in kv_heads:
                    update(n_far_groups, a, buf, n_tiles)

    def step(jg, src, dst):
        for a in kv_heads:
            dst[a, 0] = scores_of(ks_ref, 2 * jg, a)
            dst[a, 1] = scores_of(ks_ref, jnp.minimum(2 * jg + 1, last), a)
            update(jg, a, src)

    def two_steps(jj, carry):
        step(2 * jj, sa_ref, sb_ref)
        step(2 * jj + 1, sb_ref, sa_ref)
        return carry

    lax.fori_loop(0, n_far_groups >> 1, two_steps, 0)

    @pl.when((n_far_groups & 1) == 1)
    def _():
        step(n_far_groups - 1, sa_ref, sb_ref)
        last_update(sb_ref)

    @pl.when((n_far_groups & 1) == 0)
    def _():
        last_update(sa_ref)

    gates = jax.nn.sigmoid(gt_ref[0, 0])
    zero_row = jnp.zeros((1, cols), F32)
    pieces = []
    for a in kv_heads:
        os_ref[a] = acc_ref[a] / l_ref[a]
        _init_state(m_ref.at[a], l_ref.at[a], acc_ref.at[a])
        _softmax_pv([sw_ref[a, 0], sw_ref[a, 1], sw_ref[a, 2]],
                    [zero_row, zero_row + jnp.where(i >= 1, 0.0, NEG_INF), zero_row + jnp.where(i >= 2, 0.0, NEG_INF)],
                    [vwt_ref[0, i], vwt_ref[0, t_near], vwt_ref[0, t_edge]], [(dims_of(a), slice(None))],
                    m_ref.at[a], l_ref.at[a], acc_ref.at[a])
        o_w = acc_ref[a] / l_ref[a]
        for g in range(NSA_GROUP):
            c0 = 3 * (NSA_GROUP * a + g)
            ls = slice(g * TILE, (g + 1) * TILE)
            pieces.append(gates[c0:c0 + 1, :] * oc_ref[a, :, ls] + gates[c0 + 1:c0 + 2, :] * os_ref[a, :, ls]
                          + gates[c0 + 2:c0 + 3, :] * o_w[:, ls])
    o_ref[0] = jnp.concatenate(pieces, axis=0).T.astype(o_ref.dtype)


def _cmp_to_slc(S):
    n_cmp_pad = S // CMP_STRIDE
    n_slc = S // SLC_BLOCK
    ci = np.arange(n_cmp_pad)[:, None] * CMP_STRIDE
    sj = np.arange(n_slc)[None, :] * SLC_BLOCK
    c2s = ((ci < sj + SLC_BLOCK) & (ci + CMP_LEN > sj)).astype(np.float32)
    c2s[(S - CMP_LEN) // CMP_STRIDE + 1:] = 0.0
    return jnp.asarray(c2s.T, BF16)


def _nsa_attention(proj, vt, gate_t, kcmp, vcmp_t, bias):
    B, S, _ = proj.shape
    nq = S // TILE
    n_cmp = kcmp.shape[1]
    n_slc = S // SLC_BLOCK
    qw = 2 * NSA_GROUP * HEAD_DIM
    q_blocks = N_HEADS * HEAD_DIM // LANES
    kv_blocks = NSA_KV_HEADS * HEAD_DIM // LANES

    def k_spec(which):
        base = q_blocks + which * kv_blocks
        return pl.BlockSpec((1, S, LANES), lambda b, p, i: (b, 0, base + p))

    def vt_spec(which):
        base = which * kv_blocks
        return pl.BlockSpec((1, nq, LANES, TILE), lambda b, p, i: (b, 0, base + p, 0))

    state = pltpu.VMEM((2, HEAD_DIM, NSA_GROUP * TILE), F32)
    stat = pltpu.VMEM((2, 1, NSA_GROUP * TILE), F32)
    return pl.pallas_call(
        _nsa_body,
        grid=(B, 2, nq),
        in_specs=[pl.BlockSpec((1, TILE, qw), lambda b, p, i: (b, i, p)),
                  pl.BlockSpec((1, n_cmp, LANES), lambda b, p, i: (b, 0, p)),
                  pl.BlockSpec((1, LANES, n_cmp), lambda b, p, i: (b, p, 0)),
                  k_spec(2), vt_spec(0), k_spec(3), vt_spec(1),
                  pl.BlockSpec((1, 1, LANES, TILE), lambda b, p, i: (b, i, p, 0)),
                  pl.BlockSpec((2, 2, TILE, NSA_GROUP * TILE), lambda b, p, i: (0, p, 0, 0)),
                  pl.BlockSpec((n_slc, n_cmp), lambda b, p, i: (0, 0))],
        out_specs=pl.BlockSpec((1, TILE, qw), lambda b, p, i: (b, i, p)),
        out_shape=jax.ShapeDtypeStruct((B, S, N_HEADS * HEAD_DIM), BF16),
        scratch_shapes=[pltpu.VMEM((2, n_slc, TILE), F32),
                        pltpu.VMEM((2, 2 * (TILE // SLC_BLOCK) + n_slc, NSA_GROUP * TILE), F32),
                        pltpu.VMEM((2, 2, TILE, NSA_GROUP * TILE), F32),
                        pltpu.VMEM((2, 2, TILE, NSA_GROUP * TILE), F32),
                        pltpu.VMEM((2, 3, TILE, NSA_GROUP * TILE), F32),
                        state, state, stat, stat, state],
        compiler_params=_cparams("parallel", "parallel", "arbitrary"),
        name="nsa_attention",
    )(proj, kcmp, vcmp_t, proj, vt, proj, vt, gate_t, _heads_on_lanes(bias, NSA_GROUP), _cmp_to_slc(S))


def _split_bf16(x):
    hi = x.astype(BF16)
    return hi, (x - hi.astype(F32)).astype(BF16)


_ROW_OF_EXPERT = np.arange(N_EXPERTS).reshape(N_GROUPS, EXPERTS_PER_GROUP).T.reshape(-1)


def _route(x, w_ref, b_ref, tri_ref, idx_ref, wt_ref, pos_ref, cnt_ref, base_ref):
    @pl.when(pl.program_id(0) == 0)
    def _():
        base_ref[...] = jnp.zeros(base_ref.shape, F32)

    x_hi, x_lo = _split_bf16(x)
    w_hi, w_lo = _split_bf16(w_ref[...])
    logits = (lax.dot_general(w_hi, x_hi, _NT, preferred_element_type=F32)
              + lax.dot_general(w_hi, x_lo, _NT, preferred_element_type=F32)
              + lax.dot_general(w_lo, x_hi, _NT, preferred_element_type=F32)) + b_ref[:, 0:1]
    m = jnp.max(logits, axis=0, keepdims=True)
    e = jnp.exp(logits - m)
    probs = e / jnp.sum(e, axis=0, keepdims=True)
    pk = [probs[k * N_GROUPS:(k + 1) * N_GROUPS] for k in range(EXPERTS_PER_GROUP)]
    hi1, lo1 = jnp.maximum(pk[0], pk[1]), jnp.minimum(pk[0], pk[1])
    hi2, lo2 = jnp.maximum(pk[2], pk[3]), jnp.minimum(pk[2], pk[3])
    score = jnp.maximum(hi1, hi2) + jnp.maximum(jnp.minimum(hi1, hi2), jnp.maximum(lo1, lo2))
    grp = lax.broadcasted_iota(I32, score.shape, 0)
    best = jnp.min(jnp.where(score == jnp.max(score, axis=0, keepdims=True), grp, N_GROUPS),
                   axis=0, keepdims=True)
    v = [jnp.sum(jnp.where(grp == best, p, 0.0), axis=0, keepdims=True) for p in pk]
    v1 = jnp.maximum(jnp.maximum(v[0], v[1]), jnp.maximum(v[2], v[3]))
    i1 = jnp.where(v[0] == v1, 0, jnp.where(v[1] == v1, 1, jnp.where(v[2] == v1, 2, 3)))
    rest = [jnp.where(i1 == k, -1.0, v[k]) for k in range(EXPERTS_PER_GROUP)]
    v2 = jnp.maximum(jnp.maximum(rest[0], rest[1]), jnp.maximum(rest[2], rest[3]))
    i2 = jnp.where(rest[0] == v2, 0, jnp.where(rest[1] == v2, 1, jnp.where(rest[2] == v2, 2, 3)))
    tot = v1 + v2
    idx_ref[...] = jnp.concatenate([best * EXPERTS_PER_GROUP + i1, best * EXPERTS_PER_GROUP + i2], axis=0)
    wt_ref[...] = jnp.concatenate([v1 / tot, v2 / tot], axis=0)

    row = lax.broadcasted_iota(I32, logits.shape, 0)
    hot = [jnp.where(row == ik * N_GROUPS + best, 1.0, 0.0) for ik in (i1, i2)]
    both = (hot[0] + hot[1]).astype(BF16)
    run = base_ref[:, 0:1]
    pos = [[], []]
    for c in range(logits.shape[1] // LANES):
        ls = slice(c * LANES, (c + 1) * LANES)
        before = run + jnp.dot(both[:, ls], tri_ref[...], preferred_element_type=F32) - 1.0
        for k in range(2):
            pos[k].append(jnp.sum(hot[k][:, ls] * before, axis=0, keepdims=True))
        run = before[:, LANES - 1:LANES] + 1.0
    pos_ref[...] = jnp.concatenate([jnp.concatenate(pos[0], axis=1), jnp.concatenate(pos[1], axis=1)],
                                   axis=0).astype(I32)
    base_ref[...] = jnp.broadcast_to(run, base_ref.shape)
    cnt_ref[...] = jnp.broadcast_to(run, cnt_ref.shape)


def _router_operands(router_w, router_b):
    w = router_w.T[_ROW_OF_EXPERT]
    b = jnp.broadcast_to(router_b[_ROW_OF_EXPERT][:, None], (N_EXPERTS, LANES))
    tri = jnp.asarray(np.triu(np.ones((LANES, LANES), np.float32)), BF16)
    return w, b, tri


def _proj_ln_route_body(a_ref, w_ref, x_ref, g_ref, b_ref, rw_ref, rb_ref, tri_ref,
                        o_ref, ob_ref, idx_ref, wt_ref, pos_ref, cnt_ref, base_ref):
    y = jnp.dot(a_ref[...], w_ref[...], preferred_element_type=F32)
    out = _layer_norm_rows(DEEPNORM_ALPHA * x_ref[...] + y, g_ref[...], b_ref[...])
    o_ref[...] = out
    ob_ref[...] = out.astype(BF16)
    _route(out, rw_ref, rb_ref, tri_ref, idx_ref, wt_ref, pos_ref, cnt_ref, base_ref)


def _proj_ln_route(a, w, x, g, b, router_w, router_b):
    M, K = a.shape
    D = w.shape[1]
    row = pl.BlockSpec((LN_TM, D), lambda i: (i, 0))
    vec = pl.BlockSpec((1, D), lambda i: (0, 0))
    whole = lambda shape: pl.BlockSpec(shape, lambda i: (0, 0))
    tok = lambda dt: jax.ShapeDtypeStruct((2, M), dt)
    tok_spec = pl.BlockSpec((2, LN_TM), lambda i: (0, i))
    h, hb, idx, wts, pos, cnt = pl.pallas_call(
        _proj_ln_route_body,
        grid=(M // LN_TM,),
        in_specs=[pl.BlockSpec((LN_TM, K), lambda i: (i, 0)), whole((K, D)), row, vec, vec,
                  whole((N_EXPERTS, D)), whole((N_EXPERTS, LANES)), whole((LANES, LANES))],
        out_specs=[row, row, tok_spec, tok_spec, tok_spec, whole((N_EXPERTS, LANES))],
        out_shape=[jax.ShapeDtypeStruct((M, D), F32), jax.ShapeDtypeStruct((M, D), BF16),
                   tok(I32), tok(F32), tok(I32), jax.ShapeDtypeStruct((N_EXPERTS, LANES), F32)],
        scratch_shapes=[pltpu.VMEM((N_EXPERTS, LANES), F32)],
        compiler_params=_cparams("arbitrary"),
        name="out_proj_ln_route",
    )(a, w, x, g.reshape(1, D), b.reshape(1, D), *_router_operands(router_w, router_b))
    counts = cnt[np.argsort(_ROW_OF_EXPERT), 0].astype(I32)
    return h, hb, (idx, wts, pos, counts)


def _expert_body(blk_e_ref, n_used_ref, x_ref, wg_ref, wu_ref, wd_ref, o_ref, wg_b, wu_b, wd_b):
    i = pl.program_id(0)

    @pl.when((i == 0) | (blk_e_ref[i] != blk_e_ref[jnp.maximum(i - 1, 0)]))
    def _():
        wg_b[...] = wg_ref[0, 0].astype(BF16)
        wu_b[...] = wu_ref[0, 0].astype(BF16)
        wd_b[...] = wd_ref[0, 0].astype(BF16)

    @pl.when(i < n_used_ref[0])
    def _():
        x = x_ref[...]
        gate = jnp.dot(x, wg_b[...], preferred_element_type=F32)
        up = jnp.dot(x, wu_b[...], preferred_element_type=F32)
        hid = (gate * jax.nn.sigmoid(gate) * up).astype(BF16)
        o_ref[...] = jnp.dot(hid, wd_b[...], preferred_element_type=F32).astype(o_ref.dtype)

    @pl.when(i >= n_used_ref[0])
    def _():
        o_ref[...] = jnp.zeros(o_ref.shape, o_ref.dtype)


def _experts(xs, blk_e, n_used, wg, wu, wd, layer):
    R, D = xs.shape
    n_blk = R // MOE_TB

    def live(i, be, nu):
        return jnp.minimum(i, nu[0] - 1)

    grid_spec = pltpu.PrefetchScalarGridSpec(
        num_scalar_prefetch=2,
        grid=(n_blk,),
        in_specs=[pl.BlockSpec((MOE_TB, D), lambda i, be, nu: (live(i, be, nu), 0)),
                  pl.BlockSpec((1, 1, D, D_EXPERT), lambda i, be, nu: (layer, be[i], 0, 0)),
                  pl.BlockSpec((1, 1, D, D_EXPERT), lambda i, be, nu: (layer, be[i], 0, 0)),
                  pl.BlockSpec((1, 1, D_EXPERT, D), lambda i, be, nu: (layer, be[i], 0, 0))],
        out_specs=pl.BlockSpec((MOE_TB, D), lambda i, be, nu: (i, 0)),
        scratch_shapes=[pltpu.VMEM((D, D_EXPERT), BF16), pltpu.VMEM((D, D_EXPERT), BF16),
                        pltpu.VMEM((D_EXPERT, D), BF16)],
    )
    return pl.pallas_call(
        _expert_body,
        grid_spec=grid_spec,
        out_shape=jax.ShapeDtypeStruct((R, D), BF16),
        compiler_params=_cparams("arbitrary"),
        name="moe_experts",
    )(blk_e, n_used, xs, wg, wu, wd)


def _combine_ln_body(x_ref, y0_ref, y1_ref, wt_ref, g_ref, b_ref, o_ref, ob_ref):
    ffn = y0_ref[...] * wt_ref[:, 0:1] + y1_ref[...] * wt_ref[:, HEAD_DIM:HEAD_DIM + 1]
    out = _layer_norm_rows(DEEPNORM_ALPHA * x_ref[...] + ffn, g_ref[...], b_ref[...])
    o_ref[...] = out
    ob_ref[...] = out.astype(BF16)


def _combine_ln(x, y0, y1, wt, g, b):
    M, D = x.shape
    row = pl.BlockSpec((LN_TM, D), lambda i: (i, 0))
    vec = pl.BlockSpec((1, D), lambda i: (0, 0))
    return pl.pallas_call(
        _combine_ln_body,
        grid=(M // LN_TM,),
        in_specs=[row, row, row, pl.BlockSpec((LN_TM, LANES), lambda i: (i, 0)), vec, vec],
        out_specs=[row, row],
        out_shape=[jax.ShapeDtypeStruct((M, D), F32), jax.ShapeDtypeStruct((M, D), BF16)],
        compiler_params=_cparams("parallel"),
        name="moe_combine_ln",
    )(x, y0, y1, wt, g.reshape(1, D), b.reshape(1, D))


def _moe_ln(h, hb, routing, wg, wu, wd, layer, g, b):
    N, D = h.shape
    A = 2 * N
    idx, wts, pos, counts = routing
    starts = jnp.cumsum(counts) - counts
    padded = (counts + MOE_TB - 1) // MOE_TB * MOE_TB
    pends = jnp.cumsum(padded)
    pstarts = pends - padded
    R = A + N_EXPERTS * MOE_TB
    n_blk = R // MOE_TB
    experts = jnp.arange(N_EXPERTS, dtype=I32)
    dest = pos + jnp.sum(jnp.where(idx[None] == experts[:, None, None], pstarts[:, None, None], 0), axis=0)
    tok = jnp.broadcast_to(jnp.arange(N, dtype=I32)[None, :], (2, N))
    _, tok_sorted = lax.sort_key_val(dest.reshape(A), tok.reshape(A))
    blk_row0 = jnp.arange(n_blk, dtype=I32) * MOE_TB
    blk_e = jnp.minimum(jnp.sum((pends[None, :] <= blk_row0[:, None]).astype(I32), axis=1), N_EXPERTS - 1)
    hot = blk_e[:, None] == experts[None, :]
    compact0 = blk_row0 + jnp.sum(jnp.where(hot, (starts - pstarts)[None, :], 0), axis=1)
    compact = jnp.remainder(compact0[:, None] + jnp.arange(MOE_TB, dtype=I32)[None, :], A).reshape(R)
    n_used = (pends[-1:] // MOE_TB).astype(I32)
    xs = hb[tok_sorted[compact]]
    yb = _experts(xs, blk_e, n_used, wg, wu, wd, layer)
    wt = jnp.concatenate([jnp.broadcast_to(wts[k][:, None], (N, HEAD_DIM)) for k in range(2)], axis=1)
    return _combine_ln(h, yb[dest[0]], yb[dest[1]], wt, g, b)


def _moba_layer(h, w_in, w_out, bias, g, b, router_w, router_b, B, S):
    HD = N_HEADS * HEAD_DIM
    qk = _matmul(h, w_in[:, :2 * HD].astype(BF16), _query_scale(HD, 2 * HD), BF16).reshape(B, S, 2 * HD)
    vt = _matmul_t(w_in[:, 2 * HD:].T.astype(BF16), h, B, S, BF16)
    att = _moba_attention(qk, vt, bias)
    return _proj_ln_route(att.reshape(B * S, HD), w_out.astype(BF16), h, g, b, router_w, router_b)


def _nsa_layer(h, hb, w_in, w_out, pos_k, pos_v, ck_w1, ck_w2, cv_w1, cv_w2, bias, g, b, router_w, router_b, B, S):
    HD = N_HEADS * HEAD_DIM
    kvw = NSA_KV_HEADS * HEAD_DIM
    col = lambda k: slice(HD + k * kvw, HD + (k + 1) * kvw)
    w_rows = jnp.concatenate([w_in[:, :HD + 2 * kvw], w_in[:, col(2)], w_in[:, col(4)]], axis=1)
    proj = _matmul(hb, w_rows.astype(BF16), _query_scale(HD, HD + 4 * kvw), BF16).reshape(B, S, HD + 4 * kvw)
    w_vt = jnp.concatenate([w_in[:, col(3)], w_in[:, col(5)]], axis=1).T
    vt = _matmul_t(w_vt.astype(BF16), hb, B, S, BF16)
    per_pair = 3 * N_HEADS // 2
    wg = w_in[:, HD + 6 * kvw:].reshape(D_MODEL, 2, per_pair)
    wg = jnp.pad(wg, ((0, 0), (0, 0), (0, LANES - per_pair))).reshape(D_MODEL, 2 * LANES).T
    gate_t = _matmul_t(wg.astype(BF16), hb, B, S, F32)

    def grouped(t):
        t = t.reshape(B, S, NSA_KV_HEADS, HEAD_DIM).transpose(0, 2, 1, 3)
        return t.reshape(B, NSA_KV_HEADS, S // CMP_STRIDE, CMP_STRIDE * HEAD_DIM)

    kcmp = _compress(grouped(proj[..., HD:HD + kvw]), pos_k, ck_w1, ck_w2)
    vcmp = _compress(grouped(proj[..., HD + kvw:HD + 2 * kvw]), pos_v, cv_w1, cv_w2)
    n_cmp = kcmp.shape[2]
    kcmp = kcmp.transpose(0, 2, 1, 3).reshape(B, n_cmp, kvw)
    vcmp_t = vcmp.transpose(0, 1, 3, 2).reshape(B, kvw, n_cmp)
    att = _nsa_attention(proj, vt, gate_t, kcmp, vcmp_t, bias)
    return _proj_ln_route(att.reshape(B * S, HD), w_out.astype(BF16), h, g, b, router_w, router_b)


def kernel(x, rel_bias, router_w, router_b, ln_g, ln_b, moba_w_in, moba_w_out, nsa_w_in, nsa_w_out,
           nsa_pos_k, nsa_pos_v, nsa_ck_w1, nsa_ck_w2, nsa_cv_w1, nsa_cv_w2,
           moe_w_gate, moe_w_up, moe_w_down):
    B, S, D = x.shape
    bias = _bias_tiles(rel_bias)
    h = x.reshape(B * S, D)
    h, hb, routing = _moba_layer(h, moba_w_in[0], moba_w_out[0], bias, ln_g[0, 0], ln_b[0, 0],
                                 router_w, router_b, B, S)
    h, hb = _moe_ln(h, hb, routing, moe_w_gate, moe_w_up, moe_w_down, 0, ln_g[0, 1], ln_b[0, 1])
    h, hb, routing = _nsa_layer(h, hb, nsa_w_in[0], nsa_w_out[0], nsa_pos_k[0], nsa_pos_v[0],
                                nsa_ck_w1[0], nsa_ck_w2[0], nsa_cv_w1[0], nsa_cv_w2[0],
                                bias, ln_g[1, 0], ln_b[1, 0], router_w, router_b, B, S)
    h, hb = _moe_ln(h, hb, routing, moe_w_gate, moe_w_up, moe_w_down, 1, ln_g[1, 1], ln_b[1, 1])
    return h.reshape(B, S, D)
```

```python
import math
from functools import partial

import numpy as np
import jax
import jax.numpy as jnp
from jax import lax
from jax.experimental import pallas as pl
from jax.experimental.pallas import tpu as pltpu

F32, BF16, I32 = jnp.float32, jnp.bfloat16, jnp.int32

D_MODEL = 1024
N_HEADS = 16
HEAD_DIM = 64
DEPTH = 2
NEG_INF = -1e30
LN_EPS = 1e-5
MOBA_BLOCK = 256
MOBA_TOPK = 3
NSA_KV_HEADS = 4
NSA_GROUP = N_HEADS // NSA_KV_HEADS
CMP_LEN = 32
CMP_STRIDE = 16
CMP_HIDDEN = 256
SLC_BLOCK = 64
SLC_TOPN = 16
SLC_LOCAL = 2
WINDOW = 512
REL_BUCKETS = 32
REL_MAX_DIST = 128
N_EXPERTS = 32
N_GROUPS = 8
EXPERTS_PER_GROUP = N_EXPERTS // N_GROUPS
D_EXPERT = 512
DEEPNORM_ALPHA = (2 * DEPTH) ** 0.25
LOG2E = math.log2(math.e)
Q_SCALE = HEAD_DIM ** -0.5 * LOG2E

LANES = 128
SUBLANES = 8
ONES_ROWS = 16
TILE = 256
MM_TM = 512
MM_TN = 1024
LN_TM = 1024
MOE_TB = 512
VMEM_LIMIT = 48 * 1024 * 1024

_NT = (((1,), (1,)), ((), ()))


def _cparams(*sem):
    return pltpu.CompilerParams(dimension_semantics=sem, vmem_limit_bytes=VMEM_LIMIT)


def _in_proj_body(n_t, a_ref, w_ref, c_ref, *refs):
    wt_refs, o_ref, ot_refs = refs[:n_t], refs[n_t], refs[n_t + 1:]
    a = a_ref[...].astype(BF16)
    for c in range(o_ref.shape[1] // MM_TN):
        cols = slice(c * MM_TN, (c + 1) * MM_TN)
        acc = jnp.dot(a, w_ref[:, cols], preferred_element_type=F32)
        o_ref[:, cols] = (acc * c_ref[:, cols]).astype(o_ref.dtype)
    for wt_ref, ot_ref in zip(wt_refs, ot_refs):
        r = lax.dot_general(wt_ref[...], a, _NT, preferred_element_type=F32)
        for t in range(ot_ref.shape[1]):
            ot_ref[0, t] = r[:, t * TILE:(t + 1) * TILE].astype(ot_ref.dtype)


def _in_proj(a, w, col_scale, w_ts, t_dtypes, B, S):
    M, K = a.shape
    N = w.shape[1]
    per_seq = S // MM_TM
    sub = MM_TM // TILE
    whole = lambda shape: pl.BlockSpec(shape, lambda i: (0, 0))
    outs = pl.pallas_call(
        partial(_in_proj_body, len(w_ts)),
        grid=(M // MM_TM,),
        in_specs=[pl.BlockSpec((MM_TM, K), lambda i: (i, 0)), whole((K, N)), whole((1, N))]
                 + [whole(w_t.shape) for w_t in w_ts],
        out_specs=[pl.BlockSpec((MM_TM, N), lambda i: (i, 0))]
                  + [pl.BlockSpec((1, sub, w_t.shape[0], TILE), lambda i: (i // per_seq, i % per_seq, 0, 0))
                     for w_t in w_ts],
        out_shape=[jax.ShapeDtypeStruct((M, N), BF16)]
                  + [jax.ShapeDtypeStruct((B, S // TILE, w_t.shape[0], TILE), dt) for w_t, dt in zip(w_ts, t_dtypes)],
        compiler_params=_cparams("parallel"),
        name="in_proj",
    )(a, w, col_scale.reshape(1, N), *w_ts)
    return outs


def _query_scale(n_query_cols, n_cols):
    return jnp.where(jnp.arange(n_cols) < n_query_cols, Q_SCALE, 1.0).astype(F32)


def _layer_norm_rows(z, g, b):
    mu = jnp.mean(z, axis=-1, keepdims=True)
    zc = z - mu
    var = jnp.mean(zc * zc, axis=-1, keepdims=True)
    return zc * lax.rsqrt(var + LN_EPS) * g + b


def _t5_bucket_np(rel):
    n = np.maximum(rel, 0)
    max_exact = REL_BUCKETS // 2
    nf = np.maximum(n, 1).astype(np.float32)
    large = max_exact + (np.log(nf / np.float32(max_exact))
                         / np.float32(math.log(REL_MAX_DIST / max_exact))
                         * np.float32(REL_BUCKETS - max_exact)).astype(np.int32)
    large = np.minimum(large, REL_BUCKETS - 1)
    return np.where(n < max_exact, n, large).astype(np.int32)


def _bias_body(tbl_ref, bk_ref, o_ref):
    h = pl.program_id(0)
    for dl in range(2):
        bk = bk_ref[dl]
        acc = jnp.zeros((TILE, TILE), F32)
        for b in range(REL_BUCKETS):
            acc = jnp.where(bk == b, tbl_ref[h * REL_BUCKETS + b], acc)
        o_ref[dl, 0] = acc * LOG2E


def _bias_tiles(rel_bias):
    key = np.arange(TILE)[:, None]
    qry = np.arange(TILE)[None, :]
    assert int(_t5_bucket_np(np.array(TILE + 1))) == REL_BUCKETS - 1
    bk = np.stack([_t5_bucket_np(qry - key), _t5_bucket_np(TILE + qry - key)])
    return pl.pallas_call(
        _bias_body,
        grid=(N_HEADS,),
        in_specs=[pl.BlockSpec(memory_space=pltpu.SMEM),
                  pl.BlockSpec((2, TILE, TILE), lambda h: (0, 0, 0))],
        out_specs=pl.BlockSpec((2, 1, TILE, TILE), lambda h: (0, h, 0, 0)),
        out_shape=jax.ShapeDtypeStruct((2, N_HEADS, TILE, TILE), F32),
        name="t5_bias_tiles",
    )(rel_bias.T.reshape(-1), jnp.asarray(bk))


def _heads_on_lanes(bias, per_block):
    two, H, T, _ = bias.shape
    b = bias.reshape(two, H // per_block, per_block, T, T).transpose(0, 1, 3, 2, 4)
    return b.reshape(two, H // per_block, T, per_block * T)


def _init_state(m_ref, l_ref, acc_ref):
    m_ref[...] = jnp.full(m_ref.shape, NEG_INF, F32)
    l_ref[...] = jnp.zeros(l_ref.shape, F32)
    acc_ref[...] = jnp.zeros(acc_ref.shape, F32)


def _rank_before(vals, rows):
    idx = lax.broadcasted_iota(I32, vals.shape, 0)
    rank = jnp.zeros(vals.shape, I32)
    for m in range(rows):
        row = vals[m:m + 1, :]
        beats = (row > vals) | ((row == vals) & (idx > m))
        rank = rank + jnp.where(beats, 1, 0)
    return rank


MOBA_STREAMS = 4


def _softmax_pv(scores, adds, vts, heads, m_ref, l_ref, acc_ref):
    def fold(x, op):
        return op(x.reshape(x.shape[0] // SUBLANES, SUBLANES, x.shape[1]), axis=0)

    m_prev = m_ref[...]
    m_part = None
    for s, add in zip(scores, adds):
        part = fold(s, jnp.max) + add
        m_part = part if m_part is None else jnp.maximum(m_part, part)
    m_new = jnp.maximum(m_prev, jnp.max(m_part, axis=0, keepdims=True))
    a = jnp.exp2(m_prev - m_new)
    probs = [jnp.exp2(s - (m_new - add)) for s, add in zip(scores, adds)]
    vt = jnp.concatenate(vts, axis=1)
    pb = jnp.concatenate([p.astype(BF16) for p in probs], axis=0)
    ones = jnp.ones((ONES_ROWS, vt.shape[1]), BF16)
    pv = jnp.concatenate([jnp.dot(jnp.concatenate([vt[rows], ones], axis=0), pb[:, cols],
                                  preferred_element_type=F32) for rows, cols in heads], axis=1)
    l_ref[...] = a * l_ref[...] + pv[HEAD_DIM:HEAD_DIM + 1, :]
    acc_ref[...] = a * acc_ref[...] + pv[:HEAD_DIM, :]
    m_ref[...] = m_new


def _moba_body(q_ref, k_ref, vt_ref, bias_ref, o_ref, kmean_ref, radd_ref, sa_ref, sb_ref, m_ref, l_ref, acc_ref):
    i = pl.program_id(2)
    nb = k_ref.shape[1] // TILE
    streams = range(MOBA_STREAMS)
    lanes_of = lambda s: slice(s * LANES, (s + 1) * LANES)

    @pl.when(i == 0)
    def _():
        for s in streams:
            for n in range(nb):
                kb = k_ref[0, n * TILE:(n + 1) * TILE, lanes_of(s)].astype(F32)
                kmean_ref[s, n:n + 1, :] = jnp.sum(kb, axis=0, keepdims=True) * (1.0 / TILE)

    n_far = jnp.maximum(i - 1, 0)
    n_far_groups = (n_far + 1) >> 1
    last = nb - 1

    def key_tile(t, s):
        return k_ref[0, pl.ds(pl.multiple_of(t * TILE, TILE), TILE), lanes_of(s)]

    def far_tiles(j):
        return 2 * j, jnp.minimum(2 * j + 1, last)

    lane = lax.broadcasted_iota(I32, (TILE, LANES), 1)
    key = lax.broadcasted_iota(I32, (TILE, 2 * TILE), 0)
    qry = lax.broadcasted_iota(I32, (TILE, 2 * TILE), 1) & (TILE - 1)
    causal_neg = jnp.where(key <= qry, 0.0, NEG_INF)
    t_near = jnp.maximum(i - 1, 0)
    q2s = []
    for s in streams:
        q = q_ref[0, :, lanes_of(s)]
        zero = jnp.zeros_like(q)
        q2 = jnp.concatenate([jnp.where(lane < HEAD_DIM, q, zero),
                              jnp.where(lane >= HEAD_DIM, q, zero)], axis=0)
        q2s.append(q2)
        sa_ref[s, 0] = (lax.dot_general(key_tile(i, s), q2, _NT, preferred_element_type=F32)
                        + (bias_ref[0, s] + causal_neg))
        sa_ref[s, 1] = lax.dot_general(key_tile(t_near, s), q2, _NT, preferred_element_type=F32) + bias_ref[1, s]
        km = kmean_ref[s]
        k_hi = km.astype(BF16)
        k_lo = (km - k_hi.astype(F32)).astype(BF16)
        gate = (lax.dot_general(k_hi, q2, _NT, preferred_element_type=F32)
                + lax.dot_general(k_lo, q2, _NT, preferred_element_type=F32))
        blk = lax.broadcasted_iota(I32, gate.shape, 0)
        gate = jnp.where(blk < i, gate, -jnp.inf)
        rank = _rank_before(gate, nb)
        neg = jnp.where((rank < MOBA_TOPK) & (blk < i), 0.0, NEG_INF)
        far_bias = bias_ref[1, s, 0:1, :]
        near_row = jnp.full((1, 2 * TILE), NEG_INF, F32)
        radd_ref[s, 0:1, :] = jnp.zeros((1, 2 * TILE), F32)
        for n in range(nb):
            row = neg[n:n + 1, :]
            near_row = jnp.where(n == i - 1, row, near_row)
            radd_ref[s, 2 + n:3 + n, :] = jnp.where(n < n_far, far_bias + row, NEG_INF)
        radd_ref[s, 1:2, :] = near_row
        _init_state(m_ref.at[s], l_ref.at[s], acc_ref.at[s])

    pair_heads = [(slice(h * HEAD_DIM, (h + 1) * HEAD_DIM), slice(h * TILE, (h + 1) * TILE)) for h in range(2)]

    def update(j, s, buf):
        first = j == 0
        t0 = jnp.where(first, i, 2 * j - 2)
        t1 = jnp.where(first, t_near, jnp.minimum(2 * j - 1, last))
        _softmax_pv([buf[s, 0], buf[s, 1]],
                    [radd_ref[s, pl.ds(2 * j, 1), :], radd_ref[s, pl.ds(2 * j + 1, 1), :]],
                    [vt_ref[0, t0, lanes_of(s), :], vt_ref[0, t1, lanes_of(s), :]], pair_heads,
                    m_ref.at[s], l_ref.at[s], acc_ref.at[s])

    def step(j, src, dst):
        for s in streams:
            ta, tb = far_tiles(j)
            dst[s, 0] = lax.dot_general(key_tile(ta, s), q2s[s], _NT, preferred_element_type=F32)
            dst[s, 1] = lax.dot_general(key_tile(tb, s), q2s[s], _NT, preferred_element_type=F32)
            update(j, s, src)

    def two_steps(jj, carry):
        step(2 * jj, sa_ref, sb_ref)
        step(2 * jj + 1, sb_ref, sa_ref)
        return carry

    lax.fori_loop(0, n_far_groups >> 1, two_steps, 0)

    @pl.when((n_far_groups & 1) == 1)
    def _():
        step(n_far_groups - 1, sa_ref, sb_ref)
        for s in streams:
            update(n_far_groups, s, sb_ref)

    @pl.when((n_far_groups & 1) == 0)
    def _():
        for s in streams:
            update(n_far_groups, s, sa_ref)

    for s in streams:
        o = acc_ref[s] / l_ref[s]
        o = jnp.concatenate([o[:, :TILE], o[:, TILE:]], axis=0)
        o_ref[0, :, lanes_of(s)] = o.T.astype(o_ref.dtype)


def _moba_attention(qk, vt, bias):
    B, S, _ = qk.shape
    n_steps = N_HEADS // 2 // MOBA_STREAMS
    nq = S // TILE
    w = MOBA_STREAMS * LANES
    return pl.pallas_call(
        _moba_body,
        grid=(B, n_steps, nq),
        in_specs=[pl.BlockSpec((1, TILE, w), lambda b, p, i: (b, i, p)),
                  pl.BlockSpec((1, S, w), lambda b, p, i: (b, 0, n_steps + p)),
                  pl.BlockSpec((1, nq, w, TILE), lambda b, p, i: (b, 0, p, 0)),
                  pl.BlockSpec((2, MOBA_STREAMS, TILE, 2 * TILE), lambda b, p, i: (0, p, 0, 0))],
        out_specs=pl.BlockSpec((1, TILE, w), lambda b, p, i: (b, i, p)),
        out_shape=jax.ShapeDtypeStruct((B, S, N_HEADS * HEAD_DIM), BF16),
        scratch_shapes=[pltpu.VMEM((MOBA_STREAMS, nq, LANES), F32),
                        pltpu.VMEM((MOBA_STREAMS, 2 + nq, 2 * TILE), F32),
                        pltpu.VMEM((MOBA_STREAMS, 2, TILE, 2 * TILE), F32),
                        pltpu.VMEM((MOBA_STREAMS, 2, TILE, 2 * TILE), F32),
                        pltpu.VMEM((MOBA_STREAMS, 1, 2 * TILE), F32),
                        pltpu.VMEM((MOBA_STREAMS, 1, 2 * TILE), F32),
                        pltpu.VMEM((MOBA_STREAMS, HEAD_DIM, 2 * TILE), F32)],
        compiler_params=_cparams("parallel", "parallel", "arbitrary"),
        name="moba_attention",
    )(qk, qk, vt, _heads_on_lanes(bias, 2))


def _gelu_tanh(x):
    return 0.5 * x * (1.0 + jnp.tanh(math.sqrt(2.0 / math.pi) * (x + 0.044715 * (x * x * x))))


def _compress_body(t_ref, pos_ref, w1_ref, w2_ref, o_ref):
    groups = t_ref.shape[2]
    half = t_ref.shape[3]
    t = t_ref[0].reshape(NSA_KV_HEADS * groups, half).astype(F32)
    first = jnp.dot((t + pos_ref[0:1, :]).astype(BF16), w1_ref[0:half, :], preferred_element_type=F32)
    second = jnp.dot((t + pos_ref[1:2, :]).astype(BF16), w1_ref[half:2 * half, :],
                     preferred_element_type=F32)
    rows = first.shape[0]
    pre = first + pltpu.roll(second, rows - 1, 0)
    out = jnp.dot(_gelu_tanh(pre).astype(BF16), w2_ref[...], preferred_element_type=F32)
    for h in range(NSA_KV_HEADS):
        o_ref[0, h] = out[h * groups:(h + 1) * groups].astype(o_ref.dtype)


def _compress(t, pos, w1, w2):
    B, Hkv, groups, half = t.shape
    return pl.pallas_call(
        _compress_body,
        grid=(B,),
        in_specs=[pl.BlockSpec((1, Hkv, groups, half), lambda b: (b, 0, 0, 0)),
                  pl.BlockSpec((2, half), lambda b: (0, 0)),
                  pl.BlockSpec((2 * half, CMP_HIDDEN), lambda b: (0, 0)),
                  pl.BlockSpec((CMP_HIDDEN, HEAD_DIM), lambda b: (0, 0))],
        out_specs=pl.BlockSpec((1, Hkv, groups, HEAD_DIM), lambda b: (b, 0, 0, 0)),
        out_shape=jax.ShapeDtypeStruct((B, Hkv, groups, HEAD_DIM), BF16),
        compiler_params=_cparams("parallel"),
        name="nsa_compress",
    )(t, pos.reshape(2, half), w1.astype(BF16), w2.astype(BF16))


def _swap_halves(x):
    return jnp.concatenate([x[:, HEAD_DIM:], x[:, :HEAD_DIM]], axis=1)


def _group_lanes(x):
    return jnp.concatenate([x] * NSA_GROUP, axis=1)


def _nsa_body(q_ref, kc_ref, vct_ref, ks_ref, vst_ref, kw_ref, vwt_ref, gt_ref, bias_ref, c2s_ref,
              o_ref, selneg_ref, radd_ref, sa_ref, sb_ref, sw_ref, oc_ref, os_ref, m_ref, l_ref, acc_ref):
    i = pl.program_id(2)
    nb = ks_ref.shape[1] // TILE
    n_cmp = kc_ref.shape[1]
    n_slc = c2s_ref.shape[0]
    per_tile = TILE // SLC_BLOCK
    cols = NSA_GROUP * TILE
    kv_heads = range(2)
    n_far = jnp.maximum(i - 1, 0)
    n_far_groups = (n_far + 1) >> 1
    last = nb - 1
    t_near = jnp.maximum(i - 1, 0)
    t_edge = jnp.maximum(i - 2, 0)

    dims_of = lambda a: slice(a * HEAD_DIM, (a + 1) * HEAD_DIM)

    def key_tile(k_ref, t):
        return k_ref[0, pl.ds(pl.multiple_of(t * TILE, TILE), TILE), :]

    def scores_of(k_ref, t, a):
        return lax.dot_general(key_tile(k_ref, t), q4s[a], _NT, preferred_element_type=F32)

    lane = lax.broadcasted_iota(I32, (TILE, LANES), 1)
    lo_half = lane < HEAD_DIM
    qpos = i * TILE + (lax.broadcasted_iota(I32, (n_cmp, cols), 1) & (TILE - 1))
    cmp_valid = CMP_STRIDE * lax.broadcasted_iota(I32, (n_cmp, cols), 0) + (CMP_LEN - 1) <= qpos
    key = lax.broadcasted_iota(I32, (TILE, TILE), 0)
    qry = lax.broadcasted_iota(I32, (TILE, TILE), 1)
    diag_neg = _group_lanes(jnp.where(key <= qry, 0.0, NEG_INF))
    edge_neg = _group_lanes(jnp.where(key > qry, 0.0, NEG_INF))
    qall = q_ref[0]
    q4s = []

    for a in kv_heads:
        keep = lo_half if a == 0 else jnp.logical_not(lo_half)
        heads = []
        for g in range(NSA_GROUP):
            cb = a * 2 + g // 2
            x = qall[:, cb * LANES:(cb + 1) * LANES]
            if g % 2 != a:
                x = _swap_halves(x)
            heads.append(jnp.where(keep, x, jnp.zeros_like(x)))
        q4s.append(jnp.concatenate(heads, axis=0))
        far_bias = bias_ref[1, a, 0:1, :]

        sa_ref[a, 0] = scores_of(ks_ref, i, a) + (bias_ref[0, a] + diag_neg)
        sa_ref[a, 1] = scores_of(ks_ref, t_near, a) + bias_ref[1, a]
        sw_ref[a, 0] = scores_of(kw_ref, i, a) + (bias_ref[0, a] + diag_neg)
        sw_ref[a, 1] = scores_of(kw_ref, t_near, a) + bias_ref[1, a]
        sw_ref[a, 2] = scores_of(kw_ref, t_edge, a) + (far_bias + edge_neg)

        s_c = lax.dot_general(kc_ref[0], q4s[a], _NT, preferred_element_type=F32)
        s_c = jnp.where(cmp_valid, s_c, NEG_INF)
        m_c = jnp.max(s_c, axis=0, keepdims=True)
        e_c = jnp.where(cmp_valid, jnp.exp2(s_c - m_c), 0.0)
        l_c = jnp.sum(e_c, axis=0, keepdims=True)
        p_c = e_c / jnp.where(l_c > 0.0, l_c, 1.0)
        oc_ref[a] = jnp.dot(vct_ref[0, dims_of(a), :], p_c.astype(BF16), preferred_element_type=F32)

        p_sum = p_c[:, 0:TILE]
        for g in range(1, NSA_GROUP):
            p_sum = p_sum + p_c[:, g * TILE:(g + 1) * TILE]
        p_hi = p_sum.astype(BF16)
        p_lo = (p_sum - p_hi.astype(F32)).astype(BF16)
        imp = (jnp.dot(c2s_ref[...], p_hi, preferred_element_type=F32)
               + jnp.dot(c2s_ref[...], p_lo, preferred_element_type=F32))
        j = lax.broadcasted_iota(I32, imp.shape, 0)
        qb = (i * TILE + lax.broadcasted_iota(I32, imp.shape, 1)) >> int(math.log2(SLC_BLOCK))
        forced = (j == 0) | ((j <= qb) & (j > qb - SLC_LOCAL))
        imp = jnp.where(forced, jnp.inf, jnp.where(j > qb, -jnp.inf, imp))
        rank = _rank_before(imp, n_slc)
        selneg = jnp.where((rank < SLC_TOPN) & (j <= qb), 0.0, NEG_INF)
        selneg_ref[a] = selneg
        for c in range(per_tile):
            radd_ref[a, c:c + 1, :] = _group_lanes(selneg_ref[a, pl.ds(per_tile * i + c, 1), :])
            near_row = _group_lanes(selneg_ref[a, pl.ds(per_tile * t_near + c, 1), :])
            radd_ref[a, per_tile + c:per_tile + c + 1, :] = jnp.where(i >= 1, near_row, NEG_INF)
        for blk in range(n_slc):
            row = far_bias + _group_lanes(selneg[blk:blk + 1, :])
            r = 2 * per_tile + blk
            radd_ref[a, r:r + 1, :] = jnp.where(blk // per_tile < n_far, row, NEG_INF)
        _init_state(m_ref.at[a], l_ref.at[a], acc_ref.at[a])

    def update(jg, a, buf):
        first = jg == 0
        t0 = jnp.where(first, i, 2 * jg - 2)
        t1 = jnp.where(first, t_near, jnp.minimum(2 * jg - 1, last))
        scores, adds = [], []
        for t in range(2):
            for c in range(per_tile):
                scores.append(buf[a, t, c * SLC_BLOCK:(c + 1) * SLC_BLOCK, :])
                adds.append(radd_ref[a, pl.ds(2 * per_tile * jg + per_tile * t + c, 1), :])
        _softmax_pv(scores, adds, [vst_ref[0, t0], vst_ref[0, t1]], [(dims_of(a), slice(None))],
                    m_ref.at[a], l_ref.at[a], acc_ref.at[a])

    def step(jg, src, dst):
        for a in kv_heads:
            dst[a, 0] = scores_of(ks_ref, 2 * jg, a)
            dst[a, 1] = scores_of(ks_ref, jnp.minimum(2 * jg + 1, last), a)
            update(jg, a, src)

    def two_steps(jj, carry):
        step(2 * jj, sa_ref, sb_ref)
        step(2 * jj + 1, sb_ref, sa_ref)
        return carry

    lax.fori_loop(0, n_far_groups >> 1, two_steps, 0)

    @pl.when((n_far_groups & 1) == 1)
    def _():
        step(n_far_groups - 1, sa_ref, sb_ref)
        for a in kv_heads:
            update(n_far_groups, a, sb_ref)

    @pl.when((n_far_groups & 1) == 0)
    def _():
        for a in kv_heads:
            update(n_far_groups, a, sa_ref)

    gates = jax.nn.sigmoid(gt_ref[0, 0])
    zero_row = jnp.zeros((1, cols), F32)
    pieces = []
    for a in kv_heads:
        os_ref[a] = acc_ref[a] / l_ref[a]
        _init_state(m_ref.at[a], l_ref.at[a], acc_ref.at[a])
        _softmax_pv([sw_ref[a, 0], sw_ref[a, 1], sw_ref[a, 2]],
                    [zero_row, zero_row + jnp.where(i >= 1, 0.0, NEG_INF), zero_row + jnp.where(i >= 2, 0.0, NEG_INF)],
                    [vwt_ref[0, i], vwt_ref[0, t_near], vwt_ref[0, t_edge]], [(dims_of(a), slice(None))],
                    m_ref.at[a], l_ref.at[a], acc_ref.at[a])
        o_w = acc_ref[a] / l_ref[a]
        for g in range(NSA_GROUP):
            c0 = 3 * (NSA_GROUP * a + g)
            ls = slice(g * TILE, (g + 1) * TILE)
            pieces.append(gates[c0:c0 + 1, :] * oc_ref[a, :, ls] + gates[c0 + 1:c0 + 2, :] * os_ref[a, :, ls]
                          + gates[c0 + 2:c0 + 3, :] * o_w[:, ls])
    o_ref[0] = jnp.concatenate(pieces, axis=0).T.astype(o_ref.dtype)


def _cmp_to_slc(S):
    n_cmp_pad = S // CMP_STRIDE
    n_slc = S // SLC_BLOCK
    ci = np.arange(n_cmp_pad)[:, None] * CMP_STRIDE
    sj = np.arange(n_slc)[None, :] * SLC_BLOCK
    c2s = ((ci < sj + SLC_BLOCK) & (ci + CMP_LEN > sj)).astype(np.float32)
    c2s[(S - CMP_LEN) // CMP_STRIDE + 1:] = 0.0
    return jnp.asarray(c2s.T, BF16)


def _nsa_attention(proj, vt, gate_t, kcmp, vcmp_t, bias):
    B, S, _ = proj.shape
    nq = S // TILE
    n_cmp = kcmp.shape[1]
    n_slc = S // SLC_BLOCK
    qw = 2 * NSA_GROUP * HEAD_DIM
    q_blocks = N_HEADS * HEAD_DIM // LANES
    kv_blocks = NSA_KV_HEADS * HEAD_DIM // LANES

    def k_spec(which):
        base = q_blocks + which * kv_blocks
        return pl.BlockSpec((1, S, LANES), lambda b, p, i: (b, 0, base + p))

    def vt_spec(which):
        base = which * kv_blocks
        return pl.BlockSpec((1, nq, LANES, TILE), lambda b, p, i: (b, 0, base + p, 0))

    state = pltpu.VMEM((2, HEAD_DIM, NSA_GROUP * TILE), F32)
    stat = pltpu.VMEM((2, 1, NSA_GROUP * TILE), F32)
    return pl.pallas_call(
        _nsa_body,
        grid=(B, 2, nq),
        in_specs=[pl.BlockSpec((1, TILE, qw), lambda b, p, i: (b, i, p)),
                  pl.BlockSpec((1, n_cmp, LANES), lambda b, p, i: (b, 0, p)),
                  pl.BlockSpec((1, LANES, n_cmp), lambda b, p, i: (b, p, 0)),
                  k_spec(2), vt_spec(0), k_spec(3), vt_spec(1),
                  pl.BlockSpec((1, 1, LANES, TILE), lambda b, p, i: (b, i, p, 0)),
                  pl.BlockSpec((2, 2, TILE, NSA_GROUP * TILE), lambda b, p, i: (0, p, 0, 0)),
                  pl.BlockSpec((n_slc, n_cmp), lambda b, p, i: (0, 0))],
        out_specs=pl.BlockSpec((1, TILE, qw), lambda b, p, i: (b, i, p)),
        out_shape=jax.ShapeDtypeStruct((B, S, N_HEADS * HEAD_DIM), BF16),
        scratch_shapes=[pltpu.VMEM((2, n_slc, TILE), F32),
                        pltpu.VMEM((2, 2 * (TILE // SLC_BLOCK) + n_slc, NSA_GROUP * TILE), F32),
                        pltpu.VMEM((2, 2, TILE, NSA_GROUP * TILE), F32),
                        pltpu.VMEM((2, 2, TILE, NSA_GROUP * TILE), F32),
                        pltpu.VMEM((2, 3, TILE, NSA_GROUP * TILE), F32),
                        state, state, stat, stat, state],
        compiler_params=_cparams("parallel", "parallel", "arbitrary"),
        name="nsa_attention",
    )(proj, kcmp, vcmp_t, proj, vt, proj, vt, gate_t, _heads_on_lanes(bias, NSA_GROUP), _cmp_to_slc(S))


def _split_bf16(x):
    hi = x.astype(BF16)
    return hi, (x - hi.astype(F32)).astype(BF16)


_ROW_OF_EXPERT = np.arange(N_EXPERTS).reshape(N_GROUPS, EXPERTS_PER_GROUP).T.reshape(-1)


def _route(x, w_ref, b_ref, tri_ref, idx_ref, wt_ref, pos_ref, cnt_ref, base_ref):
    @pl.when(pl.program_id(0) == 0)
    def _():
        base_ref[...] = jnp.zeros(base_ref.shape, F32)

    x_hi, x_lo = _split_bf16(x)
    w_hi, w_lo = _split_bf16(w_ref[...])
    logits = (lax.dot_general(w_hi, x_hi, _NT, preferred_element_type=F32)
              + lax.dot_general(w_hi, x_lo, _NT, preferred_element_type=F32)
              + lax.dot_general(w_lo, x_hi, _NT, preferred_element_type=F32)) + b_ref[:, 0:1]
    m = jnp.max(logits, axis=0, keepdims=True)
    e = jnp.exp(logits - m)
    probs = e / jnp.sum(e, axis=0, keepdims=True)
    pk = [probs[k * N_GROUPS:(k + 1) * N_GROUPS] for k in range(EXPERTS_PER_GROUP)]
    hi1, lo1 = jnp.maximum(pk[0], pk[1]), jnp.minimum(pk[0], pk[1])
    hi2, lo2 = jnp.maximum(pk[2], pk[3]), jnp.minimum(pk[2], pk[3])
    score = jnp.maximum(hi1, hi2) + jnp.maximum(jnp.minimum(hi1, hi2), jnp.maximum(lo1, lo2))
    grp = lax.broadcasted_iota(I32, score.shape, 0)
    best = jnp.min(jnp.where(score == jnp.max(score, axis=0, keepdims=True), grp, N_GROUPS),
                   axis=0, keepdims=True)
    v = [jnp.sum(jnp.where(grp == best, p, 0.0), axis=0, keepdims=True) for p in pk]
    v1 = jnp.maximum(jnp.maximum(v[0], v[1]), jnp.maximum(v[2], v[3]))
    i1 = jnp.where(v[0] == v1, 0, jnp.where(v[1] == v1, 1, jnp.where(v[2] == v1, 2, 3)))
    rest = [jnp.where(i1 == k, -1.0, v[k]) for k in range(EXPERTS_PER_GROUP)]
    v2 = jnp.maximum(jnp.maximum(rest[0], rest[1]), jnp.maximum(rest[2], rest[3]))
    i2 = jnp.where(rest[0] == v2, 0, jnp.where(rest[1] == v2, 1, jnp.where(rest[2] == v2, 2, 3)))
    tot = v1 + v2
    idx_ref[...] = jnp.concatenate([best * EXPERTS_PER_GROUP + i1, best * EXPERTS_PER_GROUP + i2], axis=0)
    wt_ref[...] = jnp.concatenate([v1 / tot, v2 / tot], axis=0)

    row = lax.broadcasted_iota(I32, logits.shape, 0)
    hot = [jnp.where(row == ik * N_GROUPS + best, 1.0, 0.0) for ik in (i1, i2)]
    both = (hot[0] + hot[1]).astype(BF16)
    run = base_ref[:, 0:1]
    pos = [[], []]
    for c in range(logits.shape[1] // LANES):
        ls = slice(c * LANES, (c + 1) * LANES)
        before = run + jnp.dot(both[:, ls], tri_ref[...], preferred_element_type=F32) - 1.0
        for k in range(2):
            pos[k].append(jnp.sum(hot[k][:, ls] * before, axis=0, keepdims=True))
        run = before[:, LANES - 1:LANES] + 1.0
    pos_ref[...] = jnp.concatenate([jnp.concatenate(pos[0], axis=1), jnp.concatenate(pos[1], axis=1)],
                                   axis=0).astype(I32)
    base_ref[...] = jnp.broadcast_to(run, base_ref.shape)
    cnt_ref[...] = jnp.broadcast_to(run, cnt_ref.shape)


def _router_operands(router_w, router_b):
    w = router_w.T[_ROW_OF_EXPERT]
    b = jnp.broadcast_to(router_b[_ROW_OF_EXPERT][:, None], (N_EXPERTS, LANES))
    tri = jnp.asarray(np.triu(np.ones((LANES, LANES), np.float32)), BF16)
    return w, b, tri


def _proj_ln_route_body(a_ref, w_ref, x_ref, g_ref, b_ref, rw_ref, rb_ref, tri_ref,
                        o_ref, ob_ref, idx_ref, wt_ref, pos_ref, cnt_ref, base_ref):
    y = jnp.dot(a_ref[...], w_ref[...], preferred_element_type=F32)
    out = _layer_norm_rows(DEEPNORM_ALPHA * x_ref[...] + y, g_ref[...], b_ref[...])
    o_ref[...] = out
    ob_ref[...] = out.astype(BF16)
    _route(out, rw_ref, rb_ref, tri_ref, idx_ref, wt_ref, pos_ref, cnt_ref, base_ref)


def _proj_ln_route(a, w, x, g, b, router_w, router_b):
    M, K = a.shape
    D = w.shape[1]
    row = pl.BlockSpec((LN_TM, D), lambda i: (i, 0))
    vec = pl.BlockSpec((1, D), lambda i: (0, 0))
    whole = lambda shape: pl.BlockSpec(shape, lambda i: (0, 0))
    tok = lambda dt: jax.ShapeDtypeStruct((2, M), dt)
    tok_spec = pl.BlockSpec((2, LN_TM), lambda i: (0, i))
    h, hb, idx, wts, pos, cnt = pl.pallas_call(
        _proj_ln_route_body,
        grid=(M // LN_TM,),
        in_specs=[pl.BlockSpec((LN_TM, K), lambda i: (i, 0)), whole((K, D)), row, vec, vec,
                  whole((N_EXPERTS, D)), whole((N_EXPERTS, LANES)), whole((LANES, LANES))],
        out_specs=[row, row, tok_spec, tok_spec, tok_spec, whole((N_EXPERTS, LANES))],
        out_shape=[jax.ShapeDtypeStruct((M, D), F32), jax.ShapeDtypeStruct((M, D), BF16),
                   tok(I32), tok(F32), tok(I32), jax.ShapeDtypeStruct((N_EXPERTS, LANES), F32)],
        scratch_shapes=[pltpu.VMEM((N_EXPERTS, LANES), F32)],
        compiler_params=_cparams("arbitrary"),
        name="out_proj_ln_route",
    )(a, w, x, g.reshape(1, D), b.reshape(1, D), *_router_operands(router_w, router_b))
    counts = cnt[np.argsort(_ROW_OF_EXPERT), 0].astype(I32)
    return h, hb, (idx, wts, pos, counts)


def _expert_body(blk_e_ref, n_used_ref, x_ref, wg_ref, wu_ref, wd_ref, o_ref, wg_b, wu_b, wd_b):
    i = pl.program_id(0)

    @pl.when((i == 0) | (blk_e_ref[i] != blk_e_ref[jnp.maximum(i - 1, 0)]))
    def _():
        wg_b[...] = wg_ref[0, 0].astype(BF16)
        wu_b[...] = wu_ref[0, 0].astype(BF16)
        wd_b[...] = wd_ref[0, 0].astype(BF16)

    @pl.when(i < n_used_ref[0])
    def _():
        x = x_ref[...]
        gate = jnp.dot(x, wg_b[...], preferred_element_type=F32)
        up = jnp.dot(x, wu_b[...], preferred_element_type=F32)
        hid = (gate * jax.nn.sigmoid(gate) * up).astype(BF16)
        o_ref[...] = jnp.dot(hid, wd_b[...], preferred_element_type=F32).astype(o_ref.dtype)

    @pl.when(i >= n_used_ref[0])
    def _():
        o_ref[...] = jnp.zeros(o_ref.shape, o_ref.dtype)


def _experts(xs, blk_e, n_used, wg, wu, wd, layer):
    R, D = xs.shape
    n_blk = R // MOE_TB

    def live(i, be, nu):
        return jnp.minimum(i, nu[0] - 1)

    grid_spec = pltpu.PrefetchScalarGridSpec(
        num_scalar_prefetch=2,
        grid=(n_blk,),
        in_specs=[pl.BlockSpec((MOE_TB, D), lambda i, be, nu: (live(i, be, nu), 0)),
                  pl.BlockSpec((1, 1, D, D_EXPERT), lambda i, be, nu: (layer, be[i], 0, 0)),
                  pl.BlockSpec((1, 1, D, D_EXPERT), lambda i, be, nu: (layer, be[i], 0, 0)),
                  pl.BlockSpec((1, 1, D_EXPERT, D), lambda i, be, nu: (layer, be[i], 0, 0))],
        out_specs=pl.BlockSpec((MOE_TB, D), lambda i, be, nu: (i, 0)),
        scratch_shapes=[pltpu.VMEM((D, D_EXPERT), BF16), pltpu.VMEM((D, D_EXPERT), BF16),
                        pltpu.VMEM((D_EXPERT, D), BF16)],
    )
    return pl.pallas_call(
        _expert_body,
        grid_spec=grid_spec,
        out_shape=jax.ShapeDtypeStruct((R, D), BF16),
        compiler_params=_cparams("arbitrary"),
        name="moe_experts",
    )(blk_e, n_used, xs, wg, wu, wd)


def _combine_ln_body(x_ref, y0_ref, y1_ref, wt_ref, g_ref, b_ref, o_ref, ob_ref):
    ffn = y0_ref[...] * wt_ref[:, 0:1] + y1_ref[...] * wt_ref[:, HEAD_DIM:HEAD_DIM + 1]
    out = _layer_norm_rows(DEEPNORM_ALPHA * x_ref[...] + ffn, g_ref[...], b_ref[...])
    o_ref[...] = out
    ob_ref[...] = out.astype(BF16)


def _combine_ln(x, y0, y1, wt, g, b):
    M, D = x.shape
    row = pl.BlockSpec((LN_TM, D), lambda i: (i, 0))
    vec = pl.BlockSpec((1, D), lambda i: (0, 0))
    return pl.pallas_call(
        _combine_ln_body,
        grid=(M // LN_TM,),
        in_specs=[row, row, row, pl.BlockSpec((LN_TM, LANES), lambda i: (i, 0)), vec, vec],
        out_specs=[row, row],
        out_shape=[jax.ShapeDtypeStruct((M, D), F32), jax.ShapeDtypeStruct((M, D), BF16)],
        compiler_params=_cparams("parallel"),
        name="moe_combine_ln",
    )(x, y0, y1, wt, g.reshape(1, D), b.reshape(1, D))


def _moe_ln(h, hb, routing, wg, wu, wd, layer, g, b):
    N, D = h.shape
    A = 2 * N
    idx, wts, pos, counts = routing
    starts = jnp.cumsum(counts) - counts
    padded = (counts + MOE_TB - 1) // MOE_TB * MOE_TB
    pends = jnp.cumsum(padded)
    pstarts = pends - padded
    R = A + N_EXPERTS * MOE_TB
    n_blk = R // MOE_TB
    experts = jnp.arange(N_EXPERTS, dtype=I32)
    dest = pos + jnp.sum(jnp.where(idx[None] == experts[:, None, None], pstarts[:, None, None], 0), axis=0)
    tok = jnp.broadcast_to(jnp.arange(N, dtype=I32)[None, :], (2, N))
    _, tok_sorted = lax.sort_key_val(dest.reshape(A), tok.reshape(A))
    blk_row0 = jnp.arange(n_blk, dtype=I32) * MOE_TB
    blk_e = jnp.minimum(jnp.sum((pends[None, :] <= blk_row0[:, None]).astype(I32), axis=1), N_EXPERTS - 1)
    hot = blk_e[:, None] == experts[None, :]
    compact0 = blk_row0 + jnp.sum(jnp.where(hot, (starts - pstarts)[None, :], 0), axis=1)
    compact = jnp.remainder(compact0[:, None] + jnp.arange(MOE_TB, dtype=I32)[None, :], A).reshape(R)
    n_used = (pends[-1:] // MOE_TB).astype(I32)
    xs = hb[tok_sorted[compact]]
    yb = _experts(xs, blk_e, n_used, wg, wu, wd, layer)
    wt = jnp.concatenate([jnp.broadcast_to(wts[k][:, None], (N, HEAD_DIM)) for k in range(2)], axis=1)
    return _combine_ln(h, yb[dest[0]], yb[dest[1]], wt, g, b)


def _moba_layer(h, w_in, w_out, bias, g, b, router_w, router_b, B, S):
    HD = N_HEADS * HEAD_DIM
    qk, vt = _in_proj(h, w_in[:, :2 * HD].astype(BF16), _query_scale(HD, 2 * HD),
                      [w_in[:, 2 * HD:].T.astype(BF16)], [BF16], B, S)
    att = _moba_attention(qk.reshape(B, S, 2 * HD), vt, bias)
    return _proj_ln_route(att.reshape(B * S, HD), w_out.astype(BF16), h, g, b, router_w, router_b)


def _nsa_layer(h, hb, w_in, w_out, pos_k, pos_v, ck_w1, ck_w2, cv_w1, cv_w2, bias, g, b, router_w, router_b, B, S):
    HD = N_HEADS * HEAD_DIM
    kvw = NSA_KV_HEADS * HEAD_DIM
    col = lambda k: slice(HD + k * kvw, HD + (k + 1) * kvw)
    w_rows = jnp.concatenate([w_in[:, :HD + 2 * kvw], w_in[:, col(2)], w_in[:, col(4)]], axis=1)
    w_vt = jnp.concatenate([w_in[:, col(3)], w_in[:, col(5)]], axis=1).T
    per_pair = 3 * N_HEADS // 2
    wg = w_in[:, HD + 6 * kvw:].reshape(D_MODEL, 2, per_pair)
    wg = jnp.pad(wg, ((0, 0), (0, 0), (0, LANES - per_pair))).reshape(D_MODEL, 2 * LANES).T
    proj, vt, gate_t = _in_proj(hb, w_rows.astype(BF16), _query_scale(HD, HD + 4 * kvw),
                                [w_vt.astype(BF16), wg.astype(BF16)], [BF16, F32], B, S)
    proj = proj.reshape(B, S, HD + 4 * kvw)

    def grouped(t):
        t = t.reshape(B, S, NSA_KV_HEADS, HEAD_DIM).transpose(0, 2, 1, 3)
        return t.reshape(B, NSA_KV_HEADS, S // CMP_STRIDE, CMP_STRIDE * HEAD_DIM)

    kcmp = _compress(grouped(proj[..., HD:HD + kvw]), pos_k, ck_w1, ck_w2)
    vcmp = _compress(grouped(proj[..., HD + kvw:HD + 2 * kvw]), pos_v, cv_w1, cv_w2)
    n_cmp = kcmp.shape[2]
    kcmp = kcmp.transpose(0, 2, 1, 3).reshape(B, n_cmp, kvw)
    vcmp_t = vcmp.transpose(0, 1, 3, 2).reshape(B, kvw, n_cmp)
    att = _nsa_attention(proj, vt, gate_t, kcmp, vcmp_t, bias)
    return _proj_ln_route(att.reshape(B * S, HD), w_out.astype(BF16), h, g, b, router_w, router_b)


def kernel(x, rel_bias, router_w, router_b, ln_g, ln_b, moba_w_in, moba_w_out, nsa_w_in, nsa_w_out,
           nsa_pos_k, nsa_pos_v, nsa_ck_w1, nsa_ck_w2, nsa_cv_w1, nsa_cv_w2,
           moe_w_gate, moe_w_up, moe_w_down):
    B, S, D = x.shape
    bias = _bias_tiles(rel_bias)
    h = x.reshape(B * S, D)
    h, hb, routing = _moba_layer(h, moba_w_in[0], moba_w_out[0], bias, ln_g[0, 0], ln_b[0, 0],
                                 router_w, router_b, B, S)
    h, hb = _moe_ln(h, hb, routing, moe_w_gate, moe_w_up, moe_w_down, 0, ln_g[0, 1], ln_b[0, 1])
    h, hb, routing = _nsa_layer(h, hb, nsa_w_in[0], nsa_w_out[0], nsa_pos_k[0], nsa_pos_v[0],
                                nsa_ck_w1[0], nsa_ck_w2[0], nsa_cv_w1[0], nsa_cv_w2[0],
                                bias, ln_g[1, 0], ln_b[1, 0], router_w, router_b, B, S)
    h, hb = _moe_ln(h, hb, routing, moe_w_gate, moe_w_up, moe_w_down, 1, ln_g[1, 1], ln_b[1, 1])
    return h.reshape(B, S, D)
```

```python
import math
from functools import partial

import numpy as np
import jax
import jax.numpy as jnp
from jax import lax
from jax.experimental import pallas as pl
from jax.experimental.pallas import tpu as pltpu

F32, BF16, I32 = jnp.float32, jnp.bfloat16, jnp.int32

D_MODEL = 1024
N_HEADS = 16
HEAD_DIM = 64
DEPTH = 2
NEG_INF = -1e30
LN_EPS = 1e-5
MOBA_BLOCK = 256
MOBA_TOPK = 3
NSA_KV_HEADS = 4
NSA_GROUP = N_HEADS // NSA_KV_HEADS
CMP_LEN = 32
CMP_STRIDE = 16
CMP_HIDDEN = 256
SLC_BLOCK = 64
SLC_TOPN = 16
SLC_LOCAL = 2
WINDOW = 512
REL_BUCKETS = 32
REL_MAX_DIST = 128
N_EXPERTS = 32
N_GROUPS = 8
EXPERTS_PER_GROUP = N_EXPERTS // N_GROUPS
D_EXPERT = 512
DEEPNORM_ALPHA = (2 * DEPTH) ** 0.25
LOG2E = math.log2(math.e)
Q_SCALE = HEAD_DIM ** -0.5 * LOG2E

LANES = 128
SUBLANES = 8
ONES_ROWS = 16
TILE = 256
MM_TM = 512
MM_TN = 1024
LN_TM = 1024
MOE_TB = 256
VMEM_LIMIT = 48 * 1024 * 1024

_NT = (((1,), (1,)), ((), ()))


def _cparams(*sem):
    return pltpu.CompilerParams(dimension_semantics=sem, vmem_limit_bytes=VMEM_LIMIT)


def _in_proj_body(n_t, a_ref, w_ref, c_ref, *refs):
    wt_refs, o_ref, ot_refs = refs[:n_t], refs[n_t], refs[n_t + 1:]
    a = a_ref[...].astype(BF16)
    for c in range(o_ref.shape[1] // MM_TN):
        cols = slice(c * MM_TN, (c + 1) * MM_TN)
        acc = jnp.dot(a, w_ref[:, cols], preferred_element_type=F32)
        o_ref[:, cols] = (acc * c_ref[:, cols]).astype(o_ref.dtype)
    for wt_ref, ot_ref in zip(wt_refs, ot_refs):
        r = lax.dot_general(wt_ref[...], a, _NT, preferred_element_type=F32)
        for t in range(ot_ref.shape[1]):
            ot_ref[0, t] = r[:, t * TILE:(t + 1) * TILE].astype(ot_ref.dtype)


def _in_proj(a, w, col_scale, w_ts, t_dtypes, B, S):
    M, K = a.shape
    N = w.shape[1]
    per_seq = S // MM_TM
    sub = MM_TM // TILE
    whole = lambda shape: pl.BlockSpec(shape, lambda i: (0, 0))
    outs = pl.pallas_call(
        partial(_in_proj_body, len(w_ts)),
        grid=(M // MM_TM,),
        in_specs=[pl.BlockSpec((MM_TM, K), lambda i: (i, 0)), whole((K, N)), whole((1, N))]
                 + [whole(w_t.shape) for w_t in w_ts],
        out_specs=[pl.BlockSpec((MM_TM, N), lambda i: (i, 0))]
                  + [pl.BlockSpec((1, sub, w_t.shape[0], TILE), lambda i: (i // per_seq, i % per_seq, 0, 0))
                     for w_t in w_ts],
        out_shape=[jax.ShapeDtypeStruct((M, N), BF16)]
                  + [jax.ShapeDtypeStruct((B, S // TILE, w_t.shape[0], TILE), dt) for w_t, dt in zip(w_ts, t_dtypes)],
        compiler_params=_cparams("parallel"),
        name="in_proj",
    )(a, w, col_scale.reshape(1, N), *w_ts)
    return outs


def _query_scale(n_query_cols, n_cols):
    return jnp.where(jnp.arange(n_cols) < n_query_cols, Q_SCALE, 1.0).astype(F32)


def _layer_norm_rows(z, g, b):
    mu = jnp.mean(z, axis=-1, keepdims=True)
    zc = z - mu
    var = jnp.mean(zc * zc, axis=-1, keepdims=True)
    return zc * lax.rsqrt(var + LN_EPS) * g + b


def _t5_bucket_np(rel):
    n = np.maximum(rel, 0)
    max_exact = REL_BUCKETS // 2
    nf = np.maximum(n, 1).astype(np.float32)
    large = max_exact + (np.log(nf / np.float32(max_exact))
                         / np.float32(math.log(REL_MAX_DIST / max_exact))
                         * np.float32(REL_BUCKETS - max_exact)).astype(np.int32)
    large = np.minimum(large, REL_BUCKETS - 1)
    return np.where(n < max_exact, n, large).astype(np.int32)


def _bias_body(tbl_ref, bk_ref, o_ref):
    h = pl.program_id(0)
    for dl in range(2):
        bk = bk_ref[dl]
        acc = jnp.zeros((TILE, TILE), F32)
        for b in range(REL_BUCKETS):
            acc = jnp.where(bk == b, tbl_ref[h * REL_BUCKETS + b], acc)
        o_ref[dl, 0] = acc * LOG2E


def _bias_tiles(rel_bias):
    key = np.arange(TILE)[:, None]
    qry = np.arange(TILE)[None, :]
    assert int(_t5_bucket_np(np.array(TILE + 1))) == REL_BUCKETS - 1
    bk = np.stack([_t5_bucket_np(qry - key), _t5_bucket_np(TILE + qry - key)])
    return pl.pallas_call(
        _bias_body,
        grid=(N_HEADS,),
        in_specs=[pl.BlockSpec(memory_space=pltpu.SMEM),
                  pl.BlockSpec((2, TILE, TILE), lambda h: (0, 0, 0))],
        out_specs=pl.BlockSpec((2, 1, TILE, TILE), lambda h: (0, h, 0, 0)),
        out_shape=jax.ShapeDtypeStruct((2, N_HEADS, TILE, TILE), F32),
        name="t5_bias_tiles",
    )(rel_bias.T.reshape(-1), jnp.asarray(bk))


def _heads_on_lanes(bias, per_block):
    two, H, T, _ = bias.shape
    b = bias.reshape(two, H // per_block, per_block, T, T).transpose(0, 1, 3, 2, 4)
    return b.reshape(two, H // per_block, T, per_block * T)


def _init_state(m_ref, l_ref, acc_ref):
    m_ref[...] = jnp.full(m_ref.shape, NEG_INF, F32)
    l_ref[...] = jnp.zeros(l_ref.shape, F32)
    acc_ref[...] = jnp.zeros(acc_ref.shape, F32)


def _rank_before(vals, rows):
    idx = lax.broadcasted_iota(I32, vals.shape, 0)
    rank = jnp.zeros(vals.shape, I32)
    for m in range(rows):
        row = vals[m:m + 1, :]
        beats = (row > vals) | ((row == vals) & (idx > m))
        rank = rank + jnp.where(beats, 1, 0)
    return rank


MOBA_STREAMS = 4


def _softmax_pv(scores, adds, vts, heads, m_ref, l_ref, acc_ref):
    def fold(x, op):
        return op(x.reshape(x.shape[0] // SUBLANES, SUBLANES, x.shape[1]), axis=0)

    m_prev = m_ref[...]
    m_part = None
    for s, add in zip(scores, adds):
        part = fold(s, jnp.max) + add
        m_part = part if m_part is None else jnp.maximum(m_part, part)
    m_new = jnp.maximum(m_prev, jnp.max(m_part, axis=0, keepdims=True))
    a = jnp.exp2(m_prev - m_new)
    probs = [jnp.exp2(s - (m_new - add)) for s, add in zip(scores, adds)]
    vt = jnp.concatenate(vts, axis=1)
    pb = jnp.concatenate([p.astype(BF16) for p in probs], axis=0)
    ones = jnp.ones((ONES_ROWS, vt.shape[1]), BF16)
    pv = jnp.concatenate([jnp.dot(jnp.concatenate([vt[rows], ones], axis=0), pb[:, cols],
                                  preferred_element_type=F32) for rows, cols in heads], axis=1)
    l_ref[...] = a * l_ref[...] + pv[HEAD_DIM:HEAD_DIM + 1, :]
    acc_ref[...] = a * acc_ref[...] + pv[:HEAD_DIM, :]
    m_ref[...] = m_new


def _moba_body(q_ref, k_ref, vt_ref, bias_ref, o_ref, kmean_ref, radd_ref, sa_ref, sb_ref, m_ref, l_ref, acc_ref):
    i = pl.program_id(2)
    nb = k_ref.shape[1] // TILE
    streams = range(MOBA_STREAMS)
    lanes_of = lambda s: slice(s * LANES, (s + 1) * LANES)

    @pl.when(i == 0)
    def _():
        for s in streams:
            for n in range(nb):
                kb = k_ref[0, n * TILE:(n + 1) * TILE, lanes_of(s)].astype(F32)
                kmean_ref[s, n:n + 1, :] = jnp.sum(kb, axis=0, keepdims=True) * (1.0 / TILE)

    n_far = jnp.maximum(i - 1, 0)
    n_far_groups = (n_far + 1) >> 1
    last = nb - 1

    def key_tile(t, s):
        return k_ref[0, pl.ds(pl.multiple_of(t * TILE, TILE), TILE), lanes_of(s)]

    def far_tiles(j):
        return 2 * j, jnp.minimum(2 * j + 1, last)

    lane = lax.broadcasted_iota(I32, (TILE, LANES), 1)
    key = lax.broadcasted_iota(I32, (TILE, 2 * TILE), 0)
    qry = lax.broadcasted_iota(I32, (TILE, 2 * TILE), 1) & (TILE - 1)
    causal_neg = jnp.where(key <= qry, 0.0, NEG_INF)
    t_near = jnp.maximum(i - 1, 0)
    q2s = []
    for s in streams:
        q = q_ref[0, :, lanes_of(s)]
        zero = jnp.zeros_like(q)
        q2 = jnp.concatenate([jnp.where(lane < HEAD_DIM, q, zero),
                              jnp.where(lane >= HEAD_DIM, q, zero)], axis=0)
        q2s.append(q2)
        sa_ref[s, 0] = (lax.dot_general(key_tile(i, s), q2, _NT, preferred_element_type=F32)
                        + (bias_ref[0, s] + causal_neg))
        sa_ref[s, 1] = lax.dot_general(key_tile(t_near, s), q2, _NT, preferred_element_type=F32) + bias_ref[1, s]
        km = kmean_ref[s]
        k_hi = km.astype(BF16)
        k_lo = (km - k_hi.astype(F32)).astype(BF16)
        gate = (lax.dot_general(k_hi, q2, _NT, preferred_element_type=F32)
                + lax.dot_general(k_lo, q2, _NT, preferred_element_type=F32))
        blk = lax.broadcasted_iota(I32, gate.shape, 0)
        gate = jnp.where(blk < i, gate, -jnp.inf)
        rank = _rank_before(gate, nb)
        neg = jnp.where((rank < MOBA_TOPK) & (blk < i), 0.0, NEG_INF)
        far_bias = bias_ref[1, s, 0:1, :]
        near_row = jnp.full((1, 2 * TILE), NEG_INF, F32)
        radd_ref[s, 0:1, :] = jnp.zeros((1, 2 * TILE), F32)
        for n in range(nb):
            row = neg[n:n + 1, :]
            near_row = jnp.where(n == i - 1, row, near_row)
            radd_ref[s, 2 + n:3 + n, :] = jnp.where(n < n_far, far_bias + row, NEG_INF)
        radd_ref[s, 1:2, :] = near_row
        _init_state(m_ref.at[s], l_ref.at[s], acc_ref.at[s])

    pair_heads = [(slice(h * HEAD_DIM, (h + 1) * HEAD_DIM), slice(h * TILE, (h + 1) * TILE)) for h in range(2)]

    def update(j, s, buf):
        first = j == 0
        t0 = jnp.where(first, i, 2 * j - 2)
        t1 = jnp.where(first, t_near, jnp.minimum(2 * j - 1, last))
        _softmax_pv([buf[s, 0], buf[s, 1]],
                    [radd_ref[s, pl.ds(2 * j, 1), :], radd_ref[s, pl.ds(2 * j + 1, 1), :]],
                    [vt_ref[0, t0, lanes_of(s), :], vt_ref[0, t1, lanes_of(s), :]], pair_heads,
                    m_ref.at[s], l_ref.at[s], acc_ref.at[s])

    def step(j, src, dst):
        for s in streams:
            ta, tb = far_tiles(j)
            dst[s, 0] = lax.dot_general(key_tile(ta, s), q2s[s], _NT, preferred_element_type=F32)
            dst[s, 1] = lax.dot_general(key_tile(tb, s), q2s[s], _NT, preferred_element_type=F32)
            update(j, s, src)

    def two_steps(jj, carry):
        step(2 * jj, sa_ref, sb_ref)
        step(2 * jj + 1, sb_ref, sa_ref)
        return carry

    lax.fori_loop(0, n_far_groups >> 1, two_steps, 0)

    @pl.when((n_far_groups & 1) == 1)
    def _():
        step(n_far_groups - 1, sa_ref, sb_ref)
        for s in streams:
            update(n_far_groups, s, sb_ref)

    @pl.when((n_far_groups & 1) == 0)
    def _():
        for s in streams:
            update(n_far_groups, s, sa_ref)

    for s in streams:
        o = acc_ref[s] / l_ref[s]
        o = jnp.concatenate([o[:, :TILE], o[:, TILE:]], axis=0)
        o_ref[0, :, lanes_of(s)] = o.T.astype(o_ref.dtype)


def _moba_attention(qk, vt, bias):
    B, S, _ = qk.shape
    n_steps = N_HEADS // 2 // MOBA_STREAMS
    nq = S // TILE
    w = MOBA_STREAMS * LANES
    return pl.pallas_call(
        _moba_body,
        grid=(B, n_steps, nq),
        in_specs=[pl.BlockSpec((1, TILE, w), lambda b, p, i: (b, i, p)),
                  pl.BlockSpec((1, S, w), lambda b, p, i: (b, 0, n_steps + p)),
                  pl.BlockSpec((1, nq, w, TILE), lambda b, p, i: (b, 0, p, 0)),
                  pl.BlockSpec((2, MOBA_STREAMS, TILE, 2 * TILE), lambda b, p, i: (0, p, 0, 0))],
        out_specs=pl.BlockSpec((1, TILE, w), lambda b, p, i: (b, i, p)),
        out_shape=jax.ShapeDtypeStruct((B, S, N_HEADS * HEAD_DIM), BF16),
        scratch_shapes=[pltpu.VMEM((MOBA_STREAMS, nq, LANES), F32),
                        pltpu.VMEM((MOBA_STREAMS, 2 + nq, 2 * TILE), F32),
                        pltpu.VMEM((MOBA_STREAMS, 2, TILE, 2 * TILE), F32),
                        pltpu.VMEM((MOBA_STREAMS, 2, TILE, 2 * TILE), F32),
                        pltpu.VMEM((MOBA_STREAMS, 1, 2 * TILE), F32),
                        pltpu.VMEM((MOBA_STREAMS, 1, 2 * TILE), F32),
                        pltpu.VMEM((MOBA_STREAMS, HEAD_DIM, 2 * TILE), F32)],
        compiler_params=_cparams("parallel", "parallel", "arbitrary"),
        name="moba_attention",
    )(qk, qk, vt, _heads_on_lanes(bias, 2))


def _gelu_tanh(x):
    return 0.5 * x * (1.0 + jnp.tanh(math.sqrt(2.0 / math.pi) * (x + 0.044715 * (x * x * x))))


def _compress_body(t_ref, pos_ref, w1_ref, w2_ref, o_ref):
    groups = t_ref.shape[2]
    half = t_ref.shape[3]
    t = t_ref[0].reshape(NSA_KV_HEADS * groups, half).astype(F32)
    first = jnp.dot((t + pos_ref[0:1, :]).astype(BF16), w1_ref[0:half, :], preferred_element_type=F32)
    second = jnp.dot((t + pos_ref[1:2, :]).astype(BF16), w1_ref[half:2 * half, :],
                     preferred_element_type=F32)
    rows = first.shape[0]
    pre = first + pltpu.roll(second, rows - 1, 0)
    out = jnp.dot(_gelu_tanh(pre).astype(BF16), w2_ref[...], preferred_element_type=F32)
    for h in range(NSA_KV_HEADS):
        o_ref[0, h] = out[h * groups:(h + 1) * groups].astype(o_ref.dtype)


def _compress(t, pos, w1, w2):
    B, Hkv, groups, half = t.shape
    return pl.pallas_call(
        _compress_body,
        grid=(B,),
        in_specs=[pl.BlockSpec((1, Hkv, groups, half), lambda b: (b, 0, 0, 0)),
                  pl.BlockSpec((2, half), lambda b: (0, 0)),
                  pl.BlockSpec((2 * half, CMP_HIDDEN), lambda b: (0, 0)),
                  pl.BlockSpec((CMP_HIDDEN, HEAD_DIM), lambda b: (0, 0))],
        out_specs=pl.BlockSpec((1, Hkv, groups, HEAD_DIM), lambda b: (b, 0, 0, 0)),
        out_shape=jax.ShapeDtypeStruct((B, Hkv, groups, HEAD_DIM), BF16),
        compiler_params=_cparams("parallel"),
        name="nsa_compress",
    )(t, pos.reshape(2, half), w1.astype(BF16), w2.astype(BF16))


def _swap_halves(x):
    return jnp.concatenate([x[:, HEAD_DIM:], x[:, :HEAD_DIM]], axis=1)


def _group_lanes(x):
    return jnp.concatenate([x] * NSA_GROUP, axis=1)


def _nsa_body(q_ref, kc_ref, vct_ref, ks_ref, vst_ref, kw_ref, vwt_ref, gt_ref, bias_ref, c2s_ref,
              o_ref, selneg_ref, radd_ref, sa_ref, sb_ref, sw_ref, oc_ref, os_ref, m_ref, l_ref, acc_ref):
    i = pl.program_id(2)
    nb = ks_ref.shape[1] // TILE
    n_cmp = kc_ref.shape[1]
    n_slc = c2s_ref.shape[0]
    per_tile = TILE // SLC_BLOCK
    cols = NSA_GROUP * TILE
    kv_heads = range(2)
    n_far = jnp.maximum(i - 1, 0)
    n_far_groups = (n_far + 1) >> 1
    last = nb - 1
    t_near = jnp.maximum(i - 1, 0)
    t_edge = jnp.maximum(i - 2, 0)

    dims_of = lambda a: slice(a * HEAD_DIM, (a + 1) * HEAD_DIM)

    def key_tile(k_ref, t):
        return k_ref[0, pl.ds(pl.multiple_of(t * TILE, TILE), TILE), :]

    def scores_of(k_ref, t, a):
        return lax.dot_general(key_tile(k_ref, t), q4s[a], _NT, preferred_element_type=F32)

    lane = lax.broadcasted_iota(I32, (TILE, LANES), 1)
    lo_half = lane < HEAD_DIM
    qpos = i * TILE + (lax.broadcasted_iota(I32, (n_cmp, cols), 1) & (TILE - 1))
    cmp_valid = CMP_STRIDE * lax.broadcasted_iota(I32, (n_cmp, cols), 0) + (CMP_LEN - 1) <= qpos
    key = lax.broadcasted_iota(I32, (TILE, TILE), 0)
    qry = lax.broadcasted_iota(I32, (TILE, TILE), 1)
    diag_neg = _group_lanes(jnp.where(key <= qry, 0.0, NEG_INF))
    edge_neg = _group_lanes(jnp.where(key > qry, 0.0, NEG_INF))
    qall = q_ref[0]
    q4s = []

    for a in kv_heads:
        keep = lo_half if a == 0 else jnp.logical_not(lo_half)
        heads = []
        for g in range(NSA_GROUP):
            cb = a * 2 + g // 2
            x = qall[:, cb * LANES:(cb + 1) * LANES]
            if g % 2 != a:
                x = _swap_halves(x)
            heads.append(jnp.where(keep, x, jnp.zeros_like(x)))
        q4s.append(jnp.concatenate(heads, axis=0))
        far_bias = bias_ref[1, a, 0:1, :]

        sa_ref[a, 0] = scores_of(ks_ref, i, a) + (bias_ref[0, a] + diag_neg)
        sa_ref[a, 1] = scores_of(ks_ref, t_near, a) + bias_ref[1, a]
        sw_ref[a, 0] = scores_of(kw_ref, i, a) + (bias_ref[0, a] + diag_neg)
        sw_ref[a, 1] = scores_of(kw_ref, t_near, a) + bias_ref[1, a]
        sw_ref[a, 2] = scores_of(kw_ref, t_edge, a) + (far_bias + edge_neg)

        s_c = lax.dot_general(kc_ref[0], q4s[a], _NT, preferred_element_type=F32)
        s_c = jnp.where(cmp_valid, s_c, NEG_INF)
        m_c = jnp.max(s_c, axis=0, keepdims=True)
        e_c = jnp.where(cmp_valid, jnp.exp2(s_c - m_c), 0.0)
        l_c = jnp.sum(e_c, axis=0, keepdims=True)
        p_c = e_c / jnp.where(l_c > 0.0, l_c, 1.0)
        oc_ref[a] = jnp.dot(vct_ref[0, dims_of(a), :], p_c.astype(BF16), preferred_element_type=F32)

        p_sum = p_c[:, 0:TILE]
        for g in range(1, NSA_GROUP):
            p_sum = p_sum + p_c[:, g * TILE:(g + 1) * TILE]
        p_hi = p_sum.astype(BF16)
        p_lo = (p_sum - p_hi.astype(F32)).astype(BF16)
        imp = (jnp.dot(c2s_ref[...], p_hi, preferred_element_type=F32)
               + jnp.dot(c2s_ref[...], p_lo, preferred_element_type=F32))
        j = lax.broadcasted_iota(I32, imp.shape, 0)
        qb = (i * TILE + lax.broadcasted_iota(I32, imp.shape, 1)) >> int(math.log2(SLC_BLOCK))
        forced = (j == 0) | ((j <= qb) & (j > qb - SLC_LOCAL))
        imp = jnp.where(forced, jnp.inf, jnp.where(j > qb, -jnp.inf, imp))
        rank = _rank_before(imp, n_slc)
        selneg = jnp.where((rank < SLC_TOPN) & (j <= qb), 0.0, NEG_INF)
        selneg_ref[a] = selneg
        for c in range(per_tile):
            radd_ref[a, c:c + 1, :] = _group_lanes(selneg_ref[a, pl.ds(per_tile * i + c, 1), :])
            near_row = _group_lanes(selneg_ref[a, pl.ds(per_tile * t_near + c, 1), :])
            radd_ref[a, per_tile + c:per_tile + c + 1, :] = jnp.where(i >= 1, near_row, NEG_INF)
        for blk in range(n_slc):
            row = far_bias + _group_lanes(selneg[blk:blk + 1, :])
            r = 2 * per_tile + blk
            radd_ref[a, r:r + 1, :] = jnp.where(blk // per_tile < n_far, row, NEG_INF)
        _init_state(m_ref.at[a], l_ref.at[a], acc_ref.at[a])

    def update(jg, a, buf):
        first = jg == 0
        t0 = jnp.where(first, i, 2 * jg - 2)
        t1 = jnp.where(first, t_near, jnp.minimum(2 * jg - 1, last))
        scores, adds = [], []
        for t in range(2):
            for c in range(per_tile):
                scores.append(buf[a, t, c * SLC_BLOCK:(c + 1) * SLC_BLOCK, :])
                adds.append(radd_ref[a, pl.ds(2 * per_tile * jg + per_tile * t + c, 1), :])
        _softmax_pv(scores, adds, [vst_ref[0, t0], vst_ref[0, t1]], [(dims_of(a), slice(None))],
                    m_ref.at[a], l_ref.at[a], acc_ref.at[a])

    def step(jg, src, dst):
        for a in kv_heads:
            dst[a, 0] = scores_of(ks_ref, 2 * jg, a)
            dst[a, 1] = scores_of(ks_ref, jnp.minimum(2 * jg + 1, last), a)
            update(jg, a, src)

    def two_steps(jj, carry):
        step(2 * jj, sa_ref, sb_ref)
        step(2 * jj + 1, sb_ref, sa_ref)
        return carry

    lax.fori_loop(0, n_far_groups >> 1, two_steps, 0)

    @pl.when((n_far_groups & 1) == 1)
    def _():
        step(n_far_groups - 1, sa_ref, sb_ref)
        for a in kv_heads:
            update(n_far_groups, a, sb_ref)

    @pl.when((n_far_groups & 1) == 0)
    def _():
        for a in kv_heads:
            update(n_far_groups, a, sa_ref)

    gates = jax.nn.sigmoid(gt_ref[0, 0])
    zero_row = jnp.zeros((1, cols), F32)
    pieces = []
    for a in kv_heads:
        os_ref[a] = acc_ref[a] / l_ref[a]
        _init_state(m_ref.at[a], l_ref.at[a], acc_ref.at[a])
        _softmax_pv([sw_ref[a, 0], sw_ref[a, 1], sw_ref[a, 2]],
                    [zero_row, zero_row + jnp.where(i >= 1, 0.0, NEG_INF), zero_row + jnp.where(i >= 2, 0.0, NEG_INF)],
                    [vwt_ref[0, i], vwt_ref[0, t_near], vwt_ref[0, t_edge]], [(dims_of(a), slice(None))],
                    m_ref.at[a], l_ref.at[a], acc_ref.at[a])
        o_w = acc_ref[a] / l_ref[a]
        for g in range(NSA_GROUP):
            c0 = 3 * (NSA_GROUP * a + g)
            ls = slice(g * TILE, (g + 1) * TILE)
            pieces.append(gates[c0:c0 + 1, :] * oc_ref[a, :, ls] + gates[c0 + 1:c0 + 2, :] * os_ref[a, :, ls]
                          + gates[c0 + 2:c0 + 3, :] * o_w[:, ls])
    o_ref[0] = jnp.concatenate(pieces, axis=0).T.astype(o_ref.dtype)


def _cmp_to_slc(S):
    n_cmp_pad = S // CMP_STRIDE
    n_slc = S // SLC_BLOCK
    ci = np.arange(n_cmp_pad)[:, None] * CMP_STRIDE
    sj = np.arange(n_slc)[None, :] * SLC_BLOCK
    c2s = ((ci < sj + SLC_BLOCK) & (ci + CMP_LEN > sj)).astype(np.float32)
    c2s[(S - CMP_LEN) // CMP_STRIDE + 1:] = 0.0
    return jnp.asarray(c2s.T, BF16)


def _nsa_attention(proj, vt, gate_t, kcmp, vcmp_t, bias):
    B, S, _ = proj.shape
    nq = S // TILE
    n_cmp = kcmp.shape[1]
    n_slc = S // SLC_BLOCK
    qw = 2 * NSA_GROUP * HEAD_DIM
    q_blocks = N_HEADS * HEAD_DIM // LANES
    kv_blocks = NSA_KV_HEADS * HEAD_DIM // LANES

    def k_spec(which):
        base = q_blocks + which * kv_blocks
        return pl.BlockSpec((1, S, LANES), lambda b, p, i: (b, 0, base + p))

    def vt_spec(which):
        base = which * kv_blocks
        return pl.BlockSpec((1, nq, LANES, TILE), lambda b, p, i: (b, 0, base + p, 0))

    state = pltpu.VMEM((2, HEAD_DIM, NSA_GROUP * TILE), F32)
    stat = pltpu.VMEM((2, 1, NSA_GROUP * TILE), F32)
    return pl.pallas_call(
        _nsa_body,
        grid=(B, 2, nq),
        in_specs=[pl.BlockSpec((1, TILE, qw), lambda b, p, i: (b, i, p)),
                  pl.BlockSpec((1, n_cmp, LANES), lambda b, p, i: (b, 0, p)),
                  pl.BlockSpec((1, LANES, n_cmp), lambda b, p, i: (b, p, 0)),
                  k_spec(2), vt_spec(0), k_spec(3), vt_spec(1),
                  pl.BlockSpec((1, 1, LANES, TILE), lambda b, p, i: (b, i, p, 0)),
                  pl.BlockSpec((2, 2, TILE, NSA_GROUP * TILE), lambda b, p, i: (0, p, 0, 0)),
                  pl.BlockSpec((n_slc, n_cmp), lambda b, p, i: (0, 0))],
        out_specs=pl.BlockSpec((1, TILE, qw), lambda b, p, i: (b, i, p)),
        out_shape=jax.ShapeDtypeStruct((B, S, N_HEADS * HEAD_DIM), BF16),
        scratch_shapes=[pltpu.VMEM((2, n_slc, TILE), F32),
                        pltpu.VMEM((2, 2 * (TILE // SLC_BLOCK) + n_slc, NSA_GROUP * TILE), F32),
                        pltpu.VMEM((2, 2, TILE, NSA_GROUP * TILE), F32),
                        pltpu.VMEM((2, 2, TILE, NSA_GROUP * TILE), F32),
                        pltpu.VMEM((2, 3, TILE, NSA_GROUP * TILE), F32),
                        state, state, stat, stat, state],
        compiler_params=_cparams("parallel", "parallel", "arbitrary"),
        name="nsa_attention",
    )(proj, kcmp, vcmp_t, proj, vt, proj, vt, gate_t, _heads_on_lanes(bias, NSA_GROUP), _cmp_to_slc(S))


def _split_bf16(x):
    hi = x.astype(BF16)
    return hi, (x - hi.astype(F32)).astype(BF16)


_ROW_OF_EXPERT = np.arange(N_EXPERTS).reshape(N_GROUPS, EXPERTS_PER_GROUP).T.reshape(-1)


def _route(x, w_ref, b_ref, tri_ref, idx_ref, wt_ref, pos_ref, cnt_ref, base_ref):
    @pl.when(pl.program_id(0) == 0)
    def _():
        base_ref[...] = jnp.zeros(base_ref.shape, F32)

    x_hi, x_lo = _split_bf16(x)
    w_hi, w_lo = _split_bf16(w_ref[...])
    logits = (lax.dot_general(w_hi, x_hi, _NT, preferred_element_type=F32)
              + lax.dot_general(w_hi, x_lo, _NT, preferred_element_type=F32)
              + lax.dot_general(w_lo, x_hi, _NT, preferred_element_type=F32)) + b_ref[:, 0:1]
    m = jnp.max(logits, axis=0, keepdims=True)
    e = jnp.exp(logits - m)
    probs = e / jnp.sum(e, axis=0, keepdims=True)
    pk = [probs[k * N_GROUPS:(k + 1) * N_GROUPS] for k in range(EXPERTS_PER_GROUP)]
    hi1, lo1 = jnp.maximum(pk[0], pk[1]), jnp.minimum(pk[0], pk[1])
    hi2, lo2 = jnp.maximum(pk[2], pk[3]), jnp.minimum(pk[2], pk[3])
    score = jnp.maximum(hi1, hi2) + jnp.maximum(jnp.minimum(hi1, hi2), jnp.maximum(lo1, lo2))
    grp = lax.broadcasted_iota(I32, score.shape, 0)
    best = jnp.min(jnp.where(score == jnp.max(score, axis=0, keepdims=True), grp, N_GROUPS),
                   axis=0, keepdims=True)
    v = [jnp.sum(jnp.where(grp == best, p, 0.0), axis=0, keepdims=True) for p in pk]
    v1 = jnp.maximum(jnp.maximum(v[0], v[1]), jnp.maximum(v[2], v[3]))
    i1 = jnp.where(v[0] == v1, 0, jnp.where(v[1] == v1, 1, jnp.where(v[2] == v1, 2, 3)))
    rest = [jnp.where(i1 == k, -1.0, v[k]) for k in range(EXPERTS_PER_GROUP)]
    v2 = jnp.maximum(jnp.maximum(rest[0], rest[1]), jnp.maximum(rest[2], rest[3]))
    i2 = jnp.where(rest[0] == v2, 0, jnp.where(rest[1] == v2, 1, jnp.where(rest[2] == v2, 2, 3)))
    tot = v1 + v2
    idx_ref[...] = jnp.concatenate([best * EXPERTS_PER_GROUP + i1, best * EXPERTS_PER_GROUP + i2], axis=0)
    wt_ref[...] = jnp.concatenate([v1 / tot, v2 / tot], axis=0)

    row = lax.broadcasted_iota(I32, logits.shape, 0)
    hot = [jnp.where(row == ik * N_GROUPS + best, 1.0, 0.0) for ik in (i1, i2)]
    both = (hot[0] + hot[1]).astype(BF16)
    run = base_ref[:, 0:1]
    pos = [[], []]
    for c in range(logits.shape[1] // LANES):
        ls = slice(c * LANES, (c + 1) * LANES)
        before = run + jnp.dot(both[:, ls], tri_ref[...], preferred_element_type=F32) - 1.0
        for k in range(2):
            pos[k].append(jnp.sum(hot[k][:, ls] * before, axis=0, keepdims=True))
        run = before[:, LANES - 1:LANES] + 1.0
    pos_ref[...] = jnp.concatenate([jnp.concatenate(pos[0], axis=1), jnp.concatenate(pos[1], axis=1)],
                                   axis=0).astype(I32)
    base_ref[...] = jnp.broadcast_to(run, base_ref.shape)
    cnt_ref[...] = jnp.broadcast_to(run, cnt_ref.shape)


def _router_operands(router_w, router_b):
    w = router_w.T[_ROW_OF_EXPERT]
    b = jnp.broadcast_to(router_b[_ROW_OF_EXPERT][:, None], (N_EXPERTS, LANES))
    tri = jnp.asarray(np.triu(np.ones((LANES, LANES), np.float32)), BF16)
    return w, b, tri


def _proj_ln_route_body(a_ref, w_ref, x_ref, g_ref, b_ref, rw_ref, rb_ref, tri_ref,
                        o_ref, ob_ref, idx_ref, wt_ref, pos_ref, cnt_ref, base_ref):
    y = jnp.dot(a_ref[...], w_ref[...], preferred_element_type=F32)
    out = _layer_norm_rows(DEEPNORM_ALPHA * x_ref[...] + y, g_ref[...], b_ref[...])
    o_ref[...] = out
    ob_ref[...] = out.astype(BF16)
    _route(out, rw_ref, rb_ref, tri_ref, idx_ref, wt_ref, pos_ref, cnt_ref, base_ref)


def _proj_ln_route(a, w, x, g, b, router_w, router_b):
    M, K = a.shape
    D = w.shape[1]
    row = pl.BlockSpec((LN_TM, D), lambda i: (i, 0))
    vec = pl.BlockSpec((1, D), lambda i: (0, 0))
    whole = lambda shape: pl.BlockSpec(shape, lambda i: (0, 0))
    tok = lambda dt: jax.ShapeDtypeStruct((2, M), dt)
    tok_spec = pl.BlockSpec((2, LN_TM), lambda i: (0, i))
    h, hb, idx, wts, pos, cnt = pl.pallas_call(
        _proj_ln_route_body,
        grid=(M // LN_TM,),
        in_specs=[pl.BlockSpec((LN_TM, K), lambda i: (i, 0)), whole((K, D)), row, vec, vec,
                  whole((N_EXPERTS, D)), whole((N_EXPERTS, LANES)), whole((LANES, LANES))],
        out_specs=[row, row, tok_spec, tok_spec, tok_spec, whole((N_EXPERTS, LANES))],
        out_shape=[jax.ShapeDtypeStruct((M, D), F32), jax.ShapeDtypeStruct((M, D), BF16),
                   tok(I32), tok(F32), tok(I32), jax.ShapeDtypeStruct((N_EXPERTS, LANES), F32)],
        scratch_shapes=[pltpu.VMEM((N_EXPERTS, LANES), F32)],
        compiler_params=_cparams("arbitrary"),
        name="out_proj_ln_route",
    )(a, w, x, g.reshape(1, D), b.reshape(1, D), *_router_operands(router_w, router_b))
    counts = cnt[np.argsort(_ROW_OF_EXPERT), 0].astype(I32)
    return h, hb, (idx, wts, pos, counts)


def _expert_body(blk_e_ref, n_used_ref, x_ref, wg_ref, wu_ref, wd_ref, o_ref, wg_b, wu_b, wd_b):
    i = pl.program_id(0)

    @pl.when((i == 0) | (blk_e_ref[i] != blk_e_ref[jnp.maximum(i - 1, 0)]))
    def _():
        wg_b[...] = wg_ref[0, 0].astype(BF16)
        wu_b[...] = wu_ref[0, 0].astype(BF16)
        wd_b[...] = wd_ref[0, 0].astype(BF16)

    @pl.when(i < n_used_ref[0])
    def _():
        x = x_ref[...]
        gate = jnp.dot(x, wg_b[...], preferred_element_type=F32)
        up = jnp.dot(x, wu_b[...], preferred_element_type=F32)
        hid = (gate * jax.nn.sigmoid(gate) * up).astype(BF16)
        o_ref[...] = jnp.dot(hid, wd_b[...], preferred_element_type=F32).astype(o_ref.dtype)

    @pl.when(i >= n_used_ref[0])
    def _():
        o_ref[...] = jnp.zeros(o_ref.shape, o_ref.dtype)


def _experts(xs, blk_e, n_used, wg, wu, wd, layer):
    R, D = xs.shape
    n_blk = R // MOE_TB

    def live(i, be, nu):
        return jnp.minimum(i, nu[0] - 1)

    grid_spec = pltpu.PrefetchScalarGridSpec(
        num_scalar_prefetch=2,
        grid=(n_blk,),
        in_specs=[pl.BlockSpec((MOE_TB, D), lambda i, be, nu: (live(i, be, nu), 0)),
                  pl.BlockSpec((1, 1, D, D_EXPERT), lambda i, be, nu: (layer, be[i], 0, 0)),
                  pl.BlockSpec((1, 1, D, D_EXPERT), lambda i, be, nu: (layer, be[i], 0, 0)),
                  pl.BlockSpec((1, 1, D_EXPERT, D), lambda i, be, nu: (layer, be[i], 0, 0))],
        out_specs=pl.BlockSpec((MOE_TB, D), lambda i, be, nu: (i, 0)),
        scratch_shapes=[pltpu.VMEM((D, D_EXPERT), BF16), pltpu.VMEM((D, D_EXPERT), BF16),
                        pltpu.VMEM((D_EXPERT, D), BF16)],
    )
    return pl.pallas_call(
        _expert_body,
        grid_spec=grid_spec,
        out_shape=jax.ShapeDtypeStruct((R, D), BF16),
        compiler_params=_cparams("arbitrary"),
        name="moe_experts",
    )(blk_e, n_used, xs, wg, wu, wd)


def _combine_ln_body(x_ref, y0_ref, y1_ref, wt_ref, g_ref, b_ref, o_ref, ob_ref):
    ffn = y0_ref[...] * wt_ref[:, 0:1] + y1_ref[...] * wt_ref[:, HEAD_DIM:HEAD_DIM + 1]
    out = _layer_norm_rows(DEEPNORM_ALPHA * x_ref[...] + ffn, g_ref[...], b_ref[...])
    o_ref[...] = out
    ob_ref[...] = out.astype(BF16)


def _combine_ln(x, y0, y1, wt, g, b):
    M, D = x.shape
    row = pl.BlockSpec((LN_TM, D), lambda i: (i, 0))
    vec = pl.BlockSpec((1, D), lambda i: (0, 0))
    return pl.pallas_call(
        _combine_ln_body,
        grid=(M // LN_TM,),
        in_specs=[row, row, row, pl.BlockSpec((LN_TM, LANES), lambda i: (i, 0)), vec, vec],
        out_specs=[row, row],
        out_shape=[jax.ShapeDtypeStruct((M, D), F32), jax.ShapeDtypeStruct((M, D), BF16)],
        compiler_params=_cparams("parallel"),
        name="moe_combine_ln",
    )(x, y0, y1, wt, g.reshape(1, D), b.reshape(1, D))


def _moe_ln(h, hb, routing, wg, wu, wd, layer, g, b):
    N, D = h.shape
    A = 2 * N
    idx, wts, pos, counts = routing
    starts = jnp.cumsum(counts) - counts
    padded = (counts + MOE_TB - 1) // MOE_TB * MOE_TB
    pends = jnp.cumsum(padded)
    pstarts = pends - padded
    R = A + N_EXPERTS * MOE_TB
    n_blk = R // MOE_TB
    experts = jnp.arange(N_EXPERTS, dtype=I32)
    dest = pos + jnp.sum(jnp.where(idx[None] == experts[:, None, None], pstarts[:, None, None], 0), axis=0)
    tok = jnp.broadcast_to(jnp.arange(N, dtype=I32)[None, :], (2, N))
    _, tok_sorted = lax.sort_key_val(dest.reshape(A), tok.reshape(A))
    blk_row0 = jnp.arange(n_blk, dtype=I32) * MOE_TB
    blk_e = jnp.minimum(jnp.sum((pends[None, :] <= blk_row0[:, None]).astype(I32), axis=1), N_EXPERTS - 1)
    hot = blk_e[:, None] == experts[None, :]
    compact0 = blk_row0 + jnp.sum(jnp.where(hot, (starts - pstarts)[None, :], 0), axis=1)
    compact = jnp.remainder(compact0[:, None] + jnp.arange(MOE_TB, dtype=I32)[None, :], A).reshape(R)
    n_used = (pends[-1:] // MOE_TB).astype(I32)
    xs = hb[tok_sorted[compact]]
    yb = _experts(xs, blk_e, n_used, wg, wu, wd, layer)
    wt = jnp.concatenate([jnp.broadcast_to(wts[k][:, None], (N, HEAD_DIM)) for k in range(2)], axis=1)
    return _combine_ln(h, yb[dest[0]], yb[dest[1]], wt, g, b)


def _moba_layer(h, w_in, w_out, bias, g, b, router_w, router_b, B, S):
    HD = N_HEADS * HEAD_DIM
    qk, vt = _in_proj(h, w_in[:, :2 * HD].astype(BF16), _query_scale(HD, 2 * HD),
                      [w_in[:, 2 * HD:].T.astype(BF16)], [BF16], B, S)
    att = _moba_attention(qk.reshape(B, S, 2 * HD), vt, bias)
    return _proj_ln_route(att.reshape(B * S, HD), w_out.astype(BF16), h, g, b, router_w, router_b)


def _nsa_layer(h, hb, w_in, w_out, pos_k, pos_v, ck_w1, ck_w2, cv_w1, cv_w2, bias, g, b, router_w, router_b, B, S):
    HD = N_HEADS * HEAD_DIM
    kvw = NSA_KV_HEADS * HEAD_DIM
    col = lambda k: slice(HD + k * kvw, HD + (k + 1) * kvw)
    w_rows = jnp.concatenate([w_in[:, :HD + 2 * kvw], w_in[:, col(2)], w_in[:, col(4)]], axis=1)
    w_vt = jnp.concatenate([w_in[:, col(3)], w_in[:, col(5)]], axis=1).T
    per_pair = 3 * N_HEADS // 2
    wg = w_in[:, HD + 6 * kvw:].reshape(D_MODEL, 2, per_pair)
    wg = jnp.pad(wg, ((0, 0), (0, 0), (0, LANES - per_pair))).reshape(D_MODEL, 2 * LANES).T
    proj, vt, gate_t = _in_proj(hb, w_rows.astype(BF16), _query_scale(HD, HD + 4 * kvw),
                                [w_vt.astype(BF16), wg.astype(BF16)], [BF16, F32], B, S)
    proj = proj.reshape(B, S, HD + 4 * kvw)

    def grouped(t):
        t = t.reshape(B, S, NSA_KV_HEADS, HEAD_DIM).transpose(0, 2, 1, 3)
        return t.reshape(B, NSA_KV_HEADS, S // CMP_STRIDE, CMP_STRIDE * HEAD_DIM)

    kcmp = _compress(grouped(proj[..., HD:HD + kvw]), pos_k, ck_w1, ck_w2)
    vcmp = _compress(grouped(proj[..., HD + kvw:HD + 2 * kvw]), pos_v, cv_w1, cv_w2)
    n_cmp = kcmp.shape[2]
    kcmp = kcmp.transpose(0, 2, 1, 3).reshape(B, n_cmp, kvw)
    vcmp_t = vcmp.transpose(0, 1, 3, 2).reshape(B, kvw, n_cmp)
    att = _nsa_attention(proj, vt, gate_t, kcmp, vcmp_t, bias)
    return _proj_ln_route(att.reshape(B * S, HD), w_out.astype(BF16), h, g, b, router_w, router_b)


N_CHAINS = 2


def kernel(x, rel_bias, router_w, router_b, ln_g, ln_b, moba_w_in, moba_w_out, nsa_w_in, nsa_w_out,
           nsa_pos_k, nsa_pos_v, nsa_ck_w1, nsa_ck_w2, nsa_cv_w1, nsa_cv_w2,
           moe_w_gate, moe_w_up, moe_w_down):
    B, S, D = x.shape
    bias = _bias_tiles(rel_bias)
    Bc = B // N_CHAINS

    def trunk(xc):
        h = xc.reshape(Bc * S, D)
        h, hb, routing = _moba_layer(h, moba_w_in[0], moba_w_out[0], bias, ln_g[0, 0], ln_b[0, 0],
                                     router_w, router_b, Bc, S)
        h, hb = _moe_ln(h, hb, routing, moe_w_gate, moe_w_up, moe_w_down, 0, ln_g[0, 1], ln_b[0, 1])
        h, hb, routing = _nsa_layer(h, hb, nsa_w_in[0], nsa_w_out[0], nsa_pos_k[0], nsa_pos_v[0],
                                    nsa_ck_w1[0], nsa_ck_w2[0], nsa_cv_w1[0], nsa_cv_w2[0],
                                    bias, ln_g[1, 0], ln_b[1, 0], router_w, router_b, Bc, S)
        h, hb = _moe_ln(h, hb, routing, moe_w_gate, moe_w_up, moe_w_down, 1, ln_g[1, 1], ln_b[1, 1])
        return h.reshape(Bc, S, D)

    return jnp.concatenate([trunk(x[c * Bc:(c + 1) * Bc]) for c in range(N_CHAINS)], axis=0)
```

```python
import math
from functools import partial

import numpy as np
import jax
import jax.numpy as jnp
from jax import lax
from jax.experimental import pallas as pl
from jax.experimental.pallas import tpu as pltpu

F32, BF16, I32 = jnp.float32, jnp.bfloat16, jnp.int32

D_MODEL = 1024
N_HEADS = 16
HEAD_DIM = 64
DEPTH = 2
NEG_INF = -1e30
LN_EPS = 1e-5
MOBA_BLOCK = 256
MOBA_TOPK = 3
NSA_KV_HEADS = 4
NSA_GROUP = N_HEADS // NSA_KV_HEADS
CMP_LEN = 32
CMP_STRIDE = 16
CMP_HIDDEN = 256
SLC_BLOCK = 64
SLC_TOPN = 16
SLC_LOCAL = 2
WINDOW = 512
REL_BUCKETS = 32
REL_MAX_DIST = 128
N_EXPERTS = 32
N_GROUPS = 8
EXPERTS_PER_GROUP = N_EXPERTS // N_GROUPS
D_EXPERT = 512
DEEPNORM_ALPHA = (2 * DEPTH) ** 0.25
LOG2E = math.log2(math.e)
Q_SCALE = HEAD_DIM ** -0.5 * LOG2E

LANES = 128
SUBLANES = 8
ONES_ROWS = 16
TILE = MOBA_BLOCK
MM_TM = 512
MM_TN = 1024
LN_TM = 1024
MOE_TB = 512
assert WINDOW == 2 * TILE and TILE % SLC_BLOCK == 0 and SLC_BLOCK % SUBLANES == 0 and MM_TM % TILE == 0
VMEM_LIMIT = 48 * 1024 * 1024

_NT = (((1,), (1,)), ((), ()))


def _cparams(*sem):
    return pltpu.CompilerParams(dimension_semantics=sem, vmem_limit_bytes=VMEM_LIMIT)


def _in_proj_body(n_t, a_ref, w_ref, c_ref, *refs):
    wt_refs, o_ref, ot_refs = refs[:n_t], refs[n_t], refs[n_t + 1:]
    a = a_ref[...].astype(BF16)
    for c in range(o_ref.shape[1] // MM_TN):
        cols = slice(c * MM_TN, (c + 1) * MM_TN)
        acc = jnp.dot(a, w_ref[:, cols], preferred_element_type=F32)
        o_ref[:, cols] = (acc * c_ref[:, cols]).astype(o_ref.dtype)
    for wt_ref, ot_ref in zip(wt_refs, ot_refs):
        r = lax.dot_general(wt_ref[...], a, _NT, preferred_element_type=F32)
        for t in range(ot_ref.shape[1]):
            ot_ref[0, t] = r[:, t * TILE:(t + 1) * TILE].astype(ot_ref.dtype)


def _in_proj(a, w, col_scale, w_ts, t_dtypes, B, S):
    M, K = a.shape
    N = w.shape[1]
    assert M == B * S and S % MM_TM == 0 and N % MM_TN == 0
    per_seq = S // MM_TM
    sub = MM_TM // TILE
    whole = lambda shape: pl.BlockSpec(shape, lambda i: (0, 0))
    outs = pl.pallas_call(
        partial(_in_proj_body, len(w_ts)),
        grid=(M // MM_TM,),
        in_specs=[pl.BlockSpec((MM_TM, K), lambda i: (i, 0)), whole((K, N)), whole((1, N))]
                 + [whole(w_t.shape) for w_t in w_ts],
        out_specs=[pl.BlockSpec((MM_TM, N), lambda i: (i, 0))]
                  + [pl.BlockSpec((1, sub, w_t.shape[0], TILE), lambda i: (i // per_seq, i % per_seq, 0, 0))
                     for w_t in w_ts],
        out_shape=[jax.ShapeDtypeStruct((M, N), BF16)]
                  + [jax.ShapeDtypeStruct((B, S // TILE, w_t.shape[0], TILE), dt) for w_t, dt in zip(w_ts, t_dtypes)],
        compiler_params=_cparams("parallel"),
        name="in_proj",
    )(a, w, col_scale.reshape(1, N), *w_ts)
    return outs


def _query_scale(n_query_cols, n_cols):
    return jnp.where(jnp.arange(n_cols) < n_query_cols, Q_SCALE, 1.0).astype(F32)


def _layer_norm_rows(z, g, b):
    mu = jnp.mean(z, axis=-1, keepdims=True)
    zc = z - mu
    var = jnp.mean(zc * zc, axis=-1, keepdims=True)
    return zc * lax.rsqrt(var + LN_EPS) * g + b


def _t5_bucket_np(rel):
    n = np.maximum(rel, 0)
    max_exact = REL_BUCKETS // 2
    nf = np.maximum(n, 1).astype(np.float32)
    large = max_exact + (np.log(nf / np.float32(max_exact))
                         / np.float32(math.log(REL_MAX_DIST / max_exact))
                         * np.float32(REL_BUCKETS - max_exact)).astype(np.int32)
    large = np.minimum(large, REL_BUCKETS - 1)
    return np.where(n < max_exact, n, large).astype(np.int32)


def _bias_body(tbl_ref, bk_ref, o_ref):
    h = pl.program_id(0)
    for dl in range(2):
        bk = bk_ref[dl]
        acc = jnp.zeros((TILE, TILE), F32)
        for b in range(REL_BUCKETS):
            acc = jnp.where(bk == b, tbl_ref[h * REL_BUCKETS + b], acc)
        o_ref[dl, 0] = acc * LOG2E


def _bias_tiles(rel_bias):
    key = np.arange(TILE)[:, None]
    qry = np.arange(TILE)[None, :]
    assert int(_t5_bucket_np(np.array(TILE + 1))) == REL_BUCKETS - 1
    bk = np.stack([_t5_bucket_np(qry - key), _t5_bucket_np(TILE + qry - key)])
    return pl.pallas_call(
        _bias_body,
        grid=(N_HEADS,),
        in_specs=[pl.BlockSpec(memory_space=pltpu.SMEM),
                  pl.BlockSpec((2, TILE, TILE), lambda h: (0, 0, 0))],
        out_specs=pl.BlockSpec((2, 1, TILE, TILE), lambda h: (0, h, 0, 0)),
        out_shape=jax.ShapeDtypeStruct((2, N_HEADS, TILE, TILE), F32),
        name="t5_bias_tiles",
    )(rel_bias.T.reshape(-1), jnp.asarray(bk))


def _heads_on_lanes(bias, per_block):
    two, H, T, _ = bias.shape
    b = bias.reshape(two, H // per_block, per_block, T, T).transpose(0, 1, 3, 2, 4)
    return b.reshape(two, H // per_block, T, per_block * T)


def _init_state(m_ref, l_ref, acc_ref):
    m_ref[...] = jnp.full(m_ref.shape, NEG_INF, F32)
    l_ref[...] = jnp.zeros(l_ref.shape, F32)
    acc_ref[...] = jnp.zeros(acc_ref.shape, F32)


def _rank_before(vals, rows):
    idx = lax.broadcasted_iota(I32, vals.shape, 0)
    rank = jnp.zeros(vals.shape, I32)
    for m in range(rows):
        row = vals[m:m + 1, :]
        beats = (row > vals) | ((row == vals) & (idx > m))
        rank = rank + jnp.where(beats, 1, 0)
    return rank


MOBA_STREAMS = 4


def _softmax_pv(scores, adds, vts, heads, m_ref, l_ref, acc_ref):
    def fold(x, op):
        return op(x.reshape(x.shape[0] // SUBLANES, SUBLANES, x.shape[1]), axis=0)

    m_prev = m_ref[...]
    m_part = None
    for s, add in zip(scores, adds):
        part = fold(s, jnp.max) + add
        m_part = part if m_part is None else jnp.maximum(m_part, part)
    m_new = jnp.maximum(m_prev, jnp.max(m_part, axis=0, keepdims=True))
    a = jnp.exp2(m_prev - m_new)
    probs = [jnp.exp2(s - (m_new - add)) for s, add in zip(scores, adds)]
    vt = jnp.concatenate(vts, axis=1)
    pb = jnp.concatenate([p.astype(BF16) for p in probs], axis=0)
    ones = jnp.ones((ONES_ROWS, vt.shape[1]), BF16)
    pv = jnp.concatenate([jnp.dot(jnp.concatenate([vt[rows], ones], axis=0), pb[:, cols],
                                  preferred_element_type=F32) for rows, cols in heads], axis=1)
    l_ref[...] = a * l_ref[...] + pv[HEAD_DIM:HEAD_DIM + 1, :]
    acc_ref[...] = a * acc_ref[...] + pv[:HEAD_DIM, :]
    m_ref[...] = m_new


def _moba_body(q_ref, k_ref, vt_ref, bias_ref, o_ref, kmean_ref, radd_ref, sa_ref, sb_ref, m_ref, l_ref, acc_ref):
    i = pl.program_id(2)
    nb = k_ref.shape[1] // TILE
    streams = range(MOBA_STREAMS)
    lanes_of = lambda s: slice(s * LANES, (s + 1) * LANES)

    @pl.when(i == 0)
    def _():
        for s in streams:
            for n in range(nb):
                kb = k_ref[0, n * TILE:(n + 1) * TILE, lanes_of(s)].astype(F32)
                kmean_ref[s, n:n + 1, :] = jnp.sum(kb, axis=0, keepdims=True) * (1.0 / TILE)

    n_far = jnp.maximum(i - 1, 0)
    n_far_groups = (n_far + 1) >> 1
    last = nb - 1

    def key_tile(t, s):
        return k_ref[0, pl.ds(pl.multiple_of(t * TILE, TILE), TILE), lanes_of(s)]

    def far_tiles(j):
        return 2 * j, jnp.minimum(2 * j + 1, last)

    lane = lax.broadcasted_iota(I32, (TILE, LANES), 1)
    key = lax.broadcasted_iota(I32, (TILE, 2 * TILE), 0)
    qry = lax.broadcasted_iota(I32, (TILE, 2 * TILE), 1) & (TILE - 1)
    causal_neg = jnp.where(key <= qry, 0.0, NEG_INF)
    t_near = jnp.maximum(i - 1, 0)
    q2s = []
    for s in streams:
        q = q_ref[0, :, lanes_of(s)]
        zero = jnp.zeros_like(q)
        q2 = jnp.concatenate([jnp.where(lane < HEAD_DIM, q, zero),
                              jnp.where(lane >= HEAD_DIM, q, zero)], axis=0)
        q2s.append(q2)
        sa_ref[s, 0] = (lax.dot_general(key_tile(i, s), q2, _NT, preferred_element_type=F32)
                        + (bias_ref[0, s] + causal_neg))
        sa_ref[s, 1] = lax.dot_general(key_tile(t_near, s), q2, _NT, preferred_element_type=F32) + bias_ref[1, s]
        km = kmean_ref[s]
        k_hi = km.astype(BF16)
        k_lo = (km - k_hi.astype(F32)).astype(BF16)
        gate = (lax.dot_general(k_hi, q2, _NT, preferred_element_type=F32)
                + lax.dot_general(k_lo, q2, _NT, preferred_element_type=F32))
        blk = lax.broadcasted_iota(I32, gate.shape, 0)
        gate = jnp.where(blk < i, gate, -jnp.inf)
        rank = _rank_before(gate, nb)
        neg = jnp.where((rank < MOBA_TOPK) & (blk < i), 0.0, NEG_INF)
        far_bias = bias_ref[1, s, 0:1, :]
        near_row = jnp.full((1, 2 * TILE), NEG_INF, F32)
        radd_ref[s, 0:1, :] = jnp.zeros((1, 2 * TILE), F32)
        for n in range(nb):
            row = neg[n:n + 1, :]
            near_row = jnp.where(n == i - 1, row, near_row)
            radd_ref[s, 2 + n:3 + n, :] = jnp.where(n < n_far, far_bias + row, NEG_INF)
        radd_ref[s, 1:2, :] = near_row
        _init_state(m_ref.at[s], l_ref.at[s], acc_ref.at[s])

    pair_heads = [(slice(h * HEAD_DIM, (h + 1) * HEAD_DIM), slice(h * TILE, (h + 1) * TILE)) for h in range(2)]

    def update(j, s, buf):
        first = j == 0
        t0 = jnp.where(first, i, 2 * j - 2)
        t1 = jnp.where(first, t_near, jnp.minimum(2 * j - 1, last))
        _softmax_pv([buf[s, 0], buf[s, 1]],
                    [radd_ref[s, pl.ds(2 * j, 1), :], radd_ref[s, pl.ds(2 * j + 1, 1), :]],
                    [vt_ref[0, t0, lanes_of(s), :], vt_ref[0, t1, lanes_of(s), :]], pair_heads,
                    m_ref.at[s], l_ref.at[s], acc_ref.at[s])

    def step(j, src, dst):
        for s in streams:
            ta, tb = far_tiles(j)
            dst[s, 0] = lax.dot_general(key_tile(ta, s), q2s[s], _NT, preferred_element_type=F32)
            dst[s, 1] = lax.dot_general(key_tile(tb, s), q2s[s], _NT, preferred_element_type=F32)
            update(j, s, src)

    def two_steps(jj, carry):
        step(2 * jj, sa_ref, sb_ref)
        step(2 * jj + 1, sb_ref, sa_ref)
        return carry

    lax.fori_loop(0, n_far_groups >> 1, two_steps, 0)

    @pl.when((n_far_groups & 1) == 1)
    def _():
        step(n_far_groups - 1, sa_ref, sb_ref)
        for s in streams:
            update(n_far_groups, s, sb_ref)

    @pl.when((n_far_groups & 1) == 0)
    def _():
        for s in streams:
            update(n_far_groups, s, sa_ref)

    for s in streams:
        o = acc_ref[s] / l_ref[s]
        o = jnp.concatenate([o[:, :TILE], o[:, TILE:]], axis=0)
        o_ref[0, :, lanes_of(s)] = o.T.astype(o_ref.dtype)


def _moba_attention(qk, vt, bias):
    B, S, _ = qk.shape
    assert S % TILE == 0 and N_HEADS % (2 * MOBA_STREAMS) == 0 and 2 * HEAD_DIM == LANES
    n_steps = N_HEADS // 2 // MOBA_STREAMS
    nq = S // TILE
    w = MOBA_STREAMS * LANES
    return pl.pallas_call(
        _moba_body,
        grid=(B, n_steps, nq),
        in_specs=[pl.BlockSpec((1, TILE, w), lambda b, p, i: (b, i, p)),
                  pl.BlockSpec((1, S, w), lambda b, p, i: (b, 0, n_steps + p)),
                  pl.BlockSpec((1, nq, w, TILE), lambda b, p, i: (b, 0, p, 0)),
                  pl.BlockSpec((2, MOBA_STREAMS, TILE, 2 * TILE), lambda b, p, i: (0, p, 0, 0))],
        out_specs=pl.BlockSpec((1, TILE, w), lambda b, p, i: (b, i, p)),
        out_shape=jax.ShapeDtypeStruct((B, S, N_HEADS * HEAD_DIM), BF16),
        scratch_shapes=[pltpu.VMEM((MOBA_STREAMS, nq, LANES), F32),
                        pltpu.VMEM((MOBA_STREAMS, 2 + nq, 2 * TILE), F32),
                        pltpu.VMEM((MOBA_STREAMS, 2, TILE, 2 * TILE), F32),
                        pltpu.VMEM((MOBA_STREAMS, 2, TILE, 2 * TILE), F32),
                        pltpu.VMEM((MOBA_STREAMS, 1, 2 * TILE), F32),
                        pltpu.VMEM((MOBA_STREAMS, 1, 2 * TILE), F32),
                        pltpu.VMEM((MOBA_STREAMS, HEAD_DIM, 2 * TILE), F32)],
        compiler_params=_cparams("parallel", "parallel", "arbitrary"),
        name="moba_attention",
    )(qk, qk, vt, _heads_on_lanes(bias, 2))


def _gelu_tanh(x):
    return 0.5 * x * (1.0 + jnp.tanh(math.sqrt(2.0 / math.pi) * (x + 0.044715 * (x * x * x))))


def _compress_body(t_ref, pos_ref, w1_ref, w2_ref, o_ref):
    groups = t_ref.shape[2]
    half = t_ref.shape[3]
    t = t_ref[0].reshape(NSA_KV_HEADS * groups, half).astype(F32)
    first = jnp.dot((t + pos_ref[0:1, :]).astype(BF16), w1_ref[0:half, :], preferred_element_type=F32)
    second = jnp.dot((t + pos_ref[1:2, :]).astype(BF16), w1_ref[half:2 * half, :],
                     preferred_element_type=F32)
    rows = first.shape[0]
    pre = first + pltpu.roll(second, rows - 1, 0)
    out = jnp.dot(_gelu_tanh(pre).astype(BF16), w2_ref[...], preferred_element_type=F32)
    for h in range(NSA_KV_HEADS):
        o_ref[0, h] = out[h * groups:(h + 1) * groups].astype(o_ref.dtype)


def _compress(t, pos, w1, w2):
    B, Hkv, groups, half = t.shape
    return pl.pallas_call(
        _compress_body,
        grid=(B,),
        in_specs=[pl.BlockSpec((1, Hkv, groups, half), lambda b: (b, 0, 0, 0)),
                  pl.BlockSpec((2, half), lambda b: (0, 0)),
                  pl.BlockSpec((2 * half, CMP_HIDDEN), lambda b: (0, 0)),
                  pl.BlockSpec((CMP_HIDDEN, HEAD_DIM), lambda b: (0, 0))],
        out_specs=pl.BlockSpec((1, Hkv, groups, HEAD_DIM), lambda b: (b, 0, 0, 0)),
        out_shape=jax.ShapeDtypeStruct((B, Hkv, groups, HEAD_DIM), BF16),
        compiler_params=_cparams("parallel"),
        name="nsa_compress",
    )(t, pos.reshape(2, half), w1.astype(BF16), w2.astype(BF16))


def _swap_halves(x):
    return jnp.concatenate([x[:, HEAD_DIM:], x[:, :HEAD_DIM]], axis=1)


def _group_lanes(x):
    return jnp.concatenate([x] * NSA_GROUP, axis=1)


def _nsa_body(q_ref, kc_ref, vct_ref, ks_ref, vst_ref, kw_ref, vwt_ref, gt_ref, bias_ref, c2s_ref,
              o_ref, selneg_ref, radd_ref, sa_ref, sb_ref, sw_ref, oc_ref, os_ref, m_ref, l_ref, acc_ref):
    i = pl.program_id(2)
    nb = ks_ref.shape[1] // TILE
    n_cmp = kc_ref.shape[1]
    n_slc = c2s_ref.shape[0]
    per_tile = TILE // SLC_BLOCK
    cols = NSA_GROUP * TILE
    kv_heads = range(2)
    n_far = jnp.maximum(i - 1, 0)
    n_far_groups = (n_far + 1) >> 1
    last = nb - 1
    t_near = jnp.maximum(i - 1, 0)
    t_edge = jnp.maximum(i - 2, 0)

    dims_of = lambda a: slice(a * HEAD_DIM, (a + 1) * HEAD_DIM)

    def key_tile(k_ref, t):
        return k_ref[0, pl.ds(pl.multiple_of(t * TILE, TILE), TILE), :]

    def scores_of(k_ref, t, a):
        return lax.dot_general(key_tile(k_ref, t), q4s[a], _NT, preferred_element_type=F32)

    lane = lax.broadcasted_iota(I32, (TILE, LANES), 1)
    lo_half = lane < HEAD_DIM
    qpos = i * TILE + (lax.broadcasted_iota(I32, (n_cmp, cols), 1) & (TILE - 1))
    cmp_valid = CMP_STRIDE * lax.broadcasted_iota(I32, (n_cmp, cols), 0) + (CMP_LEN - 1) <= qpos
    key = lax.broadcasted_iota(I32, (TILE, TILE), 0)
    qry = lax.broadcasted_iota(I32, (TILE, TILE), 1)
    diag_neg = _group_lanes(jnp.where(key <= qry, 0.0, NEG_INF))
    edge_neg = _group_lanes(jnp.where(key > qry, 0.0, NEG_INF))
    qall = q_ref[0]
    q4s = []

    for a in kv_heads:
        keep = lo_half if a == 0 else jnp.logical_not(lo_half)
        heads = []
        for g in range(NSA_GROUP):
            cb = a * 2 + g // 2
            x = qall[:, cb * LANES:(cb + 1) * LANES]
            if g % 2 != a:
                x = _swap_halves(x)
            heads.append(jnp.where(keep, x, jnp.zeros_like(x)))
        q4s.append(jnp.concatenate(heads, axis=0))
        far_bias = bias_ref[1, a, 0:1, :]

        sa_ref[a, 0] = scores_of(ks_ref, i, a) + (bias_ref[0, a] + diag_neg)
        sa_ref[a, 1] = scores_of(ks_ref, t_near, a) + bias_ref[1, a]
        sw_ref[a, 0] = scores_of(kw_ref, i, a) + (bias_ref[0, a] + diag_neg)
        sw_ref[a, 1] = scores_of(kw_ref, t_near, a) + bias_ref[1, a]
        sw_ref[a, 2] = scores_of(kw_ref, t_edge, a) + (far_bias + edge_neg)

        s_c = lax.dot_general(kc_ref[0], q4s[a], _NT, preferred_element_type=F32)
        s_c = jnp.where(cmp_valid, s_c, NEG_INF)
        m_c = jnp.max(s_c, axis=0, keepdims=True)
        e_c = jnp.where(cmp_valid, jnp.exp2(s_c - m_c), 0.0)
        l_c = jnp.sum(e_c, axis=0, keepdims=True)
        p_c = e_c / jnp.where(l_c > 0.0, l_c, 1.0)
        oc_ref[a] = jnp.dot(vct_ref[0, dims_of(a), :], p_c.astype(BF16), preferred_element_type=F32)

        p_sum = p_c[:, 0:TILE]
        for g in range(1, NSA_GROUP):
            p_sum = p_sum + p_c[:, g * TILE:(g + 1) * TILE]
        p_hi = p_sum.astype(BF16)
        p_lo = (p_sum - p_hi.astype(F32)).astype(BF16)
        imp = (jnp.dot(c2s_ref[...], p_hi, preferred_element_type=F32)
               + jnp.dot(c2s_ref[...], p_lo, preferred_element_type=F32))
        j = lax.broadcasted_iota(I32, imp.shape, 0)
        qb = (i * TILE + lax.broadcasted_iota(I32, imp.shape, 1)) >> int(math.log2(SLC_BLOCK))
        forced = (j == 0) | ((j <= qb) & (j > qb - SLC_LOCAL))
        imp = jnp.where(forced, jnp.inf, jnp.where(j > qb, -jnp.inf, imp))
        rank = _rank_before(imp, n_slc)
        selneg = jnp.where((rank < SLC_TOPN) & (j <= qb), 0.0, NEG_INF)
        selneg_ref[a] = selneg
        for c in range(per_tile):
            radd_ref[a, c:c + 1, :] = _group_lanes(selneg_ref[a, pl.ds(per_tile * i + c, 1), :])
            near_row = _group_lanes(selneg_ref[a, pl.ds(per_tile * t_near + c, 1), :])
            radd_ref[a, per_tile + c:per_tile + c + 1, :] = jnp.where(i >= 1, near_row, NEG_INF)
        for blk in range(n_slc):
            row = far_bias + _group_lanes(selneg[blk:blk + 1, :])
            r = 2 * per_tile + blk
            radd_ref[a, r:r + 1, :] = jnp.where(blk // per_tile < n_far, row, NEG_INF)
        _init_state(m_ref.at[a], l_ref.at[a], acc_ref.at[a])

    def update(jg, a, buf):
        first = jg == 0
        t0 = jnp.where(first, i, 2 * jg - 2)
        t1 = jnp.where(first, t_near, jnp.minimum(2 * jg - 1, last))
        scores, adds = [], []
        for t in range(2):
            for c in range(per_tile):
                scores.append(buf[a, t, c * SLC_BLOCK:(c + 1) * SLC_BLOCK, :])
                adds.append(radd_ref[a, pl.ds(2 * per_tile * jg + per_tile * t + c, 1), :])
        _softmax_pv(scores, adds, [vst_ref[0, t0], vst_ref[0, t1]], [(dims_of(a), slice(None))],
                    m_ref.at[a], l_ref.at[a], acc_ref.at[a])

    def step(jg, src, dst):
        for a in kv_heads:
            dst[a, 0] = scores_of(ks_ref, 2 * jg, a)
            dst[a, 1] = scores_of(ks_ref, jnp.minimum(2 * jg + 1, last), a)
            update(jg, a, src)

    def two_steps(jj, carry):
        step(2 * jj, sa_ref, sb_ref)
        step(2 * jj + 1, sb_ref, sa_ref)
        return carry

    lax.fori_loop(0, n_far_groups >> 1, two_steps, 0)

    @pl.when((n_far_groups & 1) == 1)
    def _():
        step(n_far_groups - 1, sa_ref, sb_ref)
        for a in kv_heads:
            update(n_far_groups, a, sb_ref)

    @pl.when((n_far_groups & 1) == 0)
    def _():
        for a in kv_heads:
            update(n_far_groups, a, sa_ref)

    gates = jax.nn.sigmoid(gt_ref[0, 0])
    zero_row = jnp.zeros((1, cols), F32)
    pieces = []
    for a in kv_heads:
        os_ref[a] = acc_ref[a] / l_ref[a]
        _init_state(m_ref.at[a], l_ref.at[a], acc_ref.at[a])
        _softmax_pv([sw_ref[a, 0], sw_ref[a, 1], sw_ref[a, 2]],
                    [zero_row, zero_row + jnp.where(i >= 1, 0.0, NEG_INF), zero_row + jnp.where(i >= 2, 0.0, NEG_INF)],
                    [vwt_ref[0, i], vwt_ref[0, t_near], vwt_ref[0, t_edge]], [(dims_of(a), slice(None))],
                    m_ref.at[a], l_ref.at[a], acc_ref.at[a])
        o_w = acc_ref[a] / l_ref[a]
        for g in range(NSA_GROUP):
            c0 = 3 * (NSA_GROUP * a + g)
            ls = slice(g * TILE, (g + 1) * TILE)
            pieces.append(gates[c0:c0 + 1, :] * oc_ref[a, :, ls] + gates[c0 + 1:c0 + 2, :] * os_ref[a, :, ls]
                          + gates[c0 + 2:c0 + 3, :] * o_w[:, ls])
    o_ref[0] = jnp.concatenate(pieces, axis=0).T.astype(o_ref.dtype)


def _cmp_to_slc(S):
    n_cmp_pad = S // CMP_STRIDE
    n_slc = S // SLC_BLOCK
    ci = np.arange(n_cmp_pad)[:, None] * CMP_STRIDE
    sj = np.arange(n_slc)[None, :] * SLC_BLOCK
    c2s = ((ci < sj + SLC_BLOCK) & (ci + CMP_LEN > sj)).astype(np.float32)
    c2s[(S - CMP_LEN) // CMP_STRIDE + 1:] = 0.0
    return jnp.asarray(c2s.T, BF16)


def _nsa_attention(proj, vt, gate_t, kcmp, vcmp_t, bias):
    B, S, _ = proj.shape
    assert S % TILE == 0 and NSA_KV_HEADS == 4 and 2 * HEAD_DIM == LANES
    nq = S // TILE
    n_cmp = kcmp.shape[1]
    n_slc = S // SLC_BLOCK
    qw = 2 * NSA_GROUP * HEAD_DIM
    q_blocks = N_HEADS * HEAD_DIM // LANES
    kv_blocks = NSA_KV_HEADS * HEAD_DIM // LANES

    def k_spec(which):
        base = q_blocks + which * kv_blocks
        return pl.BlockSpec((1, S, LANES), lambda b, p, i: (b, 0, base + p))

    def vt_spec(which):
        base = which * kv_blocks
        return pl.BlockSpec((1, nq, LANES, TILE), lambda b, p, i: (b, 0, base + p, 0))

    state = pltpu.VMEM((2, HEAD_DIM, NSA_GROUP * TILE), F32)
    stat = pltpu.VMEM((2, 1, NSA_GROUP * TILE), F32)
    return pl.pallas_call(
        _nsa_body,
        grid=(B, 2, nq),
        in_specs=[pl.BlockSpec((1, TILE, qw), lambda b, p, i: (b, i, p)),
                  pl.BlockSpec((1, n_cmp, LANES), lambda b, p, i: (b, 0, p)),
                  pl.BlockSpec((1, LANES, n_cmp), lambda b, p, i: (b, p, 0)),
                  k_spec(2), vt_spec(0), k_spec(3), vt_spec(1),
                  pl.BlockSpec((1, 1, LANES, TILE), lambda b, p, i: (b, i, p, 0)),
                  pl.BlockSpec((2, 2, TILE, NSA_GROUP * TILE), lambda b, p, i: (0, p, 0, 0)),
                  pl.BlockSpec((n_slc, n_cmp), lambda b, p, i: (0, 0))],
        out_specs=pl.BlockSpec((1, TILE, qw), lambda b, p, i: (b, i, p)),
        out_shape=jax.ShapeDtypeStruct((B, S, N_HEADS * HEAD_DIM), BF16),
        scratch_shapes=[pltpu.VMEM((2, n_slc, TILE), F32),
                        pltpu.VMEM((2, 2 * (TILE // SLC_BLOCK) + n_slc, NSA_GROUP * TILE), F32),
                        pltpu.VMEM((2, 2, TILE, NSA_GROUP * TILE), F32),
                        pltpu.VMEM((2, 2, TILE, NSA_GROUP * TILE), F32),
                        pltpu.VMEM((2, 3, TILE, NSA_GROUP * TILE), F32),
                        state, state, stat, stat, state],
        compiler_params=_cparams("parallel", "parallel", "arbitrary"),
        name="nsa_attention",
    )(proj, kcmp, vcmp_t, proj, vt, proj, vt, gate_t, _heads_on_lanes(bias, NSA_GROUP), _cmp_to_slc(S))


def _split_bf16(x):
    hi = x.astype(BF16)
    return hi, (x - hi.astype(F32)).astype(BF16)


_ROW_OF_EXPERT = np.arange(N_EXPERTS).reshape(N_GROUPS, EXPERTS_PER_GROUP).T.reshape(-1)


def _route(x, w_ref, b_ref, tri_ref, idx_ref, wt_ref, pos_ref, cnt_ref, base_ref):
    @pl.when(pl.program_id(0) == 0)
    def _():
        base_ref[...] = jnp.zeros(base_ref.shape, F32)

    x_hi, x_lo = _split_bf16(x)
    w_hi, w_lo = _split_bf16(w_ref[...])
    logits = (lax.dot_general(w_hi, x_hi, _NT, preferred_element_type=F32)
              + lax.dot_general(w_hi, x_lo, _NT, preferred_element_type=F32)
              + lax.dot_general(w_lo, x_hi, _NT, preferred_element_type=F32)) + b_ref[:, 0:1]
    m = jnp.max(logits, axis=0, keepdims=True)
    e = jnp.exp(logits - m)
    probs = e / jnp.sum(e, axis=0, keepdims=True)
    pk = [probs[k * N_GROUPS:(k + 1) * N_GROUPS] for k in range(EXPERTS_PER_GROUP)]
    hi1, lo1 = jnp.maximum(pk[0], pk[1]), jnp.minimum(pk[0], pk[1])
    hi2, lo2 = jnp.maximum(pk[2], pk[3]), jnp.minimum(pk[2], pk[3])
    score = jnp.maximum(hi1, hi2) + jnp.maximum(jnp.minimum(hi1, hi2), jnp.maximum(lo1, lo2))
    grp = lax.broadcasted_iota(I32, score.shape, 0)
    best = jnp.min(jnp.where(score == jnp.max(score, axis=0, keepdims=True), grp, N_GROUPS),
                   axis=0, keepdims=True)
    v = [jnp.sum(jnp.where(grp == best, p, 0.0), axis=0, keepdims=True) for p in pk]
    v1 = jnp.maximum(jnp.maximum(v[0], v[1]), jnp.maximum(v[2], v[3]))
    i1 = jnp.where(v[0] == v1, 0, jnp.where(v[1] == v1, 1, jnp.where(v[2] == v1, 2, 3)))
    rest = [jnp.where(i1 == k, -1.0, v[k]) for k in range(EXPERTS_PER_GROUP)]
    v2 = jnp.maximum(jnp.maximum(rest[0], rest[1]), jnp.maximum(rest[2], rest[3]))
    i2 = jnp.where(rest[0] == v2, 0, jnp.where(rest[1] == v2, 1, jnp.where(rest[2] == v2, 2, 3)))
    tot = v1 + v2
    idx_ref[...] = jnp.concatenate([best * EXPERTS_PER_GROUP + i1, best * EXPERTS_PER_GROUP + i2], axis=0)
    wt_ref[...] = jnp.concatenate([v1 / tot, v2 / tot], axis=0)

    row = lax.broadcasted_iota(I32, logits.shape, 0)
    hot = [jnp.where(row == ik * N_GROUPS + best, 1.0, 0.0) for ik in (i1, i2)]
    both = (hot[0] + hot[1]).astype(BF16)
    chunks = [slice(c * LANES, (c + 1) * LANES) for c in range(logits.shape[1] // LANES)]
    prefix = [jnp.dot(both[:, ls], tri_ref[...], preferred_element_type=F32) for ls in chunks]
    run = base_ref[:, 0:1]
    pos = [[], []]
    for ls, pre in zip(chunks, prefix):
        before = run + pre - 1.0
        for k in range(2):
            pos[k].append(jnp.sum(hot[k][:, ls] * before, axis=0, keepdims=True))
        run = run + pre[:, LANES - 1:LANES]
    pos_ref[...] = jnp.concatenate([jnp.concatenate(pos[0], axis=1), jnp.concatenate(pos[1], axis=1)],
                                   axis=0).astype(I32)
    base_ref[...] = jnp.broadcast_to(run, base_ref.shape)
    cnt_ref[...] = jnp.broadcast_to(run, cnt_ref.shape)


def _router_operands(router_w, router_b):
    w = router_w.T[_ROW_OF_EXPERT]
    b = jnp.broadcast_to(router_b[_ROW_OF_EXPERT][:, None], (N_EXPERTS, LANES))
    tri = jnp.asarray(np.triu(np.ones((LANES, LANES), np.float32)), BF16)
    return w, b, tri


def _proj_ln_route_body(a_ref, w_ref, x_ref, g_ref, b_ref, rw_ref, rb_ref, tri_ref,
                        o_ref, ob_ref, idx_ref, wt_ref, pos_ref, cnt_ref, base_ref):
    y = jnp.dot(a_ref[...], w_ref[...], preferred_element_type=F32)
    out = _layer_norm_rows(DEEPNORM_ALPHA * x_ref[...] + y, g_ref[...], b_ref[...])
    o_ref[...] = out
    ob_ref[...] = out.astype(BF16)
    _route(out, rw_ref, rb_ref, tri_ref, idx_ref, wt_ref, pos_ref, cnt_ref, base_ref)


def _proj_ln_route(a, w, x, g, b, router_w, router_b):
    M, K = a.shape
    D = w.shape[1]
    assert M % LN_TM == 0 and LN_TM % LANES == 0
    row = pl.BlockSpec((LN_TM, D), lambda i: (i, 0))
    vec = pl.BlockSpec((1, D), lambda i: (0, 0))
    whole = lambda shape: pl.BlockSpec(shape, lambda i: (0, 0))
    tok = lambda dt: jax.ShapeDtypeStruct((2, M), dt)
    tok_spec = pl.BlockSpec((2, LN_TM), lambda i: (0, i))
    h, hb, idx, wts, pos, cnt = pl.pallas_call(
        _proj_ln_route_body,
        grid=(M // LN_TM,),
        in_specs=[pl.BlockSpec((LN_TM, K), lambda i: (i, 0)), whole((K, D)), row, vec, vec,
                  whole((N_EXPERTS, D)), whole((N_EXPERTS, LANES)), whole((LANES, LANES))],
        out_specs=[row, row, tok_spec, tok_spec, tok_spec, whole((N_EXPERTS, LANES))],
        out_shape=[jax.ShapeDtypeStruct((M, D), F32), jax.ShapeDtypeStruct((M, D), BF16),
                   tok(I32), tok(F32), tok(I32), jax.ShapeDtypeStruct((N_EXPERTS, LANES), F32)],
        scratch_shapes=[pltpu.VMEM((N_EXPERTS, LANES), F32)],
        compiler_params=_cparams("arbitrary"),
        name="out_proj_ln_route",
    )(a, w, x, g.reshape(1, D), b.reshape(1, D), *_router_operands(router_w, router_b))
    counts = cnt[np.argsort(_ROW_OF_EXPERT), 0].astype(I32)
    return h, hb, (idx, wts, pos, counts)


def _expert_body(blk_e_ref, n_used_ref, x_ref, wg_ref, wu_ref, wd_ref, o_ref, wg_b, wu_b, wd_b):
    i = pl.program_id(0)

    @pl.when((i == 0) | (blk_e_ref[i] != blk_e_ref[jnp.maximum(i - 1, 0)]))
    def _():
        wg_b[...] = wg_ref[0, 0].astype(BF16)
        wu_b[...] = wu_ref[0, 0].astype(BF16)
        wd_b[...] = wd_ref[0, 0].astype(BF16)

    @pl.when(i < n_used_ref[0])
    def _():
        x = x_ref[...]
        gate = jnp.dot(x, wg_b[...], preferred_element_type=F32)
        up = jnp.dot(x, wu_b[...], preferred_element_type=F32)
        hid = (gate * jax.nn.sigmoid(gate) * up).astype(BF16)
        o_ref[...] = jnp.dot(hid, wd_b[...], preferred_element_type=F32).astype(o_ref.dtype)

    @pl.when(i >= n_used_ref[0])
    def _():
        o_ref[...] = jnp.zeros(o_ref.shape, o_ref.dtype)


def _experts(xs, blk_e, n_used, wg, wu, wd, layer):
    R, D = xs.shape
    n_blk = R // MOE_TB

    def live(i, be, nu):
        return jnp.minimum(i, nu[0] - 1)

    grid_spec = pltpu.PrefetchScalarGridSpec(
        num_scalar_prefetch=2,
        grid=(n_blk,),
        in_specs=[pl.BlockSpec((MOE_TB, D), lambda i, be, nu: (live(i, be, nu), 0)),
                  pl.BlockSpec((1, 1, D, D_EXPERT), lambda i, be, nu: (layer, be[i], 0, 0)),
                  pl.BlockSpec((1, 1, D, D_EXPERT), lambda i, be, nu: (layer, be[i], 0, 0)),
                  pl.BlockSpec((1, 1, D_EXPERT, D), lambda i, be, nu: (layer, be[i], 0, 0))],
        out_specs=pl.BlockSpec((MOE_TB, D), lambda i, be, nu: (i, 0)),
        scratch_shapes=[pltpu.VMEM((D, D_EXPERT), BF16), pltpu.VMEM((D, D_EXPERT), BF16),
                        pltpu.VMEM((D_EXPERT, D), BF16)],
    )
    return pl.pallas_call(
        _expert_body,
        grid_spec=grid_spec,
        out_shape=jax.ShapeDtypeStruct((R, D), BF16),
        compiler_params=_cparams("arbitrary"),
        name="moe_experts",
    )(blk_e, n_used, xs, wg, wu, wd)


def _combine_ln_body(x_ref, y0_ref, y1_ref, wt_ref, g_ref, b_ref, o_ref, ob_ref):
    ffn = y0_ref[...] * wt_ref[:, 0:1] + y1_ref[...] * wt_ref[:, HEAD_DIM:HEAD_DIM + 1]
    out = _layer_norm_rows(DEEPNORM_ALPHA * x_ref[...] + ffn, g_ref[...], b_ref[...])
    o_ref[...] = out
    ob_ref[...] = out.astype(BF16)


def _combine_ln(x, y0, y1, wt, g, b):
    M, D = x.shape
    assert M % LN_TM == 0
    row = pl.BlockSpec((LN_TM, D), lambda i: (i, 0))
    vec = pl.BlockSpec((1, D), lambda i: (0, 0))
    return pl.pallas_call(
        _combine_ln_body,
        grid=(M // LN_TM,),
        in_specs=[row, row, row, pl.BlockSpec((LN_TM, LANES), lambda i: (i, 0)), vec, vec],
        out_specs=[row, row],
        out_shape=[jax.ShapeDtypeStruct((M, D), F32), jax.ShapeDtypeStruct((M, D), BF16)],
        compiler_params=_cparams("parallel"),
        name="moe_combine_ln",
    )(x, y0, y1, wt, g.reshape(1, D), b.reshape(1, D))


def _moe_ln(h, hb, routing, wg, wu, wd, layer, g, b):
    N, D = h.shape
    A = 2 * N
    idx, wts, pos, counts = routing
    starts = jnp.cumsum(counts) - counts
    padded = (counts + MOE_TB - 1) // MOE_TB * MOE_TB
    pends = jnp.cumsum(padded)
    pstarts = pends - padded
    R = A + N_EXPERTS * MOE_TB
    n_blk = R // MOE_TB
    experts = jnp.arange(N_EXPERTS, dtype=I32)
    dest = pos + jnp.sum(jnp.where(idx[None] == experts[:, None, None], pstarts[:, None, None], 0), axis=0)
    tok = jnp.broadcast_to(jnp.arange(N, dtype=I32)[None, :], (2, N))
    _, tok_sorted = lax.sort_key_val(dest.reshape(A), tok.reshape(A))
    blk_row0 = jnp.arange(n_blk, dtype=I32) * MOE_TB
    blk_e = jnp.minimum(jnp.sum((pends[None, :] <= blk_row0[:, None]).astype(I32), axis=1), N_EXPERTS - 1)
    hot = blk_e[:, None] == experts[None, :]
    compact0 = blk_row0 + jnp.sum(jnp.where(hot, (starts - pstarts)[None, :], 0), axis=1)
    compact = jnp.remainder(compact0[:, None] + jnp.arange(MOE_TB, dtype=I32)[None, :], A).reshape(R)
    n_used = (pends[-1:] // MOE_TB).astype(I32)
    xs = hb[tok_sorted[compact]]
    yb = _experts(xs, blk_e, n_used, wg, wu, wd, layer)
    wt = jnp.concatenate([jnp.broadcast_to(wts[k][:, None], (N, HEAD_DIM)) for k in range(2)], axis=1)
    return _combine_ln(h, yb[dest[0]], yb[dest[1]], wt, g, b)


def _moba_layer(h, w_in, w_out, bias, g, b, router_w, router_b, B, S):
    HD = N_HEADS * HEAD_DIM
    qk, vt = _in_proj(h, w_in[:, :2 * HD].astype(BF16), _query_scale(HD, 2 * HD),
                      [w_in[:, 2 * HD:].T.astype(BF16)], [BF16], B, S)
    att = _moba_attention(qk.reshape(B, S, 2 * HD), vt, bias)
    return _proj_ln_route(att.reshape(B * S, HD), w_out.astype(BF16), h, g, b, router_w, router_b)


def _nsa_layer(h, hb, w_in, w_out, pos_k, pos_v, ck_w1, ck_w2, cv_w1, cv_w2, bias, g, b, router_w, router_b, B, S):
    HD = N_HEADS * HEAD_DIM
    kvw = NSA_KV_HEADS * HEAD_DIM
    col = lambda k: slice(HD + k * kvw, HD + (k + 1) * kvw)
    w_rows = jnp.concatenate([w_in[:, :HD + 2 * kvw], w_in[:, col(2)], w_in[:, col(4)]], axis=1)
    w_vt = jnp.concatenate([w_in[:, col(3)], w_in[:, col(5)]], axis=1).T
    per_pair = 3 * N_HEADS // 2
    wg = w_in[:, HD + 6 * kvw:].reshape(D_MODEL, 2, per_pair)
    wg = jnp.pad(wg, ((0, 0), (0, 0), (0, LANES - per_pair))).reshape(D_MODEL, 2 * LANES).T
    proj, vt, gate_t = _in_proj(hb, w_rows.astype(BF16), _query_scale(HD, HD + 4 * kvw),
                                [w_vt.astype(BF16), wg.astype(BF16)], [BF16, F32], B, S)
    proj = proj.reshape(B, S, HD + 4 * kvw)

    def grouped(t):
        t = t.reshape(B, S, NSA_KV_HEADS, HEAD_DIM).transpose(0, 2, 1, 3)
        return t.reshape(B, NSA_KV_HEADS, S // CMP_STRIDE, CMP_STRIDE * HEAD_DIM)

    kcmp = _compress(grouped(proj[..., HD:HD + kvw]), pos_k, ck_w1, ck_w2)
    vcmp = _compress(grouped(proj[..., HD + kvw:HD + 2 * kvw]), pos_v, cv_w1, cv_w2)
    n_cmp = kcmp.shape[2]
    kcmp = kcmp.transpose(0, 2, 1, 3).reshape(B, n_cmp, kvw)
    vcmp_t = vcmp.transpose(0, 1, 3, 2).reshape(B, kvw, n_cmp)
    att = _nsa_attention(proj, vt, gate_t, kcmp, vcmp_t, bias)
    return _proj_ln_route(att.reshape(B * S, HD), w_out.astype(BF16), h, g, b, router_w, router_b)


def kernel(x, rel_bias, router_w, router_b, ln_g, ln_b, moba_w_in, moba_w_out, nsa_w_in, nsa_w_out,
           nsa_pos_k, nsa_pos_v, nsa_ck_w1, nsa_ck_w2, nsa_cv_w1, nsa_cv_w2,
           moe_w_gate, moe_w_up, moe_w_down):
    B, S, D = x.shape
    bias = _bias_tiles(rel_bias)
    h = x.reshape(B * S, D)
    h, hb, routing = _moba_layer(h, moba_w_in[0], moba_w_out[0], bias, ln_g[0, 0], ln_b[0, 0],
                                 router_w, router_b, B, S)
    h, hb = _moe_ln(h, hb, routing, moe_w_gate, moe_w_up, moe_w_down, 0, ln_g[0, 1], ln_b[0, 1])
    h, hb, routing = _nsa_layer(h, hb, nsa_w_in[0], nsa_w_out[0], nsa_pos_k[0], nsa_pos_v[0],
                                nsa_ck_w1[0], nsa_ck_w2[0], nsa_cv_w1[0], nsa_cv_w2[0],
                                bias, ln_g[1, 0], ln_b[1, 0], router_w, router_b, B, S)
    h, hb = _moe_ln(h, hb, routing, moe_w_gate, moe_w_up, moe_w_down, 1, ln_g[1, 1], ln_b[1, 1])
    return h.reshape(B, S, D)
```

```python
import math
from functools import partial

import numpy as np
import jax
import jax.numpy as jnp
from jax import lax
from jax.experimental import pallas as pl
from jax.experimental.pallas import tpu as pltpu

F32, BF16, I32 = jnp.float32, jnp.bfloat16, jnp.int32

D_MODEL = 1024
N_HEADS = 16
HEAD_DIM = 64
DEPTH = 2
NEG_INF = -1e30
LN_EPS = 1e-5
MOBA_BLOCK = 256
MOBA_TOPK = 3
NSA_KV_HEADS = 4
NSA_GROUP = N_HEADS // NSA_KV_HEADS
CMP_LEN = 32
CMP_STRIDE = 16
CMP_HIDDEN = 256
SLC_BLOCK = 64
SLC_TOPN = 16
SLC_LOCAL = 2
WINDOW = 512
REL_BUCKETS = 32
REL_MAX_DIST = 128
N_EXPERTS = 32
N_GROUPS = 8
EXPERTS_PER_GROUP = N_EXPERTS // N_GROUPS
D_EXPERT = 512
DEEPNORM_ALPHA = (2 * DEPTH) ** 0.25
LOG2E = math.log2(math.e)
Q_SCALE = HEAD_DIM ** -0.5 * LOG2E

LANES = 128
SUBLANES = 8
ONES_ROWS = 16
TILE = MOBA_BLOCK
MM_TM = 512
MM_TN = 1024
LN_TM = 1024
MOE_TB = 512
assert WINDOW == 2 * TILE and TILE % SLC_BLOCK == 0 and SLC_BLOCK % SUBLANES == 0 and MM_TM % TILE == 0
VMEM_LIMIT = 56 * 1024 * 1024

_NT = (((1,), (1,)), ((), ()))


def _cparams(*sem):
    return pltpu.CompilerParams(dimension_semantics=sem, vmem_limit_bytes=VMEM_LIMIT)


def _in_proj_body(n_t, a_ref, w_ref, c_ref, *refs):
    wt_refs, o_ref, ot_refs = refs[:n_t], refs[n_t], refs[n_t + 1:]
    a = a_ref[...].astype(BF16)
    for c in range(o_ref.shape[1] // MM_TN):
        cols = slice(c * MM_TN, (c + 1) * MM_TN)
        acc = jnp.dot(a, w_ref[:, cols], preferred_element_type=F32)
        o_ref[:, cols] = (acc * c_ref[:, cols]).astype(o_ref.dtype)
    for wt_ref, ot_ref in zip(wt_refs, ot_refs):
        r = lax.dot_general(wt_ref[...], a, _NT, preferred_element_type=F32)
        for t in range(ot_ref.shape[1]):
            ot_ref[0, t] = r[:, t * TILE:(t + 1) * TILE].astype(ot_ref.dtype)


def _in_proj(a, w, col_scale, w_ts, t_dtypes, B, S):
    M, K = a.shape
    N = w.shape[1]
    assert M == B * S and S % MM_TM == 0 and N % MM_TN == 0
    per_seq = S // MM_TM
    sub = MM_TM // TILE
    whole = lambda shape: pl.BlockSpec(shape, lambda i: (0, 0))
    outs = pl.pallas_call(
        partial(_in_proj_body, len(w_ts)),
        grid=(M // MM_TM,),
        in_specs=[pl.BlockSpec((MM_TM, K), lambda i: (i, 0)), whole((K, N)), whole((1, N))]
                 + [whole(w_t.shape) for w_t in w_ts],
        out_specs=[pl.BlockSpec((MM_TM, N), lambda i: (i, 0))]
                  + [pl.BlockSpec((1, sub, w_t.shape[0], TILE), lambda i: (i // per_seq, i % per_seq, 0, 0))
                     for w_t in w_ts],
        out_shape=[jax.ShapeDtypeStruct((M, N), BF16)]
                  + [jax.ShapeDtypeStruct((B, S // TILE, w_t.shape[0], TILE), dt) for w_t, dt in zip(w_ts, t_dtypes)],
        compiler_params=_cparams("parallel"),
        name="in_proj",
    )(a, w, col_scale.reshape(1, N), *w_ts)
    return outs


def _query_scale(n_query_cols, n_cols):
    return jnp.where(jnp.arange(n_cols) < n_query_cols, Q_SCALE, 1.0).astype(F32)


def _layer_norm_rows(z, g, b):
    mu = jnp.mean(z, axis=-1, keepdims=True)
    zc = z - mu
    var = jnp.mean(zc * zc, axis=-1, keepdims=True)
    return zc * lax.rsqrt(var + LN_EPS) * g + b


def _t5_bucket_np(rel):
    n = np.maximum(rel, 0)
    max_exact = REL_BUCKETS // 2
    nf = np.maximum(n, 1).astype(np.float32)
    large = max_exact + (np.log(nf / np.float32(max_exact))
                         / np.float32(math.log(REL_MAX_DIST / max_exact))
                         * np.float32(REL_BUCKETS - max_exact)).astype(np.int32)
    large = np.minimum(large, REL_BUCKETS - 1)
    return np.where(n < max_exact, n, large).astype(np.int32)


def _bias_body(tbl_ref, bk_ref, o_ref):
    h = pl.program_id(0)
    for dl in range(2):
        bk = bk_ref[dl]
        acc = jnp.zeros((TILE, TILE), F32)
        for b in range(REL_BUCKETS):
            acc = jnp.where(bk == b, tbl_ref[h * REL_BUCKETS + b], acc)
        o_ref[dl, 0] = acc * LOG2E


def _bias_tiles(rel_bias):
    key = np.arange(TILE)[:, None]
    qry = np.arange(TILE)[None, :]
    assert int(_t5_bucket_np(np.array(TILE + 1))) == REL_BUCKETS - 1
    bk = np.stack([_t5_bucket_np(qry - key), _t5_bucket_np(TILE + qry - key)])
    return pl.pallas_call(
        _bias_body,
        grid=(N_HEADS,),
        in_specs=[pl.BlockSpec(memory_space=pltpu.SMEM),
                  pl.BlockSpec((2, TILE, TILE), lambda h: (0, 0, 0))],
        out_specs=pl.BlockSpec((2, 1, TILE, TILE), lambda h: (0, h, 0, 0)),
        out_shape=jax.ShapeDtypeStruct((2, N_HEADS, TILE, TILE), F32),
        name="t5_bias_tiles",
    )(rel_bias.T.reshape(-1), jnp.asarray(bk))


def _heads_on_lanes(bias, per_block):
    two, H, T, _ = bias.shape
    b = bias.reshape(two, H // per_block, per_block, T, T).transpose(0, 1, 3, 2, 4)
    return b.reshape(two, H // per_block, T, per_block * T)


def _init_state(m_ref, l_ref, acc_ref):
    m_ref[...] = jnp.full(m_ref.shape, NEG_INF, F32)
    l_ref[...] = jnp.zeros(l_ref.shape, F32)
    acc_ref[...] = jnp.zeros(acc_ref.shape, F32)


def _rank_before(vals, rows):
    idx = lax.broadcasted_iota(I32, vals.shape, 0)
    rank = jnp.zeros(vals.shape, I32)
    for m in range(rows):
        row = vals[m:m + 1, :]
        beats = (row > vals) | ((row == vals) & (idx > m))
        rank = rank + jnp.where(beats, 1, 0)
    return rank


MOBA_STREAMS = 8


def _softmax_pv(scores, adds, vts, heads, m_ref, l_ref, acc_ref):
    def fold(x, op):
        return op(x.reshape(x.shape[0] // SUBLANES, SUBLANES, x.shape[1]), axis=0)

    m_prev = m_ref[...]
    m_part = None
    for s, add in zip(scores, adds):
        part = fold(s, jnp.max) + add
        m_part = part if m_part is None else jnp.maximum(m_part, part)
    m_new = jnp.maximum(m_prev, jnp.max(m_part, axis=0, keepdims=True))
    a = jnp.exp2(m_prev - m_new)
    probs = [jnp.exp2(s - (m_new - add)) for s, add in zip(scores, adds)]
    vt = jnp.concatenate(vts, axis=1)
    pb = jnp.concatenate([p.astype(BF16) for p in probs], axis=0)
    ones = jnp.ones((ONES_ROWS, vt.shape[1]), BF16)
    pv = jnp.concatenate([jnp.dot(jnp.concatenate([vt[rows], ones], axis=0), pb[:, cols],
                                  preferred_element_type=F32) for rows, cols in heads], axis=1)
    l_ref[...] = a * l_ref[...] + pv[HEAD_DIM:HEAD_DIM + 1, :]
    acc_ref[...] = a * acc_ref[...] + pv[:HEAD_DIM, :]
    m_ref[...] = m_new


def _moba_body(q_ref, k_ref, vt_ref, bias_ref, o_ref, kmean_ref, radd_ref, sa_ref, sb_ref, m_ref, l_ref, acc_ref):
    i = pl.program_id(2)
    nb = k_ref.shape[1] // TILE
    streams = range(MOBA_STREAMS)
    lanes_of = lambda s: slice(s * LANES, (s + 1) * LANES)

    @pl.when(i == 0)
    def _():
        for s in streams:
            for n in range(nb):
                kb = k_ref[0, n * TILE:(n + 1) * TILE, lanes_of(s)].astype(F32)
                kmean_ref[s, n:n + 1, :] = jnp.sum(kb, axis=0, keepdims=True) * (1.0 / TILE)

    n_far = jnp.maximum(i - 1, 0)
    n_far_groups = (n_far + 1) >> 1
    last = nb - 1

    def key_tile(t, s):
        return k_ref[0, pl.ds(pl.multiple_of(t * TILE, TILE), TILE), lanes_of(s)]

    def far_tiles(j):
        return 2 * j, jnp.minimum(2 * j + 1, last)

    lane = lax.broadcasted_iota(I32, (TILE, LANES), 1)
    key = lax.broadcasted_iota(I32, (TILE, 2 * TILE), 0)
    qry = lax.broadcasted_iota(I32, (TILE, 2 * TILE), 1) & (TILE - 1)
    causal_neg = jnp.where(key <= qry, 0.0, NEG_INF)
    t_near = jnp.maximum(i - 1, 0)
    q2s = []
    for s in streams:
        q = q_ref[0, :, lanes_of(s)]
        zero = jnp.zeros_like(q)
        q2 = jnp.concatenate([jnp.where(lane < HEAD_DIM, q, zero),
                              jnp.where(lane >= HEAD_DIM, q, zero)], axis=0)
        q2s.append(q2)
        sa_ref[s, 0] = (lax.dot_general(key_tile(i, s), q2, _NT, preferred_element_type=F32)
                        + (bias_ref[0, s] + causal_neg))
        sa_ref[s, 1] = lax.dot_general(key_tile(t_near, s), q2, _NT, preferred_element_type=F32) + bias_ref[1, s]
        km = kmean_ref[s]
        k_hi = km.astype(BF16)
        k_lo = (km - k_hi.astype(F32)).astype(BF16)
        gate = (lax.dot_general(k_hi, q2, _NT, preferred_element_type=F32)
                + lax.dot_general(k_lo, q2, _NT, preferred_element_type=F32))
        blk = lax.broadcasted_iota(I32, gate.shape, 0)
        gate = jnp.where(blk < i, gate, -jnp.inf)
        rank = _rank_before(gate, nb)
        neg = jnp.where((rank < MOBA_TOPK) & (blk < i), 0.0, NEG_INF)
        far_bias = bias_ref[1, s, 0:1, :]
        near_row = jnp.full((1, 2 * TILE), NEG_INF, F32)
        radd_ref[s, 0:1, :] = jnp.zeros((1, 2 * TILE), F32)
        for n in range(nb):
            row = neg[n:n + 1, :]
            near_row = jnp.where(n == i - 1, row, near_row)
            radd_ref[s, 2 + n:3 + n, :] = jnp.where(n < n_far, far_bias + row, NEG_INF)
        radd_ref[s, 1:2, :] = near_row
        _init_state(m_ref.at[s], l_ref.at[s], acc_ref.at[s])

    pair_heads = [(slice(h * HEAD_DIM, (h + 1) * HEAD_DIM), slice(h * TILE, (h + 1) * TILE)) for h in range(2)]

    def update(j, s, buf):
        first = j == 0
        t0 = jnp.where(first, i, 2 * j - 2)
        t1 = jnp.where(first, t_near, jnp.minimum(2 * j - 1, last))
        _softmax_pv([buf[s, 0], buf[s, 1]],
                    [radd_ref[s, pl.ds(2 * j, 1), :], radd_ref[s, pl.ds(2 * j + 1, 1), :]],
                    [vt_ref[0, t0, lanes_of(s), :], vt_ref[0, t1, lanes_of(s), :]], pair_heads,
                    m_ref.at[s], l_ref.at[s], acc_ref.at[s])

    def step(j, src, dst):
        for s in streams:
            ta, tb = far_tiles(j)
            dst[s, 0] = lax.dot_general(key_tile(ta, s), q2s[s], _NT, preferred_element_type=F32)
            dst[s, 1] = lax.dot_general(key_tile(tb, s), q2s[s], _NT, preferred_element_type=F32)
            update(j, s, src)

    def two_steps(jj, carry):
        step(2 * jj, sa_ref, sb_ref)
        step(2 * jj + 1, sb_ref, sa_ref)
        return carry

    lax.fori_loop(0, n_far_groups >> 1, two_steps, 0)

    @pl.when((n_far_groups & 1) == 1)
    def _():
        step(n_far_groups - 1, sa_ref, sb_ref)
        for s in streams:
            update(n_far_groups, s, sb_ref)

    @pl.when((n_far_groups & 1) == 0)
    def _():
        for s in streams:
            update(n_far_groups, s, sa_ref)

    for s in streams:
        o = acc_ref[s] / l_ref[s]
        o = jnp.concatenate([o[:, :TILE], o[:, TILE:]], axis=0)
        o_ref[0, :, lanes_of(s)] = o.T.astype(o_ref.dtype)


def _moba_attention(qk, vt, bias):
    B, S, _ = qk.shape
    assert S % TILE == 0 and N_HEADS % (2 * MOBA_STREAMS) == 0 and 2 * HEAD_DIM == LANES
    n_steps = N_HEADS // 2 // MOBA_STREAMS
    nq = S // TILE
    w = MOBA_STREAMS * LANES
    return pl.pallas_call(
        _moba_body,
        grid=(B, n_steps, nq),
        in_specs=[pl.BlockSpec((1, TILE, w), lambda b, p, i: (b, i, p)),
                  pl.BlockSpec((1, S, w), lambda b, p, i: (b, 0, n_steps + p)),
                  pl.BlockSpec((1, nq, w, TILE), lambda b, p, i: (b, 0, p, 0)),
                  pl.BlockSpec((2, MOBA_STREAMS, TILE, 2 * TILE), lambda b, p, i: (0, p, 0, 0))],
        out_specs=pl.BlockSpec((1, TILE, w), lambda b, p, i: (b, i, p)),
        out_shape=jax.ShapeDtypeStruct((B, S, N_HEADS * HEAD_DIM), BF16),
        scratch_shapes=[pltpu.VMEM((MOBA_STREAMS, nq, LANES), F32),
                        pltpu.VMEM((MOBA_STREAMS, 2 + nq, 2 * TILE), F32),
                        pltpu.VMEM((MOBA_STREAMS, 2, TILE, 2 * TILE), F32),
                        pltpu.VMEM((MOBA_STREAMS, 2, TILE, 2 * TILE), F32),
                        pltpu.VMEM((MOBA_STREAMS, 1, 2 * TILE), F32),
                        pltpu.VMEM((MOBA_STREAMS, 1, 2 * TILE), F32),
                        pltpu.VMEM((MOBA_STREAMS, HEAD_DIM, 2 * TILE), F32)],
        compiler_params=_cparams("parallel", "parallel", "arbitrary"),
        name="moba_attention",
    )(qk, qk, vt, _heads_on_lanes(bias, 2))


def _gelu_tanh(x):
    return 0.5 * x * (1.0 + jnp.tanh(math.sqrt(2.0 / math.pi) * (x + 0.044715 * (x * x * x))))


def _compress_body(t_ref, pos_ref, w1_ref, w2_ref, o_ref):
    groups = t_ref.shape[2]
    half = t_ref.shape[3]
    t = t_ref[0].reshape(NSA_KV_HEADS * groups, half).astype(F32)
    first = jnp.dot((t + pos_ref[0:1, :]).astype(BF16), w1_ref[0:half, :], preferred_element_type=F32)
    second = jnp.dot((t + pos_ref[1:2, :]).astype(BF16), w1_ref[half:2 * half, :],
                     preferred_element_type=F32)
    rows = first.shape[0]
    pre = first + pltpu.roll(second, rows - 1, 0)
    out = jnp.dot(_gelu_tanh(pre).astype(BF16), w2_ref[...], preferred_element_type=F32)
    for h in range(NSA_KV_HEADS):
        o_ref[0, h] = out[h * groups:(h + 1) * groups].astype(o_ref.dtype)


def _compress(t, pos, w1, w2):
    B, Hkv, groups, half = t.shape
    return pl.pallas_call(
        _compress_body,
        grid=(B,),
        in_specs=[pl.BlockSpec((1, Hkv, groups, half), lambda b: (b, 0, 0, 0)),
                  pl.BlockSpec((2, half), lambda b: (0, 0)),
                  pl.BlockSpec((2 * half, CMP_HIDDEN), lambda b: (0, 0)),
                  pl.BlockSpec((CMP_HIDDEN, HEAD_DIM), lambda b: (0, 0))],
        out_specs=pl.BlockSpec((1, Hkv, groups, HEAD_DIM), lambda b: (b, 0, 0, 0)),
        out_shape=jax.ShapeDtypeStruct((B, Hkv, groups, HEAD_DIM), BF16),
        compiler_params=_cparams("parallel"),
        name="nsa_compress",
    )(t, pos.reshape(2, half), w1.astype(BF16), w2.astype(BF16))


def _swap_halves(x):
    return jnp.concatenate([x[:, HEAD_DIM:], x[:, :HEAD_DIM]], axis=1)


def _group_lanes(x):
    return jnp.concatenate([x] * NSA_GROUP, axis=1)


def _nsa_body(q_ref, kc_ref, vct_ref, ks_ref, vst_ref, kw_ref, vwt_ref, gt_ref, bias_ref, c2s_ref,
              o_ref, selneg_ref, radd_ref, sa_ref, sb_ref, sw_ref, oc_ref, os_ref, m_ref, l_ref, acc_ref):
    i = pl.program_id(2)
    nb = ks_ref.shape[1] // TILE
    n_cmp = kc_ref.shape[1]
    n_slc = c2s_ref.shape[0]
    per_tile = TILE // SLC_BLOCK
    cols = NSA_GROUP * TILE
    kv_heads = range(2)
    n_far = jnp.maximum(i - 1, 0)
    n_far_groups = (n_far + 1) >> 1
    last = nb - 1
    t_near = jnp.maximum(i - 1, 0)
    t_edge = jnp.maximum(i - 2, 0)

    dims_of = lambda a: slice(a * HEAD_DIM, (a + 1) * HEAD_DIM)

    def key_tile(k_ref, t):
        return k_ref[0, pl.ds(pl.multiple_of(t * TILE, TILE), TILE), :]

    def scores_of(k_ref, t, a):
        return lax.dot_general(key_tile(k_ref, t), q4s[a], _NT, preferred_element_type=F32)

    lane = lax.broadcasted_iota(I32, (TILE, LANES), 1)
    lo_half = lane < HEAD_DIM
    qpos = i * TILE + (lax.broadcasted_iota(I32, (n_cmp, cols), 1) & (TILE - 1))
    cmp_valid = CMP_STRIDE * lax.broadcasted_iota(I32, (n_cmp, cols), 0) + (CMP_LEN - 1) <= qpos
    key = lax.broadcasted_iota(I32, (TILE, TILE), 0)
    qry = lax.broadcasted_iota(I32, (TILE, TILE), 1)
    diag_neg = _group_lanes(jnp.where(key <= qry, 0.0, NEG_INF))
    edge_neg = _group_lanes(jnp.where(key > qry, 0.0, NEG_INF))
    qall = q_ref[0]
    q4s = []

    for a in kv_heads:
        keep = lo_half if a == 0 else jnp.logical_not(lo_half)
        heads = []
        for g in range(NSA_GROUP):
            cb = a * 2 + g // 2
            x = qall[:, cb * LANES:(cb + 1) * LANES]
            if g % 2 != a:
                x = _swap_halves(x)
            heads.append(jnp.where(keep, x, jnp.zeros_like(x)))
        q4s.append(jnp.concatenate(heads, axis=0))
        far_bias = bias_ref[1, a, 0:1, :]

        sa_ref[a, 0] = scores_of(ks_ref, i, a) + (bias_ref[0, a] + diag_neg)
        sa_ref[a, 1] = scores_of(ks_ref, t_near, a) + bias_ref[1, a]
        sw_ref[a, 0] = scores_of(kw_ref, i, a) + (bias_ref[0, a] + diag_neg)
        sw_ref[a, 1] = scores_of(kw_ref, t_near, a) + bias_ref[1, a]
        sw_ref[a, 2] = scores_of(kw_ref, t_edge, a) + (far_bias + edge_neg)

        s_c = lax.dot_general(kc_ref[0], q4s[a], _NT, preferred_element_type=F32)
        s_c = jnp.where(cmp_valid, s_c, NEG_INF)
        m_c = jnp.max(s_c, axis=0, keepdims=True)
        e_c = jnp.where(cmp_valid, jnp.exp2(s_c - m_c), 0.0)
        l_c = jnp.sum(e_c, axis=0, keepdims=True)
        p_c = e_c / jnp.where(l_c > 0.0, l_c, 1.0)
        oc_ref[a] = jnp.dot(vct_ref[0, dims_of(a), :], p_c.astype(BF16), preferred_element_type=F32)

        p_sum = p_c[:, 0:TILE]
        for g in range(1, NSA_GROUP):
            p_sum = p_sum + p_c[:, g * TILE:(g + 1) * TILE]
        p_hi = p_sum.astype(BF16)
        p_lo = (p_sum - p_hi.astype(F32)).astype(BF16)
        imp = (jnp.dot(c2s_ref[...], p_hi, preferred_element_type=F32)
               + jnp.dot(c2s_ref[...], p_lo, preferred_element_type=F32))
        j = lax.broadcasted_iota(I32, imp.shape, 0)
        qb = (i * TILE + lax.broadcasted_iota(I32, imp.shape, 1)) >> int(math.log2(SLC_BLOCK))
        forced = (j == 0) | ((j <= qb) & (j > qb - SLC_LOCAL))
        imp = jnp.where(forced, jnp.inf, jnp.where(j > qb, -jnp.inf, imp))
        rank = _rank_before(imp, n_slc)
        selneg = jnp.where((rank < SLC_TOPN) & (j <= qb), 0.0, NEG_INF)
        selneg_ref[a] = selneg
        for c in range(per_tile):
            radd_ref[a, c:c + 1, :] = _group_lanes(selneg_ref[a, pl.ds(per_tile * i + c, 1), :])
            near_row = _group_lanes(selneg_ref[a, pl.ds(per_tile * t_near + c, 1), :])
            radd_ref[a, per_tile + c:per_tile + c + 1, :] = jnp.where(i >= 1, near_row, NEG_INF)
        for blk in range(n_slc):
            row = far_bias + _group_lanes(selneg[blk:blk + 1, :])
            r = 2 * per_tile + blk
            radd_ref[a, r:r + 1, :] = jnp.where(blk // per_tile < n_far, row, NEG_INF)
        _init_state(m_ref.at[a], l_ref.at[a], acc_ref.at[a])

    def update(jg, a, buf):
        first = jg == 0
        t0 = jnp.where(first, i, 2 * jg - 2)
        t1 = jnp.where(first, t_near, jnp.minimum(2 * jg - 1, last))
        scores, adds = [], []
        for t in range(2):
            for c in range(per_tile):
                scores.append(buf[a, t, c * SLC_BLOCK:(c + 1) * SLC_BLOCK, :])
                adds.append(radd_ref[a, pl.ds(2 * per_tile * jg + per_tile * t + c, 1), :])
        _softmax_pv(scores, adds, [vst_ref[0, t0], vst_ref[0, t1]], [(dims_of(a), slice(None))],
                    m_ref.at[a], l_ref.at[a], acc_ref.at[a])

    def step(jg, src, dst):
        for a in kv_heads:
            dst[a, 0] = scores_of(ks_ref, 2 * jg, a)
            dst[a, 1] = scores_of(ks_ref, jnp.minimum(2 * jg + 1, last), a)
            update(jg, a, src)

    def two_steps(jj, carry):
        step(2 * jj, sa_ref, sb_ref)
        step(2 * jj + 1, sb_ref, sa_ref)
        return carry

    lax.fori_loop(0, n_far_groups >> 1, two_steps, 0)

    @pl.when((n_far_groups & 1) == 1)
    def _():
        step(n_far_groups - 1, sa_ref, sb_ref)
        for a in kv_heads:
            update(n_far_groups, a, sb_ref)

    @pl.when((n_far_groups & 1) == 0)
    def _():
        for a in kv_heads:
            update(n_far_groups, a, sa_ref)

    gates = jax.nn.sigmoid(gt_ref[0, 0])
    zero_row = jnp.zeros((1, cols), F32)
    pieces = []
    for a in kv_heads:
        os_ref[a] = acc_ref[a] / l_ref[a]
        _init_state(m_ref.at[a], l_ref.at[a], acc_ref.at[a])
        _softmax_pv([sw_ref[a, 0], sw_ref[a, 1], sw_ref[a, 2]],
                    [zero_row, zero_row + jnp.where(i >= 1, 0.0, NEG_INF), zero_row + jnp.where(i >= 2, 0.0, NEG_INF)],
                    [vwt_ref[0, i], vwt_ref[0, t_near], vwt_ref[0, t_edge]], [(dims_of(a), slice(None))],
                    m_ref.at[a], l_ref.at[a], acc_ref.at[a])
        o_w = acc_ref[a] / l_ref[a]
        for g in range(NSA_GROUP):
            c0 = 3 * (NSA_GROUP * a + g)
            ls = slice(g * TILE, (g + 1) * TILE)
            pieces.append(gates[c0:c0 + 1, :] * oc_ref[a, :, ls] + gates[c0 + 1:c0 + 2, :] * os_ref[a, :, ls]
                          + gates[c0 + 2:c0 + 3, :] * o_w[:, ls])
    o_ref[0] = jnp.concatenate(pieces, axis=0).T.astype(o_ref.dtype)


def _cmp_to_slc(S):
    n_cmp_pad = S // CMP_STRIDE
    n_slc = S // SLC_BLOCK
    ci = np.arange(n_cmp_pad)[:, None] * CMP_STRIDE
    sj = np.arange(n_slc)[None, :] * SLC_BLOCK
    c2s = ((ci < sj + SLC_BLOCK) & (ci + CMP_LEN > sj)).astype(np.float32)
    c2s[(S - CMP_LEN) // CMP_STRIDE + 1:] = 0.0
    return jnp.asarray(c2s.T, BF16)


def _nsa_attention(proj, vt, gate_t, kcmp, vcmp_t, bias):
    B, S, _ = proj.shape
    assert S % TILE == 0 and NSA_KV_HEADS == 4 and 2 * HEAD_DIM == LANES
    nq = S // TILE
    n_cmp = kcmp.shape[1]
    n_slc = S // SLC_BLOCK
    qw = 2 * NSA_GROUP * HEAD_DIM
    q_blocks = N_HEADS * HEAD_DIM // LANES
    kv_blocks = NSA_KV_HEADS * HEAD_DIM // LANES

    def k_spec(which):
        base = q_blocks + which * kv_blocks
        return pl.BlockSpec((1, S, LANES), lambda b, p, i: (b, 0, base + p))

    def vt_spec(which):
        base = which * kv_blocks
        return pl.BlockSpec((1, nq, LANES, TILE), lambda b, p, i: (b, 0, base + p, 0))

    state = pltpu.VMEM((2, HEAD_DIM, NSA_GROUP * TILE), F32)
    stat = pltpu.VMEM((2, 1, NSA_GROUP * TILE), F32)
    return pl.pallas_call(
        _nsa_body,
        grid=(B, 2, nq),
        in_specs=[pl.BlockSpec((1, TILE, qw), lambda b, p, i: (b, i, p)),
                  pl.BlockSpec((1, n_cmp, LANES), lambda b, p, i: (b, 0, p)),
                  pl.BlockSpec((1, LANES, n_cmp), lambda b, p, i: (b, p, 0)),
                  k_spec(2), vt_spec(0), k_spec(3), vt_spec(1),
                  pl.BlockSpec((1, 1, LANES, TILE), lambda b, p, i: (b, i, p, 0)),
                  pl.BlockSpec((2, 2, TILE, NSA_GROUP * TILE), lambda b, p, i: (0, p, 0, 0)),
                  pl.BlockSpec((n_slc, n_cmp), lambda b, p, i: (0, 0))],
        out_specs=pl.BlockSpec((1, TILE, qw), lambda b, p, i: (b, i, p)),
        out_shape=jax.ShapeDtypeStruct((B, S, N_HEADS * HEAD_DIM), BF16),
        scratch_shapes=[pltpu.VMEM((2, n_slc, TILE), F32),
                        pltpu.VMEM((2, 2 * (TILE // SLC_BLOCK) + n_slc, NSA_GROUP * TILE), F32),
                        pltpu.VMEM((2, 2, TILE, NSA_GROUP * TILE), F32),
                        pltpu.VMEM((2, 2, TILE, NSA_GROUP * TILE), F32),
                        pltpu.VMEM((2, 3, TILE, NSA_GROUP * TILE), F32),
                        state, state, stat, stat, state],
        compiler_params=_cparams("parallel", "parallel", "arbitrary"),
        name="nsa_attention",
    )(proj, kcmp, vcmp_t, proj, vt, proj, vt, gate_t, _heads_on_lanes(bias, NSA_GROUP), _cmp_to_slc(S))


def _split_bf16(x):
    hi = x.astype(BF16)
    return hi, (x - hi.astype(F32)).astype(BF16)


_ROW_OF_EXPERT = np.arange(N_EXPERTS).reshape(N_GROUPS, EXPERTS_PER_GROUP).T.reshape(-1)


def _route(x, w_ref, b_ref, tri_ref, idx_ref, wt_ref, pos_ref, cnt_ref, base_ref):
    @pl.when(pl.program_id(0) == 0)
    def _():
        base_ref[...] = jnp.zeros(base_ref.shape, F32)

    x_hi, x_lo = _split_bf16(x)
    w_hi, w_lo = _split_bf16(w_ref[...])
    logits = (lax.dot_general(w_hi, x_hi, _NT, preferred_element_type=F32)
              + lax.dot_general(w_hi, x_lo, _NT, preferred_element_type=F32)
              + lax.dot_general(w_lo, x_hi, _NT, preferred_element_type=F32)) + b_ref[:, 0:1]
    m = jnp.max(logits, axis=0, keepdims=True)
    e = jnp.exp(logits - m)
    probs = e / jnp.sum(e, axis=0, keepdims=True)
    pk = [probs[k * N_GROUPS:(k + 1) * N_GROUPS] for k in range(EXPERTS_PER_GROUP)]
    hi1, lo1 = jnp.maximum(pk[0], pk[1]), jnp.minimum(pk[0], pk[1])
    hi2, lo2 = jnp.maximum(pk[2], pk[3]), jnp.minimum(pk[2], pk[3])
    score = jnp.maximum(hi1, hi2) + jnp.maximum(jnp.minimum(hi1, hi2), jnp.maximum(lo1, lo2))
    grp = lax.broadcasted_iota(I32, score.shape, 0)
    best = jnp.min(jnp.where(score == jnp.max(score, axis=0, keepdims=True), grp, N_GROUPS),
                   axis=0, keepdims=True)
    v = [jnp.sum(jnp.where(grp == best, p, 0.0), axis=0, keepdims=True) for p in pk]
    v1 = jnp.maximum(jnp.maximum(v[0], v[1]), jnp.maximum(v[2], v[3]))
    i1 = jnp.where(v[0] == v1, 0, jnp.where(v[1] == v1, 1, jnp.where(v[2] == v1, 2, 3)))
    rest = [jnp.where(i1 == k, -1.0, v[k]) for k in range(EXPERTS_PER_GROUP)]
    v2 = jnp.maximum(jnp.maximum(rest[0], rest[1]), jnp.maximum(rest[2], rest[3]))
    i2 = jnp.where(rest[0] == v2, 0, jnp.where(rest[1] == v2, 1, jnp.where(rest[2] == v2, 2, 3)))
    tot = v1 + v2
    idx_ref[...] = jnp.concatenate([best * EXPERTS_PER_GROUP + i1, best * EXPERTS_PER_GROUP + i2], axis=0)
    wt_ref[...] = jnp.concatenate([v1 / tot, v2 / tot], axis=0)

    row = lax.broadcasted_iota(I32, logits.shape, 0)
    hot = [jnp.where(row == ik * N_GROUPS + best, 1.0, 0.0) for ik in (i1, i2)]
    both = (hot[0] + hot[1]).astype(BF16)
    chunks = [slice(c * LANES, (c + 1) * LANES) for c in range(logits.shape[1] // LANES)]
    prefix = [jnp.dot(both[:, ls], tri_ref[...], preferred_element_type=F32) for ls in chunks]
    run = base_ref[:, 0:1]
    pos = [[], []]
    for ls, pre in zip(chunks, prefix):
        before = run + pre - 1.0
        for k in range(2):
            pos[k].append(jnp.sum(hot[k][:, ls] * before, axis=0, keepdims=True))
        run = run + pre[:, LANES - 1:LANES]
    pos_ref[...] = jnp.concatenate([jnp.concatenate(pos[0], axis=1), jnp.concatenate(pos[1], axis=1)],
                                   axis=0).astype(I32)
    base_ref[...] = jnp.broadcast_to(run, base_ref.shape)
    cnt_ref[...] = jnp.broadcast_to(run, cnt_ref.shape)


def _router_operands(router_w, router_b):
    w = router_w.T[_ROW_OF_EXPERT]
    b = jnp.broadcast_to(router_b[_ROW_OF_EXPERT][:, None], (N_EXPERTS, LANES))
    tri = jnp.asarray(np.triu(np.ones((LANES, LANES), np.float32)), BF16)
    return w, b, tri


def _proj_ln_route_body(a_ref, w_ref, x_ref, g_ref, b_ref, rw_ref, rb_ref, tri_ref,
                        o_ref, ob_ref, idx_ref, wt_ref, pos_ref, cnt_ref, base_ref):
    y = jnp.dot(a_ref[...], w_ref[...], preferred_element_type=F32)
    out = _layer_norm_rows(DEEPNORM_ALPHA * x_ref[...] + y, g_ref[...], b_ref[...])
    o_ref[...] = out
    ob_ref[...] = out.astype(BF16)
    _route(out, rw_ref, rb_ref, tri_ref, idx_ref, wt_ref, pos_ref, cnt_ref, base_ref)


def _proj_ln_route(a, w, x, g, b, router_w, router_b):
    M, K = a.shape
    D = w.shape[1]
    assert M % LN_TM == 0 and LN_TM % LANES == 0
    row = pl.BlockSpec((LN_TM, D), lambda i: (i, 0))
    vec = pl.BlockSpec((1, D), lambda i: (0, 0))
    whole = lambda shape: pl.BlockSpec(shape, lambda i: (0, 0))
    tok = lambda dt: jax.ShapeDtypeStruct((2, M), dt)
    tok_spec = pl.BlockSpec((2, LN_TM), lambda i: (0, i))
    h, hb, idx, wts, pos, cnt = pl.pallas_call(
        _proj_ln_route_body,
        grid=(M // LN_TM,),
        in_specs=[pl.BlockSpec((LN_TM, K), lambda i: (i, 0)), whole((K, D)), row, vec, vec,
                  whole((N_EXPERTS, D)), whole((N_EXPERTS, LANES)), whole((LANES, LANES))],
        out_specs=[row, row, tok_spec, tok_spec, tok_spec, whole((N_EXPERTS, LANES))],
        out_shape=[jax.ShapeDtypeStruct((M, D), F32), jax.ShapeDtypeStruct((M, D), BF16),
                   tok(I32), tok(F32), tok(I32), jax.ShapeDtypeStruct((N_EXPERTS, LANES), F32)],
        scratch_shapes=[pltpu.VMEM((N_EXPERTS, LANES), F32)],
        compiler_params=_cparams("arbitrary"),
        name="out_proj_ln_route",
    )(a, w, x, g.reshape(1, D), b.reshape(1, D), *_router_operands(router_w, router_b))
    counts = cnt[np.argsort(_ROW_OF_EXPERT), 0].astype(I32)
    return h, hb, (idx, wts, pos, counts)


def _expert_body(blk_e_ref, n_used_ref, x_ref, wg_ref, wu_ref, wd_ref, o_ref, wg_b, wu_b, wd_b):
    i = pl.program_id(0)

    @pl.when((i == 0) | (blk_e_ref[i] != blk_e_ref[jnp.maximum(i - 1, 0)]))
    def _():
        wg_b[...] = wg_ref[0, 0].astype(BF16)
        wu_b[...] = wu_ref[0, 0].astype(BF16)
        wd_b[...] = wd_ref[0, 0].astype(BF16)

    @pl.when(i < n_used_ref[0])
    def _():
        x = x_ref[...]
        gate = jnp.dot(x, wg_b[...], preferred_element_type=F32)
        up = jnp.dot(x, wu_b[...], preferred_element_type=F32)
        hid = (gate * jax.nn.sigmoid(gate) * up).astype(BF16)
        o_ref[...] = jnp.dot(hid, wd_b[...], preferred_element_type=F32).astype(o_ref.dtype)

    @pl.when(i >= n_used_ref[0])
    def _():
        o_ref[...] = jnp.zeros(o_ref.shape, o_ref.dtype)


def _experts(xs, blk_e, n_used, wg, wu, wd, layer):
    R, D = xs.shape
    n_blk = R // MOE_TB

    def live(i, be, nu):
        return jnp.minimum(i, nu[0] - 1)

    grid_spec = pltpu.PrefetchScalarGridSpec(
        num_scalar_prefetch=2,
        grid=(n_blk,),
        in_specs=[pl.BlockSpec((MOE_TB, D), lambda i, be, nu: (live(i, be, nu), 0)),
                  pl.BlockSpec((1, 1, D, D_EXPERT), lambda i, be, nu: (layer, be[i], 0, 0)),
                  pl.BlockSpec((1, 1, D, D_EXPERT), lambda i, be, nu: (layer, be[i], 0, 0)),
                  pl.BlockSpec((1, 1, D_EXPERT, D), lambda i, be, nu: (layer, be[i], 0, 0))],
        out_specs=pl.BlockSpec((MOE_TB, D), lambda i, be, nu: (i, 0)),
        scratch_shapes=[pltpu.VMEM((D, D_EXPERT), BF16), pltpu.VMEM((D, D_EXPERT), BF16),
                        pltpu.VMEM((D_EXPERT, D), BF16)],
    )
    return pl.pallas_call(
        _expert_body,
        grid_spec=grid_spec,
        out_shape=jax.ShapeDtypeStruct((R, D), BF16),
        compiler_params=_cparams("arbitrary"),
        name="moe_experts",
    )(blk_e, n_used, xs, wg, wu, wd)


def _combine_ln_body(x_ref, y0_ref, y1_ref, wt_ref, g_ref, b_ref, o_ref, ob_ref):
    ffn = y0_ref[...] * wt_ref[:, 0:1] + y1_ref[...] * wt_ref[:, HEAD_DIM:HEAD_DIM + 1]
    out = _layer_norm_rows(DEEPNORM_ALPHA * x_ref[...] + ffn, g_ref[...], b_ref[...])
    o_ref[...] = out
    ob_ref[...] = out.astype(BF16)


def _combine_ln(x, y0, y1, wt, g, b):
    M, D = x.shape
    assert M % LN_TM == 0
    row = pl.BlockSpec((LN_TM, D), lambda i: (i, 0))
    vec = pl.BlockSpec((1, D), lambda i: (0, 0))
    return pl.pallas_call(
        _combine_ln_body,
        grid=(M // LN_TM,),
        in_specs=[row, row, row, pl.BlockSpec((LN_TM, LANES), lambda i: (i, 0)), vec, vec],
        out_specs=[row, row],
        out_shape=[jax.ShapeDtypeStruct((M, D), F32), jax.ShapeDtypeStruct((M, D), BF16)],
        compiler_params=_cparams("parallel"),
        name="moe_combine_ln",
    )(x, y0, y1, wt, g.reshape(1, D), b.reshape(1, D))


def _moe_ln(h, hb, routing, wg, wu, wd, layer, g, b):
    N, D = h.shape
    A = 2 * N
    idx, wts, pos, counts = routing
    starts = jnp.cumsum(counts) - counts
    padded = (counts + MOE_TB - 1) // MOE_TB * MOE_TB
    pends = jnp.cumsum(padded)
    pstarts = pends - padded
    R = A + N_EXPERTS * MOE_TB
    n_blk = R // MOE_TB
    experts = jnp.arange(N_EXPERTS, dtype=I32)
    dest = pos + jnp.sum(jnp.where(idx[None] == experts[:, None, None], pstarts[:, None, None], 0), axis=0)
    tok = jnp.broadcast_to(jnp.arange(N, dtype=I32)[None, :], (2, N))
    _, tok_sorted = lax.sort_key_val(dest.reshape(A), tok.reshape(A))
    blk_row0 = jnp.arange(n_blk, dtype=I32) * MOE_TB
    blk_e = jnp.minimum(jnp.sum((pends[None, :] <= blk_row0[:, None]).astype(I32), axis=1), N_EXPERTS - 1)
    hot = blk_e[:, None] == experts[None, :]
    compact0 = blk_row0 + jnp.sum(jnp.where(hot, (starts - pstarts)[None, :], 0), axis=1)
    compact = jnp.remainder(compact0[:, None] + jnp.arange(MOE_TB, dtype=I32)[None, :], A).reshape(R)
    n_used = (pends[-1:] // MOE_TB).astype(I32)
    xs = hb[tok_sorted[compact]]
    yb = _experts(xs, blk_e, n_used, wg, wu, wd, layer)
    wt = jnp.concatenate([jnp.broadcast_to(wts[k][:, None], (N, HEAD_DIM)) for k in range(2)], axis=1)
    return _combine_ln(h, yb[dest[0]], yb[dest[1]], wt, g, b)


def _moba_layer(h, w_in, w_out, bias, g, b, router_w, router_b, B, S):
    HD = N_HEADS * HEAD_DIM
    qk, vt = _in_proj(h, w_in[:, :2 * HD].astype(BF16), _query_scale(HD, 2 * HD),
                      [w_in[:, 2 * HD:].T.astype(BF16)], [BF16], B, S)
    att = _moba_attention(qk.reshape(B, S, 2 * HD), vt, bias)
    return _proj_ln_route(att.reshape(B * S, HD), w_out.astype(BF16), h, g, b, router_w, router_b)


def _nsa_layer(h, hb, w_in, w_out, pos_k, pos_v, ck_w1, ck_w2, cv_w1, cv_w2, bias, g, b, router_w, router_b, B, S):
    HD = N_HEADS * HEAD_DIM
    kvw = NSA_KV_HEADS * HEAD_DIM
    col = lambda k: slice(HD + k * kvw, HD + (k + 1) * kvw)
    w_rows = jnp.concatenate([w_in[:, :HD + 2 * kvw], w_in[:, col(2)], w_in[:, col(4)]], axis=1)
    w_vt = jnp.concatenate([w_in[:, col(3)], w_in[:, col(5)]], axis=1).T
    per_pair = 3 * N_HEADS // 2
    wg = w_in[:, HD + 6 * kvw:].reshape(D_MODEL, 2, per_pair)
    wg = jnp.pad(wg, ((0, 0), (0, 0), (0, LANES - per_pair))).reshape(D_MODEL, 2 * LANES).T
    proj, vt, gate_t = _in_proj(hb, w_rows.astype(BF16), _query_scale(HD, HD + 4 * kvw),
                                [w_vt.astype(BF16), wg.astype(BF16)], [BF16, F32], B, S)
    proj = proj.reshape(B, S, HD + 4 * kvw)

    def grouped(t):
        t = t.reshape(B, S, NSA_KV_HEADS, HEAD_DIM).transpose(0, 2, 1, 3)
        return t.reshape(B, NSA_KV_HEADS, S // CMP_STRIDE, CMP_STRIDE * HEAD_DIM)

    kcmp = _compress(grouped(proj[..., HD:HD + kvw]), pos_k, ck_w1, ck_w2)
    vcmp = _compress(grouped(proj[..., HD + kvw:HD + 2 * kvw]), pos_v, cv_w1, cv_w2)
    n_cmp = kcmp.shape[2]
    kcmp = kcmp.transpose(0, 2, 1, 3).reshape(B, n_cmp, kvw)
    vcmp_t = vcmp.transpose(0, 1, 3, 2).reshape(B, kvw, n_cmp)
    att = _nsa_attention(proj, vt, gate_t, kcmp, vcmp_t, bias)
    return _proj_ln_route(att.reshape(B * S, HD), w_out.astype(BF16), h, g, b, router_w, router_b)


def kernel(x, rel_bias, router_w, router_b, ln_g, ln_b, moba_w_in, moba_w_out, nsa_w_in, nsa_w_out,
           nsa_pos_k, nsa_pos_v, nsa_ck_w1, nsa_ck_w2, nsa_cv_w1, nsa_cv_w2,
           moe_w_gate, moe_w_up, moe_w_down):
    B, S, D = x.shape
    bias = _bias_tiles(rel_bias)
    h = x.reshape(B * S, D)
    h, hb, routing = _moba_layer(h, moba_w_in[0], moba_w_out[0], bias, ln_g[0, 0], ln_b[0, 0],
                                 router_w, router_b, B, S)
    h, hb = _moe_ln(h, hb, routing, moe_w_gate, moe_w_up, moe_w_down, 0, ln_g[0, 1], ln_b[0, 1])
    h, hb, routing = _nsa_layer(h, hb, nsa_w_in[0], nsa_w_out[0], nsa_pos_k[0], nsa_pos_v[0],
                                nsa_ck_w1[0], nsa_ck_w2[0], nsa_cv_w1[0], nsa_cv_w2[0],
                                bias, ln_g[1, 0], ln_b[1, 0], router_w, router_b, B, S)
    h, hb = _moe_ln(h, hb, routing, moe_w_gate, moe_w_up, moe_w_down, 1, ln_g[1, 1], ln_b[1, 1])
    return h.reshape(B, S, D)
```

```python
import math
from functools import partial

import numpy as np
import jax
import jax.numpy as jnp
from jax import lax
from jax.experimental import pallas as pl
from jax.experimental.pallas import tpu as pltpu

F32, BF16, I32 = jnp.float32, jnp.bfloat16, jnp.int32

D_MODEL = 1024
N_HEADS = 16
HEAD_DIM = 64
DEPTH = 2
NEG_INF = -1e30
LN_EPS = 1e-5
MOBA_BLOCK = 256
MOBA_TOPK = 3
NSA_KV_HEADS = 4
NSA_GROUP = N_HEADS // NSA_KV_HEADS
CMP_LEN = 32
CMP_STRIDE = 16
CMP_HIDDEN = 256
SLC_BLOCK = 64
SLC_TOPN = 16
SLC_LOCAL = 2
WINDOW = 512
REL_BUCKETS = 32
REL_MAX_DIST = 128
N_EXPERTS = 32
N_GROUPS = 8
EXPERTS_PER_GROUP = N_EXPERTS // N_GROUPS
D_EXPERT = 512
DEEPNORM_ALPHA = (2 * DEPTH) ** 0.25
LOG2E = math.log2(math.e)
Q_SCALE = HEAD_DIM ** -0.5 * LOG2E

LANES = 128
SUBLANES = 8
ONES_ROWS = 16
TILE = MOBA_BLOCK
MM_TM = 512
MM_TN = 1024
LN_TM = 1024
MOE_TB = 512
assert WINDOW == 2 * TILE and TILE % SLC_BLOCK == 0 and SLC_BLOCK % SUBLANES == 0 and MM_TM % TILE == 0
VMEM_LIMIT = 56 * 1024 * 1024

_NT = (((1,), (1,)), ((), ()))


def _cparams(*sem):
    return pltpu.CompilerParams(dimension_semantics=sem, vmem_limit_bytes=VMEM_LIMIT)


def _in_proj_body(n_t, a_ref, w_ref, c_ref, *refs):
    wt_refs, o_ref, ot_refs = refs[:n_t], refs[n_t], refs[n_t + 1:]
    a = a_ref[...].astype(BF16)
    for c in range(o_ref.shape[1] // MM_TN):
        cols = slice(c * MM_TN, (c + 1) * MM_TN)
        acc = jnp.dot(a, w_ref[:, cols], preferred_element_type=F32)
        o_ref[:, cols] = (acc * c_ref[:, cols]).astype(o_ref.dtype)
    for wt_ref, ot_ref in zip(wt_refs, ot_refs):
        r = lax.dot_general(wt_ref[...], a, _NT, preferred_element_type=F32)
        for t in range(ot_ref.shape[1]):
            ot_ref[0, t] = r[:, t * TILE:(t + 1) * TILE].astype(ot_ref.dtype)


def _in_proj(a, w, col_scale, w_ts, t_dtypes, B, S):
    M, K = a.shape
    N = w.shape[1]
    assert M == B * S and S % MM_TM == 0 and N % MM_TN == 0
    per_seq = S // MM_TM
    sub = MM_TM // TILE
    whole = lambda shape: pl.BlockSpec(shape, lambda i: (0, 0))
    outs = pl.pallas_call(
        partial(_in_proj_body, len(w_ts)),
        grid=(M // MM_TM,),
        in_specs=[pl.BlockSpec((MM_TM, K), lambda i: (i, 0)), whole((K, N)), whole((1, N))]
                 + [whole(w_t.shape) for w_t in w_ts],
        out_specs=[pl.BlockSpec((MM_TM, N), lambda i: (i, 0))]
                  + [pl.BlockSpec((1, sub, w_t.shape[0], TILE), lambda i: (i // per_seq, i % per_seq, 0, 0))
                     for w_t in w_ts],
        out_shape=[jax.ShapeDtypeStruct((M, N), BF16)]
                  + [jax.ShapeDtypeStruct((B, S // TILE, w_t.shape[0], TILE), dt) for w_t, dt in zip(w_ts, t_dtypes)],
        compiler_params=_cparams("parallel"),
        name="in_proj",
    )(a, w, col_scale.reshape(1, N), *w_ts)
    return outs


def _query_scale(n_query_cols, n_cols):
    return jnp.where(jnp.arange(n_cols) < n_query_cols, Q_SCALE, 1.0).astype(F32)


def _layer_norm_rows(z, g, b):
    mu = jnp.mean(z, axis=-1, keepdims=True)
    zc = z - mu
    var = jnp.mean(zc * zc, axis=-1, keepdims=True)
    return zc * lax.rsqrt(var + LN_EPS) * g + b


def _t5_bucket_np(rel):
    n = np.maximum(rel, 0)
    max_exact = REL_BUCKETS // 2
    nf = np.maximum(n, 1).astype(np.float32)
    large = max_exact + (np.log(nf / np.float32(max_exact))
                         / np.float32(math.log(REL_MAX_DIST / max_exact))
                         * np.float32(REL_BUCKETS - max_exact)).astype(np.int32)
    large = np.minimum(large, REL_BUCKETS - 1)
    return np.where(n < max_exact, n, large).astype(np.int32)


def _bias_body(tbl_ref, bk_ref, o_ref):
    h = pl.program_id(0)
    for dl in range(2):
        bk = bk_ref[dl]
        acc = jnp.zeros((TILE, TILE), F32)
        for b in range(REL_BUCKETS):
            acc = jnp.where(bk == b, tbl_ref[h * REL_BUCKETS + b], acc)
        o_ref[dl, 0] = acc * LOG2E


def _bias_tiles(rel_bias):
    key = np.arange(TILE)[:, None]
    qry = np.arange(TILE)[None, :]
    assert int(_t5_bucket_np(np.array(TILE + 1))) == REL_BUCKETS - 1
    bk = np.stack([_t5_bucket_np(qry - key), _t5_bucket_np(TILE + qry - key)])
    return pl.pallas_call(
        _bias_body,
        grid=(N_HEADS,),
        in_specs=[pl.BlockSpec(memory_space=pltpu.SMEM),
                  pl.BlockSpec((2, TILE, TILE), lambda h: (0, 0, 0))],
        out_specs=pl.BlockSpec((2, 1, TILE, TILE), lambda h: (0, h, 0, 0)),
        out_shape=jax.ShapeDtypeStruct((2, N_HEADS, TILE, TILE), F32),
        name="t5_bias_tiles",
    )(rel_bias.T.reshape(-1), jnp.asarray(bk))


def _heads_on_lanes(bias, per_block):
    two, H, T, _ = bias.shape
    b = bias.reshape(two, H // per_block, per_block, T, T).transpose(0, 1, 3, 2, 4)
    return b.reshape(two, H // per_block, T, per_block * T)


def _init_state(m_ref, l_ref, acc_ref):
    m_ref[...] = jnp.full(m_ref.shape, NEG_INF, F32)
    l_ref[...] = jnp.zeros(l_ref.shape, F32)
    acc_ref[...] = jnp.zeros(acc_ref.shape, F32)


def _rank_before(vals, rows):
    idx = lax.broadcasted_iota(I32, vals.shape, 0)
    rank = jnp.zeros(vals.shape, I32)
    for m in range(rows):
        row = vals[m:m + 1, :]
        beats = (row > vals) | ((row == vals) & (idx > m))
        rank = rank + jnp.where(beats, 1, 0)
    return rank


MOBA_STREAMS = 8


def _softmax_pv(scores, adds, vts, heads, m_ref, l_ref, acc_ref):
    def fold(x, op):
        return op(x.reshape(x.shape[0] // SUBLANES, SUBLANES, x.shape[1]), axis=0)

    m_prev = m_ref[...]
    m_part = None
    for s, add in zip(scores, adds):
        part = fold(s, jnp.max) + add
        m_part = part if m_part is None else jnp.maximum(m_part, part)
    m_new = jnp.maximum(m_prev, jnp.max(m_part, axis=0, keepdims=True))
    a = jnp.exp2(m_prev - m_new)
    probs = [jnp.exp2(s - (m_new - add)) for s, add in zip(scores, adds)]
    vt = jnp.concatenate(vts, axis=1)
    pb = jnp.concatenate([p.astype(BF16) for p in probs], axis=0)
    ones = jnp.ones((ONES_ROWS, vt.shape[1]), BF16)
    pv = jnp.concatenate([jnp.dot(jnp.concatenate([vt[rows], ones], axis=0), pb[:, cols],
                                  preferred_element_type=F32) for rows, cols in heads], axis=1)
    l_ref[...] = a * l_ref[...] + pv[HEAD_DIM:HEAD_DIM + 1, :]
    acc_ref[...] = a * acc_ref[...] + pv[:HEAD_DIM, :]
    m_ref[...] = m_new


def _moba_body(q_ref, k_ref, vt_ref, bias_ref, o_ref, kmean_ref, radd_ref, sa_ref, sb_ref, m_ref, l_ref, acc_ref):
    i = pl.program_id(2)
    nb = k_ref.shape[1] // TILE
    streams = range(MOBA_STREAMS)
    lanes_of = lambda s: slice(s * LANES, (s + 1) * LANES)

    @pl.when(i == 0)
    def _():
        for s in streams:
            for n in range(nb):
                kb = k_ref[0, n * TILE:(n + 1) * TILE, lanes_of(s)].astype(F32)
                kmean_ref[s, n:n + 1, :] = jnp.sum(kb, axis=0, keepdims=True) * (1.0 / TILE)

    n_far = jnp.maximum(i - 1, 0)
    n_far_groups = (n_far + 1) >> 1
    last = nb - 1

    def key_tile(t, s):
        return k_ref[0, pl.ds(pl.multiple_of(t * TILE, TILE), TILE), lanes_of(s)]

    def far_tiles(j):
        return 2 * j, jnp.minimum(2 * j + 1, last)

    lane = lax.broadcasted_iota(I32, (TILE, LANES), 1)
    key = lax.broadcasted_iota(I32, (TILE, 2 * TILE), 0)
    qry = lax.broadcasted_iota(I32, (TILE, 2 * TILE), 1) & (TILE - 1)
    causal_neg = jnp.where(key <= qry, 0.0, NEG_INF)
    t_near = jnp.maximum(i - 1, 0)
    q2s = []
    for s in streams:
        q = q_ref[0, :, lanes_of(s)]
        zero = jnp.zeros_like(q)
        q2 = jnp.concatenate([jnp.where(lane < HEAD_DIM, q, zero),
                              jnp.where(lane >= HEAD_DIM, q, zero)], axis=0)
        q2s.append(q2)
        sa_ref[s, 0] = (lax.dot_general(key_tile(i, s), q2, _NT, preferred_element_type=F32)
                        + (bias_ref[0, s] + causal_neg))
        sa_ref[s, 1] = lax.dot_general(key_tile(t_near, s), q2, _NT, preferred_element_type=F32) + bias_ref[1, s]
        km = kmean_ref[s]
        k_hi = km.astype(BF16)
        k_lo = (km - k_hi.astype(F32)).astype(BF16)
        gate = (lax.dot_general(k_hi, q2, _NT, preferred_element_type=F32)
                + lax.dot_general(k_lo, q2, _NT, preferred_element_type=F32))
        blk = lax.broadcasted_iota(I32, gate.shape, 0)
        gate = jnp.where(blk < i, gate, -jnp.inf)
        rank = _rank_before(gate, nb)
        neg = jnp.where((rank < MOBA_TOPK) & (blk < i), 0.0, NEG_INF)
        far_bias = bias_ref[1, s, 0:1, :]
        near_row = jnp.full((1, 2 * TILE), NEG_INF, F32)
        radd_ref[s, 0:1, :] = jnp.zeros((1, 2 * TILE), F32)
        for n in range(nb):
            row = neg[n:n + 1, :]
            near_row = jnp.where(n == i - 1, row, near_row)
            radd_ref[s, 2 + n:3 + n, :] = jnp.where(n < n_far, far_bias + row, NEG_INF)
        radd_ref[s, 1:2, :] = near_row
        _init_state(m_ref.at[s], l_ref.at[s], acc_ref.at[s])

    pair_heads = [(slice(h * HEAD_DIM, (h + 1) * HEAD_DIM), slice(h * TILE, (h + 1) * TILE)) for h in range(2)]

    def update(j, s, buf):
        first = j == 0
        t0 = jnp.where(first, i, 2 * j - 2)
        t1 = jnp.where(first, t_near, jnp.minimum(2 * j - 1, last))
        _softmax_pv([buf[s, 0], buf[s, 1]],
                    [radd_ref[s, pl.ds(2 * j, 1), :], radd_ref[s, pl.ds(2 * j + 1, 1), :]],
                    [vt_ref[0, t0, lanes_of(s), :], vt_ref[0, t1, lanes_of(s), :]], pair_heads,
                    m_ref.at[s], l_ref.at[s], acc_ref.at[s])

    def step(j, src, dst):
        for s in streams:
            ta, tb = far_tiles(j)
            dst[s, 0] = lax.dot_general(key_tile(ta, s), q2s[s], _NT, preferred_element_type=F32)
            dst[s, 1] = lax.dot_general(key_tile(tb, s), q2s[s], _NT, preferred_element_type=F32)
            update(j, s, src)

    def two_steps(jj, carry):
        step(2 * jj, sa_ref, sb_ref)
        step(2 * jj + 1, sb_ref, sa_ref)
        return carry

    lax.fori_loop(0, n_far_groups >> 1, two_steps, 0)

    @pl.when((n_far_groups & 1) == 1)
    def _():
        step(n_far_groups - 1, sa_ref, sb_ref)
        for s in streams:
            update(n_far_groups, s, sb_ref)

    @pl.when((n_far_groups & 1) == 0)
    def _():
        for s in streams:
            update(n_far_groups, s, sa_ref)

    for s in streams:
        o = acc_ref[s] / l_ref[s]
        o = jnp.concatenate([o[:, :TILE], o[:, TILE:]], axis=0)
        o_ref[0, :, lanes_of(s)] = o.T.astype(o_ref.dtype)


def _moba_attention(qk, vt, bias):
    B, S, _ = qk.shape
    assert S % TILE == 0 and N_HEADS % (2 * MOBA_STREAMS) == 0 and 2 * HEAD_DIM == LANES
    n_steps = N_HEADS // 2 // MOBA_STREAMS
    nq = S // TILE
    w = MOBA_STREAMS * LANES
    return pl.pallas_call(
        _moba_body,
        grid=(B, n_steps, nq),
        in_specs=[pl.BlockSpec((1, TILE, w), lambda b, p, i: (b, i, p)),
                  pl.BlockSpec((1, S, w), lambda b, p, i: (b, 0, n_steps + p)),
                  pl.BlockSpec((1, nq, w, TILE), lambda b, p, i: (b, 0, p, 0)),
                  pl.BlockSpec((2, MOBA_STREAMS, TILE, 2 * TILE), lambda b, p, i: (0, p, 0, 0))],
        out_specs=pl.BlockSpec((1, TILE, w), lambda b, p, i: (b, i, p)),
        out_shape=jax.ShapeDtypeStruct((B, S, N_HEADS * HEAD_DIM), BF16),
        scratch_shapes=[pltpu.VMEM((MOBA_STREAMS, nq, LANES), F32),
                        pltpu.VMEM((MOBA_STREAMS, 2 + nq, 2 * TILE), F32),
                        pltpu.VMEM((MOBA_STREAMS, 2, TILE, 2 * TILE), F32),
                        pltpu.VMEM((MOBA_STREAMS, 2, TILE, 2 * TILE), F32),
                        pltpu.VMEM((MOBA_STREAMS, 1, 2 * TILE), F32),
                        pltpu.VMEM((MOBA_STREAMS, 1, 2 * TILE), F32),
                        pltpu.VMEM((MOBA_STREAMS, HEAD_DIM, 2 * TILE), F32)],
        compiler_params=_cparams("parallel", "parallel", "arbitrary"),
        name="moba_attention",
    )(qk, qk, vt, _heads_on_lanes(bias, 2))


def _gelu_tanh(x):
    return 0.5 * x * (1.0 + jnp.tanh(math.sqrt(2.0 / math.pi) * (x + 0.044715 * (x * x * x))))


def _compress_body(t_ref, pos_ref, w1_ref, w2_ref, o_ref):
    groups = t_ref.shape[2]
    half = t_ref.shape[3]
    t = t_ref[0].reshape(NSA_KV_HEADS * groups, half).astype(F32)
    first = jnp.dot((t + pos_ref[0:1, :]).astype(BF16), w1_ref[0:half, :], preferred_element_type=F32)
    second = jnp.dot((t + pos_ref[1:2, :]).astype(BF16), w1_ref[half:2 * half, :],
                     preferred_element_type=F32)
    rows = first.shape[0]
    pre = first + pltpu.roll(second, rows - 1, 0)
    out = jnp.dot(_gelu_tanh(pre).astype(BF16), w2_ref[...], preferred_element_type=F32)
    for h in range(NSA_KV_HEADS):
        o_ref[0, h] = out[h * groups:(h + 1) * groups].astype(o_ref.dtype)


def _compress(t, pos, w1, w2):
    B, Hkv, groups, half = t.shape
    return pl.pallas_call(
        _compress_body,
        grid=(B,),
        in_specs=[pl.BlockSpec((1, Hkv, groups, half), lambda b: (b, 0, 0, 0)),
                  pl.BlockSpec((2, half), lambda b: (0, 0)),
                  pl.BlockSpec((2 * half, CMP_HIDDEN), lambda b: (0, 0)),
                  pl.BlockSpec((CMP_HIDDEN, HEAD_DIM), lambda b: (0, 0))],
        out_specs=pl.BlockSpec((1, Hkv, groups, HEAD_DIM), lambda b: (b, 0, 0, 0)),
        out_shape=jax.ShapeDtypeStruct((B, Hkv, groups, HEAD_DIM), BF16),
        compiler_params=_cparams("parallel"),
        name="nsa_compress",
    )(t, pos.reshape(2, half), w1.astype(BF16), w2.astype(BF16))


NSA_PAIRS = 2


def _swap_halves(x):
    return jnp.concatenate([x[:, HEAD_DIM:], x[:, :HEAD_DIM]], axis=1)


def _group_lanes(x):
    return jnp.concatenate([x] * NSA_GROUP, axis=1)


def _nsa_body(q_ref, kc_ref, vct_ref, ks_ref, vst_ref, kw_ref, vwt_ref, gt_ref, bias_ref, c2s_ref,
              o_ref, selneg_ref, radd_ref, sa_ref, sb_ref, sw_ref, oc_ref, os_ref, m_ref, l_ref, acc_ref):
    i = pl.program_id(2)
    nb = ks_ref.shape[1] // TILE
    n_cmp = kc_ref.shape[1]
    n_slc = c2s_ref.shape[0]
    per_tile = TILE // SLC_BLOCK
    cols = NSA_GROUP * TILE
    kv_heads = range(2 * NSA_PAIRS)
    pair_lanes = lambda a: slice((a // 2) * LANES, (a // 2 + 1) * LANES)
    n_far = jnp.maximum(i - 1, 0)
    n_far_groups = (n_far + 1) >> 1
    last = nb - 1
    t_near = jnp.maximum(i - 1, 0)
    t_edge = jnp.maximum(i - 2, 0)

    dims_of = lambda a: slice(a * HEAD_DIM, (a + 1) * HEAD_DIM)

    def key_tile(k_ref, t):
        return k_ref[0, pl.ds(pl.multiple_of(t * TILE, TILE), TILE), :]

    def scores_of(k_ref, t, a):
        return lax.dot_general(key_tile(k_ref, t)[:, pair_lanes(a)], q4s[a], _NT, preferred_element_type=F32)

    lane = lax.broadcasted_iota(I32, (TILE, LANES), 1)
    lo_half = lane < HEAD_DIM
    qpos = i * TILE + (lax.broadcasted_iota(I32, (n_cmp, cols), 1) & (TILE - 1))
    cmp_valid = CMP_STRIDE * lax.broadcasted_iota(I32, (n_cmp, cols), 0) + (CMP_LEN - 1) <= qpos
    key = lax.broadcasted_iota(I32, (TILE, TILE), 0)
    qry = lax.broadcasted_iota(I32, (TILE, TILE), 1)
    diag_neg = _group_lanes(jnp.where(key <= qry, 0.0, NEG_INF))
    edge_neg = _group_lanes(jnp.where(key > qry, 0.0, NEG_INF))
    qall = q_ref[0]
    q4s = []

    for a in kv_heads:
        keep = lo_half if a % 2 == 0 else jnp.logical_not(lo_half)
        heads = []
        for g in range(NSA_GROUP):
            cb = a * 2 + g // 2
            x = qall[:, cb * LANES:(cb + 1) * LANES]
            if g % 2 != a % 2:
                x = _swap_halves(x)
            heads.append(jnp.where(keep, x, jnp.zeros_like(x)))
        q4s.append(jnp.concatenate(heads, axis=0))
        far_bias = bias_ref[1, a, 0:1, :]

        sa_ref[a, 0] = scores_of(ks_ref, i, a) + (bias_ref[0, a] + diag_neg)
        sa_ref[a, 1] = scores_of(ks_ref, t_near, a) + bias_ref[1, a]
        sw_ref[a, 0] = scores_of(kw_ref, i, a) + (bias_ref[0, a] + diag_neg)
        sw_ref[a, 1] = scores_of(kw_ref, t_near, a) + bias_ref[1, a]
        sw_ref[a, 2] = scores_of(kw_ref, t_edge, a) + (far_bias + edge_neg)

        s_c = lax.dot_general(kc_ref[0, :, pair_lanes(a)], q4s[a], _NT, preferred_element_type=F32)
        s_c = jnp.where(cmp_valid, s_c, NEG_INF)
        m_c = jnp.max(s_c, axis=0, keepdims=True)
        e_c = jnp.where(cmp_valid, jnp.exp2(s_c - m_c), 0.0)
        l_c = jnp.sum(e_c, axis=0, keepdims=True)
        p_c = e_c / jnp.where(l_c > 0.0, l_c, 1.0)
        oc_ref[a] = jnp.dot(vct_ref[0, dims_of(a), :], p_c.astype(BF16), preferred_element_type=F32)

        p_sum = p_c[:, 0:TILE]
        for g in range(1, NSA_GROUP):
            p_sum = p_sum + p_c[:, g * TILE:(g + 1) * TILE]
        p_hi = p_sum.astype(BF16)
        p_lo = (p_sum - p_hi.astype(F32)).astype(BF16)
        imp = (jnp.dot(c2s_ref[...], p_hi, preferred_element_type=F32)
               + jnp.dot(c2s_ref[...], p_lo, preferred_element_type=F32))
        j = lax.broadcasted_iota(I32, imp.shape, 0)
        qb = (i * TILE + lax.broadcasted_iota(I32, imp.shape, 1)) >> int(math.log2(SLC_BLOCK))
        forced = (j == 0) | ((j <= qb) & (j > qb - SLC_LOCAL))
        imp = jnp.where(forced, jnp.inf, jnp.where(j > qb, -jnp.inf, imp))
        rank = _rank_before(imp, n_slc)
        selneg = jnp.where((rank < SLC_TOPN) & (j <= qb), 0.0, NEG_INF)
        selneg_ref[a] = selneg
        for c in range(per_tile):
            radd_ref[a, c:c + 1, :] = _group_lanes(selneg_ref[a, pl.ds(per_tile * i + c, 1), :])
            near_row = _group_lanes(selneg_ref[a, pl.ds(per_tile * t_near + c, 1), :])
            radd_ref[a, per_tile + c:per_tile + c + 1, :] = jnp.where(i >= 1, near_row, NEG_INF)
        for blk in range(n_slc):
            row = far_bias + _group_lanes(selneg[blk:blk + 1, :])
            r = 2 * per_tile + blk
            radd_ref[a, r:r + 1, :] = jnp.where(blk // per_tile < n_far, row, NEG_INF)
        _init_state(m_ref.at[a], l_ref.at[a], acc_ref.at[a])

    def update(jg, a, buf):
        first = jg == 0
        t0 = jnp.where(first, i, 2 * jg - 2)
        t1 = jnp.where(first, t_near, jnp.minimum(2 * jg - 1, last))
        scores, adds = [], []
        for t in range(2):
            for c in range(per_tile):
                scores.append(buf[a, t, c * SLC_BLOCK:(c + 1) * SLC_BLOCK, :])
                adds.append(radd_ref[a, pl.ds(2 * per_tile * jg + per_tile * t + c, 1), :])
        _softmax_pv(scores, adds, [vst_ref[0, t0], vst_ref[0, t1]], [(dims_of(a), slice(None))],
                    m_ref.at[a], l_ref.at[a], acc_ref.at[a])

    def step(jg, src, dst):
        for a in kv_heads:
            dst[a, 0] = scores_of(ks_ref, 2 * jg, a)
            dst[a, 1] = scores_of(ks_ref, jnp.minimum(2 * jg + 1, last), a)
            update(jg, a, src)

    def two_steps(jj, carry):
        step(2 * jj, sa_ref, sb_ref)
        step(2 * jj + 1, sb_ref, sa_ref)
        return carry

    lax.fori_loop(0, n_far_groups >> 1, two_steps, 0)

    @pl.when((n_far_groups & 1) == 1)
    def _():
        step(n_far_groups - 1, sa_ref, sb_ref)
        for a in kv_heads:
            update(n_far_groups, a, sb_ref)

    @pl.when((n_far_groups & 1) == 0)
    def _():
        for a in kv_heads:
            update(n_far_groups, a, sa_ref)

    gates = jax.nn.sigmoid(gt_ref[0, 0])
    zero_row = jnp.zeros((1, cols), F32)
    pieces = []
    for a in kv_heads:
        os_ref[a] = acc_ref[a] / l_ref[a]
        _init_state(m_ref.at[a], l_ref.at[a], acc_ref.at[a])
        _softmax_pv([sw_ref[a, 0], sw_ref[a, 1], sw_ref[a, 2]],
                    [zero_row, zero_row + jnp.where(i >= 1, 0.0, NEG_INF), zero_row + jnp.where(i >= 2, 0.0, NEG_INF)],
                    [vwt_ref[0, i], vwt_ref[0, t_near], vwt_ref[0, t_edge]], [(dims_of(a), slice(None))],
                    m_ref.at[a], l_ref.at[a], acc_ref.at[a])
        o_w = acc_ref[a] / l_ref[a]
        for g in range(NSA_GROUP):
            c0 = (a // 2) * LANES + 3 * (NSA_GROUP * (a % 2) + g)
            ls = slice(g * TILE, (g + 1) * TILE)
            pieces.append(gates[c0:c0 + 1, :] * oc_ref[a, :, ls] + gates[c0 + 1:c0 + 2, :] * os_ref[a, :, ls]
                          + gates[c0 + 2:c0 + 3, :] * o_w[:, ls])
    o_ref[0] = jnp.concatenate(pieces, axis=0).T.astype(o_ref.dtype)


def _cmp_to_slc(S):
    n_cmp_pad = S // CMP_STRIDE
    n_slc = S // SLC_BLOCK
    ci = np.arange(n_cmp_pad)[:, None] * CMP_STRIDE
    sj = np.arange(n_slc)[None, :] * SLC_BLOCK
    c2s = ((ci < sj + SLC_BLOCK) & (ci + CMP_LEN > sj)).astype(np.float32)
    c2s[(S - CMP_LEN) // CMP_STRIDE + 1:] = 0.0
    return jnp.asarray(c2s.T, BF16)


def _nsa_attention(proj, vt, gate_t, kcmp, vcmp_t, bias):
    B, S, _ = proj.shape
    assert S % TILE == 0 and NSA_KV_HEADS == 4 and 2 * HEAD_DIM == LANES
    nq = S // TILE
    n_cmp = kcmp.shape[1]
    n_slc = S // SLC_BLOCK
    chains = 2 * NSA_PAIRS
    pw = NSA_PAIRS * LANES
    qw = chains * NSA_GROUP * HEAD_DIM
    n_steps = NSA_KV_HEADS // chains
    q_blocks = N_HEADS * HEAD_DIM // pw
    kv_blocks = NSA_KV_HEADS * HEAD_DIM // pw
    once = pl.Buffered(1)

    def k_spec(which):
        base = q_blocks + which * kv_blocks
        return pl.BlockSpec((1, S, pw), lambda b, p, i: (b, 0, base + p))

    def vt_spec(which):
        base = which * kv_blocks
        return pl.BlockSpec((1, nq, pw, TILE), lambda b, p, i: (b, 0, base + p, 0))

    state = pltpu.VMEM((chains, HEAD_DIM, NSA_GROUP * TILE), F32)
    stat = pltpu.VMEM((chains, 1, NSA_GROUP * TILE), F32)
    return pl.pallas_call(
        _nsa_body,
        grid=(B, n_steps, nq),
        in_specs=[pl.BlockSpec((1, TILE, qw), lambda b, p, i: (b, i, p)),
                  pl.BlockSpec((1, n_cmp, pw), lambda b, p, i: (b, 0, p)),
                  pl.BlockSpec((1, pw, n_cmp), lambda b, p, i: (b, p, 0)),
                  k_spec(2), vt_spec(0), k_spec(3), vt_spec(1),
                  pl.BlockSpec((1, 1, pw, TILE), lambda b, p, i: (b, i, p, 0)),
                  pl.BlockSpec((2, chains, TILE, NSA_GROUP * TILE), lambda b, p, i: (0, p, 0, 0),
                               pipeline_mode=once if n_steps == 1 else None),
                  pl.BlockSpec((n_slc, n_cmp), lambda b, p, i: (0, 0))],
        out_specs=pl.BlockSpec((1, TILE, qw), lambda b, p, i: (b, i, p)),
        out_shape=jax.ShapeDtypeStruct((B, S, N_HEADS * HEAD_DIM), BF16),
        scratch_shapes=[pltpu.VMEM((chains, n_slc, TILE), F32),
                        pltpu.VMEM((chains, 2 * (TILE // SLC_BLOCK) + n_slc, NSA_GROUP * TILE), F32),
                        pltpu.VMEM((chains, 2, TILE, NSA_GROUP * TILE), F32),
                        pltpu.VMEM((chains, 2, TILE, NSA_GROUP * TILE), F32),
                        pltpu.VMEM((chains, 3, TILE, NSA_GROUP * TILE), F32),
                        state, state, stat, stat, state],
        compiler_params=_cparams("parallel", "parallel", "arbitrary"),
        name="nsa_attention",
    )(proj, kcmp, vcmp_t, proj, vt, proj, vt, gate_t, _heads_on_lanes(bias, NSA_GROUP), _cmp_to_slc(S))


def _split_bf16(x):
    hi = x.astype(BF16)
    return hi, (x - hi.astype(F32)).astype(BF16)


_ROW_OF_EXPERT = np.arange(N_EXPERTS).reshape(N_GROUPS, EXPERTS_PER_GROUP).T.reshape(-1)


def _route(x, w_ref, b_ref, tri_ref, idx_ref, wt_ref, pos_ref, cnt_ref, base_ref):
    @pl.when(pl.program_id(0) == 0)
    def _():
        base_ref[...] = jnp.zeros(base_ref.shape, F32)

    x_hi, x_lo = _split_bf16(x)
    w_hi, w_lo = _split_bf16(w_ref[...])
    logits = (lax.dot_general(w_hi, x_hi, _NT, preferred_element_type=F32)
              + lax.dot_general(w_hi, x_lo, _NT, preferred_element_type=F32)
              + lax.dot_general(w_lo, x_hi, _NT, preferred_element_type=F32)) + b_ref[:, 0:1]
    m = jnp.max(logits, axis=0, keepdims=True)
    e = jnp.exp(logits - m)
    probs = e / jnp.sum(e, axis=0, keepdims=True)
    pk = [probs[k * N_GROUPS:(k + 1) * N_GROUPS] for k in range(EXPERTS_PER_GROUP)]
    hi1, lo1 = jnp.maximum(pk[0], pk[1]), jnp.minimum(pk[0], pk[1])
    hi2, lo2 = jnp.maximum(pk[2], pk[3]), jnp.minimum(pk[2], pk[3])
    score = jnp.maximum(hi1, hi2) + jnp.maximum(jnp.minimum(hi1, hi2), jnp.maximum(lo1, lo2))
    grp = lax.broadcasted_iota(I32, score.shape, 0)
    best = jnp.min(jnp.where(score == jnp.max(score, axis=0, keepdims=True), grp, N_GROUPS),
                   axis=0, keepdims=True)
    v = [jnp.sum(jnp.where(grp == best, p, 0.0), axis=0, keepdims=True) for p in pk]
    v1 = jnp.maximum(jnp.maximum(v[0], v[1]), jnp.maximum(v[2], v[3]))
    i1 = jnp.where(v[0] == v1, 0, jnp.where(v[1] == v1, 1, jnp.where(v[2] == v1, 2, 3)))
    rest = [jnp.where(i1 == k, -1.0, v[k]) for k in range(EXPERTS_PER_GROUP)]
    v2 = jnp.maximum(jnp.maximum(rest[0], rest[1]), jnp.maximum(rest[2], rest[3]))
    i2 = jnp.where(rest[0] == v2, 0, jnp.where(rest[1] == v2, 1, jnp.where(rest[2] == v2, 2, 3)))
    tot = v1 + v2
    idx_ref[...] = jnp.concatenate([best * EXPERTS_PER_GROUP + i1, best * EXPERTS_PER_GROUP + i2], axis=0)
    wt_ref[...] = jnp.concatenate([v1 / tot, v2 / tot], axis=0)

    row = lax.broadcasted_iota(I32, logits.shape, 0)
    hot = [jnp.where(row == ik * N_GROUPS + best, 1.0, 0.0) for ik in (i1, i2)]
    both = (hot[0] + hot[1]).astype(BF16)
    chunks = [slice(c * LANES, (c + 1) * LANES) for c in range(logits.shape[1] // LANES)]
    prefix = [jnp.dot(both[:, ls], tri_ref[...], preferred_element_type=F32) for ls in chunks]
    run = base_ref[:, 0:1]
    pos = [[], []]
    for ls, pre in zip(chunks, prefix):
        before = run + pre - 1.0
        for k in range(2):
            pos[k].append(jnp.sum(hot[k][:, ls] * before, axis=0, keepdims=True))
        run = run + pre[:, LANES - 1:LANES]
    pos_ref[...] = jnp.concatenate([jnp.concatenate(pos[0], axis=1), jnp.concatenate(pos[1], axis=1)],
                                   axis=0).astype(I32)
    base_ref[...] = jnp.broadcast_to(run, base_ref.shape)
    cnt_ref[...] = jnp.broadcast_to(run, cnt_ref.shape)


def _router_operands(router_w, router_b):
    w = router_w.T[_ROW_OF_EXPERT]
    b = jnp.broadcast_to(router_b[_ROW_OF_EXPERT][:, None], (N_EXPERTS, LANES))
    tri = jnp.asarray(np.triu(np.ones((LANES, LANES), np.float32)), BF16)
    return w, b, tri


def _proj_ln_route_body(a_ref, w_ref, x_ref, g_ref, b_ref, rw_ref, rb_ref, tri_ref,
                        o_ref, ob_ref, idx_ref, wt_ref, pos_ref, cnt_ref, base_ref):
    y = jnp.dot(a_ref[...], w_ref[...], preferred_element_type=F32)
    out = _layer_norm_rows(DEEPNORM_ALPHA * x_ref[...] + y, g_ref[...], b_ref[...])
    o_ref[...] = out
    ob_ref[...] = out.astype(BF16)
    _route(out, rw_ref, rb_ref, tri_ref, idx_ref, wt_ref, pos_ref, cnt_ref, base_ref)


def _proj_ln_route(a, w, x, g, b, router_w, router_b):
    M, K = a.shape
    D = w.shape[1]
    assert M % LN_TM == 0 and LN_TM % LANES == 0
    row = pl.BlockSpec((LN_TM, D), lambda i: (i, 0))
    vec = pl.BlockSpec((1, D), lambda i: (0, 0))
    whole = lambda shape: pl.BlockSpec(shape, lambda i: (0, 0))
    tok = lambda dt: jax.ShapeDtypeStruct((2, M), dt)
    tok_spec = pl.BlockSpec((2, LN_TM), lambda i: (0, i))
    h, hb, idx, wts, pos, cnt = pl.pallas_call(
        _proj_ln_route_body,
        grid=(M // LN_TM,),
        in_specs=[pl.BlockSpec((LN_TM, K), lambda i: (i, 0)), whole((K, D)), row, vec, vec,
                  whole((N_EXPERTS, D)), whole((N_EXPERTS, LANES)), whole((LANES, LANES))],
        out_specs=[row, row, tok_spec, tok_spec, tok_spec, whole((N_EXPERTS, LANES))],
        out_shape=[jax.ShapeDtypeStruct((M, D), F32), jax.ShapeDtypeStruct((M, D), BF16),
                   tok(I32), tok(F32), tok(I32), jax.ShapeDtypeStruct((N_EXPERTS, LANES), F32)],
        scratch_shapes=[pltpu.VMEM((N_EXPERTS, LANES), F32)],
        compiler_params=_cparams("arbitrary"),
        name="out_proj_ln_route",
    )(a, w, x, g.reshape(1, D), b.reshape(1, D), *_router_operands(router_w, router_b))
    counts = cnt[np.argsort(_ROW_OF_EXPERT), 0].astype(I32)
    return h, hb, (idx, wts, pos, counts)


def _expert_body(blk_e_ref, n_used_ref, x_ref, wg_ref, wu_ref, wd_ref, o_ref, wg_b, wu_b, wd_b):
    i = pl.program_id(0)

    @pl.when((i == 0) | (blk_e_ref[i] != blk_e_ref[jnp.maximum(i - 1, 0)]))
    def _():
        wg_b[...] = wg_ref[0, 0].astype(BF16)
        wu_b[...] = wu_ref[0, 0].astype(BF16)
        wd_b[...] = wd_ref[0, 0].astype(BF16)

    @pl.when(i < n_used_ref[0])
    def _():
        x = x_ref[...]
        gate = jnp.dot(x, wg_b[...], preferred_element_type=F32)
        up = jnp.dot(x, wu_b[...], preferred_element_type=F32)
        hid = (gate * jax.nn.sigmoid(gate) * up).astype(BF16)
        o_ref[...] = jnp.dot(hid, wd_b[...], preferred_element_type=F32).astype(o_ref.dtype)

    @pl.when(i >= n_used_ref[0])
    def _():
        o_ref[...] = jnp.zeros(o_ref.shape, o_ref.dtype)


def _experts(xs, blk_e, n_used, wg, wu, wd, layer):
    R, D = xs.shape
    n_blk = R // MOE_TB

    def live(i, be, nu):
        return jnp.minimum(i, nu[0] - 1)

    grid_spec = pltpu.PrefetchScalarGridSpec(
        num_scalar_prefetch=2,
        grid=(n_blk,),
        in_specs=[pl.BlockSpec((MOE_TB, D), lambda i, be, nu: (live(i, be, nu), 0)),
                  pl.BlockSpec((1, 1, D, D_EXPERT), lambda i, be, nu: (layer, be[i], 0, 0)),
                  pl.BlockSpec((1, 1, D, D_EXPERT), lambda i, be, nu: (layer, be[i], 0, 0)),
                  pl.BlockSpec((1, 1, D_EXPERT, D), lambda i, be, nu: (layer, be[i], 0, 0))],
        out_specs=pl.BlockSpec((MOE_TB, D), lambda i, be, nu: (i, 0)),
        scratch_shapes=[pltpu.VMEM((D, D_EXPERT), BF16), pltpu.VMEM((D, D_EXPERT), BF16),
                        pltpu.VMEM((D_EXPERT, D), BF16)],
    )
    return pl.pallas_call(
        _expert_body,
        grid_spec=grid_spec,
        out_shape=jax.ShapeDtypeStruct((R, D), BF16),
        compiler_params=_cparams("arbitrary"),
        name="moe_experts",
    )(blk_e, n_used, xs, wg, wu, wd)


def _combine_ln_body(x_ref, y0_ref, y1_ref, wt_ref, g_ref, b_ref, o_ref, ob_ref):
    ffn = y0_ref[...] * wt_ref[:, 0:1] + y1_ref[...] * wt_ref[:, HEAD_DIM:HEAD_DIM + 1]
    out = _layer_norm_rows(DEEPNORM_ALPHA * x_ref[...] + ffn, g_ref[...], b_ref[...])
    o_ref[...] = out
    ob_ref[...] = out.astype(BF16)


def _combine_ln(x, y0, y1, wt, g, b):
    M, D = x.shape
    assert M % LN_TM == 0
    row = pl.BlockSpec((LN_TM, D), lambda i: (i, 0))
    vec = pl.BlockSpec((1, D), lambda i: (0, 0))
    return pl.pallas_call(
        _combine_ln_body,
        grid=(M // LN_TM,),
        in_specs=[row, row, row, pl.BlockSpec((LN_TM, LANES), lambda i: (i, 0)), vec, vec],
        out_specs=[row, row],
        out_shape=[jax.ShapeDtypeStruct((M, D), F32), jax.ShapeDtypeStruct((M, D), BF16)],
        compiler_params=_cparams("parallel"),
        name="moe_combine_ln",
    )(x, y0, y1, wt, g.reshape(1, D), b.reshape(1, D))


def _moe_ln(h, hb, routing, wg, wu, wd, layer, g, b):
    N, D = h.shape
    A = 2 * N
    idx, wts, pos, counts = routing
    starts = jnp.cumsum(counts) - counts
    padded = (counts + MOE_TB - 1) // MOE_TB * MOE_TB
    pends = jnp.cumsum(padded)
    pstarts = pends - padded
    R = A + N_EXPERTS * MOE_TB
    n_blk = R // MOE_TB
    experts = jnp.arange(N_EXPERTS, dtype=I32)
    dest = pos + jnp.sum(jnp.where(idx[None] == experts[:, None, None], pstarts[:, None, None], 0), axis=0)
    tok = jnp.broadcast_to(jnp.arange(N, dtype=I32)[None, :], (2, N))
    _, tok_sorted = lax.sort_key_val(dest.reshape(A), tok.reshape(A))
    blk_row0 = jnp.arange(n_blk, dtype=I32) * MOE_TB
    blk_e = jnp.minimum(jnp.sum((pends[None, :] <= blk_row0[:, None]).astype(I32), axis=1), N_EXPERTS - 1)
    hot = blk_e[:, None] == experts[None, :]
    compact0 = blk_row0 + jnp.sum(jnp.where(hot, (starts - pstarts)[None, :], 0), axis=1)
    compact = jnp.remainder(compact0[:, None] + jnp.arange(MOE_TB, dtype=I32)[None, :], A).reshape(R)
    n_used = (pends[-1:] // MOE_TB).astype(I32)
    xs = hb[tok_sorted[compact]]
    yb = _experts(xs, blk_e, n_used, wg, wu, wd, layer)
    wt = jnp.concatenate([jnp.broadcast_to(wts[k][:, None], (N, HEAD_DIM)) for k in range(2)], axis=1)
    return _combine_ln(h, yb[dest[0]], yb[dest[1]], wt, g, b)


def _moba_layer(h, w_in, w_out, bias, g, b, router_w, router_b, B, S):
    HD = N_HEADS * HEAD_DIM
    qk, vt = _in_proj(h, w_in[:, :2 * HD].astype(BF16), _query_scale(HD, 2 * HD),
                      [w_in[:, 2 * HD:].T.astype(BF16)], [BF16], B, S)
    att = _moba_attention(qk.reshape(B, S, 2 * HD), vt, bias)
    return _proj_ln_route(att.reshape(B * S, HD), w_out.astype(BF16), h, g, b, router_w, router_b)


def _nsa_layer(h, hb, w_in, w_out, pos_k, pos_v, ck_w1, ck_w2, cv_w1, cv_w2, bias, g, b, router_w, router_b, B, S):
    HD = N_HEADS * HEAD_DIM
    kvw = NSA_KV_HEADS * HEAD_DIM
    col = lambda k: slice(HD + k * kvw, HD + (k + 1) * kvw)
    w_rows = jnp.concatenate([w_in[:, :HD + 2 * kvw], w_in[:, col(2)], w_in[:, col(4)]], axis=1)
    w_vt = jnp.concatenate([w_in[:, col(3)], w_in[:, col(5)]], axis=1).T
    per_pair = 3 * N_HEADS // 2
    wg = w_in[:, HD + 6 * kvw:].reshape(D_MODEL, 2, per_pair)
    wg = jnp.pad(wg, ((0, 0), (0, 0), (0, LANES - per_pair))).reshape(D_MODEL, 2 * LANES).T
    proj, vt, gate_t = _in_proj(hb, w_rows.astype(BF16), _query_scale(HD, HD + 4 * kvw),
                                [w_vt.astype(BF16), wg.astype(BF16)], [BF16, F32], B, S)
    proj = proj.reshape(B, S, HD + 4 * kvw)

    def grouped(t):
        t = t.reshape(B, S, NSA_KV_HEADS, HEAD_DIM).transpose(0, 2, 1, 3)
        return t.reshape(B, NSA_KV_HEADS, S // CMP_STRIDE, CMP_STRIDE * HEAD_DIM)

    kcmp = _compress(grouped(proj[..., HD:HD + kvw]), pos_k, ck_w1, ck_w2)
    vcmp = _compress(grouped(proj[..., HD + kvw:HD + 2 * kvw]), pos_v, cv_w1, cv_w2)
    n_cmp = kcmp.shape[2]
    kcmp = kcmp.transpose(0, 2, 1, 3).reshape(B, n_cmp, kvw)
    vcmp_t = vcmp.transpose(0, 1, 3, 2).reshape(B, kvw, n_cmp)
    att = _nsa_attention(proj, vt, gate_t, kcmp, vcmp_t, bias)
    return _proj_ln_route(att.reshape(B * S, HD), w_out.astype(BF16), h, g, b, router_w, router_b)


def kernel(x, rel_bias, router_w, router_b, ln_g, ln_b, moba_w_in, moba_w_out, nsa_w_in, nsa_w_out,
           nsa_pos_k, nsa_pos_v, nsa_ck_w1, nsa_ck_w2, nsa_cv_w1, nsa_cv_w2,
           moe_w_gate, moe_w_up, moe_w_down):
    B, S, D = x.shape
    bias = _bias_tiles(rel_bias)
    h = x.reshape(B * S, D)
    h, hb, routing = _moba_layer(h, moba_w_in[0], moba_w_out[0], bias, ln_g[0, 0], ln_b[0, 0],
                                 router_w, router_b, B, S)
    h, hb = _moe_ln(h, hb, routing, moe_w_gate, moe_w_up, moe_w_down, 0, ln_g[0, 1], ln_b[0, 1])
    h, hb, routing = _nsa_layer(h, hb, nsa_w_in[0], nsa_w_out[0], nsa_pos_k[0], nsa_pos_v[0],
                                nsa_ck_w1[0], nsa_ck_w2[0], nsa_cv_w1[0], nsa_cv_w2[0],
                                bias, ln_g[1, 0], ln_b[1, 0], router_w, router_b, B, S)
    h, hb = _moe_ln(h, hb, routing, moe_w_gate, moe_w_up, moe_w_down, 1, ln_g[1, 1], ln_b[1, 1])
    return h.reshape(B, S, D)
```

```python
import math
from functools import partial

import numpy as np
import jax
import jax.numpy as jnp
from jax import lax
from jax.experimental import pallas as pl
from jax.experimental.pallas import tpu as pltpu

F32, BF16, I32 = jnp.float32, jnp.bfloat16, jnp.int32

D_MODEL = 1024
N_HEADS = 16
HEAD_DIM = 64
DEPTH = 2
NEG_INF = -1e30
LN_EPS = 1e-5
MOBA_BLOCK = 256
MOBA_TOPK = 3
NSA_KV_HEADS = 4
NSA_GROUP = N_HEADS // NSA_KV_HEADS
CMP_LEN = 32
CMP_STRIDE = 16
CMP_HIDDEN = 256
SLC_BLOCK = 64
SLC_TOPN = 16
SLC_LOCAL = 2
WINDOW = 512
REL_BUCKETS = 32
REL_MAX_DIST = 128
N_EXPERTS = 32
N_GROUPS = 8
EXPERTS_PER_GROUP = N_EXPERTS // N_GROUPS
D_EXPERT = 512
DEEPNORM_ALPHA = (2 * DEPTH) ** 0.25
LOG2E = math.log2(math.e)
Q_SCALE = HEAD_DIM ** -0.5 * LOG2E

LANES = 128
SUBLANES = 8
ONES_ROWS = 16
TILE = MOBA_BLOCK
MM_TM = 512
MM_TN = 1024
LN_TM = 1024
MOE_TB = 512
assert WINDOW == 2 * TILE and TILE % SLC_BLOCK == 0 and SLC_BLOCK % SUBLANES == 0 and MM_TM % TILE == 0
VMEM_LIMIT = 56 * 1024 * 1024

_NT = (((1,), (1,)), ((), ()))


def _cparams(*sem):
    return pltpu.CompilerParams(dimension_semantics=sem, vmem_limit_bytes=VMEM_LIMIT)


def _in_proj_body(n_t, a_ref, w_ref, c_ref, *refs):
    wt_refs, o_ref, ot_refs = refs[:n_t], refs[n_t], refs[n_t + 1:]
    a = a_ref[...].astype(BF16)
    for c in range(o_ref.shape[1] // MM_TN):
        cols = slice(c * MM_TN, (c + 1) * MM_TN)
        acc = jnp.dot(a, w_ref[:, cols], preferred_element_type=F32)
        o_ref[:, cols] = (acc * c_ref[:, cols]).astype(o_ref.dtype)
    for wt_ref, ot_ref in zip(wt_refs, ot_refs):
        r = lax.dot_general(wt_ref[...], a, _NT, preferred_element_type=F32)
        for t in range(ot_ref.shape[1]):
            ot_ref[0, t] = r[:, t * TILE:(t + 1) * TILE].astype(ot_ref.dtype)


def _in_proj(a, w, col_scale, w_ts, t_dtypes, B, S):
    M, K = a.shape
    N = w.shape[1]
    assert M == B * S and S % MM_TM == 0 and N % MM_TN == 0
    per_seq = S // MM_TM
    sub = MM_TM // TILE
    whole = lambda shape: pl.BlockSpec(shape, lambda i: (0, 0))
    outs = pl.pallas_call(
        partial(_in_proj_body, len(w_ts)),
        grid=(M // MM_TM,),
        in_specs=[pl.BlockSpec((MM_TM, K), lambda i: (i, 0)), whole((K, N)), whole((1, N))]
                 + [whole(w_t.shape) for w_t in w_ts],
        out_specs=[pl.BlockSpec((MM_TM, N), lambda i: (i, 0))]
                  + [pl.BlockSpec((1, sub, w_t.shape[0], TILE), lambda i: (i // per_seq, i % per_seq, 0, 0))
                     for w_t in w_ts],
        out_shape=[jax.ShapeDtypeStruct((M, N), BF16)]
                  + [jax.ShapeDtypeStruct((B, S // TILE, w_t.shape[0], TILE), dt) for w_t, dt in zip(w_ts, t_dtypes)],
        compiler_params=_cparams("parallel"),
        name="in_proj",
    )(a, w, col_scale.reshape(1, N), *w_ts)
    return outs


def _query_scale(n_query_cols, n_cols):
    return jnp.where(jnp.arange(n_cols) < n_query_cols, Q_SCALE, 1.0).astype(F32)


def _layer_norm_rows(z, g, b):
    mu = jnp.mean(z, axis=-1, keepdims=True)
    zc = z - mu
    var = jnp.mean(zc * zc, axis=-1, keepdims=True)
    return zc * lax.rsqrt(var + LN_EPS) * g + b


def _t5_bucket_np(rel):
    n = np.maximum(rel, 0)
    max_exact = REL_BUCKETS // 2
    nf = np.maximum(n, 1).astype(np.float32)
    large = max_exact + (np.log(nf / np.float32(max_exact))
                         / np.float32(math.log(REL_MAX_DIST / max_exact))
                         * np.float32(REL_BUCKETS - max_exact)).astype(np.int32)
    large = np.minimum(large, REL_BUCKETS - 1)
    return np.where(n < max_exact, n, large).astype(np.int32)


def _bias_body(tbl_ref, bk_ref, o_ref):
    h = pl.program_id(0)
    for dl in range(2):
        bk = bk_ref[dl]
        acc = jnp.zeros((TILE, TILE), F32)
        for b in range(REL_BUCKETS):
            acc = jnp.where(bk == b, tbl_ref[h * REL_BUCKETS + b], acc)
        o_ref[dl, 0] = acc * LOG2E


def _bias_tiles(rel_bias):
    key = np.arange(TILE)[:, None]
    qry = np.arange(TILE)[None, :]
    assert int(_t5_bucket_np(np.array(TILE + 1))) == REL_BUCKETS - 1
    bk = np.stack([_t5_bucket_np(qry - key), _t5_bucket_np(TILE + qry - key)])
    return pl.pallas_call(
        _bias_body,
        grid=(N_HEADS,),
        in_specs=[pl.BlockSpec(memory_space=pltpu.SMEM),
                  pl.BlockSpec((2, TILE, TILE), lambda h: (0, 0, 0))],
        out_specs=pl.BlockSpec((2, 1, TILE, TILE), lambda h: (0, h, 0, 0)),
        out_shape=jax.ShapeDtypeStruct((2, N_HEADS, TILE, TILE), F32),
        name="t5_bias_tiles",
    )(rel_bias.T.reshape(-1), jnp.asarray(bk))


def _heads_on_lanes(bias, per_block):
    two, H, T, _ = bias.shape
    b = bias.reshape(two, H // per_block, per_block, T, T).transpose(0, 1, 3, 2, 4)
    return b.reshape(two, H // per_block, T, per_block * T)


def _init_state(m_ref, l_ref, acc_ref):
    m_ref[...] = jnp.full(m_ref.shape, NEG_INF, F32)
    l_ref[...] = jnp.zeros(l_ref.shape, F32)
    acc_ref[...] = jnp.zeros(acc_ref.shape, F32)


def _rank_before(vals, rows):
    idx = lax.broadcasted_iota(I32, vals.shape, 0)
    rank = jnp.zeros(vals.shape, I32)
    for m in range(rows):
        row = vals[m:m + 1, :]
        beats = (row > vals) | ((row == vals) & (idx > m))
        rank = rank + jnp.where(beats, 1, 0)
    return rank


MOBA_STREAMS = 8


def _softmax_pv(scores, adds, vts, heads, m_ref, l_ref, acc_ref):
    def fold(x, op):
        return op(x.reshape(x.shape[0] // SUBLANES, SUBLANES, x.shape[1]), axis=0)

    m_prev = m_ref[...]
    m_part = None
    for s, add in zip(scores, adds):
        part = fold(s, jnp.max) + add
        m_part = part if m_part is None else jnp.maximum(m_part, part)
    m_new = jnp.maximum(m_prev, jnp.max(m_part, axis=0, keepdims=True))
    a = jnp.exp2(m_prev - m_new)
    probs = [jnp.exp2(s - (m_new - add)) for s, add in zip(scores, adds)]
    vt = jnp.concatenate(vts, axis=1)
    pb = jnp.concatenate([p.astype(BF16) for p in probs], axis=0)
    ones = jnp.ones((ONES_ROWS, vt.shape[1]), BF16)
    pv = jnp.concatenate([jnp.dot(jnp.concatenate([vt[rows], ones], axis=0), pb[:, cols],
                                  preferred_element_type=F32) for rows, cols in heads], axis=1)
    l_ref[...] = a * l_ref[...] + pv[HEAD_DIM:HEAD_DIM + 1, :]
    acc_ref[...] = a * acc_ref[...] + pv[:HEAD_DIM, :]
    m_ref[...] = m_new


def _moba_body(q_ref, k_ref, vt_ref, bias_ref, o_ref, kmean_ref, radd_ref, sa_ref, sb_ref, m_ref, l_ref, acc_ref):
    i = pl.program_id(2)
    nb = k_ref.shape[1] // TILE
    streams = range(MOBA_STREAMS)
    lanes_of = lambda s: slice(s * LANES, (s + 1) * LANES)

    @pl.when(i == 0)
    def _():
        for s in streams:
            for n in range(nb):
                kb = k_ref[0, n * TILE:(n + 1) * TILE, lanes_of(s)].astype(F32)
                kmean_ref[s, n:n + 1, :] = jnp.sum(kb, axis=0, keepdims=True) * (1.0 / TILE)

    n_far = jnp.maximum(i - 1, 0)
    n_far_groups = (n_far + 1) >> 1
    last = nb - 1

    def key_tile(t, s):
        return k_ref[0, pl.ds(pl.multiple_of(t * TILE, TILE), TILE), lanes_of(s)]

    def far_tiles(j):
        return 2 * j, jnp.minimum(2 * j + 1, last)

    lane = lax.broadcasted_iota(I32, (TILE, LANES), 1)
    key = lax.broadcasted_iota(I32, (TILE, 2 * TILE), 0)
    qry = lax.broadcasted_iota(I32, (TILE, 2 * TILE), 1) & (TILE - 1)
    causal_neg = jnp.where(key <= qry, 0.0, NEG_INF)
    t_near = jnp.maximum(i - 1, 0)
    q2s = []
    for s in streams:
        q = q_ref[0, :, lanes_of(s)]
        zero = jnp.zeros_like(q)
        q2 = jnp.concatenate([jnp.where(lane < HEAD_DIM, q, zero),
                              jnp.where(lane >= HEAD_DIM, q, zero)], axis=0)
        q2s.append(q2)
        sa_ref[s, 0] = (lax.dot_general(key_tile(i, s), q2, _NT, preferred_element_type=F32)
                        + (bias_ref[0, s] + causal_neg))
        sa_ref[s, 1] = lax.dot_general(key_tile(t_near, s), q2, _NT, preferred_element_type=F32) + bias_ref[1, s]
        km = kmean_ref[s]
        k_hi = km.astype(BF16)
        k_lo = (km - k_hi.astype(F32)).astype(BF16)
        gate = (lax.dot_general(k_hi, q2, _NT, preferred_element_type=F32)
                + lax.dot_general(k_lo, q2, _NT, preferred_element_type=F32))
        blk = lax.broadcasted_iota(I32, gate.shape, 0)
        gate = jnp.where(blk < i, gate, -jnp.inf)
        rank = _rank_before(gate, nb)
        neg = jnp.where((rank < MOBA_TOPK) & (blk < i), 0.0, NEG_INF)
        far_bias = bias_ref[1, s, 0:1, :]
        near_row = jnp.full((1, 2 * TILE), NEG_INF, F32)
        radd_ref[s, 0:1, :] = jnp.zeros((1, 2 * TILE), F32)
        for n in range(nb):
            row = neg[n:n + 1, :]
            near_row = jnp.where(n == i - 1, row, near_row)
            radd_ref[s, 2 + n:3 + n, :] = jnp.where(n < n_far, far_bias + row, NEG_INF)
        radd_ref[s, 1:2, :] = near_row
        _init_state(m_ref.at[s], l_ref.at[s], acc_ref.at[s])

    pair_heads = [(slice(h * HEAD_DIM, (h + 1) * HEAD_DIM), slice(h * TILE, (h + 1) * TILE)) for h in range(2)]

    def update(j, s, buf):
        first = j == 0
        t0 = jnp.where(first, i, 2 * j - 2)
        t1 = jnp.where(first, t_near, jnp.minimum(2 * j - 1, last))
        _softmax_pv([buf[s, 0], buf[s, 1]],
                    [radd_ref[s, pl.ds(2 * j, 1), :], radd_ref[s, pl.ds(2 * j + 1, 1), :]],
                    [vt_ref[0, t0, lanes_of(s), :], vt_ref[0, t1, lanes_of(s), :]], pair_heads,
                    m_ref.at[s], l_ref.at[s], acc_ref.at[s])

    def step(j, src, dst):
        for s in streams:
            ta, tb = far_tiles(j)
            dst[s, 0] = lax.dot_general(key_tile(ta, s), q2s[s], _NT, preferred_element_type=F32)
            dst[s, 1] = lax.dot_general(key_tile(tb, s), q2s[s], _NT, preferred_element_type=F32)
            update(j, s, src)

    def two_steps(jj, carry):
        step(2 * jj, sa_ref, sb_ref)
        step(2 * jj + 1, sb_ref, sa_ref)
        return carry

    lax.fori_loop(0, n_far_groups >> 1, two_steps, 0)

    @pl.when((n_far_groups & 1) == 1)
    def _():
        step(n_far_groups - 1, sa_ref, sb_ref)
        for s in streams:
            update(n_far_groups, s, sb_ref)

    @pl.when((n_far_groups & 1) == 0)
    def _():
        for s in streams:
            update(n_far_groups, s, sa_ref)

    for s in streams:
        o = acc_ref[s] / l_ref[s]
        o = jnp.concatenate([o[:, :TILE], o[:, TILE:]], axis=0)
        o_ref[0, :, lanes_of(s)] = o.T.astype(o_ref.dtype)


def _moba_attention(qk, vt, bias):
    B, S, _ = qk.shape
    assert S % TILE == 0 and N_HEADS % (2 * MOBA_STREAMS) == 0 and 2 * HEAD_DIM == LANES
    n_steps = N_HEADS // 2 // MOBA_STREAMS
    nq = S // TILE
    w = MOBA_STREAMS * LANES
    return pl.pallas_call(
        _moba_body,
        grid=(B, n_steps, nq),
        in_specs=[pl.BlockSpec((1, TILE, w), lambda b, p, i: (b, i, p)),
                  pl.BlockSpec((1, S, w), lambda b, p, i: (b, 0, n_steps + p)),
                  pl.BlockSpec((1, nq, w, TILE), lambda b, p, i: (b, 0, p, 0)),
                  pl.BlockSpec((2, MOBA_STREAMS, TILE, 2 * TILE), lambda b, p, i: (0, p, 0, 0))],
        out_specs=pl.BlockSpec((1, TILE, w), lambda b, p, i: (b, i, p)),
        out_shape=jax.ShapeDtypeStruct((B, S, N_HEADS * HEAD_DIM), BF16),
        scratch_shapes=[pltpu.VMEM((MOBA_STREAMS, nq, LANES), F32),
                        pltpu.VMEM((MOBA_STREAMS, 2 + nq, 2 * TILE), F32),
                        pltpu.VMEM((MOBA_STREAMS, 2, TILE, 2 * TILE), F32),
                        pltpu.VMEM((MOBA_STREAMS, 2, TILE, 2 * TILE), F32),
                        pltpu.VMEM((MOBA_STREAMS, 1, 2 * TILE), F32),
                        pltpu.VMEM((MOBA_STREAMS, 1, 2 * TILE), F32),
                        pltpu.VMEM((MOBA_STREAMS, HEAD_DIM, 2 * TILE), F32)],
        compiler_params=_cparams("parallel", "parallel", "arbitrary"),
        name="moba_attention",
    )(qk, qk, vt, _heads_on_lanes(bias, 2))


def _gelu_tanh(x):
    return 0.5 * x * (1.0 + jnp.tanh(math.sqrt(2.0 / math.pi) * (x + 0.044715 * (x * x * x))))


def _compress_body(t_ref, pos_ref, w1_ref, w2_ref, o_ref):
    groups = t_ref.shape[2]
    half = t_ref.shape[3]
    t = t_ref[0].reshape(NSA_KV_HEADS * groups, half).astype(F32)
    first = jnp.dot((t + pos_ref[0:1, :]).astype(BF16), w1_ref[0:half, :], preferred_element_type=F32)
    second = jnp.dot((t + pos_ref[1:2, :]).astype(BF16), w1_ref[half:2 * half, :],
                     preferred_element_type=F32)
    rows = first.shape[0]
    pre = first + pltpu.roll(second, rows - 1, 0)
    out = jnp.dot(_gelu_tanh(pre).astype(BF16), w2_ref[...], preferred_element_type=F32)
    for h in range(NSA_KV_HEADS):
        o_ref[0, h] = out[h * groups:(h + 1) * groups].astype(o_ref.dtype)


def _compress(t, pos, w1, w2):
    B, Hkv, groups, half = t.shape
    return pl.pallas_call(
        _compress_body,
        grid=(B,),
        in_specs=[pl.BlockSpec((1, Hkv, groups, half), lambda b: (b, 0, 0, 0)),
                  pl.BlockSpec((2, half), lambda b: (0, 0)),
                  pl.BlockSpec((2 * half, CMP_HIDDEN), lambda b: (0, 0)),
                  pl.BlockSpec((CMP_HIDDEN, HEAD_DIM), lambda b: (0, 0))],
        out_specs=pl.BlockSpec((1, Hkv, groups, HEAD_DIM), lambda b: (b, 0, 0, 0)),
        out_shape=jax.ShapeDtypeStruct((B, Hkv, groups, HEAD_DIM), BF16),
        compiler_params=_cparams("parallel"),
        name="nsa_compress",
    )(t, pos.reshape(2, half), w1.astype(BF16), w2.astype(BF16))


NSA_PAIRS = 2


def _swap_halves(x):
    return jnp.concatenate([x[:, HEAD_DIM:], x[:, :HEAD_DIM]], axis=1)


def _group_lanes(x):
    return jnp.concatenate([x] * NSA_GROUP, axis=1)


def _nsa_body(q_ref, kc_ref, vct_ref, ks_ref, vst_ref, kw_ref, vwt_ref, gt_ref, bias_ref, c2s_ref,
              o_ref, selneg_ref, radd_ref, sa_ref, sb_ref, sw_ref, oc_ref, os_ref, m_ref, l_ref, acc_ref):
    i = pl.program_id(2)
    nb = ks_ref.shape[1] // TILE
    n_cmp = kc_ref.shape[1]
    n_slc = c2s_ref.shape[0]
    per_tile = TILE // SLC_BLOCK
    cols = NSA_GROUP * TILE
    kv_heads = range(2 * NSA_PAIRS)
    pair_lanes = lambda a: slice((a // 2) * LANES, (a // 2 + 1) * LANES)
    n_far = jnp.maximum(i - 1, 0)
    n_far_groups = (n_far + 1) >> 1
    last = nb - 1
    t_near = jnp.maximum(i - 1, 0)
    t_edge = jnp.maximum(i - 2, 0)

    dims_of = lambda a: slice(a * HEAD_DIM, (a + 1) * HEAD_DIM)

    def key_tile(k_ref, t):
        return k_ref[0, pl.ds(pl.multiple_of(t * TILE, TILE), TILE), :]

    def scores_of(k_ref, t, a):
        return lax.dot_general(key_tile(k_ref, t)[:, pair_lanes(a)], q4s[a], _NT, preferred_element_type=F32)

    lane = lax.broadcasted_iota(I32, (TILE, LANES), 1)
    lo_half = lane < HEAD_DIM
    qpos = i * TILE + (lax.broadcasted_iota(I32, (n_cmp, cols), 1) & (TILE - 1))
    cmp_valid = CMP_STRIDE * lax.broadcasted_iota(I32, (n_cmp, cols), 0) + (CMP_LEN - 1) <= qpos
    key = lax.broadcasted_iota(I32, (TILE, TILE), 0)
    qry = lax.broadcasted_iota(I32, (TILE, TILE), 1)
    diag_neg = _group_lanes(jnp.where(key <= qry, 0.0, NEG_INF))
    edge_neg = _group_lanes(jnp.where(key > qry, 0.0, NEG_INF))
    qall = q_ref[0]
    q4s = []

    for a in kv_heads:
        keep = lo_half if a % 2 == 0 else jnp.logical_not(lo_half)
        heads = []
        for g in range(NSA_GROUP):
            cb = a * 2 + g // 2
            x = qall[:, cb * LANES:(cb + 1) * LANES]
            if g % 2 != a % 2:
                x = _swap_halves(x)
            heads.append(jnp.where(keep, x, jnp.zeros_like(x)))
        q4s.append(jnp.concatenate(heads, axis=0))
        far_bias = bias_ref[1, a, 0:1, :]

        sa_ref[a, 0] = scores_of(ks_ref, i, a) + (bias_ref[0, a] + diag_neg)
        sa_ref[a, 1] = scores_of(ks_ref, t_near, a) + bias_ref[1, a]
        sw_ref[a, 0] = scores_of(kw_ref, i, a) + (bias_ref[0, a] + diag_neg)
        sw_ref[a, 1] = scores_of(kw_ref, t_near, a) + bias_ref[1, a]
        sw_ref[a, 2] = scores_of(kw_ref, t_edge, a) + (far_bias + edge_neg)

        s_c = lax.dot_general(kc_ref[0, :, pair_lanes(a)], q4s[a], _NT, preferred_element_type=F32)
        s_c = jnp.where(cmp_valid, s_c, NEG_INF)
        m_c = jnp.max(s_c, axis=0, keepdims=True)
        e_c = jnp.where(cmp_valid, jnp.exp2(s_c - m_c), 0.0)
        l_c = jnp.sum(e_c, axis=0, keepdims=True)
        p_c = e_c / jnp.where(l_c > 0.0, l_c, 1.0)
        oc_ref[a] = jnp.dot(vct_ref[0, dims_of(a), :], p_c.astype(BF16), preferred_element_type=F32)

        p_sum = p_c[:, 0:TILE]
        for g in range(1, NSA_GROUP):
            p_sum = p_sum + p_c[:, g * TILE:(g + 1) * TILE]
        p_hi = p_sum.astype(BF16)
        p_lo = (p_sum - p_hi.astype(F32)).astype(BF16)
        imp = (jnp.dot(c2s_ref[...], p_hi, preferred_element_type=F32)
               + jnp.dot(c2s_ref[...], p_lo, preferred_element_type=F32))
        j = lax.broadcasted_iota(I32, imp.shape, 0)
        qb = (i * TILE + lax.broadcasted_iota(I32, imp.shape, 1)) >> int(math.log2(SLC_BLOCK))
        forced = (j == 0) | ((j <= qb) & (j > qb - SLC_LOCAL))
        imp = jnp.where(forced, jnp.inf, jnp.where(j > qb, -jnp.inf, imp))
        rank = _rank_before(imp, n_slc)
        selneg = jnp.where((rank < SLC_TOPN) & (j <= qb), 0.0, NEG_INF)
        selneg_ref[a] = selneg
        for c in range(per_tile):
            radd_ref[a, c:c + 1, :] = _group_lanes(selneg_ref[a, pl.ds(per_tile * i + c, 1), :])
            near_row = _group_lanes(selneg_ref[a, pl.ds(per_tile * t_near + c, 1), :])
            radd_ref[a, per_tile + c:per_tile + c + 1, :] = jnp.where(i >= 1, near_row, NEG_INF)
        for blk in range(n_slc):
            row = far_bias + _group_lanes(selneg[blk:blk + 1, :])
            r = 2 * per_tile + blk
            radd_ref[a, r:r + 1, :] = jnp.where(blk // per_tile < n_far, row, NEG_INF)
        _init_state(m_ref.at[a], l_ref.at[a], acc_ref.at[a])

    def update(jg, a, buf):
        first = jg == 0
        t0 = jnp.where(first, i, 2 * jg - 2)
        t1 = jnp.where(first, t_near, jnp.minimum(2 * jg - 1, last))
        scores, adds = [], []
        for t in range(2):
            for c in range(per_tile):
                scores.append(buf[a, t, c * SLC_BLOCK:(c + 1) * SLC_BLOCK, :])
                adds.append(radd_ref[a, pl.ds(2 * per_tile * jg + per_tile * t + c, 1), :])
        _softmax_pv(scores, adds, [vst_ref[0, t0], vst_ref[0, t1]], [(dims_of(a), slice(None))],
                    m_ref.at[a], l_ref.at[a], acc_ref.at[a])

    def step(jg, src, dst):
        for a in kv_heads:
            dst[a, 0] = scores_of(ks_ref, 2 * jg, a)
            dst[a, 1] = scores_of(ks_ref, jnp.minimum(2 * jg + 1, last), a)
            update(jg, a, src)

    def two_steps(jj, carry):
        step(2 * jj, sa_ref, sb_ref)
        step(2 * jj + 1, sb_ref, sa_ref)
        return carry

    lax.fori_loop(0, n_far_groups >> 1, two_steps, 0)

    @pl.when((n_far_groups & 1) == 1)
    def _():
        step(n_far_groups - 1, sa_ref, sb_ref)
        for a in kv_heads:
            update(n_far_groups, a, sb_ref)

    @pl.when((n_far_groups & 1) == 0)
    def _():
        for a in kv_heads:
            update(n_far_groups, a, sa_ref)

    gates = jax.nn.sigmoid(gt_ref[0, 0])
    zero_row = jnp.zeros((1, cols), F32)
    pieces = []
    for a in kv_heads:
        os_ref[a] = acc_ref[a] / l_ref[a]
        _init_state(m_ref.at[a], l_ref.at[a], acc_ref.at[a])
        _softmax_pv([sw_ref[a, 0], sw_ref[a, 1], sw_ref[a, 2]],
                    [zero_row, zero_row + jnp.where(i >= 1, 0.0, NEG_INF), zero_row + jnp.where(i >= 2, 0.0, NEG_INF)],
                    [vwt_ref[0, i], vwt_ref[0, t_near], vwt_ref[0, t_edge]], [(dims_of(a), slice(None))],
                    m_ref.at[a], l_ref.at[a], acc_ref.at[a])
        o_w = acc_ref[a] / l_ref[a]
        for g in range(NSA_GROUP):
            c0 = (a // 2) * LANES + 3 * (NSA_GROUP * (a % 2) + g)
            ls = slice(g * TILE, (g + 1) * TILE)
            pieces.append(gates[c0:c0 + 1, :] * oc_ref[a, :, ls] + gates[c0 + 1:c0 + 2, :] * os_ref[a, :, ls]
                          + gates[c0 + 2:c0 + 3, :] * o_w[:, ls])
    o_ref[0] = jnp.concatenate(pieces, axis=0).T.astype(o_ref.dtype)


def _cmp_to_slc(S):
    n_cmp_pad = S // CMP_STRIDE
    n_slc = S // SLC_BLOCK
    ci = np.arange(n_cmp_pad)[:, None] * CMP_STRIDE
    sj = np.arange(n_slc)[None, :] * SLC_BLOCK
    c2s = ((ci < sj + SLC_BLOCK) & (ci + CMP_LEN > sj)).astype(np.float32)
    c2s[(S - CMP_LEN) // CMP_STRIDE + 1:] = 0.0
    return jnp.asarray(c2s.T, BF16)


def _nsa_attention(proj, vt, gate_t, kcmp, vcmp_t, bias):
    B, S, _ = proj.shape
    assert S % TILE == 0 and NSA_KV_HEADS == 4 and 2 * HEAD_DIM == LANES
    nq = S // TILE
    n_cmp = kcmp.shape[1]
    n_slc = S // SLC_BLOCK
    chains = 2 * NSA_PAIRS
    pw = NSA_PAIRS * LANES
    qw = chains * NSA_GROUP * HEAD_DIM
    n_steps = NSA_KV_HEADS // chains
    q_blocks = N_HEADS * HEAD_DIM // pw
    kv_blocks = NSA_KV_HEADS * HEAD_DIM // pw
    once = pl.Buffered(1)

    def k_spec(which):
        base = q_blocks + which * kv_blocks
        return pl.BlockSpec((1, S, pw), lambda b, p, i: (b, 0, base + p))

    def vt_spec(which):
        base = which * kv_blocks
        return pl.BlockSpec((1, nq, pw, TILE), lambda b, p, i: (b, 0, base + p, 0))

    state = pltpu.VMEM((chains, HEAD_DIM, NSA_GROUP * TILE), F32)
    stat = pltpu.VMEM((chains, 1, NSA_GROUP * TILE), F32)
    return pl.pallas_call(
        _nsa_body,
        grid=(B, n_steps, nq),
        in_specs=[pl.BlockSpec((1, TILE, qw), lambda b, p, i: (b, i, p)),
                  pl.BlockSpec((1, n_cmp, pw), lambda b, p, i: (b, 0, p)),
                  pl.BlockSpec((1, pw, n_cmp), lambda b, p, i: (b, p, 0)),
                  k_spec(2), vt_spec(0), k_spec(3), vt_spec(1),
                  pl.BlockSpec((1, 1, pw, TILE), lambda b, p, i: (b, i, p, 0)),
                  pl.BlockSpec((2, chains, TILE, NSA_GROUP * TILE), lambda b, p, i: (0, p, 0, 0),
                               pipeline_mode=once if n_steps == 1 else None),
                  pl.BlockSpec((n_slc, n_cmp), lambda b, p, i: (0, 0))],
        out_specs=pl.BlockSpec((1, TILE, qw), lambda b, p, i: (b, i, p)),
        out_shape=jax.ShapeDtypeStruct((B, S, N_HEADS * HEAD_DIM), BF16),
        scratch_shapes=[pltpu.VMEM((chains, n_slc, TILE), F32),
                        pltpu.VMEM((chains, 2 * (TILE // SLC_BLOCK) + n_slc, NSA_GROUP * TILE), F32),
                        pltpu.VMEM((chains, 2, TILE, NSA_GROUP * TILE), F32),
                        pltpu.VMEM((chains, 2, TILE, NSA_GROUP * TILE), F32),
                        pltpu.VMEM((chains, 3, TILE, NSA_GROUP * TILE), F32),
                        state, state, stat, stat, state],
        compiler_params=_cparams("parallel", "parallel", "arbitrary"),
        name="nsa_attention",
    )(proj, kcmp, vcmp_t, proj, vt, proj, vt, gate_t, _heads_on_lanes(bias, NSA_GROUP), _cmp_to_slc(S))


def _split_bf16(x):
    hi = x.astype(BF16)
    return hi, (x - hi.astype(F32)).astype(BF16)


_ROW_OF_EXPERT = np.arange(N_EXPERTS).reshape(N_GROUPS, EXPERTS_PER_GROUP).T.reshape(-1)


def _route(x, w_ref, b_ref, tri_ref, idx_ref, wt_ref, pos_ref, cnt_ref, base_ref):
    @pl.when(pl.program_id(0) == 0)
    def _():
        base_ref[...] = jnp.zeros(base_ref.shape, F32)

    x_hi, x_lo = _split_bf16(x)
    w_hi, w_lo = _split_bf16(w_ref[...])
    logits = (lax.dot_general(w_hi, x_hi, _NT, preferred_element_type=F32)
              + lax.dot_general(w_hi, x_lo, _NT, preferred_element_type=F32)
              + lax.dot_general(w_lo, x_hi, _NT, preferred_element_type=F32)) + b_ref[:, 0:1]
    m = jnp.max(logits, axis=0, keepdims=True)
    e = jnp.exp(logits - m)
    probs = e / jnp.sum(e, axis=0, keepdims=True)
    pk = [probs[k * N_GROUPS:(k + 1) * N_GROUPS] for k in range(EXPERTS_PER_GROUP)]
    hi1, lo1 = jnp.maximum(pk[0], pk[1]), jnp.minimum(pk[0], pk[1])
    hi2, lo2 = jnp.maximum(pk[2], pk[3]), jnp.minimum(pk[2], pk[3])
    score = jnp.maximum(hi1, hi2) + jnp.maximum(jnp.minimum(hi1, hi2), jnp.maximum(lo1, lo2))
    grp = lax.broadcasted_iota(I32, score.shape, 0)
    best = jnp.min(jnp.where(score == jnp.max(score, axis=0, keepdims=True), grp, N_GROUPS),
                   axis=0, keepdims=True)
    v = [jnp.sum(jnp.where(grp == best, p, 0.0), axis=0, keepdims=True) for p in pk]
    v1 = jnp.maximum(jnp.maximum(v[0], v[1]), jnp.maximum(v[2], v[3]))
    i1 = jnp.where(v[0] == v1, 0, jnp.where(v[1] == v1, 1, jnp.where(v[2] == v1, 2, 3)))
    rest = [jnp.where(i1 == k, -1.0, v[k]) for k in range(EXPERTS_PER_GROUP)]
    v2 = jnp.maximum(jnp.maximum(rest[0], rest[1]), jnp.maximum(rest[2], rest[3]))
    i2 = jnp.where(rest[0] == v2, 0, jnp.where(rest[1] == v2, 1, jnp.where(rest[2] == v2, 2, 3)))
    tot = v1 + v2
    idx_ref[...] = jnp.concatenate([best * EXPERTS_PER_GROUP + i1, best * EXPERTS_PER_GROUP + i2], axis=0)
    wt_ref[...] = jnp.concatenate([v1 / tot, v2 / tot], axis=0)

    row = lax.broadcasted_iota(I32, logits.shape, 0)
    hot = [jnp.where(row == ik * N_GROUPS + best, 1.0, 0.0) for ik in (i1, i2)]
    both = (hot[0] + hot[1]).astype(BF16)
    chunks = [slice(c * LANES, (c + 1) * LANES) for c in range(logits.shape[1] // LANES)]
    prefix = [jnp.dot(both[:, ls], tri_ref[...], preferred_element_type=F32) for ls in chunks]
    run = base_ref[:, 0:1]
    pos = [[], []]
    for ls, pre in zip(chunks, prefix):
        before = run + pre - 1.0
        for k in range(2):
            pos[k].append(jnp.sum(hot[k][:, ls] * before, axis=0, keepdims=True))
        run = run + pre[:, LANES - 1:LANES]
    pos_ref[...] = jnp.concatenate([jnp.concatenate(pos[0], axis=1), jnp.concatenate(pos[1], axis=1)],
                                   axis=0).astype(I32)
    base_ref[...] = jnp.broadcast_to(run, base_ref.shape)
    cnt_ref[...] = jnp.broadcast_to(run, cnt_ref.shape)


def _router_operands(router_w, router_b):
    w = router_w.T[_ROW_OF_EXPERT]
    b = jnp.broadcast_to(router_b[_ROW_OF_EXPERT][:, None], (N_EXPERTS, LANES))
    tri = jnp.asarray(np.triu(np.ones((LANES, LANES), np.float32)), BF16)
    return w, b, tri


def _proj_ln_route_body(a_ref, w_ref, x_ref, g_ref, b_ref, rw_ref, rb_ref, tri_ref,
                        o_ref, ob_ref, idx_ref, wt_ref, pos_ref, cnt_ref, base_ref):
    y = jnp.dot(a_ref[...], w_ref[...], preferred_element_type=F32)
    out = _layer_norm_rows(DEEPNORM_ALPHA * x_ref[...] + y, g_ref[...], b_ref[...])
    o_ref[...] = out
    ob_ref[...] = out.astype(BF16)
    _route(out, rw_ref, rb_ref, tri_ref, idx_ref, wt_ref, pos_ref, cnt_ref, base_ref)


def _proj_ln_route(a, w, x, g, b, router_w, router_b):
    M, K = a.shape
    D = w.shape[1]
    assert M % LN_TM == 0 and LN_TM % LANES == 0
    row = pl.BlockSpec((LN_TM, D), lambda i: (i, 0))
    vec = pl.BlockSpec((1, D), lambda i: (0, 0))
    whole = lambda shape: pl.BlockSpec(shape, lambda i: (0, 0))
    tok = lambda dt: jax.ShapeDtypeStruct((2, M), dt)
    tok_spec = pl.BlockSpec((2, LN_TM), lambda i: (0, i))
    h, hb, idx, wts, pos, cnt = pl.pallas_call(
        _proj_ln_route_body,
        grid=(M // LN_TM,),
        in_specs=[pl.BlockSpec((LN_TM, K), lambda i: (i, 0)), whole((K, D)), row, vec, vec,
                  whole((N_EXPERTS, D)), whole((N_EXPERTS, LANES)), whole((LANES, LANES))],
        out_specs=[row, row, tok_spec, tok_spec, tok_spec, whole((N_EXPERTS, LANES))],
        out_shape=[jax.ShapeDtypeStruct((M, D), F32), jax.ShapeDtypeStruct((M, D), BF16),
                   tok(I32), tok(F32), tok(I32), jax.ShapeDtypeStruct((N_EXPERTS, LANES), F32)],
        scratch_shapes=[pltpu.VMEM((N_EXPERTS, LANES), F32)],
        compiler_params=_cparams("arbitrary"),
        name="out_proj_ln_route",
    )(a, w, x, g.reshape(1, D), b.reshape(1, D), *_router_operands(router_w, router_b))
    counts = cnt[np.argsort(_ROW_OF_EXPERT), 0].astype(I32)
    return h, hb, (idx, wts, pos, counts)


def _cast_body(wg_ref, wu_ref, wd_ref, og_ref, ou_ref, od_ref):
    og_ref[...] = wg_ref[0].astype(BF16)
    ou_ref[...] = wu_ref[0].astype(BF16)
    od_ref[...] = wd_ref[0].astype(BF16)


def _expert_weights_bf16(wg, wu, wd, layer):
    E, D, F = wg.shape[1:]
    up = pl.BlockSpec((1, 1, D, F), lambda e: (layer, e, 0, 0))
    down = pl.BlockSpec((1, 1, F, D), lambda e: (layer, e, 0, 0))
    return pl.pallas_call(
        _cast_body,
        grid=(E,),
        in_specs=[up, up, down],
        out_specs=[pl.BlockSpec((1, D, F), lambda e: (e, 0, 0)), pl.BlockSpec((1, D, F), lambda e: (e, 0, 0)),
                   pl.BlockSpec((1, F, D), lambda e: (e, 0, 0))],
        out_shape=[jax.ShapeDtypeStruct((E, D, F), BF16), jax.ShapeDtypeStruct((E, D, F), BF16),
                   jax.ShapeDtypeStruct((E, F, D), BF16)],
        compiler_params=_cparams("parallel"),
        name="moe_weight_cast",
    )(wg, wu, wd)


def _expert_body(blk_e_ref, n_used_ref, x_ref, wg_ref, wu_ref, wd_ref, o_ref):
    i = pl.program_id(0)

    @pl.when(i < n_used_ref[0])
    def _():
        x = x_ref[...]
        gate = jnp.dot(x, wg_ref[0], preferred_element_type=F32)
        up = jnp.dot(x, wu_ref[0], preferred_element_type=F32)
        hid = (gate * jax.nn.sigmoid(gate) * up).astype(BF16)
        o_ref[...] = jnp.dot(hid, wd_ref[0], preferred_element_type=F32).astype(o_ref.dtype)

    @pl.when(i >= n_used_ref[0])
    def _():
        o_ref[...] = jnp.zeros(o_ref.shape, o_ref.dtype)


def _experts(xs, blk_e, n_used, wg, wu, wd):
    R, D = xs.shape
    n_blk = R // MOE_TB

    def live(i, be, nu):
        return jnp.minimum(i, nu[0] - 1)

    grid_spec = pltpu.PrefetchScalarGridSpec(
        num_scalar_prefetch=2,
        grid=(n_blk,),
        in_specs=[pl.BlockSpec((MOE_TB, D), lambda i, be, nu: (live(i, be, nu), 0)),
                  pl.BlockSpec((1, D, D_EXPERT), lambda i, be, nu: (be[i], 0, 0)),
                  pl.BlockSpec((1, D, D_EXPERT), lambda i, be, nu: (be[i], 0, 0)),
                  pl.BlockSpec((1, D_EXPERT, D), lambda i, be, nu: (be[i], 0, 0))],
        out_specs=pl.BlockSpec((MOE_TB, D), lambda i, be, nu: (i, 0)),
    )
    return pl.pallas_call(
        _expert_body,
        grid_spec=grid_spec,
        out_shape=jax.ShapeDtypeStruct((R, D), BF16),
        compiler_params=_cparams("arbitrary"),
        name="moe_experts",
    )(blk_e, n_used, xs, wg, wu, wd)


def _combine_ln_body(x_ref, y0_ref, y1_ref, wt_ref, g_ref, b_ref, o_ref, ob_ref):
    ffn = y0_ref[...] * wt_ref[:, 0:1] + y1_ref[...] * wt_ref[:, HEAD_DIM:HEAD_DIM + 1]
    out = _layer_norm_rows(DEEPNORM_ALPHA * x_ref[...] + ffn, g_ref[...], b_ref[...])
    o_ref[...] = out
    ob_ref[...] = out.astype(BF16)


def _combine_ln(x, y0, y1, wt, g, b):
    M, D = x.shape
    assert M % LN_TM == 0
    row = pl.BlockSpec((LN_TM, D), lambda i: (i, 0))
    vec = pl.BlockSpec((1, D), lambda i: (0, 0))
    return pl.pallas_call(
        _combine_ln_body,
        grid=(M // LN_TM,),
        in_specs=[row, row, row, pl.BlockSpec((LN_TM, LANES), lambda i: (i, 0)), vec, vec],
        out_specs=[row, row],
        out_shape=[jax.ShapeDtypeStruct((M, D), F32), jax.ShapeDtypeStruct((M, D), BF16)],
        compiler_params=_cparams("parallel"),
        name="moe_combine_ln",
    )(x, y0, y1, wt, g.reshape(1, D), b.reshape(1, D))


def _moe_ln(h, hb, routing, wg, wu, wd, layer, g, b):
    N, D = h.shape
    A = 2 * N
    idx, wts, pos, counts = routing
    starts = jnp.cumsum(counts) - counts
    padded = (counts + MOE_TB - 1) // MOE_TB * MOE_TB
    pends = jnp.cumsum(padded)
    pstarts = pends - padded
    R = A + N_EXPERTS * MOE_TB
    n_blk = R // MOE_TB
    experts = jnp.arange(N_EXPERTS, dtype=I32)
    dest = pos + jnp.sum(jnp.where(idx[None] == experts[:, None, None], pstarts[:, None, None], 0), axis=0)
    tok = jnp.broadcast_to(jnp.arange(N, dtype=I32)[None, :], (2, N))
    _, tok_sorted = lax.sort_key_val(dest.reshape(A), tok.reshape(A))
    blk_row0 = jnp.arange(n_blk, dtype=I32) * MOE_TB
    blk_e = jnp.minimum(jnp.sum((pends[None, :] <= blk_row0[:, None]).astype(I32), axis=1), N_EXPERTS - 1)
    hot = blk_e[:, None] == experts[None, :]
    compact0 = blk_row0 + jnp.sum(jnp.where(hot, (starts - pstarts)[None, :], 0), axis=1)
    compact = jnp.remainder(compact0[:, None] + jnp.arange(MOE_TB, dtype=I32)[None, :], A).reshape(R)
    n_used = (pends[-1:] // MOE_TB).astype(I32)
    xs = hb[tok_sorted[compact]]
    yb = _experts(xs, blk_e, n_used, *_expert_weights_bf16(wg, wu, wd, layer))
    wt = jnp.concatenate([jnp.broadcast_to(wts[k][:, None], (N, HEAD_DIM)) for k in range(2)], axis=1)
    return _combine_ln(h, yb[dest[0]], yb[dest[1]], wt, g, b)


def _moba_layer(h, w_in, w_out, bias, g, b, router_w, router_b, B, S):
    HD = N_HEADS * HEAD_DIM
    qk, vt = _in_proj(h, w_in[:, :2 * HD].astype(BF16), _query_scale(HD, 2 * HD),
                      [w_in[:, 2 * HD:].T.astype(BF16)], [BF16], B, S)
    att = _moba_attention(qk.reshape(B, S, 2 * HD), vt, bias)
    return _proj_ln_route(att.reshape(B * S, HD), w_out.astype(BF16), h, g, b, router_w, router_b)


def _nsa_layer(h, hb, w_in, w_out, pos_k, pos_v, ck_w1, ck_w2, cv_w1, cv_w2, bias, g, b, router_w, router_b, B, S):
    HD = N_HEADS * HEAD_DIM
    kvw = NSA_KV_HEADS * HEAD_DIM
    col = lambda k: slice(HD + k * kvw, HD + (k + 1) * kvw)
    w_rows = jnp.concatenate([w_in[:, :HD + 2 * kvw], w_in[:, col(2)], w_in[:, col(4)]], axis=1)
    w_vt = jnp.concatenate([w_in[:, col(3)], w_in[:, col(5)]], axis=1).T
    per_pair = 3 * N_HEADS // 2
    wg = w_in[:, HD + 6 * kvw:].reshape(D_MODEL, 2, per_pair)
    wg = jnp.pad(wg, ((0, 0), (0, 0), (0, LANES - per_pair))).reshape(D_MODEL, 2 * LANES).T
    proj, vt, gate_t = _in_proj(hb, w_rows.astype(BF16), _query_scale(HD, HD + 4 * kvw),
                                [w_vt.astype(BF16), wg.astype(BF16)], [BF16, F32], B, S)
    proj = proj.reshape(B, S, HD + 4 * kvw)

    def grouped(t):
        t = t.reshape(B, S, NSA_KV_HEADS, HEAD_DIM).transpose(0, 2, 1, 3)
        return t.reshape(B, NSA_KV_HEADS, S // CMP_STRIDE, CMP_STRIDE * HEAD_DIM)

    kcmp = _compress(grouped(proj[..., HD:HD + kvw]), pos_k, ck_w1, ck_w2)
    vcmp = _compress(grouped(proj[..., HD + kvw:HD + 2 * kvw]), pos_v, cv_w1, cv_w2)
    n_cmp = kcmp.shape[2]
    kcmp = kcmp.transpose(0, 2, 1, 3).reshape(B, n_cmp, kvw)
    vcmp_t = vcmp.transpose(0, 1, 3, 2).reshape(B, kvw, n_cmp)
    att = _nsa_attention(proj, vt, gate_t, kcmp, vcmp_t, bias)
    return _proj_ln_route(att.reshape(B * S, HD), w_out.astype(BF16), h, g, b, router_w, router_b)


def kernel(x, rel_bias, router_w, router_b, ln_g, ln_b, moba_w_in, moba_w_out, nsa_w_in, nsa_w_out,
           nsa_pos_k, nsa_pos_v, nsa_ck_w1, nsa_ck_w2, nsa_cv_w1, nsa_cv_w2,
           moe_w_gate, moe_w_up, moe_w_down):
    B, S, D = x.shape
    bias = _bias_tiles(rel_bias)
    h = x.reshape(B * S, D)
    h, hb, routing = _moba_layer(h, moba_w_in[0], moba_w_out[0], bias, ln_g[0, 0], ln_b[0, 0],
                                 router_w, router_b, B, S)
    h, hb = _moe_ln(h, hb, routing, moe_w_gate, moe_w_up, moe_w_down, 0, ln_g[0, 1], ln_b[0, 1])
    h, hb, routing = _nsa_layer(h, hb, nsa_w_in[0], nsa_w_out[0], nsa_pos_k[0], nsa_pos_v[0],
                                nsa_ck_w1[0], nsa_ck_w2[0], nsa_cv_w1[0], nsa_cv_w2[0],
                                bias, ln_g[1, 0], ln_b[1, 0], router_w, router_b, B, S)
    h, hb = _moe_ln(h, hb, routing, moe_w_gate, moe_w_up, moe_w_down, 1, ln_g[1, 1], ln_b[1, 1])
    return h.reshape(B, S, D)
```

```python
import math
from functools import partial

import numpy as np
import jax
import jax.numpy as jnp
from jax import lax
from jax.experimental import pallas as pl
from jax.experimental.pallas import tpu as pltpu

F32, BF16, I32 = jnp.float32, jnp.bfloat16, jnp.int32

D_MODEL = 1024
N_HEADS = 16
HEAD_DIM = 64
DEPTH = 2
NEG_INF = -1e30
LN_EPS = 1e-5
MOBA_BLOCK = 256
MOBA_TOPK = 3
NSA_KV_HEADS = 4
NSA_GROUP = N_HEADS // NSA_KV_HEADS
CMP_LEN = 32
CMP_STRIDE = 16
CMP_HIDDEN = 256
SLC_BLOCK = 64
SLC_TOPN = 16
SLC_LOCAL = 2
WINDOW = 512
REL_BUCKETS = 32
REL_MAX_DIST = 128
N_EXPERTS = 32
N_GROUPS = 8
EXPERTS_PER_GROUP = N_EXPERTS // N_GROUPS
D_EXPERT = 512
DEEPNORM_ALPHA = (2 * DEPTH) ** 0.25
LOG2E = math.log2(math.e)
Q_SCALE = HEAD_DIM ** -0.5 * LOG2E

LANES = 128
SUBLANES = 8
ONES_ROWS = 16
TILE = MOBA_BLOCK
MM_TM = 512
MM_TN = 1024
LN_TM = 1024
MOE_TB = 512
assert WINDOW == 2 * TILE and TILE % SLC_BLOCK == 0 and SLC_BLOCK % SUBLANES == 0 and MM_TM % TILE == 0
VMEM_LIMIT = 56 * 1024 * 1024

_NT = (((1,), (1,)), ((), ()))


def _cparams(*sem):
    return pltpu.CompilerParams(dimension_semantics=sem, vmem_limit_bytes=VMEM_LIMIT)


def _in_proj_body(n_t, a_ref, w_ref, c_ref, *refs):
    wt_refs, o_ref, ot_refs = refs[:n_t], refs[n_t], refs[n_t + 1:]
    a = a_ref[...].astype(BF16)
    for c in range(o_ref.shape[1] // MM_TN):
        cols = slice(c * MM_TN, (c + 1) * MM_TN)
        acc = jnp.dot(a, w_ref[:, cols], preferred_element_type=F32)
        o_ref[:, cols] = (acc * c_ref[:, cols]).astype(o_ref.dtype)
    for wt_ref, ot_ref in zip(wt_refs, ot_refs):
        r = lax.dot_general(wt_ref[...], a, _NT, preferred_element_type=F32)
        for t in range(ot_ref.shape[1]):
            ot_ref[0, t] = r[:, t * TILE:(t + 1) * TILE].astype(ot_ref.dtype)


def _in_proj(a, w, col_scale, w_ts, t_dtypes, B, S):
    M, K = a.shape
    N = w.shape[1]
    assert M == B * S and S % MM_TM == 0 and N % MM_TN == 0
    per_seq = S // MM_TM
    sub = MM_TM // TILE
    whole = lambda shape: pl.BlockSpec(shape, lambda i: (0, 0))
    outs = pl.pallas_call(
        partial(_in_proj_body, len(w_ts)),
        grid=(M // MM_TM,),
        in_specs=[pl.BlockSpec((MM_TM, K), lambda i: (i, 0)), whole((K, N)), whole((1, N))]
                 + [whole(w_t.shape) for w_t in w_ts],
        out_specs=[pl.BlockSpec((MM_TM, N), lambda i: (i, 0))]
                  + [pl.BlockSpec((1, sub, w_t.shape[0], TILE), lambda i: (i // per_seq, i % per_seq, 0, 0))
                     for w_t in w_ts],
        out_shape=[jax.ShapeDtypeStruct((M, N), BF16)]
                  + [jax.ShapeDtypeStruct((B, S // TILE, w_t.shape[0], TILE), dt) for w_t, dt in zip(w_ts, t_dtypes)],
        compiler_params=_cparams("parallel"),
        name="in_proj",
    )(a, w, col_scale.reshape(1, N), *w_ts)
    return outs


def _query_scale(n_query_cols, n_cols):
    return jnp.where(jnp.arange(n_cols) < n_query_cols, Q_SCALE, 1.0).astype(F32)


def _layer_norm_rows(z, g, b):
    mu = jnp.mean(z, axis=-1, keepdims=True)
    zc = z - mu
    var = jnp.mean(zc * zc, axis=-1, keepdims=True)
    return zc * lax.rsqrt(var + LN_EPS) * g + b


def _t5_bucket_np(rel):
    n = np.maximum(rel, 0)
    max_exact = REL_BUCKETS // 2
    nf = np.maximum(n, 1).astype(np.float32)
    large = max_exact + (np.log(nf / np.float32(max_exact))
                         / np.float32(math.log(REL_MAX_DIST / max_exact))
                         * np.float32(REL_BUCKETS - max_exact)).astype(np.int32)
    large = np.minimum(large, REL_BUCKETS - 1)
    return np.where(n < max_exact, n, large).astype(np.int32)


def _bias_body(tbl_ref, bk_ref, o_ref):
    h = pl.program_id(0)
    for dl in range(2):
        bk = bk_ref[dl]
        acc = jnp.zeros((TILE, TILE), F32)
        for b in range(REL_BUCKETS):
            acc = jnp.where(bk == b, tbl_ref[h * REL_BUCKETS + b], acc)
        o_ref[dl, 0] = acc * LOG2E


def _bias_tiles(rel_bias):
    key = np.arange(TILE)[:, None]
    qry = np.arange(TILE)[None, :]
    assert int(_t5_bucket_np(np.array(TILE + 1))) == REL_BUCKETS - 1
    bk = np.stack([_t5_bucket_np(qry - key), _t5_bucket_np(TILE + qry - key)])
    return pl.pallas_call(
        _bias_body,
        grid=(N_HEADS,),
        in_specs=[pl.BlockSpec(memory_space=pltpu.SMEM),
                  pl.BlockSpec((2, TILE, TILE), lambda h: (0, 0, 0))],
        out_specs=pl.BlockSpec((2, 1, TILE, TILE), lambda h: (0, h, 0, 0)),
        out_shape=jax.ShapeDtypeStruct((2, N_HEADS, TILE, TILE), F32),
        name="t5_bias_tiles",
    )(rel_bias.T.reshape(-1), jnp.asarray(bk))


def _heads_on_lanes(bias, per_block):
    two, H, T, _ = bias.shape
    b = bias.reshape(two, H // per_block, per_block, T, T).transpose(0, 1, 3, 2, 4)
    return b.reshape(two, H // per_block, T, per_block * T)


def _init_state(m_ref, l_ref, acc_ref):
    m_ref[...] = jnp.full(m_ref.shape, NEG_INF, F32)
    l_ref[...] = jnp.zeros(l_ref.shape, F32)
    acc_ref[...] = jnp.zeros(acc_ref.shape, F32)


def _rank_before(vals, rows):
    idx = lax.broadcasted_iota(I32, vals.shape, 0)
    rank = jnp.zeros(vals.shape, I32)
    for m in range(rows):
        row = vals[m:m + 1, :]
        beats = (row > vals) | ((row == vals) & (idx > m))
        rank = rank + jnp.where(beats, 1, 0)
    return rank


MOBA_STREAMS = 8


def _softmax_pv(scores, adds, vts, heads, m_ref, l_ref, acc_ref):
    def fold(x, op):
        return op(x.reshape(x.shape[0] // SUBLANES, SUBLANES, x.shape[1]), axis=0)

    m_prev = m_ref[...]
    m_part = None
    for s, add in zip(scores, adds):
        part = fold(s, jnp.max) + add
        m_part = part if m_part is None else jnp.maximum(m_part, part)
    m_new = jnp.maximum(m_prev, jnp.max(m_part, axis=0, keepdims=True))
    a = jnp.exp2(m_prev - m_new)
    probs = [jnp.exp2(s - (m_new - add)) for s, add in zip(scores, adds)]
    vt = jnp.concatenate(vts, axis=1)
    pb = jnp.concatenate([p.astype(BF16) for p in probs], axis=0)
    ones = jnp.ones((ONES_ROWS, vt.shape[1]), BF16)
    pv = jnp.concatenate([jnp.dot(jnp.concatenate([vt[rows], ones], axis=0), pb[:, cols],
                                  preferred_element_type=F32) for rows, cols in heads], axis=1)
    l_ref[...] = a * l_ref[...] + pv[HEAD_DIM:HEAD_DIM + 1, :]
    acc_ref[...] = a * acc_ref[...] + pv[:HEAD_DIM, :]
    m_ref[...] = m_new


def _moba_body(q_ref, k_ref, vt_ref, bias_ref, o_ref, kmean_ref, radd_ref, sa_ref, sb_ref, m_ref, l_ref, acc_ref):
    i = pl.program_id(2)
    nb = k_ref.shape[1] // TILE
    streams = range(MOBA_STREAMS)
    lanes_of = lambda s: slice(s * LANES, (s + 1) * LANES)

    @pl.when(i == 0)
    def _():
        for s in streams:
            for n in range(nb):
                kb = k_ref[0, n * TILE:(n + 1) * TILE, lanes_of(s)].astype(F32)
                kmean_ref[s, n:n + 1, :] = jnp.sum(kb, axis=0, keepdims=True) * (1.0 / TILE)

    n_far = jnp.maximum(i - 1, 0)
    n_far_groups = (n_far + 1) >> 1
    last = nb - 1

    def key_tile(t, s):
        return k_ref[0, pl.ds(pl.multiple_of(t * TILE, TILE), TILE), lanes_of(s)]

    def far_tiles(j):
        return 2 * j, jnp.minimum(2 * j + 1, last)

    lane = lax.broadcasted_iota(I32, (TILE, LANES), 1)
    key = lax.broadcasted_iota(I32, (TILE, 2 * TILE), 0)
    qry = lax.broadcasted_iota(I32, (TILE, 2 * TILE), 1) & (TILE - 1)
    causal_neg = jnp.where(key <= qry, 0.0, NEG_INF)
    t_near = jnp.maximum(i - 1, 0)
    q2s = []
    for s in streams:
        q = q_ref[0, :, lanes_of(s)]
        zero = jnp.zeros_like(q)
        q2 = jnp.concatenate([jnp.where(lane < HEAD_DIM, q, zero),
                              jnp.where(lane >= HEAD_DIM, q, zero)], axis=0)
        q2s.append(q2)
        sa_ref[s, 0] = (lax.dot_general(key_tile(i, s), q2, _NT, preferred_element_type=F32)
                        + (bias_ref[0, s] + causal_neg))
        sa_ref[s, 1] = lax.dot_general(key_tile(t_near, s), q2, _NT, preferred_element_type=F32) + bias_ref[1, s]
        km = kmean_ref[s]
        k_hi = km.astype(BF16)
        k_lo = (km - k_hi.astype(F32)).astype(BF16)
        gate = (lax.dot_general(k_hi, q2, _NT, preferred_element_type=F32)
                + lax.dot_general(k_lo, q2, _NT, preferred_element_type=F32))
        blk = lax.broadcasted_iota(I32, gate.shape, 0)
        gate = jnp.where(blk < i, gate, -jnp.inf)
        rank = _rank_before(gate, nb)
        neg = jnp.where((rank < MOBA_TOPK) & (blk < i), 0.0, NEG_INF)
        far_bias = bias_ref[1, s, 0:1, :]
        near_row = jnp.full((1, 2 * TILE), NEG_INF, F32)
        radd_ref[s, 0:1, :] = jnp.zeros((1, 2 * TILE), F32)
        for n in range(nb):
            row = neg[n:n + 1, :]
            near_row = jnp.where(n == i - 1, row, near_row)
            radd_ref[s, 2 + n:3 + n, :] = jnp.where(n < n_far, far_bias + row, NEG_INF)
        radd_ref[s, 1:2, :] = near_row
        _init_state(m_ref.at[s], l_ref.at[s], acc_ref.at[s])

    pair_heads = [(slice(h * HEAD_DIM, (h + 1) * HEAD_DIM), slice(h * TILE, (h + 1) * TILE)) for h in range(2)]

    def update(j, s, buf):
        first = j == 0
        t0 = jnp.where(first, i, 2 * j - 2)
        t1 = jnp.where(first, t_near, jnp.minimum(2 * j - 1, last))
        _softmax_pv([buf[s, 0], buf[s, 1]],
                    [radd_ref[s, pl.ds(2 * j, 1), :], radd_ref[s, pl.ds(2 * j + 1, 1), :]],
                    [vt_ref[0, t0, lanes_of(s), :], vt_ref[0, t1, lanes_of(s), :]], pair_heads,
                    m_ref.at[s], l_ref.at[s], acc_ref.at[s])

    def step(j, src, dst):
        for s in streams:
            ta, tb = far_tiles(j)
            dst[s, 0] = lax.dot_general(key_tile(ta, s), q2s[s], _NT, preferred_element_type=F32)
            dst[s, 1] = lax.dot_general(key_tile(tb, s), q2s[s], _NT, preferred_element_type=F32)
            update(j, s, src)

    def two_steps(jj, carry):
        step(2 * jj, sa_ref, sb_ref)
        step(2 * jj + 1, sb_ref, sa_ref)
        return carry

    lax.fori_loop(0, n_far_groups >> 1, two_steps, 0)

    @pl.when((n_far_groups & 1) == 1)
    def _():
        step(n_far_groups - 1, sa_ref, sb_ref)
        for s in streams:
            update(n_far_groups, s, sb_ref)

    @pl.when((n_far_groups & 1) == 0)
    def _():
        for s in streams:
            update(n_far_groups, s, sa_ref)

    for s in streams:
        o = acc_ref[s] / l_ref[s]
        o = jnp.concatenate([o[:, :TILE], o[:, TILE:]], axis=0)
        o_ref[0, :, lanes_of(s)] = o.T.astype(o_ref.dtype)


def _moba_attention(qk, vt, bias):
    B, S, _ = qk.shape
    assert S % TILE == 0 and N_HEADS % (2 * MOBA_STREAMS) == 0 and 2 * HEAD_DIM == LANES
    n_steps = N_HEADS // 2 // MOBA_STREAMS
    nq = S // TILE
    w = MOBA_STREAMS * LANES
    return pl.pallas_call(
        _moba_body,
        grid=(B, n_steps, nq),
        in_specs=[pl.BlockSpec((1, TILE, w), lambda b, p, i: (b, i, p)),
                  pl.BlockSpec((1, S, w), lambda b, p, i: (b, 0, n_steps + p)),
                  pl.BlockSpec((1, nq, w, TILE), lambda b, p, i: (b, 0, p, 0)),
                  pl.BlockSpec((2, MOBA_STREAMS, TILE, 2 * TILE), lambda b, p, i: (0, p, 0, 0))],
        out_specs=pl.BlockSpec((1, TILE, w), lambda b, p, i: (b, i, p)),
        out_shape=jax.ShapeDtypeStruct((B, S, N_HEADS * HEAD_DIM), BF16),
        scratch_shapes=[pltpu.VMEM((MOBA_STREAMS, nq, LANES), F32),
                        pltpu.VMEM((MOBA_STREAMS, 2 + nq, 2 * TILE), F32),
                        pltpu.VMEM((MOBA_STREAMS, 2, TILE, 2 * TILE), F32),
                        pltpu.VMEM((MOBA_STREAMS, 2, TILE, 2 * TILE), F32),
                        pltpu.VMEM((MOBA_STREAMS, 1, 2 * TILE), F32),
                        pltpu.VMEM((MOBA_STREAMS, 1, 2 * TILE), F32),
                        pltpu.VMEM((MOBA_STREAMS, HEAD_DIM, 2 * TILE), F32)],
        compiler_params=_cparams("parallel", "parallel", "arbitrary"),
        name="moba_attention",
    )(qk, qk, vt, _heads_on_lanes(bias, 2))


def _gelu_tanh(x):
    return 0.5 * x * (1.0 + jnp.tanh(math.sqrt(2.0 / math.pi) * (x + 0.044715 * (x * x * x))))


def _compress_body(t_ref, pos_ref, w1_ref, w2_ref, o_ref):
    groups = t_ref.shape[2]
    half = t_ref.shape[3]
    t = t_ref[0].reshape(NSA_KV_HEADS * groups, half).astype(F32)
    first = jnp.dot((t + pos_ref[0:1, :]).astype(BF16), w1_ref[0:half, :], preferred_element_type=F32)
    second = jnp.dot((t + pos_ref[1:2, :]).astype(BF16), w1_ref[half:2 * half, :],
                     preferred_element_type=F32)
    rows = first.shape[0]
    pre = first + pltpu.roll(second, rows - 1, 0)
    out = jnp.dot(_gelu_tanh(pre).astype(BF16), w2_ref[...], preferred_element_type=F32)
    for h in range(NSA_KV_HEADS):
        o_ref[0, h] = out[h * groups:(h + 1) * groups].astype(o_ref.dtype)


def _compress(t, pos, w1, w2):
    B, Hkv, groups, half = t.shape
    return pl.pallas_call(
        _compress_body,
        grid=(B,),
        in_specs=[pl.BlockSpec((1, Hkv, groups, half), lambda b: (b, 0, 0, 0)),
                  pl.BlockSpec((2, half), lambda b: (0, 0)),
                  pl.BlockSpec((2 * half, CMP_HIDDEN), lambda b: (0, 0)),
                  pl.BlockSpec((CMP_HIDDEN, HEAD_DIM), lambda b: (0, 0))],
        out_specs=pl.BlockSpec((1, Hkv, groups, HEAD_DIM), lambda b: (b, 0, 0, 0)),
        out_shape=jax.ShapeDtypeStruct((B, Hkv, groups, HEAD_DIM), BF16),
        compiler_params=_cparams("parallel"),
        name="nsa_compress",
    )(t, pos.reshape(2, half), w1.astype(BF16), w2.astype(BF16))


NSA_PAIRS = 2


def _swap_halves(x):
    return jnp.concatenate([x[:, HEAD_DIM:], x[:, :HEAD_DIM]], axis=1)


def _group_lanes(x):
    return jnp.concatenate([x] * NSA_GROUP, axis=1)


def _nsa_body(q_ref, kc_ref, vct_ref, ks_ref, vst_ref, kw_ref, vwt_ref, gt_ref, bias_ref, c2s_ref,
              o_ref, selneg_ref, radd_ref, sa_ref, sb_ref, sw_ref, oc_ref, os_ref, m_ref, l_ref, acc_ref):
    i = pl.program_id(2)
    nb = ks_ref.shape[1] // TILE
    n_cmp = kc_ref.shape[1]
    n_slc = c2s_ref.shape[0]
    per_tile = TILE // SLC_BLOCK
    cols = NSA_GROUP * TILE
    kv_heads = range(2 * NSA_PAIRS)
    pair_lanes = lambda a: slice((a // 2) * LANES, (a // 2 + 1) * LANES)
    n_far = jnp.maximum(i - 1, 0)
    n_far_groups = (n_far + 1) >> 1
    last = nb - 1
    t_near = jnp.maximum(i - 1, 0)
    t_edge = jnp.maximum(i - 2, 0)

    dims_of = lambda a: slice(a * HEAD_DIM, (a + 1) * HEAD_DIM)

    def key_tile(k_ref, t):
        return k_ref[0, pl.ds(pl.multiple_of(t * TILE, TILE), TILE), :]

    def scores_of(k_ref, t, a):
        return lax.dot_general(key_tile(k_ref, t)[:, pair_lanes(a)], q4s[a], _NT, preferred_element_type=F32)

    lane = lax.broadcasted_iota(I32, (TILE, LANES), 1)
    lo_half = lane < HEAD_DIM
    qpos = i * TILE + (lax.broadcasted_iota(I32, (n_cmp, cols), 1) & (TILE - 1))
    cmp_valid = CMP_STRIDE * lax.broadcasted_iota(I32, (n_cmp, cols), 0) + (CMP_LEN - 1) <= qpos
    key = lax.broadcasted_iota(I32, (TILE, TILE), 0)
    qry = lax.broadcasted_iota(I32, (TILE, TILE), 1)
    diag_neg = _group_lanes(jnp.where(key <= qry, 0.0, NEG_INF))
    edge_neg = _group_lanes(jnp.where(key > qry, 0.0, NEG_INF))
    qall = q_ref[0]
    q4s = []

    for a in kv_heads:
        keep = lo_half if a % 2 == 0 else jnp.logical_not(lo_half)
        heads = []
        for g in range(NSA_GROUP):
            cb = a * 2 + g // 2
            x = qall[:, cb * LANES:(cb + 1) * LANES]
            if g % 2 != a % 2:
                x = _swap_halves(x)
            heads.append(jnp.where(keep, x, jnp.zeros_like(x)))
        q4s.append(jnp.concatenate(heads, axis=0))
        far_bias = bias_ref[1, a, 0:1, :]

        sa_ref[a, 0] = scores_of(ks_ref, i, a) + (bias_ref[0, a] + diag_neg)
        sa_ref[a, 1] = scores_of(ks_ref, t_near, a) + bias_ref[1, a]
        sw_ref[a, 0] = scores_of(kw_ref, i, a) + (bias_ref[0, a] + diag_neg)
        sw_ref[a, 1] = scores_of(kw_ref, t_near, a) + bias_ref[1, a]
        sw_ref[a, 2] = scores_of(kw_ref, t_edge, a) + (far_bias + edge_neg)

        s_c = lax.dot_general(kc_ref[0, :, pair_lanes(a)], q4s[a], _NT, preferred_element_type=F32)
        s_c = jnp.where(cmp_valid, s_c, NEG_INF)
        m_c = jnp.max(s_c, axis=0, keepdims=True)
        e_c = jnp.where(cmp_valid, jnp.exp2(s_c - m_c), 0.0)
        l_c = jnp.sum(e_c, axis=0, keepdims=True)
        p_c = e_c / jnp.where(l_c > 0.0, l_c, 1.0)
        oc_ref[a] = jnp.dot(vct_ref[0, dims_of(a), :], p_c.astype(BF16), preferred_element_type=F32)

        p_sum = p_c[:, 0:TILE]
        for g in range(1, NSA_GROUP):
            p_sum = p_sum + p_c[:, g * TILE:(g + 1) * TILE]
        p_hi = p_sum.astype(BF16)
        p_lo = (p_sum - p_hi.astype(F32)).astype(BF16)
        imp = (jnp.dot(c2s_ref[...], p_hi, preferred_element_type=F32)
               + jnp.dot(c2s_ref[...], p_lo, preferred_element_type=F32))
        j = lax.broadcasted_iota(I32, imp.shape, 0)
        qb = (i * TILE + lax.broadcasted_iota(I32, imp.shape, 1)) >> int(math.log2(SLC_BLOCK))
        forced = (j == 0) | ((j <= qb) & (j > qb - SLC_LOCAL))
        imp = jnp.where(forced, jnp.inf, jnp.where(j > qb, -jnp.inf, imp))
        rank = _rank_before(imp, n_slc)
        selneg = jnp.where((rank < SLC_TOPN) & (j <= qb), 0.0, NEG_INF)
        selneg_ref[a] = selneg
        for c in range(per_tile):
            radd_ref[a, c:c + 1, :] = _group_lanes(selneg_ref[a, pl.ds(per_tile * i + c, 1), :])
            near_row = _group_lanes(selneg_ref[a, pl.ds(per_tile * t_near + c, 1), :])
            radd_ref[a, per_tile + c:per_tile + c + 1, :] = jnp.where(i >= 1, near_row, NEG_INF)
        for blk in range(n_slc):
            row = far_bias + _group_lanes(selneg[blk:blk + 1, :])
            r = 2 * per_tile + blk
            radd_ref[a, r:r + 1, :] = jnp.where(blk // per_tile < n_far, row, NEG_INF)
        _init_state(m_ref.at[a], l_ref.at[a], acc_ref.at[a])

    def update(jg, a, buf):
        first = jg == 0
        t0 = jnp.where(first, i, 2 * jg - 2)
        t1 = jnp.where(first, t_near, jnp.minimum(2 * jg - 1, last))
        scores, adds = [], []
        for t in range(2):
            for c in range(per_tile):
                scores.append(buf[a, t, c * SLC_BLOCK:(c + 1) * SLC_BLOCK, :])
                adds.append(radd_ref[a, pl.ds(2 * per_tile * jg + per_tile * t + c, 1), :])
        _softmax_pv(scores, adds, [vst_ref[0, t0], vst_ref[0, t1]], [(dims_of(a), slice(None))],
                    m_ref.at[a], l_ref.at[a], acc_ref.at[a])

    def step(jg, src, dst):
        for a in kv_heads:
            dst[a, 0] = scores_of(ks_ref, 2 * jg, a)
            dst[a, 1] = scores_of(ks_ref, jnp.minimum(2 * jg + 1, last), a)
            update(jg, a, src)

    def two_steps(jj, carry):
        step(2 * jj, sa_ref, sb_ref)
        step(2 * jj + 1, sb_ref, sa_ref)
        return carry

    lax.fori_loop(0, n_far_groups >> 1, two_steps, 0)

    @pl.when((n_far_groups & 1) == 1)
    def _():
        step(n_far_groups - 1, sa_ref, sb_ref)
        for a in kv_heads:
            update(n_far_groups, a, sb_ref)

    @pl.when((n_far_groups & 1) == 0)
    def _():
        for a in kv_heads:
            update(n_far_groups, a, sa_ref)

    gates = jax.nn.sigmoid(gt_ref[0, 0])
    zero_row = jnp.zeros((1, cols), F32)
    pieces = []
    for a in kv_heads:
        os_ref[a] = acc_ref[a] / l_ref[a]
        _init_state(m_ref.at[a], l_ref.at[a], acc_ref.at[a])
        _softmax_pv([sw_ref[a, 0], sw_ref[a, 1], sw_ref[a, 2]],
                    [zero_row, zero_row + jnp.where(i >= 1, 0.0, NEG_INF), zero_row + jnp.where(i >= 2, 0.0, NEG_INF)],
                    [vwt_ref[0, i], vwt_ref[0, t_near], vwt_ref[0, t_edge]], [(dims_of(a), slice(None))],
                    m_ref.at[a], l_ref.at[a], acc_ref.at[a])
        o_w = acc_ref[a] / l_ref[a]
        for g in range(NSA_GROUP):
            c0 = (a // 2) * LANES + 3 * (NSA_GROUP * (a % 2) + g)
            ls = slice(g * TILE, (g + 1) * TILE)
            pieces.append(gates[c0:c0 + 1, :] * oc_ref[a, :, ls] + gates[c0 + 1:c0 + 2, :] * os_ref[a, :, ls]
                          + gates[c0 + 2:c0 + 3, :] * o_w[:, ls])
    o_ref[0] = jnp.concatenate(pieces, axis=0).T.astype(o_ref.dtype)


def _cmp_to_slc(S):
    n_cmp_pad = S // CMP_STRIDE
    n_slc = S // SLC_BLOCK
    ci = np.arange(n_cmp_pad)[:, None] * CMP_STRIDE
    sj = np.arange(n_slc)[None, :] * SLC_BLOCK
    c2s = ((ci < sj + SLC_BLOCK) & (ci + CMP_LEN > sj)).astype(np.float32)
    c2s[(S - CMP_LEN) // CMP_STRIDE + 1:] = 0.0
    return jnp.asarray(c2s.T, BF16)


def _nsa_attention(proj, vt, gate_t, kcmp, vcmp_t, bias):
    B, S, _ = proj.shape
    assert S % TILE == 0 and NSA_KV_HEADS == 4 and 2 * HEAD_DIM == LANES
    nq = S // TILE
    n_cmp = kcmp.shape[1]
    n_slc = S // SLC_BLOCK
    chains = 2 * NSA_PAIRS
    pw = NSA_PAIRS * LANES
    qw = chains * NSA_GROUP * HEAD_DIM
    n_steps = NSA_KV_HEADS // chains
    q_blocks = N_HEADS * HEAD_DIM // pw
    kv_blocks = NSA_KV_HEADS * HEAD_DIM // pw
    once = pl.Buffered(1)

    def k_spec(which):
        base = q_blocks + which * kv_blocks
        return pl.BlockSpec((1, S, pw), lambda b, p, i: (b, 0, base + p))

    def vt_spec(which):
        base = which * kv_blocks
        return pl.BlockSpec((1, nq, pw, TILE), lambda b, p, i: (b, 0, base + p, 0))

    state = pltpu.VMEM((chains, HEAD_DIM, NSA_GROUP * TILE), F32)
    stat = pltpu.VMEM((chains, 1, NSA_GROUP * TILE), F32)
    return pl.pallas_call(
        _nsa_body,
        grid=(B, n_steps, nq),
        in_specs=[pl.BlockSpec((1, TILE, qw), lambda b, p, i: (b, i, p)),
                  pl.BlockSpec((1, n_cmp, pw), lambda b, p, i: (b, 0, p)),
                  pl.BlockSpec((1, pw, n_cmp), lambda b, p, i: (b, p, 0)),
                  k_spec(2), vt_spec(0), k_spec(3), vt_spec(1),
                  pl.BlockSpec((1, 1, pw, TILE), lambda b, p, i: (b, i, p, 0)),
                  pl.BlockSpec((2, chains, TILE, NSA_GROUP * TILE), lambda b, p, i: (0, p, 0, 0),
                               pipeline_mode=once if n_steps == 1 else None),
                  pl.BlockSpec((n_slc, n_cmp), lambda b, p, i: (0, 0))],
        out_specs=pl.BlockSpec((1, TILE, qw), lambda b, p, i: (b, i, p)),
        out_shape=jax.ShapeDtypeStruct((B, S, N_HEADS * HEAD_DIM), BF16),
        scratch_shapes=[pltpu.VMEM((chains, n_slc, TILE), F32),
                        pltpu.VMEM((chains, 2 * (TILE // SLC_BLOCK) + n_slc, NSA_GROUP * TILE), F32),
                        pltpu.VMEM((chains, 2, TILE, NSA_GROUP * TILE), F32),
                        pltpu.VMEM((chains, 2, TILE, NSA_GROUP * TILE), F32),
                        pltpu.VMEM((chains, 3, TILE, NSA_GROUP * TILE), F32),
                        state, state, stat, stat, state],
        compiler_params=_cparams("parallel", "parallel", "arbitrary"),
        name="nsa_attention",
    )(proj, kcmp, vcmp_t, proj, vt, proj, vt, gate_t, _heads_on_lanes(bias, NSA_GROUP), _cmp_to_slc(S))


def _split_bf16(x):
    hi = x.astype(BF16)
    return hi, (x - hi.astype(F32)).astype(BF16)


_ROW_OF_EXPERT = np.arange(N_EXPERTS).reshape(N_GROUPS, EXPERTS_PER_GROUP).T.reshape(-1)


def _route(x, w_ref, b_ref, tri_ref, idx_ref, wt_ref, pos_ref, cnt_ref, base_ref):
    @pl.when(pl.program_id(0) == 0)
    def _():
        base_ref[...] = jnp.zeros(base_ref.shape, F32)

    x_hi, x_lo = _split_bf16(x)
    w_hi, w_lo = _split_bf16(w_ref[...])
    logits = (lax.dot_general(w_hi, x_hi, _NT, preferred_element_type=F32)
              + lax.dot_general(w_hi, x_lo, _NT, preferred_element_type=F32)
              + lax.dot_general(w_lo, x_hi, _NT, preferred_element_type=F32)) + b_ref[:, 0:1]
    m = jnp.max(logits, axis=0, keepdims=True)
    e = jnp.exp(logits - m)
    probs = e / jnp.sum(e, axis=0, keepdims=True)
    pk = [probs[k * N_GROUPS:(k + 1) * N_GROUPS] for k in range(EXPERTS_PER_GROUP)]
    hi1, lo1 = jnp.maximum(pk[0], pk[1]), jnp.minimum(pk[0], pk[1])
    hi2, lo2 = jnp.maximum(pk[2], pk[3]), jnp.minimum(pk[2], pk[3])
    score = jnp.maximum(hi1, hi2) + jnp.maximum(jnp.minimum(hi1, hi2), jnp.maximum(lo1, lo2))
    grp = lax.broadcasted_iota(I32, score.shape, 0)
    best = jnp.min(jnp.where(score == jnp.max(score, axis=0, keepdims=True), grp, N_GROUPS),
                   axis=0, keepdims=True)
    v = [jnp.sum(jnp.where(grp == best, p, 0.0), axis=0, keepdims=True) for p in pk]
    v1 = jnp.maximum(jnp.maximum(v[0], v[1]), jnp.maximum(v[2], v[3]))
    i1 = jnp.where(v[0] == v1, 0, jnp.where(v[1] == v1, 1, jnp.where(v[2] == v1, 2, 3)))
    rest = [jnp.where(i1 == k, -1.0, v[k]) for k in range(EXPERTS_PER_GROUP)]
    v2 = jnp.maximum(jnp.maximum(rest[0], rest[1]), jnp.maximum(rest[2], rest[3]))
    i2 = jnp.where(rest[0] == v2, 0, jnp.where(rest[1] == v2, 1, jnp.where(rest[2] == v2, 2, 3)))
    tot = v1 + v2
    idx_ref[...] = jnp.concatenate([best * EXPERTS_PER_GROUP + i1, best * EXPERTS_PER_GROUP + i2], axis=0)
    wt_ref[...] = jnp.concatenate([v1 / tot, v2 / tot], axis=0)

    row = lax.broadcasted_iota(I32, logits.shape, 0)
    hot = [jnp.where(row == ik * N_GROUPS + best, 1.0, 0.0) for ik in (i1, i2)]
    both = (hot[0] + hot[1]).astype(BF16)
    chunks = [slice(c * LANES, (c + 1) * LANES) for c in range(logits.shape[1] // LANES)]
    prefix = [jnp.dot(both[:, ls], tri_ref[...], preferred_element_type=F32) for ls in chunks]
    run = base_ref[:, 0:1]
    pos = [[], []]
    for ls, pre in zip(chunks, prefix):
        before = run + pre - 1.0
        for k in range(2):
            pos[k].append(jnp.sum(hot[k][:, ls] * before, axis=0, keepdims=True))
        run = run + pre[:, LANES - 1:LANES]
    pos_ref[...] = jnp.concatenate([jnp.concatenate(pos[0], axis=1), jnp.concatenate(pos[1], axis=1)],
                                   axis=0).astype(I32)
    base_ref[...] = jnp.broadcast_to(run, base_ref.shape)
    cnt_ref[...] = jnp.broadcast_to(run, cnt_ref.shape)


def _router_operands(router_w, router_b):
    w = router_w.T[_ROW_OF_EXPERT]
    b = jnp.broadcast_to(router_b[_ROW_OF_EXPERT][:, None], (N_EXPERTS, LANES))
    tri = jnp.asarray(np.triu(np.ones((LANES, LANES), np.float32)), BF16)
    return w, b, tri


def _proj_ln_route_body(a_ref, w_ref, x_ref, g_ref, b_ref, rw_ref, rb_ref, tri_ref,
                        o_ref, ob_ref, idx_ref, wt_ref, pos_ref, cnt_ref, base_ref):
    y = jnp.dot(a_ref[...], w_ref[...], preferred_element_type=F32)
    out = _layer_norm_rows(DEEPNORM_ALPHA * x_ref[...] + y, g_ref[...], b_ref[...])
    o_ref[...] = out
    ob_ref[...] = out.astype(BF16)
    _route(out, rw_ref, rb_ref, tri_ref, idx_ref, wt_ref, pos_ref, cnt_ref, base_ref)


def _proj_ln_route(a, w, x, g, b, router_w, router_b):
    M, K = a.shape
    D = w.shape[1]
    assert M % LN_TM == 0 and LN_TM % LANES == 0
    row = pl.BlockSpec((LN_TM, D), lambda i: (i, 0))
    vec = pl.BlockSpec((1, D), lambda i: (0, 0))
    whole = lambda shape: pl.BlockSpec(shape, lambda i: (0, 0))
    tok = lambda dt: jax.ShapeDtypeStruct((2, M), dt)
    tok_spec = pl.BlockSpec((2, LN_TM), lambda i: (0, i))
    h, hb, idx, wts, pos, cnt = pl.pallas_call(
        _proj_ln_route_body,
        grid=(M // LN_TM,),
        in_specs=[pl.BlockSpec((LN_TM, K), lambda i: (i, 0)), whole((K, D)), row, vec, vec,
                  whole((N_EXPERTS, D)), whole((N_EXPERTS, LANES)), whole((LANES, LANES))],
        out_specs=[row, row, tok_spec, tok_spec, tok_spec, whole((N_EXPERTS, LANES))],
        out_shape=[jax.ShapeDtypeStruct((M, D), F32), jax.ShapeDtypeStruct((M, D), BF16),
                   tok(I32), tok(F32), tok(I32), jax.ShapeDtypeStruct((N_EXPERTS, LANES), F32)],
        scratch_shapes=[pltpu.VMEM((N_EXPERTS, LANES), F32)],
        compiler_params=_cparams("arbitrary"),
        name="out_proj_ln_route",
    )(a, w, x, g.reshape(1, D), b.reshape(1, D), *_router_operands(router_w, router_b))
    counts = cnt[np.argsort(_ROW_OF_EXPERT), 0].astype(I32)
    return h, hb, (idx, wts, pos, counts)


def _expert_body(layer, blk_e_ref, n_used_ref, run_ref, next_e_ref, x_ref, wg_hbm, wu_hbm, wd_hbm, o_ref,
                 wg_f, wu_f, wd_f, wg_b, wu_b, wd_b, sems):
    i = pl.program_id(0)
    e = blk_e_ref[i]

    def weight_copies(expert, slot):
        return [pltpu.make_async_copy(hbm.at[layer, expert], buf.at[slot], sems.at[slot, k])
                for k, (hbm, buf) in enumerate(((wg_hbm, wg_f), (wu_hbm, wu_f), (wd_hbm, wd_f)))]

    @pl.when(i == 0)
    def _():
        for c in weight_copies(e, 0):
            c.start()

    @pl.when((i == 0) | (e != blk_e_ref[jnp.maximum(i - 1, 0)]))
    def _():
        slot = run_ref[i] & 1
        for c in weight_copies(e, slot):
            c.wait()
        wg_b[...] = wg_f[slot].astype(BF16)
        wu_b[...] = wu_f[slot].astype(BF16)
        wd_b[...] = wd_f[slot].astype(BF16)
        nxt = next_e_ref[i]

        @pl.when(nxt >= 0)
        def _():
            for c in weight_copies(nxt, 1 - slot):
                c.start()

    @pl.when(i < n_used_ref[0])
    def _():
        x = x_ref[...]
        gate = jnp.dot(x, wg_b[...], preferred_element_type=F32)
        up = jnp.dot(x, wu_b[...], preferred_element_type=F32)
        hid = (gate * jax.nn.sigmoid(gate) * up).astype(BF16)
        o_ref[...] = jnp.dot(hid, wd_b[...], preferred_element_type=F32).astype(o_ref.dtype)

    @pl.when(i >= n_used_ref[0])
    def _():
        o_ref[...] = jnp.zeros(o_ref.shape, o_ref.dtype)


def _experts(xs, blk_e, n_used, wg, wu, wd, layer):
    R, D = xs.shape
    n_blk = R // MOE_TB

    change = jnp.concatenate([jnp.ones((1,), I32), (blk_e[1:] != blk_e[:-1]).astype(I32)])
    run = jnp.cumsum(change) - 1
    later = jnp.where(blk_e[None, :] > blk_e[:, None], blk_e[None, :], N_EXPERTS)
    next_e = jnp.min(later, axis=1)
    next_e = jnp.where(next_e < N_EXPERTS, next_e, -1).astype(I32)

    grid_spec = pltpu.PrefetchScalarGridSpec(
        num_scalar_prefetch=4,
        grid=(n_blk,),
        in_specs=[pl.BlockSpec((MOE_TB, D), lambda i, be, nu, rn, nx: (jnp.minimum(i, nu[0] - 1), 0)),
                  pl.BlockSpec(memory_space=pl.ANY), pl.BlockSpec(memory_space=pl.ANY),
                  pl.BlockSpec(memory_space=pl.ANY)],
        out_specs=pl.BlockSpec((MOE_TB, D), lambda i, be, nu, rn, nx: (i, 0)),
        scratch_shapes=[pltpu.VMEM((2, D, D_EXPERT), F32), pltpu.VMEM((2, D, D_EXPERT), F32),
                        pltpu.VMEM((2, D_EXPERT, D), F32),
                        pltpu.VMEM((D, D_EXPERT), BF16), pltpu.VMEM((D, D_EXPERT), BF16),
                        pltpu.VMEM((D_EXPERT, D), BF16),
                        pltpu.SemaphoreType.DMA((2, 3))],
    )
    return pl.pallas_call(
        partial(_expert_body, layer),
        grid_spec=grid_spec,
        out_shape=jax.ShapeDtypeStruct((R, D), BF16),
        compiler_params=_cparams("arbitrary"),
        name="moe_experts",
    )(blk_e, n_used, run.astype(I32), next_e, xs, wg, wu, wd)


def _combine_ln_body(x_ref, y0_ref, y1_ref, wt_ref, g_ref, b_ref, o_ref, ob_ref):
    ffn = y0_ref[...] * wt_ref[:, 0:1] + y1_ref[...] * wt_ref[:, HEAD_DIM:HEAD_DIM + 1]
    out = _layer_norm_rows(DEEPNORM_ALPHA * x_ref[...] + ffn, g_ref[...], b_ref[...])
    o_ref[...] = out
    ob_ref[...] = out.astype(BF16)


def _combine_ln(x, y0, y1, wt, g, b):
    M, D = x.shape
    assert M % LN_TM == 0
    row = pl.BlockSpec((LN_TM, D), lambda i: (i, 0))
    vec = pl.BlockSpec((1, D), lambda i: (0, 0))
    return pl.pallas_call(
        _combine_ln_body,
        grid=(M // LN_TM,),
        in_specs=[row, row, row, pl.BlockSpec((LN_TM, LANES), lambda i: (i, 0)), vec, vec],
        out_specs=[row, row],
        out_shape=[jax.ShapeDtypeStruct((M, D), F32), jax.ShapeDtypeStruct((M, D), BF16)],
        compiler_params=_cparams("parallel"),
        name="moe_combine_ln",
    )(x, y0, y1, wt, g.reshape(1, D), b.reshape(1, D))


def _moe_ln(h, hb, routing, wg, wu, wd, layer, g, b):
    N, D = h.shape
    A = 2 * N
    idx, wts, pos, counts = routing
    starts = jnp.cumsum(counts) - counts
    padded = (counts + MOE_TB - 1) // MOE_TB * MOE_TB
    pends = jnp.cumsum(padded)
    pstarts = pends - padded
    R = A + N_EXPERTS * MOE_TB
    n_blk = R // MOE_TB
    experts = jnp.arange(N_EXPERTS, dtype=I32)
    dest = pos + jnp.sum(jnp.where(idx[None] == experts[:, None, None], pstarts[:, None, None], 0), axis=0)
    tok = jnp.broadcast_to(jnp.arange(N, dtype=I32)[None, :], (2, N))
    _, tok_sorted = lax.sort_key_val(dest.reshape(A), tok.reshape(A))
    blk_row0 = jnp.arange(n_blk, dtype=I32) * MOE_TB
    blk_e = jnp.minimum(jnp.sum((pends[None, :] <= blk_row0[:, None]).astype(I32), axis=1), N_EXPERTS - 1)
    hot = blk_e[:, None] == experts[None, :]
    compact0 = blk_row0 + jnp.sum(jnp.where(hot, (starts - pstarts)[None, :], 0), axis=1)
    compact = jnp.remainder(compact0[:, None] + jnp.arange(MOE_TB, dtype=I32)[None, :], A).reshape(R)
    n_used = (pends[-1:] // MOE_TB).astype(I32)
    xs = hb[tok_sorted[compact]]
    yb = _experts(xs, blk_e, n_used, wg, wu, wd, layer)
    wt = jnp.concatenate([jnp.broadcast_to(wts[k][:, None], (N, HEAD_DIM)) for k in range(2)], axis=1)
    return _combine_ln(h, yb[dest[0]], yb[dest[1]], wt, g, b)


def _moba_layer(h, w_in, w_out, bias, g, b, router_w, router_b, B, S):
    HD = N_HEADS * HEAD_DIM
    qk, vt = _in_proj(h, w_in[:, :2 * HD].astype(BF16), _query_scale(HD, 2 * HD),
                      [w_in[:, 2 * HD:].T.astype(BF16)], [BF16], B, S)
    att = _moba_attention(qk.reshape(B, S, 2 * HD), vt, bias)
    return _proj_ln_route(att.reshape(B * S, HD), w_out.astype(BF16), h, g, b, router_w, router_b)


def _nsa_layer(h, hb, w_in, w_out, pos_k, pos_v, ck_w1, ck_w2, cv_w1, cv_w2, bias, g, b, router_w, router_b, B, S):
    HD = N_HEADS * HEAD_DIM
    kvw = NSA_KV_HEADS * HEAD_DIM
    col = lambda k: slice(HD + k * kvw, HD + (k + 1) * kvw)
    w_rows = jnp.concatenate([w_in[:, :HD + 2 * kvw], w_in[:, col(2)], w_in[:, col(4)]], axis=1)
    w_vt = jnp.concatenate([w_in[:, col(3)], w_in[:, col(5)]], axis=1).T
    per_pair = 3 * N_HEADS // 2
    wg = w_in[:, HD + 6 * kvw:].reshape(D_MODEL, 2, per_pair)
    wg = jnp.pad(wg, ((0, 0), (0, 0), (0, LANES - per_pair))).reshape(D_MODEL, 2 * LANES).T
    proj, vt, gate_t = _in_proj(hb, w_rows.astype(BF16), _query_scale(HD, HD + 4 * kvw),
                                [w_vt.astype(BF16), wg.astype(BF16)], [BF16, F32], B, S)
    proj = proj.reshape(B, S, HD + 4 * kvw)

    def grouped(t):
        t = t.reshape(B, S, NSA_KV_HEADS, HEAD_DIM).transpose(0, 2, 1, 3)
        return t.reshape(B, NSA_KV_HEADS, S // CMP_STRIDE, CMP_STRIDE * HEAD_DIM)

    kcmp = _compress(grouped(proj[..., HD:HD + kvw]), pos_k, ck_w1, ck_w2)
    vcmp = _compress(grouped(proj[..., HD + kvw:HD + 2 * kvw]), pos_v, cv_w1, cv_w2)
    n_cmp = kcmp.shape[2]
    kcmp = kcmp.transpose(0, 2, 1, 3).reshape(B, n_cmp, kvw)
    vcmp_t = vcmp.transpose(0, 1, 3, 2).reshape(B, kvw, n_cmp)
    att = _nsa_attention(proj, vt, gate_t, kcmp, vcmp_t, bias)
    return _proj_ln_route(att.reshape(B * S, HD), w_out.astype(BF16), h, g, b, router_w, router_b)


def kernel(x, rel_bias, router_w, router_b, ln_g, ln_b, moba_w_in, moba_w_out, nsa_w_in, nsa_w_out,
           nsa_pos_k, nsa_pos_v, nsa_ck_w1, nsa_ck_w2, nsa_cv_w1, nsa_cv_w2,
           moe_w_gate, moe_w_up, moe_w_down):
    B, S, D = x.shape
    bias = _bias_tiles(rel_bias)
    h = x.reshape(B * S, D)
    h, hb, routing = _moba_layer(h, moba_w_in[0], moba_w_out[0], bias, ln_g[0, 0], ln_b[0, 0],
                                 router_w, router_b, B, S)
    h, hb = _moe_ln(h, hb, routing, moe_w_gate, moe_w_up, moe_w_down, 0, ln_g[0, 1], ln_b[0, 1])
    h, hb, routing = _nsa_layer(h, hb, nsa_w_in[0], nsa_w_out[0], nsa_pos_k[0], nsa_pos_v[0],
                                nsa_ck_w1[0], nsa_ck_w2[0], nsa_cv_w1[0], nsa_cv_w2[0],
                                bias, ln_g[1, 0], ln_b[1, 0], router_w, router_b, B, S)
    h, hb = _moe_ln(h, hb, routing, moe_w_gate, moe_w_up, moe_w_down, 1, ln_g[1, 1], ln_b[1, 1])
    return h.reshape(B, S, D)
```
